```python
import jax
import jax.numpy as jnp
from jax import lax
import numpy as np

D_MODEL = 1024
BATCH = 4
SEQ = 4096
DEPTH = 2
DEC_BATCH = 16
DEC_SEQ = 32
PAST_LEN = 1024

CHUNK = 64
N_GROUPS = 4
GROUP_WIDTH = D_MODEL // N_GROUPS
MIX_WIDTH = N_GROUPS * GROUP_WIDTH
CONV_WIDTH = 31
CONV_HIST = CONV_WIDTH - 1
RWKV_HEAD = 64
RWKV_HEADS = GROUP_WIDTH // RWKV_HEAD
DECAY_RANK = 32
AAA_RANK = 32
GATE_RANK = 64
POOL_WINDOWS = (2, 4, 8, 16)
POOL_GROUPS = len(POOL_WINDOWS)
POOL_CH = GROUP_WIDTH // POOL_GROUPS
POOL_HIST = max(POOL_WINDOWS) - 1
HEAD_DIM = 64
N_Q_HEADS = GROUP_WIDTH // HEAD_DIM
N_KV_HEADS = 2
GQA_GROUP = N_Q_HEADS // N_KV_HEADS
WINDOW = 128
WIN_CHUNKS = WINDOW // CHUNK
D_FF = -(-(8 * D_MODEL) // (3 * 256)) * 256

A_COLS = 2 * GROUP_WIDTH
B_COLS = 3 * GROUP_WIDTH + DECAY_RANK + AAA_RANK + GATE_RANK
C_COLS = GROUP_WIDTH
D_COLS = (N_Q_HEADS + 2 * N_KV_HEADS) * HEAD_DIM
IN_COLS = A_COLS + B_COLS + C_COLS + D_COLS
IN_SPLITS = (A_COLS, A_COLS + B_COLS, A_COLS + B_COLS + C_COLS)
B_SPLITS = (GROUP_WIDTH, 2 * GROUP_WIDTH, 3 * GROUP_WIDTH,
            3 * GROUP_WIDTH + DECAY_RANK, 3 * GROUP_WIDTH + DECAY_RANK + AAA_RANK)
D_SPLITS = (N_Q_HEADS * HEAD_DIM, (N_Q_HEADS + N_KV_HEADS) * HEAD_DIM)
RMS_EPS = 1e-6
LN_EPS = 1e-5
GN_EPS = 64e-5
ATTN_SCALE = HEAD_DIM ** -0.5
NEG_INF = -1e30

kernel_name = 'hybrid_streaming_encoder_step'


def rms_norm(x, g):
    x32 = x.astype(jnp.float32)
    y = x32 * lax.rsqrt(jnp.mean(x32 * x32, axis=-1, keepdims=True) + RMS_EPS)
    return (y * g.astype(jnp.float32)).astype(x.dtype)


def layer_norm(x, g, b):
    x32 = x.astype(jnp.float32)
    mu = jnp.mean(x32, axis=-1, keepdims=True)
    var = jnp.mean(jnp.square(x32 - mu), axis=-1, keepdims=True)
    y = (x32 - mu) * lax.rsqrt(var + LN_EPS) * g.astype(jnp.float32) + b.astype(jnp.float32)
    return y.astype(x.dtype)


def conv_module(u, hist, conv_w, conv_b, ln_g, ln_b):
    val, gate = jnp.split(u, 2, axis=-1)
    glu = val * jax.nn.sigmoid(gate)
    ext = jnp.concatenate([hist.astype(glu.dtype), glu], axis=1)
    y = lax.conv_general_dilated(
        ext, conv_w[:, None, :].astype(ext.dtype), window_strides=(1,), padding='VALID',
        dimension_numbers=('NWC', 'WIO', 'NWC'), feature_group_count=GROUP_WIDTH)
    y = jax.nn.silu(layer_norm(y + conv_b, ln_g, ln_b))
    return y, ext[:, -CONV_HIST:]


def wkv7_step(S, inp):
    r_t, w_t, k_t, v_t, a_t, b_t = inp
    sa = jnp.einsum('bhvk,bhk->bhv', S, a_t)
    S = S * w_t[:, :, None, :] + sa[..., None] * b_t[:, :, None, :] + v_t[..., None] * k_t[:, :, None, :]
    return S, jnp.einsum('bhvk,bhk->bhv', S, r_t)


def rwkv7_mixer(u, shift_prev, wkv_state, mu, w0, w2, a0, a2, g2, k_k, k_a, r_k, gn_g, gn_b):
    bsz, seq_len, _ = u.shape
    f32 = jnp.float32
    prev = jnp.concatenate([shift_prev[:, None, :].astype(u.dtype), u[:, :-1]], axis=1)
    xs = u + mu * (prev - u)
    r, k, v, lat_w, lat_a, lat_g = jnp.split(xs, B_SPLITS, axis=-1)
    w_log = -jax.nn.softplus(-(w0 + jnp.tanh(lat_w) @ w2).astype(f32)) - 0.5
    decay = jnp.exp(-jnp.exp(w_log))
    a_rate = jax.nn.sigmoid((a0 + lat_a @ a2).astype(f32))
    gate = jax.nn.sigmoid(lat_g) @ g2

    def heads(t):
        return t.astype(f32).reshape(bsz, seq_len, RWKV_HEADS, RWKV_HEAD)

    kk = heads(k * k_k)
    kk = kk / jnp.maximum(jnp.sqrt(jnp.sum(kk * kk, axis=-1, keepdims=True)), 1e-12)
    k_mod = heads(k.astype(f32) * (1.0 + (a_rate - 1.0) * k_a.astype(f32)))
    rh, vh, ah, wh = heads(r), heads(v), heads(a_rate), heads(decay)
    scan_in = tuple(jnp.moveaxis(t, 1, 0) for t in (rh, wh, k_mod, vh, -kk, kk * ah))
    s_final, y = lax.scan(wkv7_step, wkv_state.astype(f32), scan_in)
    y = jnp.moveaxis(y, 0, 1)
    m = jnp.mean(y, axis=-1, keepdims=True)
    var = jnp.mean(jnp.square(y - m), axis=-1, keepdims=True)
    y = ((y - m) * lax.rsqrt(var + GN_EPS)).reshape(bsz, seq_len, GROUP_WIDTH)
    y = y * gn_g.astype(f32) + gn_b.astype(f32)
    bonus = jnp.sum(rh * k_mod * r_k.astype(f32), axis=-1, keepdims=True) * vh
    y = (y + bonus.reshape(bsz, seq_len, GROUP_WIDTH)) * gate.astype(f32)
    return y.astype(u.dtype), s_final.astype(u.dtype), u[:, -1]


def pool_mixer(u, hist, pos0, pool_w, pool_scale):
    bsz, seq_len, _ = u.shape
    ext = jnp.concatenate([hist.astype(u.dtype), u], axis=1)
    cs = jnp.pad(jnp.cumsum(ext.astype(jnp.float32), axis=1), ((0, 0), (1, 0), (0, 0)))
    pos = pos0 + jnp.arange(seq_len)
    means = []
    for gi, w in enumerate(POOL_WINDOWS):
        sl = slice(gi * POOL_CH, (gi + 1) * POOL_CH)
        end = cs[:, POOL_HIST + 1:POOL_HIST + 1 + seq_len, sl]
        start = cs[:, POOL_HIST + 1 - w:POOL_HIST + 1 - w + seq_len, sl]
        cnt = jnp.minimum(w, pos + 1).astype(jnp.float32)[None, :, None]
        means.append((end - start) / cnt)
    d = (jnp.concatenate(means, axis=-1) - u.astype(jnp.float32)).reshape(bsz, seq_len, POOL_GROUPS, POOL_CH)
    y = jnp.einsum('blgc,gcd->blgd', d, pool_w.astype(jnp.float32)).reshape(bsz, seq_len, GROUP_WIDTH)
    return (y * pool_scale.astype(jnp.float32)).astype(u.dtype), ext[:, -POOL_HIST:]


def sink_softmax(s, sinks):
    sk = sinks.astype(jnp.float32).reshape(N_KV_HEADS, GQA_GROUP)[:, :, None, None]
    m = jnp.maximum(jnp.max(s, axis=-1, keepdims=True), sk)
    p = jnp.exp(s - m)
    return p / (jnp.sum(p, axis=-1, keepdims=True) + jnp.exp(sk - m))


def attention_qkv(u, q_norm_g, k_norm_g):
    bsz, seq_len, _ = u.shape
    q, k, v = jnp.split(u, D_SPLITS, axis=-1)
    q = rms_norm(q.reshape(bsz, seq_len, N_Q_HEADS, HEAD_DIM), q_norm_g)
    k = rms_norm(k.reshape(bsz, seq_len, N_KV_HEADS, HEAD_DIM), k_norm_g)
    v = v.reshape(bsz, seq_len, N_KV_HEADS, HEAD_DIM)
    return q, k, v


def window_attention_prompt(q, k, v, sinks):
    bsz, seq_len = q.shape[0], q.shape[1]
    n_chunks = seq_len // CHUNK
    qc = q.reshape(bsz, n_chunks, CHUNK, N_KV_HEADS, GQA_GROUP, HEAD_DIM)

    def band(t):
        tc = t.reshape(bsz, n_chunks, CHUNK, N_KV_HEADS, HEAD_DIM)
        tp = jnp.pad(tc, ((0, 0), (WIN_CHUNKS, 0), (0, 0), (0, 0), (0, 0)))
        return jnp.concatenate([tp[:, j:j + n_chunks] for j in range(WIN_CHUNKS + 1)], axis=2)

    kb, vb = band(k), band(v)
    key_chunk = jnp.arange(n_chunks)[:, None] + jnp.arange(WIN_CHUNKS + 1)[None, :] - WIN_CHUNKS
    valid = jnp.repeat(key_chunk >= 0, CHUNK, axis=1)
    s = jnp.einsum('bnqhgd,bnkhd->bnhgqk', qc, kb).astype(jnp.float32) * ATTN_SCALE
    s = jnp.where(valid[None, :, None, None, None, :], s, NEG_INF)
    p = sink_softmax(s, sinks).astype(v.dtype)
    o = jnp.einsum('bnhgqk,bnkhd->bnqhgd', p, vb)
    return o.reshape(bsz, seq_len, N_Q_HEADS * HEAD_DIM)


def window_attention_sample(q, k, v, k_cache, v_cache, sinks):
    bsz, t = q.shape[0], q.shape[1]
    kf = jnp.concatenate([k_cache.astype(k.dtype), k], axis=1)
    vf = jnp.concatenate([v_cache.astype(v.dtype), v], axis=1)
    qg = q.reshape(bsz, t, N_KV_HEADS, GQA_GROUP, HEAD_DIM)
    s = jnp.einsum('bqhgd,bkhd->bhgqk', qg, kf).astype(jnp.float32) * ATTN_SCALE
    p = sink_softmax(s, sinks).astype(v.dtype)
    o = jnp.einsum('bhgqk,bkhd->bqhgd', p, vf).reshape(bsz, t, N_Q_HEADS * HEAD_DIM)
    return o, kf[:, -WINDOW:], vf[:, -WINDOW:]


def run_trunk(x, conv_hist, rwkv_state, shift_prev, pool_hist, k_cache, v_cache, pos0, w):
    bsz = x.shape[0]
    new_conv, new_rwkv, new_shift, new_pool, new_k, new_v = [], [], [], [], [], []
    for l in range(DEPTH):
        if conv_hist is None:
            ch = jnp.zeros((bsz, CONV_HIST, GROUP_WIDTH), x.dtype)
            rs = jnp.zeros((bsz, RWKV_HEADS, RWKV_HEAD, RWKV_HEAD), x.dtype)
            sp = jnp.zeros((bsz, B_COLS), x.dtype)
            ph = jnp.zeros((bsz, POOL_HIST, GROUP_WIDTH), x.dtype)
        else:
            ch, rs, sp, ph = conv_hist[l], rwkv_state[l], shift_prev[l], pool_hist[l]
        xn = rms_norm(x, w['norm_mix_g'][l])
        proj = xn @ w['w_in'][l]
        u_a, u_b, u_c, u_d = jnp.split(proj, IN_SPLITS, axis=-1)
        y_a, ch_new = conv_module(u_a, ch, w['conv_w'][l], w['conv_b'][l], w['conv_ln_g'][l], w['conv_ln_b'][l])
        y_b, rs_new, sp_new = rwkv7_mixer(
            u_b, sp, rs, w['rwkv_mu'][l], w['rwkv_w0'][l], w['rwkv_w2'][l], w['rwkv_a0'][l], w['rwkv_a2'][l],
            w['rwkv_g2'][l], w['rwkv_k_k'][l], w['rwkv_k_a'][l], w['rwkv_r_k'][l], w['rwkv_gn_g'][l], w['rwkv_gn_b'][l])
        y_c, ph_new = pool_mixer(u_c, ph, pos0, w['pool_w'][l], w['pool_scale'][l])
        q, k, v = attention_qkv(u_d, w['attn_q_norm'][l], w['attn_k_norm'][l])
        if k_cache is None:
            y_d = window_attention_prompt(q, k, v, w['attn_sinks'][l])
            k_new, v_new = k[:, -WINDOW:], v[:, -WINDOW:]
        else:
            y_d, k_new, v_new = window_attention_sample(q, k, v, k_cache[l], v_cache[l], w['attn_sinks'][l])
        x = x + jnp.concatenate([y_a, y_b, y_c, y_d], axis=-1) @ w['w_out'][l]
        hn = rms_norm(x, w['norm_ffn_g'][l])
        x = x + (jax.nn.silu(hn @ w['ffn_w_gate'][l]) * (hn @ w['ffn_w_up'][l])) @ w['ffn_w_down'][l]
        new_conv.append(ch_new)
        new_rwkv.append(rs_new)
        new_shift.append(sp_new)
        new_pool.append(ph_new)
        new_k.append(k_new)
        new_v.append(v_new)
    return x, (jnp.stack(new_conv), jnp.stack(new_rwkv), jnp.stack(new_shift),
               jnp.stack(new_pool), jnp.stack(new_k), jnp.stack(new_v))


def setup_inputs(seed: int = 0) -> dict:
    key = jax.random.key(seed)
    ks = jax.random.split(key, 40)
    f32 = jnp.float32

    def nrm(i, shape, scale=1.0, shift=0.0):
        return shift + scale * jax.random.normal(ks[i], shape, f32)

    return {
        'x_prompt': nrm(0, (BATCH, SEQ, D_MODEL)),
        'x_sample': nrm(1, (DEC_BATCH, DEC_SEQ, D_MODEL)),
        'cache_conv': nrm(2, (DEPTH, DEC_BATCH, CONV_HIST, GROUP_WIDTH), 0.5),
        'state_rwkv': nrm(3, (DEPTH, DEC_BATCH, RWKV_HEADS, RWKV_HEAD, RWKV_HEAD), 0.5),
        'state_rwkv_shift': nrm(4, (DEPTH, DEC_BATCH, B_COLS)),
        'cache_pool': nrm(5, (DEPTH, DEC_BATCH, POOL_HIST, GROUP_WIDTH)),
        'cache_k': nrm(6, (DEPTH, DEC_BATCH, WINDOW, N_KV_HEADS, HEAD_DIM)),
        'cache_v': nrm(7, (DEPTH, DEC_BATCH, WINDOW, N_KV_HEADS, HEAD_DIM)),
        'norm_mix_g': nrm(8, (DEPTH, D_MODEL), 0.1, 1.0),
        'w_in': nrm(9, (DEPTH, D_MODEL, IN_COLS), D_MODEL ** -0.5),
        'conv_w': nrm(10, (DEPTH, CONV_WIDTH, GROUP_WIDTH), CONV_WIDTH ** -0.5),
        'conv_b': nrm(11, (DEPTH, GROUP_WIDTH), 0.02),
        'conv_ln_g': nrm(12, (DEPTH, GROUP_WIDTH), 0.1, 1.0),
        'conv_ln_b': nrm(13, (DEPTH, GROUP_WIDTH), 0.02),
        'rwkv_mu': jax.random.uniform(ks[14], (DEPTH, B_COLS), f32),
        'rwkv_w0': nrm(15, (DEPTH, GROUP_WIDTH), 0.5, -1.0),
        'rwkv_w2': nrm(16, (DEPTH, DECAY_RANK, GROUP_WIDTH), 0.5 * DECAY_RANK ** -0.5),
        'rwkv_a0': nrm(17, (DEPTH, GROUP_WIDTH), 0.5),
        'rwkv_a2': nrm(18, (DEPTH, AAA_RANK, GROUP_WIDTH), 0.5 * AAA_RANK ** -0.5),
        'rwkv_g2': nrm(19, (DEPTH, GATE_RANK, GROUP_WIDTH), GATE_RANK ** -0.5),
        'rwkv_k_k': nrm(20, (DEPTH, GROUP_WIDTH), 0.1, 1.0),
        'rwkv_k_a': nrm(21, (DEPTH, GROUP_WIDTH), 0.1, 1.0),
        'rwkv_r_k': nrm(22, (DEPTH, RWKV_HEADS, RWKV_HEAD), 0.1),
        'rwkv_gn_g': nrm(23, (DEPTH, GROUP_WIDTH), 0.1, 1.0),
        'rwkv_gn_b': nrm(24, (DEPTH, GROUP_WIDTH), 0.02),
        'pool_w': nrm(25, (DEPTH, POOL_GROUPS, POOL_CH, POOL_CH), POOL_CH ** -0.5),
        'pool_scale': nrm(26, (DEPTH, GROUP_WIDTH), 0.1, 1.0),
        'attn_q_norm': nrm(27, (DEPTH, HEAD_DIM), 0.1, 1.0),
        'attn_k_norm': nrm(28, (DEPTH, HEAD_DIM), 0.1, 1.0),
        'attn_sinks': nrm(29, (DEPTH, N_Q_HEADS), 0.5),
        'w_out': nrm(30, (DEPTH, MIX_WIDTH, D_MODEL), MIX_WIDTH ** -0.5),
        'norm_ffn_g': nrm(31, (DEPTH, D_MODEL), 0.1, 1.0),
        'ffn_w_gate': nrm(32, (DEPTH, D_MODEL, D_FF), D_MODEL ** -0.5),
        'ffn_w_up': nrm(33, (DEPTH, D_MODEL, D_FF), D_MODEL ** -0.5),
        'ffn_w_down': nrm(34, (DEPTH, D_FF, D_MODEL), D_FF ** -0.5),
    }


def reference(x_prompt, x_sample, cache_conv, state_rwkv, state_rwkv_shift, cache_pool, cache_k, cache_v,
              norm_mix_g, w_in, conv_w, conv_b, conv_ln_g, conv_ln_b, rwkv_mu, rwkv_w0, rwkv_w2, rwkv_a0,
              rwkv_a2, rwkv_g2, rwkv_k_k, rwkv_k_a, rwkv_r_k, rwkv_gn_g, rwkv_gn_b, pool_w, pool_scale,
              attn_q_norm, attn_k_norm, attn_sinks, w_out, norm_ffn_g, ffn_w_gate, ffn_w_up, ffn_w_down):
    w = {
        'norm_mix_g': norm_mix_g, 'w_in': w_in, 'conv_w': conv_w, 'conv_b': conv_b,
        'conv_ln_g': conv_ln_g, 'conv_ln_b': conv_ln_b, 'rwkv_mu': rwkv_mu, 'rwkv_w0': rwkv_w0,
        'rwkv_w2': rwkv_w2, 'rwkv_a0': rwkv_a0, 'rwkv_a2': rwkv_a2, 'rwkv_g2': rwkv_g2,
        'rwkv_k_k': rwkv_k_k, 'rwkv_k_a': rwkv_k_a, 'rwkv_r_k': rwkv_r_k, 'rwkv_gn_g': rwkv_gn_g,
        'rwkv_gn_b': rwkv_gn_b, 'pool_w': pool_w, 'pool_scale': pool_scale, 'attn_q_norm': attn_q_norm,
        'attn_k_norm': attn_k_norm, 'attn_sinks': attn_sinks, 'w_out': w_out, 'norm_ffn_g': norm_ffn_g,
        'ffn_w_gate': ffn_w_gate, 'ffn_w_up': ffn_w_up, 'ffn_w_down': ffn_w_down,
    }
    y_prompt, (conv_p, rwkv_p, shift_p, pool_p, k_p, v_p) = run_trunk(
        x_prompt, None, None, None, None, None, None, 0, w)
    y_sample, (conv_s, rwkv_s, shift_s, pool_s, k_s, v_s) = run_trunk(
        x_sample, cache_conv, state_rwkv, state_rwkv_shift, cache_pool, cache_k, cache_v, PAST_LEN, w)
    return (y_prompt, y_sample, conv_p, conv_s, rwkv_p, rwkv_s, shift_p, shift_s,
            pool_p, pool_s, k_p, k_s, v_p, v_s)
```

```python
import functools
import math

import jax
import jax.numpy as jnp
from jax import lax
from jax.experimental import pallas as pl
from jax.experimental.pallas import tpu as pltpu

F32 = jnp.float32
BF16 = jnp.bfloat16

D_MODEL = 1024
DEPTH = 2
PAST_LEN = 1024
CHUNK = 64
GROUP_WIDTH = 256
CONV_WIDTH = 31
CONV_HIST = CONV_WIDTH - 1
CONV_PAD = 32
RWKV_HEAD = 64
RWKV_HEADS = 4
POOL_WINDOWS = (2, 4, 8, 16)
POOL_HIST = 15
POOL_PAD = 16
HEAD_DIM = 64
N_Q_HEADS = 4
N_KV_HEADS = 2
WINDOW = 128
D_FF = 2816
A_COLS = 512
B_COLS = 896
C_COLS = 256
D_COLS = 512
IN_COLS = A_COLS + B_COLS + C_COLS + D_COLS
RMS_EPS = 1e-6
LN_EPS = 1e-5
GN_EPS = 64e-5
ATTN_SCALE = HEAD_DIM ** -0.5
NEG_INF = -1e30

VMEM_LIMIT_BYTES = 56 * 1024 * 1024


def _dot(a, b):
    return jnp.dot(a, b, preferred_element_type=F32)


def _dot_nt(a, b):
    return lax.dot_general(a, b, (((1,), (1,)), ((), ())), preferred_element_type=F32)


def _dot_tn(a, b):
    return lax.dot_general(a, b, (((0,), (0,)), ((), ())), preferred_element_type=F32)


def _sigmoid(x):
    return 1.0 / (1.0 + jnp.exp(-x))


def _split_dot_right(x, m_bf16, terms):
    acc = None
    rem = x
    for i in range(terms):
        hi = rem.astype(BF16)
        d = _dot(hi, m_bf16)
        acc = d if acc is None else acc + d
        if i + 1 < terms:
            rem = rem - hi.astype(F32)
    return acc


def _split_dot_left(m_bf16, x, terms):
    acc = None
    rem = x
    for i in range(terms):
        hi = rem.astype(BF16)
        d = _dot(m_bf16, hi)
        acc = d if acc is None else acc + d
        if i + 1 < terms:
            rem = rem - hi.astype(F32)
    return acc


def _block_matrix(n, blk, value):
    sh = int(math.log2(blk))
    r = lax.shift_right_logical(lax.broadcasted_iota(jnp.int32, (n, n), 0), sh)
    c = lax.shift_right_logical(lax.broadcasted_iota(jnp.int32, (n, n), 1), sh)
    return jnp.where(r == c, value, 0.0).astype(BF16)


def _params(sem):
    return pltpu.CompilerParams(dimension_semantics=sem, vmem_limit_bytes=VMEM_LIMIT_BYTES)


def _proj_body(x_ref, g_ref, w_ref, ua_ref, ub_ref, uc_ref, ud_ref):
    x = x_ref[...]
    ms = jnp.mean(x * x, axis=-1, keepdims=True)
    xn = ((x * lax.rsqrt(ms + RMS_EPS)) * g_ref[...]).astype(BF16)
    c0 = 0
    for ref, width in ((ua_ref, A_COLS), (ub_ref, B_COLS), (uc_ref, C_COLS), (ud_ref, D_COLS)):
        ref[...] = _dot(xn, w_ref[:, c0:c0 + width])
        c0 += width


def _proj(x2d, g, w_bf16):
    t = x2d.shape[0]
    tm = min(t, 512)
    assert t % tm == 0
    widths = (A_COLS, B_COLS, C_COLS, D_COLS)
    return pl.pallas_call(
        _proj_body,
        grid=(t // tm,),
        in_specs=[
            pl.BlockSpec((tm, D_MODEL), lambda i: (i, 0)),
            pl.BlockSpec((1, D_MODEL), lambda i: (0, 0)),
            pl.BlockSpec((D_MODEL, IN_COLS), lambda i: (0, 0)),
        ],
        out_specs=[pl.BlockSpec((tm, w), lambda i: (i, 0)) for w in widths],
        out_shape=[jax.ShapeDtypeStruct((t, w), F32) for w in widths],
        compiler_params=_params(("parallel",)),
        name="proj",
    )(x2d, g, w_bf16)


def _conv_body(u_ref, hist_ref, w_ref, vec_ref, y_ref, tail_ref, ext_ref, *, tl, rb):
    li = pl.program_id(1)

    @pl.when(li == 0)
    def _():
        ext_ref[0:CONV_PAD, :] = hist_ref[0]

    u = u_ref[0]
    glu = u[:, :GROUP_WIDTH] * _sigmoid(u[:, GROUP_WIDTH:])
    ext_ref[CONV_PAD:CONV_PAD + tl, :] = glu
    conv_b = vec_ref[0:1, :]
    ln_g = vec_ref[1:2, :]
    ln_b = vec_ref[2:3, :]
    off = CONV_PAD - CONV_HIST
    for r0 in range(0, tl, rb):
        acc = jnp.zeros((rb, GROUP_WIDTH), F32)
        for j in range(CONV_WIDTH):
            acc = acc + w_ref[j:j + 1, :] * ext_ref[r0 + off + j:r0 + off + j + rb, :]
        acc = acc + conv_b
        mu = jnp.mean(acc, axis=-1, keepdims=True)
        d = acc - mu
        var = jnp.mean(d * d, axis=-1, keepdims=True)
        yn = d * lax.rsqrt(var + LN_EPS) * ln_g + ln_b
        y_ref[0, r0:r0 + rb, :] = yn * _sigmoid(yn)
    tail = ext_ref[tl:tl + CONV_PAD, :]
    tail_ref[0] = tail
    ext_ref[0:CONV_PAD, :] = tail


def _conv(ua, hist, conv_w, vec):
    b, l, _ = ua.shape
    tl = min(l, 512)
    rb = min(tl, 64)
    assert l % tl == 0 and tl % rb == 0 and tl >= CONV_PAD
    return pl.pallas_call(
        functools.partial(_conv_body, tl=tl, rb=rb),
        grid=(b, l // tl),
        in_specs=[
            pl.BlockSpec((1, tl, A_COLS), lambda i, j: (i, j, 0)),
            pl.BlockSpec((1, CONV_PAD, GROUP_WIDTH), lambda i, j: (i, 0, 0)),
            pl.BlockSpec((CONV_PAD, GROUP_WIDTH), lambda i, j: (0, 0)),
            pl.BlockSpec((8, GROUP_WIDTH), lambda i, j: (0, 0)),
        ],
        out_specs=[
            pl.BlockSpec((1, tl, GROUP_WIDTH), lambda i, j: (i, j, 0)),
            pl.BlockSpec((1, CONV_PAD, GROUP_WIDTH), lambda i, j: (i, 0, 0)),
        ],
        out_shape=[
            jax.ShapeDtypeStruct((b, l, GROUP_WIDTH), F32),
            jax.ShapeDtypeStruct((b, CONV_PAD, GROUP_WIDTH), F32),
        ],
        scratch_shapes=[pltpu.VMEM((CONV_PAD + tl, GROUP_WIDTH), F32)],
        compiler_params=_params(("parallel", "arbitrary")),
        name="conv",
    )(ua, hist, conv_w, vec)


def _pool_body(u_ref, hist_ref, wbd_ref, sc_ref, y_ref, ext_ref, *, tl, rb, pos0):
    li = pl.program_id(1)

    @pl.when(li == 0)
    def _():
        ext_ref[0:POOL_PAD, :] = hist_ref[0]

    ext_ref[POOL_PAD:POOL_PAD + tl, :] = u_ref[0]
    lane = lax.broadcasted_iota(jnp.int32, (1, GROUP_WIDTH), 1)
    for r0 in range(0, tl, rb):
        base = POOL_PAD + r0
        sums = []
        acc = None
        for i in range(max(POOL_WINDOWS)):
            sh = ext_ref[base - i:base - i + rb, :]
            acc = sh if acc is None else acc + sh
            if i + 1 in POOL_WINDOWS:
                sums.append(acc)
        pos = pos0 + li * tl + r0 + lax.broadcasted_iota(jnp.int32, (rb, 1), 0)
        means = [s / jnp.minimum(w, pos + 1).astype(F32) for s, w in zip(sums, POOL_WINDOWS)]
        mean = jnp.where(lane < 64, means[0],
                         jnp.where(lane < 128, means[1], jnp.where(lane < 192, means[2], means[3])))
        d = mean - ext_ref[base:base + rb, :]
        y_ref[0, r0:r0 + rb, :] = _dot(d.astype(BF16), wbd_ref[...]) * sc_ref[...]
    ext_ref[0:POOL_PAD, :] = ext_ref[tl:tl + POOL_PAD, :]


def _pool(uc, hist, wbd, scale, pos0):
    b, l, _ = uc.shape
    tl = min(l, 512)
    rb = min(tl, 128)
    assert l % tl == 0 and tl % rb == 0 and tl >= POOL_PAD
    return pl.pallas_call(
        functools.partial(_pool_body, tl=tl, rb=rb, pos0=pos0),
        grid=(b, l // tl),
        in_specs=[
            pl.BlockSpec((1, tl, GROUP_WIDTH), lambda i, j: (i, j, 0)),
            pl.BlockSpec((1, POOL_PAD, GROUP_WIDTH), lambda i, j: (i, 0, 0)),
            pl.BlockSpec((GROUP_WIDTH, GROUP_WIDTH), lambda i, j: (0, 0)),
            pl.BlockSpec((1, GROUP_WIDTH), lambda i, j: (0, 0)),
        ],
        out_specs=pl.BlockSpec((1, tl, GROUP_WIDTH), lambda i, j: (i, j, 0)),
        out_shape=jax.ShapeDtypeStruct((b, l, GROUP_WIDTH), F32),
        scratch_shapes=[pltpu.VMEM((POOL_PAD + tl, GROUP_WIDTH), F32)],
        compiler_params=_params(("parallel", "arbitrary")),
        name="pool",
    )(uc, hist, wbd, scale)


def _attn_body(sink_ref, u_ref, hk_ref, hv_ref, qg_ref, kg_ref, y_ref, kt_ref, vt_ref,
               kext_ref, vext_ref, *, tq, chunk, hist_valid):
    li = pl.program_id(1)

    @pl.when(li == 0)
    def _():
        kext_ref[0:WINDOW, :] = hk_ref[0]
        vext_ref[0:WINDOW, :] = hv_ref[0]

    u = u_ref[0]
    q = u[:, 0:256]
    k = u[:, 256:384]
    v = u[:, 384:512]
    inv = 1.0 / HEAD_DIM
    qms = _split_dot_right(q * q, _block_matrix(256, HEAD_DIM, inv), 2)
    kms = _split_dot_right(k * k, _block_matrix(128, HEAD_DIM, inv), 2)
    qn = ((q * lax.rsqrt(qms + RMS_EPS)) * qg_ref[...]).astype(BF16)
    kn = (k * lax.rsqrt(kms + RMS_EPS)) * kg_ref[...]
    kext_ref[WINDOW:WINDOW + tq, :] = kn
    vext_ref[WINDOW:WINDOW + tq, :] = v
    kw = WINDOW + chunk
    kpos = lax.broadcasted_iota(jnp.int32, (1, kw), 1)
    for c in range(tq // chunk):
        r0 = c * chunk
        keys = kext_ref[r0:r0 + kw, :].astype(BF16)
        vals = vext_ref[r0:r0 + kw, :].astype(BF16)
        outs = []
        for h in range(N_Q_HEADS):
            g = h // (N_Q_HEADS // N_KV_HEADS)
            qh = qn[r0:r0 + chunk, h * HEAD_DIM:(h + 1) * HEAD_DIM]
            kh = keys[:, g * HEAD_DIM:(g + 1) * HEAD_DIM]
            vh = vals[:, g * HEAD_DIM:(g + 1) * HEAD_DIM]
            s = _dot_nt(qh, kh) * ATTN_SCALE
            if not hist_valid and r0 < WINDOW:
                valid = jnp.logical_or(kpos >= WINDOW - r0, li > 0)
                s = jnp.where(valid, s, NEG_INF)
            sk = sink_ref[h]
            m = jnp.maximum(jnp.max(s, axis=-1, keepdims=True), sk)
            p = jnp.exp(s - m)
            den = jnp.sum(p, axis=-1, keepdims=True) + jnp.exp(sk - m)
            outs.append(_dot(p.astype(BF16), vh) / den)
        y_ref[0, r0:r0 + chunk, :] = jnp.concatenate(outs, axis=-1)
    ktail = kext_ref[tq:tq + WINDOW, :]
    vtail = vext_ref[tq:tq + WINDOW, :]
    kt_ref[0] = ktail
    vt_ref[0] = vtail
    kext_ref[0:WINDOW, :] = ktail
    vext_ref[0:WINDOW, :] = vtail


def _attn(ud, hk, hv, qg, kg, sinks, chunk, hist_valid):
    b, l, _ = ud.shape
    tq = min(l, 256)
    assert l % tq == 0 and tq % chunk == 0
    kvw = N_KV_HEADS * HEAD_DIM
    return pl.pallas_call(
        functools.partial(_attn_body, tq=tq, chunk=chunk, hist_valid=hist_valid),
        grid=(b, l // tq),
        in_specs=[
            pl.BlockSpec(memory_space=pltpu.SMEM),
            pl.BlockSpec((1, tq, D_COLS), lambda i, j: (i, j, 0)),
            pl.BlockSpec((1, WINDOW, kvw), lambda i, j: (i, 0, 0)),
            pl.BlockSpec((1, WINDOW, kvw), lambda i, j: (i, 0, 0)),
            pl.BlockSpec((1, GROUP_WIDTH), lambda i, j: (0, 0)),
            pl.BlockSpec((1, kvw), lambda i, j: (0, 0)),
        ],
        out_specs=[
            pl.BlockSpec((1, tq, GROUP_WIDTH), lambda i, j: (i, j, 0)),
            pl.BlockSpec((1, WINDOW, kvw), lambda i, j: (i, 0, 0)),
            pl.BlockSpec((1, WINDOW, kvw), lambda i, j: (i, 0, 0)),
        ],
        out_shape=[
            jax.ShapeDtypeStruct((b, l, GROUP_WIDTH), F32),
            jax.ShapeDtypeStruct((b, WINDOW, kvw), F32),
            jax.ShapeDtypeStruct((b, WINDOW, kvw), F32),
        ],
        scratch_shapes=[pltpu.VMEM((WINDOW + tq, kvw), F32), pltpu.VMEM((WINDOW + tq, kvw), F32)],
        compiler_params=_params(("parallel", "arbitrary")),
        name="attn",
    )(sinks, ud, hk, hv, qg, kg)


def _rwkv_body(u_ref, sp_ref, s0_ref, mu_ref, wl_ref, vec_ref, y_ref, sn_ref,
               prev_ref, s_ref, a_s, r_s, bt_s, kt_s, bh_s, kh_s, v_s, yacc_ref, *, tb, chunk):
    li = pl.program_id(1)
    gw = GROUP_WIDTH
    hd = RWKV_HEAD

    @pl.when(li == 0)
    def _():
        prev_ref[...] = sp_ref[0]
        s_ref[...] = s0_ref[0]

    u = u_ref[0]
    row = lax.broadcasted_iota(jnp.int32, (tb, 1), 0)
    prev = jnp.where(row == 0, prev_ref[...], pltpu.roll(u, 1, axis=0))
    prev_ref[...] = u[tb - 1:tb, :]
    xs = u + mu_ref[...] * (prev - u)
    r = xs[:, 0:gw]
    k = xs[:, gw:2 * gw]
    v = xs[:, 2 * gw:3 * gw]
    lat = xs[:, 3 * gw:B_COLS]
    lane = lax.broadcasted_iota(jnp.int32, (1, B_COLS - 3 * gw), 1)
    act = jnp.where(lane < 32, jnp.tanh(lat), jnp.where(lane < 64, lat, _sigmoid(lat)))
    lo = _dot(act.astype(BF16), wl_ref[...])
    w0 = vec_ref[0:1, :]
    a0 = vec_ref[1:2, :]
    k_k = vec_ref[2:3, :]
    k_a = vec_ref[3:4, :]
    r_k = vec_ref[4:5, :]
    gn_g = vec_ref[5:6, :]
    gn_b = vec_ref[6:7, :]
    z = -(w0 + lo[:, 0:gw])
    softplus = jnp.maximum(z, 0.0) + jnp.log(1.0 + jnp.exp(-jnp.abs(z)))
    logw = -jnp.exp(-softplus - 0.5)
    a_rate = _sigmoid(a0 + lo[:, gw:2 * gw])
    gate = lo[:, 2 * gw:3 * gw]
    ones_blk = _block_matrix(gw, hd, 1.0)
    kk = k * k_k
    kk = kk / jnp.maximum(jnp.sqrt(_split_dot_right(kk * kk, ones_blk, 2)), 1e-12)
    k_mod = k * (1.0 + (a_rate - 1.0) * k_a)
    a_v = -kk
    b_v = kk * a_rate
    bonus = _split_dot_right(r * k_mod * r_k, ones_blk, 2) * v

    nc = tb // chunk
    ti = lax.broadcasted_iota(jnp.int32, (tb, tb), 0)
    tj = lax.broadcasted_iota(jnp.int32, (tb, tb), 1)
    csh = int(math.log2(chunk))
    tri = jnp.where(jnp.logical_and(lax.shift_right_logical(ti, csh) == lax.shift_right_logical(tj, csh),
                                    tj <= ti), 1.0, 0.0).astype(BF16)
    cum = _split_dot_left(tri, logw, 3)
    cum3 = cum.reshape(nc, chunk, gw)
    cum_c = cum3[:, chunk - 1:chunk, :]
    e_end = jnp.exp(cum_c - cum3).reshape(tb, gw)
    w_c = jnp.exp(cum_c)
    e_neg = jnp.exp(-cum)
    ops = (
        (a_s, a_v * jnp.exp(cum - logw)),
        (r_s, r * jnp.exp(cum)),
        (bt_s, b_v * e_neg),
        (kt_s, k_mod * e_neg),
        (bh_s, b_v * e_end),
        (kh_s, k_mod * e_end),
        (v_s, v),
    )
    for ref, val in ops:
        vb = val.astype(BF16)
        for h in range(RWKV_HEADS):
            ref[h] = vb[:, h * hd:(h + 1) * hd]

    ri = lax.broadcasted_iota(jnp.int32, (chunk, chunk), 0)
    ci = lax.broadcasted_iota(jnp.int32, (chunk, chunk), 1)
    strict = ci < ri
    incl = ci <= ri
    n_sq = int(math.log2(chunk)) - 1
    for c in range(nc):
        rs = slice(c * chunk, (c + 1) * chunk)
        for h in range(RWKV_HEADS):
            a_c = a_s[h, rs, :]
            r_c = r_s[h, rs, :]
            v_c = v_s[h, rs, :]
            bh_c = bh_s[h, rs, :]
            ar = jnp.concatenate([a_c, r_c], axis=0)
            gb = _dot_nt(ar, bt_s[h, rs, :])
            gk = _dot_nt(ar, kt_s[h, rs, :])
            l_ab = jnp.where(strict, gb[:chunk], 0.0)
            l_ak = jnp.where(strict, gk[:chunk], 0.0)
            m_rb = jnp.where(incl, gb[chunk:], 0.0).astype(BF16)
            m_rk = jnp.where(incl, gk[chunk:], 0.0).astype(BF16)
            t_inv = jnp.where(ci == ri, 1.0, l_ab)
            lp = l_ab
            for _ in range(n_sq):
                lpb = lp.astype(BF16)
                lp = _dot(lpb, lpb)
                t_inv = t_inv + _dot(t_inv.astype(BF16), lp.astype(BF16))
            t_b = t_inv.astype(BF16)
            zv = _dot(l_ak.astype(BF16), v_c)
            ap = _dot(t_b, a_c).astype(BF16)
            uv = _dot(t_b, zv.astype(BF16)).astype(BF16)
            rp = r_c.astype(F32) + _dot(m_rb, ap)
            yv = _dot(m_rb, uv) + _dot(m_rk, v_c)
            s_old = s_ref[h]
            s_b = s_old.astype(BF16)
            y_c = _dot_nt(rp.astype(BF16), s_b) + yv
            mm = _dot_tn(ap, bh_c)
            psi = _dot_tn(uv, bh_c) + _dot_tn(v_c, kh_s[h, rs, :])
            wc_h = w_c[c][:, h * hd:(h + 1) * hd]
            s_ref[h] = s_old * wc_h + _dot(s_b, mm.astype(BF16)) + psi
            yacc_ref[rs, h * hd:(h + 1) * hd] = y_c

    y = yacc_ref[...]
    avg_blk = _block_matrix(gw, hd, 1.0 / hd)
    m = _split_dot_right(y, avg_blk, 2)
    d = y - m
    var = _split_dot_right(d * d, avg_blk, 2)
    yn = d * lax.rsqrt(var + GN_EPS) * gn_g + gn_b
    y_ref[0] = (yn + bonus) * gate
    sn_ref[0] = s_ref[...]


def _rwkv(ub, shift_prev, state, mu, wl, vec, chunk):
    b, l, _ = ub.shape
    tb = min(l, 256)
    assert l % tb == 0 and tb % chunk == 0
    hshape = (RWKV_HEADS, tb, RWKV_HEAD)
    sshape = (RWKV_HEADS, RWKV_HEAD, RWKV_HEAD)
    return pl.pallas_call(
        functools.partial(_rwkv_body, tb=tb, chunk=chunk),
        grid=(b, l // tb),
        in_specs=[
            pl.BlockSpec((1, tb, B_COLS), lambda i, j: (i, j, 0)),
            pl.BlockSpec((1, 1, B_COLS), lambda i, j: (i, 0, 0)),
            pl.BlockSpec((1,) + sshape, lambda i, j: (i, 0, 0, 0)),
            pl.BlockSpec((1, B_COLS), lambda i, j: (0, 0)),
            pl.BlockSpec((128, 3 * GROUP_WIDTH), lambda i, j: (0, 0)),
            pl.BlockSpec((8, GROUP_WIDTH), lambda i, j: (0, 0)),
        ],
        out_specs=[
            pl.BlockSpec((1, tb, GROUP_WIDTH), lambda i, j: (i, j, 0)),
            pl.BlockSpec((1,) + sshape, lambda i, j: (i, 0, 0, 0)),
        ],
        out_shape=[
            jax.ShapeDtypeStruct((b, l, GROUP_WIDTH), F32),
            jax.ShapeDtypeStruct((b,) + sshape, F32),
        ],
        scratch_shapes=[
            pltpu.VMEM((1, B_COLS), F32),
            pltpu.VMEM(sshape, F32),
        ] + [pltpu.VMEM(hshape, BF16) for _ in range(7)] + [pltpu.VMEM((tb, GROUP_WIDTH), F32)],
        compiler_params=_params(("parallel", "arbitrary")),
        name="rwkv",
    )(ub, shift_prev, state, mu, wl, vec)


def _outffn_body(x_ref, ya_ref, yb_ref, yc_ref, yd_ref, wo_ref, g_ref, wg_ref, wu_ref, wd_ref, o_ref):
    ycat = jnp.concatenate([ya_ref[...], yb_ref[...], yc_ref[...], yd_ref[...]], axis=-1).astype(BF16)
    x1 = x_ref[...] + _dot(ycat, wo_ref[...])
    ms = jnp.mean(x1 * x1, axis=-1, keepdims=True)
    hn = ((x1 * lax.rsqrt(ms + RMS_EPS)) * g_ref[...]).astype(BF16)
    hg = _dot(hn, wg_ref[...])
    hu = _dot(hn, wu_ref[...])
    act = (hg * _sigmoid(hg) * hu).astype(BF16)
    o_ref[...] = x1 + _dot(act, wd_ref[...])


def _outffn(x2d, ya, yb, yc, yd, wo, g, wg, wu, wd):
    t = x2d.shape[0]
    tm = min(t, 256)
    assert t % tm == 0
    row = lambda w: pl.BlockSpec((tm, w), lambda i: (i, 0))
    const = lambda shape: pl.BlockSpec(shape, lambda i: (0, 0), pipeline_mode=pl.Buffered(1))
    return pl.pallas_call(
        _outffn_body,
        grid=(t // tm,),
        in_specs=[
            row(D_MODEL), row(GROUP_WIDTH), row(GROUP_WIDTH), row(GROUP_WIDTH), row(GROUP_WIDTH),
            const((4 * GROUP_WIDTH, D_MODEL)),
            const((1, D_MODEL)),
            const((D_MODEL, D_FF)),
            const((D_MODEL, D_FF)),
            const((D_FF, D_MODEL)),
        ],
        out_specs=row(D_MODEL),
        out_shape=jax.ShapeDtypeStruct((t, D_MODEL), F32),
        compiler_params=_params(("parallel",)),
        name="outffn",
    )(x2d, ya, yb, yc, yd, wo, g, wg, wu, wd)


def _pad_rows(a, rows):
    return jnp.pad(a, ((0, rows - a.shape[0]), (0, 0)))


def _layer_weights(l, p):
    gw = GROUP_WIDTH
    wl = jnp.zeros((128, 3 * gw), F32)
    wl = wl.at[0:32, 0:gw].set(p['rwkv_w2'][l])
    wl = wl.at[32:64, gw:2 * gw].set(p['rwkv_a2'][l])
    wl = wl.at[64:128, 2 * gw:3 * gw].set(p['rwkv_g2'][l])
    pool_w = p['pool_w'][l]
    wbd = jnp.zeros((gw, gw), F32)
    pc = gw // len(POOL_WINDOWS)
    for g in range(len(POOL_WINDOWS)):
        wbd = wbd.at[g * pc:(g + 1) * pc, g * pc:(g + 1) * pc].set(pool_w[g])
    return dict(
        norm_mix_g=p['norm_mix_g'][l][None, :],
        w_in=p['w_in'][l].astype(BF16),
        conv_w=_pad_rows(p['conv_w'][l], CONV_PAD),
        conv_vec=_pad_rows(jnp.stack([p['conv_b'][l], p['conv_ln_g'][l], p['conv_ln_b'][l]]), 8),
        rwkv_mu=p['rwkv_mu'][l][None, :],
        rwkv_wl=wl.astype(BF16),
        rwkv_vec=_pad_rows(jnp.stack([
            p['rwkv_w0'][l], p['rwkv_a0'][l], p['rwkv_k_k'][l], p['rwkv_k_a'][l],
            p['rwkv_r_k'][l].reshape(gw), p['rwkv_gn_g'][l], p['rwkv_gn_b'][l]]), 8),
        pool_wbd=wbd.astype(BF16),
        pool_scale=p['pool_scale'][l][None, :],
        attn_qg=jnp.tile(p['attn_q_norm'][l], N_Q_HEADS)[None, :],
        attn_kg=jnp.tile(p['attn_k_norm'][l], N_KV_HEADS)[None, :],
        attn_sinks=p['attn_sinks'][l],
        w_out=p['w_out'][l].astype(BF16),
        norm_ffn_g=p['norm_ffn_g'][l][None, :],
        ffn_w_gate=p['ffn_w_gate'][l].astype(BF16),
        ffn_w_up=p['ffn_w_up'][l].astype(BF16),
        ffn_w_down=p['ffn_w_down'][l].astype(BF16),
    )


def _trunk(x, conv_hist, rwkv_state, shift_prev, pool_hist, k_cache, v_cache, pos0, chunk, weights):
    b, l, _ = x.shape
    has_past = conv_hist is not None
    kvw = N_KV_HEADS * HEAD_DIM
    x2d = x.reshape(b * l, D_MODEL)
    new = [[] for _ in range(6)]
    for li in range(DEPTH):
        w = weights[li]
        if has_past:
            ch = jnp.pad(conv_hist[li], ((0, 0), (CONV_PAD - CONV_HIST, 0), (0, 0)))
            rs = rwkv_state[li]
            sp = shift_prev[li][:, None, :]
            ph = jnp.pad(pool_hist[li], ((0, 0), (POOL_PAD - POOL_HIST, 0), (0, 0)))
            hk = k_cache[li].reshape(b, WINDOW, kvw)
            hv = v_cache[li].reshape(b, WINDOW, kvw)
        else:
            ch = jnp.zeros((b, CONV_PAD, GROUP_WIDTH), F32)
            rs = jnp.zeros((b, RWKV_HEADS, RWKV_HEAD, RWKV_HEAD), F32)
            sp = jnp.zeros((b, 1, B_COLS), F32)
            ph = jnp.zeros((b, POOL_PAD, GROUP_WIDTH), F32)
            hk = jnp.zeros((b, WINDOW, kvw), F32)
            hv = jnp.zeros((b, WINDOW, kvw), F32)
        ua, ub, uc, ud = _proj(x2d, w['norm_mix_g'], w['w_in'])
        ua = ua.reshape(b, l, A_COLS)
        ub = ub.reshape(b, l, B_COLS)
        uc = uc.reshape(b, l, C_COLS)
        ud = ud.reshape(b, l, D_COLS)
        ya, conv_tail = _conv(ua, ch, w['conv_w'], w['conv_vec'])
        yb, s_new = _rwkv(ub, sp, rs, w['rwkv_mu'], w['rwkv_wl'], w['rwkv_vec'], min(CHUNK, l))
        yc = _pool(uc, ph, w['pool_wbd'], w['pool_scale'], pos0)
        yd, k_tail, v_tail = _attn(ud, hk, hv, w['attn_qg'], w['attn_kg'], w['attn_sinks'], chunk, has_past)
        flat = lambda y: y.reshape(b * l, GROUP_WIDTH)
        x2d = _outffn(x2d, flat(ya), flat(yb), flat(yc), flat(yd), w['w_out'], w['norm_ffn_g'],
                      w['ffn_w_gate'], w['ffn_w_up'], w['ffn_w_down'])
        new[0].append(conv_tail[:, CONV_PAD - CONV_HIST:, :])
        new[1].append(s_new)
        new[2].append(ub[:, l - 1, :])
        new[3].append(uc[:, l - POOL_HIST:, :])
        new[4].append(k_tail.reshape(b, WINDOW, N_KV_HEADS, HEAD_DIM))
        new[5].append(v_tail.reshape(b, WINDOW, N_KV_HEADS, HEAD_DIM))
    return x2d.reshape(b, l, D_MODEL), tuple(jnp.stack(n) for n in new)


def kernel(x_prompt, x_sample, cache_conv, state_rwkv, state_rwkv_shift, cache_pool, cache_k, cache_v, norm_mix_g, w_in, conv_w, conv_b, conv_ln_g, conv_ln_b, rwkv_mu, rwkv_w0, rwkv_w2, rwkv_a0, rwkv_a2, rwkv_g2, rwkv_k_k, rwkv_k_a, rwkv_r_k, rwkv_gn_g, rwkv_gn_b, pool_w, pool_scale, attn_q_norm, attn_k_norm, attn_sinks, w_out, norm_ffn_g, ffn_w_gate, ffn_w_up, ffn_w_down):
    p = dict(
        norm_mix_g=norm_mix_g, w_in=w_in, conv_w=conv_w, conv_b=conv_b, conv_ln_g=conv_ln_g,
        conv_ln_b=conv_ln_b, rwkv_mu=rwkv_mu, rwkv_w0=rwkv_w0, rwkv_w2=rwkv_w2, rwkv_a0=rwkv_a0,
        rwkv_a2=rwkv_a2, rwkv_g2=rwkv_g2, rwkv_k_k=rwkv_k_k, rwkv_k_a=rwkv_k_a, rwkv_r_k=rwkv_r_k,
        rwkv_gn_g=rwkv_gn_g, rwkv_gn_b=rwkv_gn_b, pool_w=pool_w, pool_scale=pool_scale,
        attn_q_norm=attn_q_norm, attn_k_norm=attn_k_norm, attn_sinks=attn_sinks, w_out=w_out,
        norm_ffn_g=norm_ffn_g, ffn_w_gate=ffn_w_gate, ffn_w_up=ffn_w_up, ffn_w_down=ffn_w_down)
    weights = [_layer_weights(l, p) for l in range(DEPTH)]
    y_p, (conv_p, rwkv_p, shift_p, pool_p, k_p, v_p) = _trunk(
        x_prompt, None, None, None, None, None, None, 0, CHUNK, weights)
    y_s, (conv_s, rwkv_s, shift_s, pool_s, k_s, v_s) = _trunk(
        x_sample, cache_conv, state_rwkv, state_rwkv_shift, cache_pool, cache_k, cache_v,
        PAST_LEN, x_sample.shape[1], weights)
    return (y_p, y_s, conv_p, conv_s, rwkv_p, rwkv_s, shift_p, shift_s,
            pool_p, pool_s, k_p, k_s, v_p, v_s)
```

```python
import functools
import math

import jax
import jax.numpy as jnp
from jax import lax
from jax.experimental import pallas as pl
from jax.experimental.pallas import tpu as pltpu

F32 = jnp.float32
BF16 = jnp.bfloat16

D_MODEL = 1024
DEPTH = 2
PAST_LEN = 1024
CHUNK = 64
GROUP_WIDTH = 256
CONV_WIDTH = 31
CONV_HIST = CONV_WIDTH - 1
CONV_PAD = 32
RWKV_HEAD = 64
RWKV_HEADS = 4
POOL_WINDOWS = (2, 4, 8, 16)
POOL_HIST = 15
POOL_PAD = 16
HEAD_DIM = 64
N_Q_HEADS = 4
N_KV_HEADS = 2
WINDOW = 128
D_FF = 2816
A_COLS = 512
B_COLS = 896
C_COLS = 256
D_COLS = 512
IN_COLS = A_COLS + B_COLS + C_COLS + D_COLS
RMS_EPS = 1e-6
LN_EPS = 1e-5
GN_EPS = 64e-5
ATTN_SCALE = HEAD_DIM ** -0.5
NEG_INF = -1e30

VMEM_LIMIT_BYTES = 56 * 1024 * 1024


def _dot(a, b):
    return jnp.dot(a, b, preferred_element_type=F32)


def _dot_nt(a, b):
    return lax.dot_general(a, b, (((1,), (1,)), ((), ())), preferred_element_type=F32)


def _dot_tn(a, b):
    return lax.dot_general(a, b, (((0,), (0,)), ((), ())), preferred_element_type=F32)


def _sigmoid(x):
    return 1.0 / (1.0 + jnp.exp(-x))


def _split_dot_right(x, m_bf16, terms):
    acc = None
    rem = x
    for i in range(terms):
        hi = rem.astype(BF16)
        d = _dot(hi, m_bf16)
        acc = d if acc is None else acc + d
        if i + 1 < terms:
            rem = rem - hi.astype(F32)
    return acc


def _split_dot_left(m_bf16, x, terms):
    acc = None
    rem = x
    for i in range(terms):
        hi = rem.astype(BF16)
        d = _dot(m_bf16, hi)
        acc = d if acc is None else acc + d
        if i + 1 < terms:
            rem = rem - hi.astype(F32)
    return acc


def _block_matrix(n, blk, value):
    sh = int(math.log2(blk))
    r = lax.shift_right_logical(lax.broadcasted_iota(jnp.int32, (n, n), 0), sh)
    c = lax.shift_right_logical(lax.broadcasted_iota(jnp.int32, (n, n), 1), sh)
    return jnp.where(r == c, value, 0.0).astype(BF16)


def _params(sem):
    return pltpu.CompilerParams(dimension_semantics=sem, vmem_limit_bytes=VMEM_LIMIT_BYTES)


def _proj_body(x_ref, g_ref, w_ref, ua_ref, ub_ref, uc_ref, ud_ref):
    x = x_ref[...]
    ms = jnp.mean(x * x, axis=-1, keepdims=True)
    xn = ((x * lax.rsqrt(ms + RMS_EPS)) * g_ref[...]).astype(BF16)
    c0 = 0
    for ref, width in ((ua_ref, A_COLS), (ub_ref, B_COLS), (uc_ref, C_COLS), (ud_ref, D_COLS)):
        ref[...] = _dot(xn, w_ref[:, c0:c0 + width])
        c0 += width


def _proj(x2d, g, w_bf16):
    t = x2d.shape[0]
    tm = min(t, 512)
    assert t % tm == 0
    widths = (A_COLS, B_COLS, C_COLS, D_COLS)
    return pl.pallas_call(
        _proj_body,
        grid=(t // tm,),
        in_specs=[
            pl.BlockSpec((tm, D_MODEL), lambda i: (i, 0)),
            pl.BlockSpec((1, D_MODEL), lambda i: (0, 0)),
            pl.BlockSpec((D_MODEL, IN_COLS), lambda i: (0, 0)),
        ],
        out_specs=[pl.BlockSpec((tm, w), lambda i: (i, 0)) for w in widths],
        out_shape=[jax.ShapeDtypeStruct((t, w), F32) for w in widths],
        compiler_params=_params(("parallel",)),
        name="proj",
    )(x2d, g, w_bf16)


def _conv_body(u_ref, hist_ref, w_ref, vec_ref, y_ref, tail_ref, ext_ref, *, tl, rb):
    li = pl.program_id(1)

    @pl.when(li == 0)
    def _():
        ext_ref[0:CONV_PAD, :] = hist_ref[0]

    u = u_ref[0]
    glu = u[:, :GROUP_WIDTH] * _sigmoid(u[:, GROUP_WIDTH:])
    ext_ref[CONV_PAD:CONV_PAD + tl, :] = glu
    conv_b = vec_ref[0:1, :]
    ln_g = vec_ref[1:2, :]
    ln_b = vec_ref[2:3, :]
    off = CONV_PAD - CONV_HIST
    for r0 in range(0, tl, rb):
        acc = jnp.zeros((rb, GROUP_WIDTH), F32)
        for j in range(CONV_WIDTH):
            acc = acc + w_ref[j:j + 1, :] * ext_ref[r0 + off + j:r0 + off + j + rb, :]
        acc = acc + conv_b
        mu = jnp.mean(acc, axis=-1, keepdims=True)
        d = acc - mu
        var = jnp.mean(d * d, axis=-1, keepdims=True)
        yn = d * lax.rsqrt(var + LN_EPS) * ln_g + ln_b
        y_ref[0, r0:r0 + rb, :] = yn * _sigmoid(yn)
    tail = ext_ref[tl:tl + CONV_PAD, :]
    tail_ref[0] = tail
    ext_ref[0:CONV_PAD, :] = tail


def _conv(ua, hist, conv_w, vec):
    b, l, _ = ua.shape
    tl = min(l, 512)
    rb = min(tl, 64)
    assert l % tl == 0 and tl % rb == 0 and tl >= CONV_PAD
    return pl.pallas_call(
        functools.partial(_conv_body, tl=tl, rb=rb),
        grid=(b, l // tl),
        in_specs=[
            pl.BlockSpec((1, tl, A_COLS), lambda i, j: (i, j, 0)),
            pl.BlockSpec((1, CONV_PAD, GROUP_WIDTH), lambda i, j: (i, 0, 0)),
            pl.BlockSpec((CONV_PAD, GROUP_WIDTH), lambda i, j: (0, 0)),
            pl.BlockSpec((8, GROUP_WIDTH), lambda i, j: (0, 0)),
        ],
        out_specs=[
            pl.BlockSpec((1, tl, GROUP_WIDTH), lambda i, j: (i, j, 0)),
            pl.BlockSpec((1, CONV_PAD, GROUP_WIDTH), lambda i, j: (i, 0, 0)),
        ],
        out_shape=[
            jax.ShapeDtypeStruct((b, l, GROUP_WIDTH), F32),
            jax.ShapeDtypeStruct((b, CONV_PAD, GROUP_WIDTH), F32),
        ],
        scratch_shapes=[pltpu.VMEM((CONV_PAD + tl, GROUP_WIDTH), F32)],
        compiler_params=_params(("parallel", "arbitrary")),
        name="conv",
    )(ua, hist, conv_w, vec)


def _pool_body(u_ref, hist_ref, wbd_ref, sc_ref, y_ref, ext_ref, *, tl, rb, pos0):
    li = pl.program_id(1)

    @pl.when(li == 0)
    def _():
        ext_ref[0:POOL_PAD, :] = hist_ref[0]

    ext_ref[POOL_PAD:POOL_PAD + tl, :] = u_ref[0]
    lane = lax.broadcasted_iota(jnp.int32, (1, GROUP_WIDTH), 1)
    for r0 in range(0, tl, rb):
        base = POOL_PAD + r0
        sums = []
        acc = None
        for i in range(max(POOL_WINDOWS)):
            sh = ext_ref[base - i:base - i + rb, :]
            acc = sh if acc is None else acc + sh
            if i + 1 in POOL_WINDOWS:
                sums.append(acc)
        pos = pos0 + li * tl + r0 + lax.broadcasted_iota(jnp.int32, (rb, 1), 0)
        means = [s / jnp.minimum(w, pos + 1).astype(F32) for s, w in zip(sums, POOL_WINDOWS)]
        mean = jnp.where(lane < 64, means[0],
                         jnp.where(lane < 128, means[1], jnp.where(lane < 192, means[2], means[3])))
        d = mean - ext_ref[base:base + rb, :]
        y_ref[0, r0:r0 + rb, :] = _dot(d.astype(BF16), wbd_ref[...]) * sc_ref[...]
    ext_ref[0:POOL_PAD, :] = ext_ref[tl:tl + POOL_PAD, :]


def _pool(uc, hist, wbd, scale, pos0):
    b, l, _ = uc.shape
    tl = min(l, 512)
    rb = min(tl, 128)
    assert l % tl == 0 and tl % rb == 0 and tl >= POOL_PAD
    return pl.pallas_call(
        functools.partial(_pool_body, tl=tl, rb=rb, pos0=pos0),
        grid=(b, l // tl),
        in_specs=[
            pl.BlockSpec((1, tl, GROUP_WIDTH), lambda i, j: (i, j, 0)),
            pl.BlockSpec((1, POOL_PAD, GROUP_WIDTH), lambda i, j: (i, 0, 0)),
            pl.BlockSpec((GROUP_WIDTH, GROUP_WIDTH), lambda i, j: (0, 0)),
            pl.BlockSpec((1, GROUP_WIDTH), lambda i, j: (0, 0)),
        ],
        out_specs=pl.BlockSpec((1, tl, GROUP_WIDTH), lambda i, j: (i, j, 0)),
        out_shape=jax.ShapeDtypeStruct((b, l, GROUP_WIDTH), F32),
        scratch_shapes=[pltpu.VMEM((POOL_PAD + tl, GROUP_WIDTH), F32)],
        compiler_params=_params(("parallel", "arbitrary")),
        name="pool",
    )(uc, hist, wbd, scale)


def _attn_body(sink_ref, u_ref, hk_ref, hv_ref, qg_ref, kg_ref, y_ref, kt_ref, vt_ref,
               kext_ref, vext_ref, *, tq, chunk, hist_valid):
    li = pl.program_id(1)

    @pl.when(li == 0)
    def _():
        kext_ref[0:WINDOW, :] = hk_ref[0]
        vext_ref[0:WINDOW, :] = hv_ref[0]

    u = u_ref[0]
    q = u[:, 0:256]
    k = u[:, 256:384]
    v = u[:, 384:512]
    inv = 1.0 / HEAD_DIM
    qms = _split_dot_right(q * q, _block_matrix(256, HEAD_DIM, inv), 2)
    kms = _split_dot_right(k * k, _block_matrix(128, HEAD_DIM, inv), 2)
    qn = ((q * lax.rsqrt(qms + RMS_EPS)) * qg_ref[...]).astype(BF16)
    kn = (k * lax.rsqrt(kms + RMS_EPS)) * kg_ref[...]
    kext_ref[WINDOW:WINDOW + tq, :] = kn
    vext_ref[WINDOW:WINDOW + tq, :] = v
    kw = WINDOW + chunk
    kpos = lax.broadcasted_iota(jnp.int32, (1, kw), 1)
    for c in range(tq // chunk):
        r0 = c * chunk
        keys = kext_ref[r0:r0 + kw, :].astype(BF16)
        vals = vext_ref[r0:r0 + kw, :].astype(BF16)
        outs = []
        for h in range(N_Q_HEADS):
            g = h // (N_Q_HEADS // N_KV_HEADS)
            qh = qn[r0:r0 + chunk, h * HEAD_DIM:(h + 1) * HEAD_DIM]
            kh = keys[:, g * HEAD_DIM:(g + 1) * HEAD_DIM]
            vh = vals[:, g * HEAD_DIM:(g + 1) * HEAD_DIM]
            s = _dot_nt(qh, kh) * ATTN_SCALE
            if not hist_valid and r0 < WINDOW:
                valid = jnp.logical_or(kpos >= WINDOW - r0, li > 0)
                s = jnp.where(valid, s, NEG_INF)
            sk = sink_ref[h]
            m = jnp.maximum(jnp.max(s, axis=-1, keepdims=True), sk)
            p = jnp.exp(s - m)
            den = jnp.sum(p, axis=-1, keepdims=True) + jnp.exp(sk - m)
            outs.append(_dot(p.astype(BF16), vh) / den)
        y_ref[0, r0:r0 + chunk, :] = jnp.concatenate(outs, axis=-1)
    ktail = kext_ref[tq:tq + WINDOW, :]
    vtail = vext_ref[tq:tq + WINDOW, :]
    kt_ref[0] = ktail
    vt_ref[0] = vtail
    kext_ref[0:WINDOW, :] = ktail
    vext_ref[0:WINDOW, :] = vtail


def _attn(ud, hk, hv, qg, kg, sinks, chunk, hist_valid):
    b, l, _ = ud.shape
    tq = min(l, 256)
    assert l % tq == 0 and tq % chunk == 0
    kvw = N_KV_HEADS * HEAD_DIM
    return pl.pallas_call(
        functools.partial(_attn_body, tq=tq, chunk=chunk, hist_valid=hist_valid),
        grid=(b, l // tq),
        in_specs=[
            pl.BlockSpec(memory_space=pltpu.SMEM),
            pl.BlockSpec((1, tq, D_COLS), lambda i, j: (i, j, 0)),
            pl.BlockSpec((1, WINDOW, kvw), lambda i, j: (i, 0, 0)),
            pl.BlockSpec((1, WINDOW, kvw), lambda i, j: (i, 0, 0)),
            pl.BlockSpec((1, GROUP_WIDTH), lambda i, j: (0, 0)),
            pl.BlockSpec((1, kvw), lambda i, j: (0, 0)),
        ],
        out_specs=[
            pl.BlockSpec((1, tq, GROUP_WIDTH), lambda i, j: (i, j, 0)),
            pl.BlockSpec((1, WINDOW, kvw), lambda i, j: (i, 0, 0)),
            pl.BlockSpec((1, WINDOW, kvw), lambda i, j: (i, 0, 0)),
        ],
        out_shape=[
            jax.ShapeDtypeStruct((b, l, GROUP_WIDTH), F32),
            jax.ShapeDtypeStruct((b, WINDOW, kvw), F32),
            jax.ShapeDtypeStruct((b, WINDOW, kvw), F32),
        ],
        scratch_shapes=[pltpu.VMEM((WINDOW + tq, kvw), F32), pltpu.VMEM((WINDOW + tq, kvw), F32)],
        compiler_params=_params(("parallel", "arbitrary")),
        name="attn",
    )(sinks, ud, hk, hv, qg, kg)


def _bdot(a, b):
    return lax.dot_general(a, b, (((2,), (1,)), ((0,), (0,))), preferred_element_type=F32)


def _bdot_nt(a, b):
    return lax.dot_general(a, b, (((2,), (2,)), ((0,), (0,))), preferred_element_type=F32)


def _bdot_tn(a, b):
    return lax.dot_general(a, b, (((1,), (1,)), ((0,), (0,))), preferred_element_type=F32)


def _rwkv_body(u_ref, sp_ref, s0_ref, mu_ref, wl_ref, vec_ref, y_ref, sn_ref,
               prev_ref, s_ref, yacc_ref, *, nb, tb, chunk):
    li = pl.program_id(0)
    gw = GROUP_WIDTH
    hd = RWKV_HEAD
    nh = RWKV_HEADS
    rows = nb * tb
    nc = tb // chunk

    @pl.when(li == 0)
    def _():
        prev_ref[...] = sp_ref[...]
        s_ref[...] = s0_ref[...]

    row = lax.broadcasted_iota(jnp.int32, (tb, 1), 0)
    mu = mu_ref[...]
    xs_parts = []
    for b in range(nb):
        ub = u_ref[b]
        prev = jnp.where(row == 0, prev_ref[b], pltpu.roll(ub, 1, axis=0))
        prev_ref[b] = ub[tb - 1:tb, :]
        xs_parts.append(ub + mu * (prev - ub))
    xs = jnp.concatenate(xs_parts, axis=0)
    r = xs[:, 0:gw]
    k = xs[:, gw:2 * gw]
    v = xs[:, 2 * gw:3 * gw]
    lat = xs[:, 3 * gw:B_COLS]
    lane_lat = lax.broadcasted_iota(jnp.int32, (1, B_COLS - 3 * gw), 1)
    act = jnp.where(lane_lat < 32, jnp.tanh(lat), jnp.where(lane_lat < 64, lat, _sigmoid(lat)))
    lo = _dot(act.astype(BF16), wl_ref[...])
    w0 = vec_ref[0:1, :]
    a0 = vec_ref[1:2, :]
    k_k = vec_ref[2:3, :]
    k_a = vec_ref[3:4, :]
    r_k = vec_ref[4:5, :]
    gn_g = vec_ref[5:6, :]
    gn_b = vec_ref[6:7, :]
    z = -(w0 + lo[:, 0:gw])
    softplus = jnp.maximum(z, 0.0) + jnp.log(1.0 + jnp.exp(-jnp.abs(z)))
    logw = -jnp.exp(-softplus - 0.5)
    a_rate = _sigmoid(a0 + lo[:, gw:2 * gw])
    gate = lo[:, 2 * gw:3 * gw]
    ones_blk = _block_matrix(gw, hd, 1.0)
    kk = k * k_k
    kk = kk / jnp.maximum(jnp.sqrt(_split_dot_right(kk * kk, ones_blk, 2)), 1e-12)
    k_mod = k * (1.0 + (a_rate - 1.0) * k_a)
    b_v = kk * a_rate
    bonus = _split_dot_right(r * k_mod * r_k, ones_blk, 2) * v

    grp = min(rows, 256)
    gi = lax.broadcasted_iota(jnp.int32, (grp, grp), 0)
    gj = lax.broadcasted_iota(jnp.int32, (grp, grp), 1)
    csh = int(math.log2(chunk))
    tri = jnp.where(jnp.logical_and(lax.shift_right_logical(gi, csh) == lax.shift_right_logical(gj, csh),
                                    gj <= gi), 1.0, 0.0).astype(BF16)
    cum = jnp.concatenate([_split_dot_left(tri, logw[g0:g0 + grp], 3) for g0 in range(0, rows, grp)], axis=0)
    cum3 = cum.reshape(nb * nc, chunk, gw)
    cum_c = cum3[:, chunk - 1:chunk, :]
    e_end = jnp.exp(cum_c - cum3).reshape(rows, gw)
    w_c = jnp.exp(cum_c)
    e_neg = jnp.exp(-cum)
    dense = dict(
        a=-kk * jnp.exp(cum - logw),
        r=r * jnp.exp(cum),
        bt=b_v * e_neg,
        kt=k_mod * e_neg,
        bh=b_v * e_end,
        kh=k_mod * e_end,
    )
    slot_hi = lax.broadcasted_iota(jnp.int32, (1, 2 * hd), 1) >= hd

    def head_tile(x, h, own_slot):
        col = x[:, (h // 2) * 2 * hd:(h // 2 + 1) * 2 * hd]
        keep = slot_hi if (h % 2 == 1) == own_slot else jnp.logical_not(slot_hi)
        return jnp.where(keep, col, 0.0)

    v_sw = jnp.concatenate([pltpu.roll(v[:, j * 2 * hd:(j + 1) * 2 * hd], hd, axis=1) for j in range(nh // 2)],
                           axis=1)
    tiles = {name: [head_tile(x, h, True) for h in range(nh)] for name, x in dense.items()}
    tiles_b = {name: [t.astype(BF16) for t in tiles[name]] for name in ("a", "bt", "kt", "bh", "kh")}
    v_tiles = [head_tile(v_sw, h, False).astype(BF16) for h in range(nh)]

    def blocks(per_head, c):
        return jnp.stack([per_head[h][b * tb + c * chunk:b * tb + (c + 1) * chunk]
                          for h in range(nh) for b in range(nb)])

    n = nh * nb
    ri = lax.broadcasted_iota(jnp.int32, (2 * chunk, 2 * chunk), 0)
    ci = jnp.bitwise_and(lax.broadcasted_iota(jnp.int32, (2 * chunk, 2 * chunk), 1), chunk - 1)
    gmask = ci < jnp.bitwise_and(ri, chunk - 1) + lax.shift_right_logical(ri, csh)
    zeros_c = jnp.zeros((n, chunk, 2 * hd), BF16)
    n_sq = int(math.log2(chunk))
    for c in range(nc):
        a_b = blocks(tiles_b["a"], c)
        r_f = blocks(tiles["r"], c)
        v_b = blocks(v_tiles, c)
        ar = jnp.concatenate([a_b, r_f.astype(BF16)], axis=1)
        bk = jnp.concatenate([blocks(tiles_b["bt"], c), blocks(tiles_b["kt"], c)], axis=1)
        bhkh = jnp.concatenate([blocks(tiles_b["bh"], c), blocks(tiles_b["kh"], c)], axis=1)
        g = jnp.where(gmask, _bdot_nt(ar, bk), 0.0)
        g_top = g[:, :chunk, :]
        g_bot = g[:, chunk:, :].astype(BF16)
        w = a_b.astype(F32) + _bdot(g_top.astype(BF16), jnp.concatenate([zeros_c, v_b], axis=1))
        p = g_top[:, :, :chunk]
        for i in range(n_sq):
            pb = p.astype(BF16)
            if i + 1 < n_sq:
                res = _bdot(pb, jnp.concatenate([w.astype(BF16), pb], axis=2))
                w = w + res[:, :, :2 * hd]
                p = res[:, :, 2 * hd:]
            else:
                w = w + _bdot(pb, w.astype(BF16))
        xv = jnp.concatenate([w.astype(BF16), v_b], axis=1)
        ry = _bdot(g_bot, xv)
        mp = _bdot_tn(xv, bhkh)
        s_old = s_ref[...]
        s_b = s_old.astype(BF16)
        y_nt = _bdot_nt((ry + r_f).astype(BF16), s_b)
        wc = jnp.stack([w_c[b * nc + c][:, (h // 2) * 2 * hd:(h // 2 + 1) * 2 * hd]
                        for h in range(nh) for b in range(nb)])
        psi = jnp.concatenate(
            [mp[h * nb:(h + 1) * nb, (1 - h % 2) * hd:(2 - h % 2) * hd, :] for h in range(nh)], axis=0)
        s_ref[...] = s_old * wc + _bdot(s_b, mp.astype(BF16)) + psi
        for b in range(nb):
            ys = [y_nt[h * nb + b] + ry[h * nb + b][:, (1 - h % 2) * hd:(2 - h % 2) * hd] for h in range(nh)]
            yacc_ref[b * tb + c * chunk:b * tb + (c + 1) * chunk, :] = jnp.concatenate(ys, axis=1)

    y = yacc_ref[...]
    avg_blk = _block_matrix(gw, hd, 1.0 / hd)
    m = _split_dot_right(y, avg_blk, 2)
    d = y - m
    var = _split_dot_right(d * d, avg_blk, 2)
    yn = d * lax.rsqrt(var + GN_EPS) * gn_g + gn_b
    out = (yn + bonus) * gate
    for b in range(nb):
        y_ref[b] = out[b * tb:(b + 1) * tb]
    sn_ref[...] = s_ref[...]


def _rwkv(ub, shift_prev, state, mu, wl, vec, chunk):
    b, l, _ = ub.shape
    tb = min(l, 256)
    assert l % tb == 0 and tb % chunk == 0
    hd = RWKV_HEAD
    n = RWKV_HEADS * b
    st = jnp.swapaxes(state, 0, 1)
    zero = jnp.zeros_like(st)
    st = jnp.stack([jnp.concatenate([st[h], zero[h]] if h % 2 == 0 else [zero[h], st[h]], axis=-1)
                    for h in range(RWKV_HEADS)]).reshape(n, hd, 2 * hd)
    y, sn = pl.pallas_call(
        functools.partial(_rwkv_body, nb=b, tb=tb, chunk=chunk),
        grid=(l // tb,),
        in_specs=[
            pl.BlockSpec((b, tb, B_COLS), lambda j: (0, j, 0)),
            pl.BlockSpec((b, 1, B_COLS), lambda j: (0, 0, 0)),
            pl.BlockSpec((n, hd, 2 * hd), lambda j: (0, 0, 0)),
            pl.BlockSpec((1, B_COLS), lambda j: (0, 0)),
            pl.BlockSpec((128, 3 * GROUP_WIDTH), lambda j: (0, 0)),
            pl.BlockSpec((8, GROUP_WIDTH), lambda j: (0, 0)),
        ],
        out_specs=[
            pl.BlockSpec((b, tb, GROUP_WIDTH), lambda j: (0, j, 0)),
            pl.BlockSpec((n, hd, 2 * hd), lambda j: (0, 0, 0)),
        ],
        out_shape=[
            jax.ShapeDtypeStruct((b, l, GROUP_WIDTH), F32),
            jax.ShapeDtypeStruct((n, hd, 2 * hd), F32),
        ],
        scratch_shapes=[
            pltpu.VMEM((b, 1, B_COLS), F32),
            pltpu.VMEM((n, hd, 2 * hd), F32),
            pltpu.VMEM((b * tb, GROUP_WIDTH), F32),
        ],
        compiler_params=_params(("arbitrary",)),
        name="rwkv",
    )(ub, shift_prev, st, mu, wl, vec)
    sn = sn.reshape(RWKV_HEADS, b, hd, 2 * hd)
    sn = jnp.stack([sn[h, :, :, (h % 2) * hd:(h % 2 + 1) * hd] for h in range(RWKV_HEADS)], axis=1)
    return y, sn


def _outffn_body(x_ref, ya_ref, yb_ref, yc_ref, yd_ref, wo_ref, g_ref, wg_ref, wu_ref, wd_ref, o_ref):
    ycat = jnp.concatenate([ya_ref[...], yb_ref[...], yc_ref[...], yd_ref[...]], axis=-1).astype(BF16)
    x1 = x_ref[...] + _dot(ycat, wo_ref[...])
    ms = jnp.mean(x1 * x1, axis=-1, keepdims=True)
    hn = ((x1 * lax.rsqrt(ms + RMS_EPS)) * g_ref[...]).astype(BF16)
    hg = _dot(hn, wg_ref[...])
    hu = _dot(hn, wu_ref[...])
    act = (hg * _sigmoid(hg) * hu).astype(BF16)
    o_ref[...] = x1 + _dot(act, wd_ref[...])


def _outffn(x2d, ya, yb, yc, yd, wo, g, wg, wu, wd):
    t = x2d.shape[0]
    tm = min(t, 256)
    assert t % tm == 0
    row = lambda w: pl.BlockSpec((tm, w), lambda i: (i, 0))
    const = lambda shape: pl.BlockSpec(shape, lambda i: (0, 0), pipeline_mode=pl.Buffered(1))
    return pl.pallas_call(
        _outffn_body,
        grid=(t // tm,),
        in_specs=[
            row(D_MODEL), row(GROUP_WIDTH), row(GROUP_WIDTH), row(GROUP_WIDTH), row(GROUP_WIDTH),
            const((4 * GROUP_WIDTH, D_MODEL)),
            const((1, D_MODEL)),
            const((D_MODEL, D_FF)),
            const((D_MODEL, D_FF)),
            const((D_FF, D_MODEL)),
        ],
        out_specs=row(D_MODEL),
        out_shape=jax.ShapeDtypeStruct((t, D_MODEL), F32),
        compiler_params=_params(("parallel",)),
        name="outffn",
    )(x2d, ya, yb, yc, yd, wo, g, wg, wu, wd)


def _pad_rows(a, rows):
    return jnp.pad(a, ((0, rows - a.shape[0]), (0, 0)))


def _layer_weights(l, p):
    gw = GROUP_WIDTH
    wl = jnp.zeros((128, 3 * gw), F32)
    wl = wl.at[0:32, 0:gw].set(p['rwkv_w2'][l])
    wl = wl.at[32:64, gw:2 * gw].set(p['rwkv_a2'][l])
    wl = wl.at[64:128, 2 * gw:3 * gw].set(p['rwkv_g2'][l])
    pool_w = p['pool_w'][l]
    wbd = jnp.zeros((gw, gw), F32)
    pc = gw // len(POOL_WINDOWS)
    for g in range(len(POOL_WINDOWS)):
        wbd = wbd.at[g * pc:(g + 1) * pc, g * pc:(g + 1) * pc].set(pool_w[g])
    return dict(
        norm_mix_g=p['norm_mix_g'][l][None, :],
        w_in=p['w_in'][l].astype(BF16),
        conv_w=_pad_rows(p['conv_w'][l], CONV_PAD),
        conv_vec=_pad_rows(jnp.stack([p['conv_b'][l], p['conv_ln_g'][l], p['conv_ln_b'][l]]), 8),
        rwkv_mu=p['rwkv_mu'][l][None, :],
        rwkv_wl=wl.astype(BF16),
        rwkv_vec=_pad_rows(jnp.stack([
            p['rwkv_w0'][l], p['rwkv_a0'][l], p['rwkv_k_k'][l], p['rwkv_k_a'][l],
            p['rwkv_r_k'][l].reshape(gw), p['rwkv_gn_g'][l], p['rwkv_gn_b'][l]]), 8),
        pool_wbd=wbd.astype(BF16),
        pool_scale=p['pool_scale'][l][None, :],
        attn_qg=jnp.tile(p['attn_q_norm'][l], N_Q_HEADS)[None, :],
        attn_kg=jnp.tile(p['attn_k_norm'][l], N_KV_HEADS)[None, :],
        attn_sinks=p['attn_sinks'][l],
        w_out=p['w_out'][l].astype(BF16),
        norm_ffn_g=p['norm_ffn_g'][l][None, :],
        ffn_w_gate=p['ffn_w_gate'][l].astype(BF16),
        ffn_w_up=p['ffn_w_up'][l].astype(BF16),
        ffn_w_down=p['ffn_w_down'][l].astype(BF16),
    )


def _trunk(x, conv_hist, rwkv_state, shift_prev, pool_hist, k_cache, v_cache, pos0, chunk, weights):
    b, l, _ = x.shape
    has_past = conv_hist is not None
    kvw = N_KV_HEADS * HEAD_DIM
    x2d = x.reshape(b * l, D_MODEL)
    new = [[] for _ in range(6)]
    for li in range(DEPTH):
        w = weights[li]
        if has_past:
            ch = jnp.pad(conv_hist[li], ((0, 0), (CONV_PAD - CONV_HIST, 0), (0, 0)))
            rs = rwkv_state[li]
            sp = shift_prev[li][:, None, :]
            ph = jnp.pad(pool_hist[li], ((0, 0), (POOL_PAD - POOL_HIST, 0), (0, 0)))
            hk = k_cache[li].reshape(b, WINDOW, kvw)
            hv = v_cache[li].reshape(b, WINDOW, kvw)
        else:
            ch = jnp.zeros((b, CONV_PAD, GROUP_WIDTH), F32)
            rs = jnp.zeros((b, RWKV_HEADS, RWKV_HEAD, RWKV_HEAD), F32)
            sp = jnp.zeros((b, 1, B_COLS), F32)
            ph = jnp.zeros((b, POOL_PAD, GROUP_WIDTH), F32)
            hk = jnp.zeros((b, WINDOW, kvw), F32)
            hv = jnp.zeros((b, WINDOW, kvw), F32)
        ua, ub, uc, ud = _proj(x2d, w['norm_mix_g'], w['w_in'])
        ua = ua.reshape(b, l, A_COLS)
        ub = ub.reshape(b, l, B_COLS)
        uc = uc.reshape(b, l, C_COLS)
        ud = ud.reshape(b, l, D_COLS)
        ya, conv_tail = _conv(ua, ch, w['conv_w'], w['conv_vec'])
        yb, s_new = _rwkv(ub, sp, rs, w['rwkv_mu'], w['rwkv_wl'], w['rwkv_vec'], min(CHUNK, l))
        yc = _pool(uc, ph, w['pool_wbd'], w['pool_scale'], pos0)
        yd, k_tail, v_tail = _attn(ud, hk, hv, w['attn_qg'], w['attn_kg'], w['attn_sinks'], chunk, has_past)
        flat = lambda y: y.reshape(b * l, GROUP_WIDTH)
        x2d = _outffn(x2d, flat(ya), flat(yb), flat(yc), flat(yd), w['w_out'], w['norm_ffn_g'],
                      w['ffn_w_gate'], w['ffn_w_up'], w['ffn_w_down'])
        new[0].append(conv_tail[:, CONV_PAD - CONV_HIST:, :])
        new[1].append(s_new)
        new[2].append(ub[:, l - 1, :])
        new[3].append(uc[:, l - POOL_HIST:, :])
        new[4].append(k_tail.reshape(b, WINDOW, N_KV_HEADS, HEAD_DIM))
        new[5].append(v_tail.reshape(b, WINDOW, N_KV_HEADS, HEAD_DIM))
    return x2d.reshape(b, l, D_MODEL), tuple(jnp.stack(n) for n in new)


def kernel(x_prompt, x_sample, cache_conv, state_rwkv, state_rwkv_shift, cache_pool, cache_k, cache_v, norm_mix_g, w_in, conv_w, conv_b, conv_ln_g, conv_ln_b, rwkv_mu, rwkv_w0, rwkv_w2, rwkv_a0, rwkv_a2, rwkv_g2, rwkv_k_k, rwkv_k_a, rwkv_r_k, rwkv_gn_g, rwkv_gn_b, pool_w, pool_scale, attn_q_norm, attn_k_norm, attn_sinks, w_out, norm_ffn_g, ffn_w_gate, ffn_w_up, ffn_w_down):
    p = dict(
        norm_mix_g=norm_mix_g, w_in=w_in, conv_w=conv_w, conv_b=conv_b, conv_ln_g=conv_ln_g,
        conv_ln_b=conv_ln_b, rwkv_mu=rwkv_mu, rwkv_w0=rwkv_w0, rwkv_w2=rwkv_w2, rwkv_a0=rwkv_a0,
        rwkv_a2=rwkv_a2, rwkv_g2=rwkv_g2, rwkv_k_k=rwkv_k_k, rwkv_k_a=rwkv_k_a, rwkv_r_k=rwkv_r_k,
        rwkv_gn_g=rwkv_gn_g, rwkv_gn_b=rwkv_gn_b, pool_w=pool_w, pool_scale=pool_scale,
        attn_q_norm=attn_q_norm, attn_k_norm=attn_k_norm, attn_sinks=attn_sinks, w_out=w_out,
        norm_ffn_g=norm_ffn_g, ffn_w_gate=ffn_w_gate, ffn_w_up=ffn_w_up, ffn_w_down=ffn_w_down)
    weights = [_layer_weights(l, p) for l in range(DEPTH)]
    y_p, (conv_p, rwkv_p, shift_p, pool_p, k_p, v_p) = _trunk(
        x_prompt, None, None, None, None, None, None, 0, CHUNK, weights)
    y_s, (conv_s, rwkv_s, shift_s, pool_s, k_s, v_s) = _trunk(
        x_sample, cache_conv, state_rwkv, state_rwkv_shift, cache_pool, cache_k, cache_v,
        PAST_LEN, x_sample.shape[1], weights)
    return (y_p, y_s, conv_p, conv_s, rwkv_p, rwkv_s, shift_p, shift_s,
            pool_p, pool_s, k_p, k_s, v_p, v_s)
```

```python
import functools
import math

import jax
import jax.numpy as jnp
from jax import lax
from jax.experimental import pallas as pl
from jax.experimental.pallas import tpu as pltpu

F32 = jnp.float32
BF16 = jnp.bfloat16

D_MODEL = 1024
DEPTH = 2
PAST_LEN = 1024
CHUNK = 64
GROUP_WIDTH = 256
CONV_WIDTH = 31
CONV_HIST = CONV_WIDTH - 1
CONV_PAD = 32
RWKV_HEAD = 64
RWKV_HEADS = 4
POOL_WINDOWS = (2, 4, 8, 16)
POOL_HIST = 15
POOL_PAD = 16
HEAD_DIM = 64
N_Q_HEADS = 4
N_KV_HEADS = 2
WINDOW = 128
D_FF = 2816
A_COLS = 512
B_COLS = 896
C_COLS = 256
D_COLS = 512
IN_COLS = A_COLS + B_COLS + C_COLS + D_COLS
RMS_EPS = 1e-6
LN_EPS = 1e-5
GN_EPS = 64e-5
ATTN_SCALE = HEAD_DIM ** -0.5
NEG_INF = -1e30

VMEM_LIMIT_BYTES = 56 * 1024 * 1024
SUBLANES = 8


def _dot(a, b):
    return jnp.dot(a, b, preferred_element_type=F32)


def _dot_nt(a, b):
    return lax.dot_general(a, b, (((1,), (1,)), ((), ())), preferred_element_type=F32)


def _dot_tn(a, b):
    return lax.dot_general(a, b, (((0,), (0,)), ((), ())), preferred_element_type=F32)


def _sigmoid(x):
    return 1.0 / (1.0 + jnp.exp(-x))


def _split_dot_right(x, m_bf16, terms):
    acc = None
    rem = x
    for i in range(terms):
        hi = rem.astype(BF16)
        d = _dot(hi, m_bf16)
        acc = d if acc is None else acc + d
        if i + 1 < terms:
            rem = rem - hi.astype(F32)
    return acc


def _split_dot_left(m_bf16, x, terms):
    acc = None
    rem = x
    for i in range(terms):
        hi = rem.astype(BF16)
        d = _dot(m_bf16, hi)
        acc = d if acc is None else acc + d
        if i + 1 < terms:
            rem = rem - hi.astype(F32)
    return acc


def _block_matrix(n, blk, value):
    sh = int(math.log2(blk))
    r = lax.shift_right_logical(lax.broadcasted_iota(jnp.int32, (n, n), 0), sh)
    c = lax.shift_right_logical(lax.broadcasted_iota(jnp.int32, (n, n), 1), sh)
    return jnp.where(r == c, value, 0.0).astype(BF16)


def _params(sem):
    return pltpu.CompilerParams(dimension_semantics=sem, vmem_limit_bytes=VMEM_LIMIT_BYTES)


def _proj_body(x_ref, g_ref, w_ref, ua_ref, ub_ref, uc_ref, ud_ref):
    x = x_ref[...]
    ms = jnp.mean(x * x, axis=-1, keepdims=True)
    xn = ((x * lax.rsqrt(ms + RMS_EPS)) * g_ref[...]).astype(BF16)
    c0 = 0
    for ref, width in ((ua_ref, A_COLS), (ub_ref, B_COLS), (uc_ref, C_COLS), (ud_ref, D_COLS)):
        ref[...] = _dot(xn, w_ref[:, c0:c0 + width])
        c0 += width


def _proj(x2d, g, w_bf16):
    t = x2d.shape[0]
    tm = min(t, 512)
    assert t % tm == 0
    widths = (A_COLS, B_COLS, C_COLS, D_COLS)
    return pl.pallas_call(
        _proj_body,
        grid=(t // tm,),
        in_specs=[
            pl.BlockSpec((tm, D_MODEL), lambda i: (i, 0)),
            pl.BlockSpec((1, D_MODEL), lambda i: (0, 0)),
            pl.BlockSpec((D_MODEL, IN_COLS), lambda i: (0, 0)),
        ],
        out_specs=[pl.BlockSpec((tm, w), lambda i: (i, 0)) for w in widths],
        out_shape=[jax.ShapeDtypeStruct((t, w), F32) for w in widths],
        compiler_params=_params(("parallel",)),
        name="proj",
    )(x2d, g, w_bf16)


def _conv_body(u_ref, hist_ref, w_ref, vec_ref, y_ref, tail_ref, ext_ref, sh_ref, *, tl, rb):
    li = pl.program_id(1)

    @pl.when(li == 0)
    def _():
        ext_ref[0:CONV_PAD, :] = hist_ref[0]

    u = u_ref[0]
    glu = u[:, :GROUP_WIDTH] * _sigmoid(u[:, GROUP_WIDTH:])
    ext_ref[CONV_PAD:CONV_PAD + tl, :] = glu
    n_sh = sh_ref.shape[1]
    for s in range(1, SUBLANES):
        sh_ref[s - 1] = ext_ref[s:s + n_sh, :]
    conv_b = vec_ref[0:1, :]
    ln_g = vec_ref[1:2, :]
    ln_b = vec_ref[2:3, :]
    off = CONV_PAD - CONV_HIST
    for r0 in range(0, tl, rb):
        acc = jnp.zeros((rb, GROUP_WIDTH), F32)
        for j in range(CONV_WIDTH):
            a, s = divmod(off + j, SUBLANES)
            base = r0 + a * SUBLANES
            rows = ext_ref[base:base + rb, :] if s == 0 else sh_ref[s - 1, base:base + rb, :]
            acc = acc + w_ref[j:j + 1, :] * rows
        acc = acc + conv_b
        mu = jnp.mean(acc, axis=-1, keepdims=True)
        d = acc - mu
        var = jnp.mean(d * d, axis=-1, keepdims=True)
        yn = d * lax.rsqrt(var + LN_EPS) * ln_g + ln_b
        y_ref[0, r0:r0 + rb, :] = yn * _sigmoid(yn)
    tail = ext_ref[tl:tl + CONV_PAD, :]
    tail_ref[0] = tail
    ext_ref[0:CONV_PAD, :] = tail


def _conv(ua, hist, conv_w, vec):
    b, l, _ = ua.shape
    tl = min(l, 512)
    rb = min(tl, 64)
    assert l % tl == 0 and tl % rb == 0 and tl >= CONV_PAD
    return pl.pallas_call(
        functools.partial(_conv_body, tl=tl, rb=rb),
        grid=(b, l // tl),
        in_specs=[
            pl.BlockSpec((1, tl, A_COLS), lambda i, j: (i, j, 0)),
            pl.BlockSpec((1, CONV_PAD, GROUP_WIDTH), lambda i, j: (i, 0, 0)),
            pl.BlockSpec((CONV_PAD, GROUP_WIDTH), lambda i, j: (0, 0)),
            pl.BlockSpec((8, GROUP_WIDTH), lambda i, j: (0, 0)),
        ],
        out_specs=[
            pl.BlockSpec((1, tl, GROUP_WIDTH), lambda i, j: (i, j, 0)),
            pl.BlockSpec((1, CONV_PAD, GROUP_WIDTH), lambda i, j: (i, 0, 0)),
        ],
        out_shape=[
            jax.ShapeDtypeStruct((b, l, GROUP_WIDTH), F32),
            jax.ShapeDtypeStruct((b, CONV_PAD, GROUP_WIDTH), F32),
        ],
        scratch_shapes=[
            pltpu.VMEM((CONV_PAD + tl, GROUP_WIDTH), F32),
            pltpu.VMEM((SUBLANES - 1, CONV_PAD + tl - SUBLANES, GROUP_WIDTH), F32),
        ],
        compiler_params=_params(("parallel", "arbitrary")),
        name="conv",
    )(ua, hist, conv_w, vec)


def _pool_body(u_ref, hist_ref, wbd_ref, sc_ref, y_ref, ext_ref, *, tl, rb, pos0):
    li = pl.program_id(1)

    @pl.when(li == 0)
    def _():
        ext_ref[0:POOL_PAD, :] = hist_ref[0]

    ext_ref[POOL_PAD:POOL_PAD + tl, :] = u_ref[0]
    lane = lax.broadcasted_iota(jnp.int32, (1, GROUP_WIDTH), 1)
    for r0 in range(0, tl, rb):
        base = POOL_PAD + r0
        sums = []
        acc = None
        for i in range(max(POOL_WINDOWS)):
            sh = ext_ref[base - i:base - i + rb, :]
            acc = sh if acc is None else acc + sh
            if i + 1 in POOL_WINDOWS:
                sums.append(acc)
        pos = pos0 + li * tl + r0 + lax.broadcasted_iota(jnp.int32, (rb, 1), 0)
        means = [s / jnp.minimum(w, pos + 1).astype(F32) for s, w in zip(sums, POOL_WINDOWS)]
        mean = jnp.where(lane < 64, means[0],
                         jnp.where(lane < 128, means[1], jnp.where(lane < 192, means[2], means[3])))
        d = mean - ext_ref[base:base + rb, :]
        y_ref[0, r0:r0 + rb, :] = _dot(d.astype(BF16), wbd_ref[...]) * sc_ref[...]
    ext_ref[0:POOL_PAD, :] = ext_ref[tl:tl + POOL_PAD, :]


def _pool(uc, hist, wbd, scale, pos0):
    b, l, _ = uc.shape
    tl = min(l, 512)
    rb = min(tl, 128)
    assert l % tl == 0 and tl % rb == 0 and tl >= POOL_PAD
    return pl.pallas_call(
        functools.partial(_pool_body, tl=tl, rb=rb, pos0=pos0),
        grid=(b, l // tl),
        in_specs=[
            pl.BlockSpec((1, tl, GROUP_WIDTH), lambda i, j: (i, j, 0)),
            pl.BlockSpec((1, POOL_PAD, GROUP_WIDTH), lambda i, j: (i, 0, 0)),
            pl.BlockSpec((GROUP_WIDTH, GROUP_WIDTH), lambda i, j: (0, 0)),
            pl.BlockSpec((1, GROUP_WIDTH), lambda i, j: (0, 0)),
        ],
        out_specs=pl.BlockSpec((1, tl, GROUP_WIDTH), lambda i, j: (i, j, 0)),
        out_shape=jax.ShapeDtypeStruct((b, l, GROUP_WIDTH), F32),
        scratch_shapes=[pltpu.VMEM((POOL_PAD + tl, GROUP_WIDTH), F32)],
        compiler_params=_params(("parallel", "arbitrary")),
        name="pool",
    )(uc, hist, wbd, scale)


def _attn_body(sink_ref, u_ref, hk_ref, hv_ref, qg_ref, kg_ref, y_ref, kt_ref, vt_ref,
               kh_ref, vh_ref, *, nb, tq, chunk, hist_valid):
    li = pl.program_id(0)
    hd = HEAD_DIM
    rows = nb * tq
    ncq = tq // chunk
    kw = WINDOW + chunk

    @pl.when(li == 0)
    def _():
        kh_ref[...] = hk_ref[...]
        vh_ref[...] = hv_ref[...]

    u = u_ref[...].reshape(rows, D_COLS)
    q = u[:, 0:256]
    k = u[:, 256:384]
    v = u[:, 384:512]
    inv = 1.0 / hd
    qms = _split_dot_right(q * q, _block_matrix(256, hd, inv), 2)
    kms = _split_dot_right(k * k, _block_matrix(128, hd, inv), 2)
    qn = (q * lax.rsqrt(qms + RMS_EPS)) * (qg_ref[...] * ATTN_SCALE)
    kn = (k * lax.rsqrt(kms + RMS_EPS)) * kg_ref[...]
    kcat = jnp.concatenate([kh_ref[...], kn.reshape(nb, tq, 2 * hd)], axis=1)
    vcat = jnp.concatenate([vh_ref[...], v.reshape(nb, tq, 2 * hd)], axis=1)
    ktail = kcat[:, tq:tq + WINDOW, :]
    vtail = vcat[:, tq:tq + WINDOW, :]
    kt_ref[...] = ktail
    vt_ref[...] = vtail
    kh_ref[...] = ktail
    vh_ref[...] = vtail
    kcat_b = kcat.astype(BF16)
    vcat_b = vcat.astype(BF16)

    slot_hi = lax.broadcasted_iota(jnp.int32, (1, 2 * hd), 1) >= hd
    q_tiles = []
    for h in range(N_Q_HEADS):
        col = qn[:, (h // 2) * 2 * hd:(h // 2 + 1) * 2 * hd]
        g = h // (N_Q_HEADS // N_KV_HEADS)
        if h % 2 != g:
            col = pltpu.roll(col, hd, axis=1)
        q_tiles.append(jnp.where(slot_hi if g == 1 else jnp.logical_not(slot_hi), col, 0.0).astype(BF16))
    pairs = [(b, c) for b in range(nb) for c in range(ncq)]
    qs = jnp.stack([jnp.concatenate([t[b * tq + c * chunk:b * tq + (c + 1) * chunk] for t in q_tiles], axis=0)
                    for b, c in pairs])
    ks = jnp.stack([kcat_b[b, c * chunk:c * chunk + kw] for b, c in pairs])
    vs = jnp.stack([vcat_b[b, c * chunk:c * chunk + kw] for b, c in pairs])
    s = _bdot_nt(qs, ks)
    if not hist_valid:
        assert ncq & (ncq - 1) == 0
        cidx = jnp.bitwise_and(lax.broadcasted_iota(jnp.int32, (len(pairs), 1, kw), 0), ncq - 1) * chunk
        kpos = lax.broadcasted_iota(jnp.int32, (len(pairs), 1, kw), 2) + cidx + (li * tq - WINDOW)
        s = jnp.where(kpos >= 0, s, NEG_INF)
    hrow = lax.broadcasted_iota(jnp.int32, (1, N_Q_HEADS * chunk, 1), 1)
    sk = jnp.full((1, N_Q_HEADS * chunk, 1), sink_ref[N_Q_HEADS - 1], F32)
    for h in range(N_Q_HEADS - 2, -1, -1):
        sk = jnp.where(hrow < (h + 1) * chunk, sink_ref[h], sk)
    m = jnp.maximum(jnp.max(s, axis=-1, keepdims=True), sk)
    p = jnp.exp(s - m)
    den = jnp.sum(p, axis=-1, keepdims=True) + jnp.exp(sk - m)
    o = _bdot(p.astype(BF16), vs) / den
    lo = jnp.logical_not(slot_hi)
    for i, (b, c) in enumerate(pairs):
        oc = o[i]
        col0 = jnp.where(lo, oc[0:chunk], pltpu.roll(oc[chunk:2 * chunk], hd, axis=1))
        col1 = jnp.where(lo, pltpu.roll(oc[2 * chunk:3 * chunk], hd, axis=1), oc[3 * chunk:4 * chunk])
        y_ref[b, c * chunk:(c + 1) * chunk, :] = jnp.concatenate([col0, col1], axis=1)


def _attn(ud, hk, hv, qg, kg, sinks, chunk, hist_valid):
    b, l, _ = ud.shape
    tq = min(l, 256)
    assert l % tq == 0 and tq % chunk == 0
    kvw = N_KV_HEADS * HEAD_DIM
    return pl.pallas_call(
        functools.partial(_attn_body, nb=b, tq=tq, chunk=chunk, hist_valid=hist_valid),
        grid=(l // tq,),
        in_specs=[
            pl.BlockSpec(memory_space=pltpu.SMEM),
            pl.BlockSpec((b, tq, D_COLS), lambda j: (0, j, 0)),
            pl.BlockSpec((b, WINDOW, kvw), lambda j: (0, 0, 0)),
            pl.BlockSpec((b, WINDOW, kvw), lambda j: (0, 0, 0)),
            pl.BlockSpec((1, GROUP_WIDTH), lambda j: (0, 0)),
            pl.BlockSpec((1, kvw), lambda j: (0, 0)),
        ],
        out_specs=[
            pl.BlockSpec((b, tq, GROUP_WIDTH), lambda j: (0, j, 0)),
            pl.BlockSpec((b, WINDOW, kvw), lambda j: (0, 0, 0)),
            pl.BlockSpec((b, WINDOW, kvw), lambda j: (0, 0, 0)),
        ],
        out_shape=[
            jax.ShapeDtypeStruct((b, l, GROUP_WIDTH), F32),
            jax.ShapeDtypeStruct((b, WINDOW, kvw), F32),
            jax.ShapeDtypeStruct((b, WINDOW, kvw), F32),
        ],
        scratch_shapes=[pltpu.VMEM((b, WINDOW, kvw), F32), pltpu.VMEM((b, WINDOW, kvw), F32)],
        compiler_params=_params(("arbitrary",)),
        name="attn",
    )(sinks, ud, hk, hv, qg, kg)


def _bdot(a, b):
    return lax.dot_general(a, b, (((2,), (1,)), ((0,), (0,))), preferred_element_type=F32)


def _bdot_nt(a, b):
    return lax.dot_general(a, b, (((2,), (2,)), ((0,), (0,))), preferred_element_type=F32)


def _bdot_tn(a, b):
    return lax.dot_general(a, b, (((1,), (1,)), ((0,), (0,))), preferred_element_type=F32)


def _rwkv_body(u_ref, sp_ref, s0_ref, mu_ref, wl_ref, vec_ref, y_ref, sn_ref,
               prev_ref, s_ref, yacc_ref, *, nb, tb, chunk):
    li = pl.program_id(0)
    gw = GROUP_WIDTH
    hd = RWKV_HEAD
    nh = RWKV_HEADS
    rows = nb * tb
    nc = tb // chunk

    @pl.when(li == 0)
    def _():
        prev_ref[...] = sp_ref[...]
        s_ref[...] = s0_ref[...]

    row = lax.broadcasted_iota(jnp.int32, (tb, 1), 0)
    mu = mu_ref[...]
    xs_parts = []
    for b in range(nb):
        ub = u_ref[b]
        prev = jnp.where(row == 0, prev_ref[b], pltpu.roll(ub, 1, axis=0))
        prev_ref[b] = ub[tb - 1:tb, :]
        xs_parts.append(ub + mu * (prev - ub))
    xs = jnp.concatenate(xs_parts, axis=0)
    r = xs[:, 0:gw]
    k = xs[:, gw:2 * gw]
    v = xs[:, 2 * gw:3 * gw]
    lat = xs[:, 3 * gw:B_COLS]
    lane_lat = lax.broadcasted_iota(jnp.int32, (1, B_COLS - 3 * gw), 1)
    act = jnp.where(lane_lat < 32, jnp.tanh(lat), jnp.where(lane_lat < 64, lat, _sigmoid(lat)))
    lo = _dot(act.astype(BF16), wl_ref[...])
    w0 = vec_ref[0:1, :]
    a0 = vec_ref[1:2, :]
    k_k = vec_ref[2:3, :]
    k_a = vec_ref[3:4, :]
    r_k = vec_ref[4:5, :]
    gn_g = vec_ref[5:6, :]
    gn_b = vec_ref[6:7, :]
    z = -(w0 + lo[:, 0:gw])
    softplus = jnp.maximum(z, 0.0) + jnp.log(1.0 + jnp.exp(-jnp.abs(z)))
    logw = -jnp.exp(-softplus - 0.5)
    a_rate = _sigmoid(a0 + lo[:, gw:2 * gw])
    gate = lo[:, 2 * gw:3 * gw]
    ones_blk = _block_matrix(gw, hd, 1.0)
    kk = k * k_k
    kk = kk / jnp.maximum(jnp.sqrt(_split_dot_right(kk * kk, ones_blk, 2)), 1e-12)
    k_mod = k * (1.0 + (a_rate - 1.0) * k_a)
    b_v = kk * a_rate
    bonus = _split_dot_right(r * k_mod * r_k, ones_blk, 2) * v

    grp = min(rows, 256)
    gi = lax.broadcasted_iota(jnp.int32, (grp, grp), 0)
    gj = lax.broadcasted_iota(jnp.int32, (grp, grp), 1)
    csh = int(math.log2(chunk))
    tri = jnp.where(jnp.logical_and(lax.shift_right_logical(gi, csh) == lax.shift_right_logical(gj, csh),
                                    gj <= gi), 1.0, 0.0).astype(BF16)
    cum = jnp.concatenate([_split_dot_left(tri, logw[g0:g0 + grp], 3) for g0 in range(0, rows, grp)], axis=0)
    cum3 = cum.reshape(nb * nc, chunk, gw)
    cum_c = cum3[:, chunk - 1:chunk, :]
    e_end = jnp.exp(cum_c - cum3).reshape(rows, gw)
    w_c = jnp.exp(cum_c)
    e_neg = jnp.exp(-cum)
    dense = dict(
        a=-kk * jnp.exp(cum - logw),
        r=r * jnp.exp(cum),
        bt=b_v * e_neg,
        kt=k_mod * e_neg,
        bh=b_v * e_end,
        kh=k_mod * e_end,
    )
    slot_hi = lax.broadcasted_iota(jnp.int32, (1, 2 * hd), 1) >= hd

    def head_tile(x, h, own_slot):
        col = x[:, (h // 2) * 2 * hd:(h // 2 + 1) * 2 * hd]
        keep = slot_hi if (h % 2 == 1) == own_slot else jnp.logical_not(slot_hi)
        return jnp.where(keep, col, 0.0)

    v_sw = jnp.concatenate([pltpu.roll(v[:, j * 2 * hd:(j + 1) * 2 * hd], hd, axis=1) for j in range(nh // 2)],
                           axis=1)
    tiles = {name: [head_tile(x, h, True) for h in range(nh)] for name, x in dense.items()}
    tiles_b = {name: [t.astype(BF16) for t in tiles[name]] for name in ("a", "bt", "kt", "bh", "kh")}
    v_tiles = [head_tile(v_sw, h, False).astype(BF16) for h in range(nh)]

    def blocks(per_head, c):
        return jnp.stack([per_head[h][b * tb + c * chunk:b * tb + (c + 1) * chunk]
                          for h in range(nh) for b in range(nb)])

    n = nh * nb
    ri = lax.broadcasted_iota(jnp.int32, (2 * chunk, 2 * chunk), 0)
    ci = jnp.bitwise_and(lax.broadcasted_iota(jnp.int32, (2 * chunk, 2 * chunk), 1), chunk - 1)
    gmask = ci < jnp.bitwise_and(ri, chunk - 1) + lax.shift_right_logical(ri, csh)
    zeros_c = jnp.zeros((n, chunk, 2 * hd), BF16)
    n_sq = int(math.log2(chunk))
    for c in range(nc):
        a_b = blocks(tiles_b["a"], c)
        r_f = blocks(tiles["r"], c)
        v_b = blocks(v_tiles, c)
        ar = jnp.concatenate([a_b, r_f.astype(BF16)], axis=1)
        bk = jnp.concatenate([blocks(tiles_b["bt"], c), blocks(tiles_b["kt"], c)], axis=1)
        bhkh = jnp.concatenate([blocks(tiles_b["bh"], c), blocks(tiles_b["kh"], c)], axis=1)
        g = jnp.where(gmask, _bdot_nt(ar, bk), 0.0)
        g_top = g[:, :chunk, :]
        g_bot = g[:, chunk:, :].astype(BF16)
        w = a_b.astype(F32) + _bdot(g_top.astype(BF16), jnp.concatenate([zeros_c, v_b], axis=1))
        p = g_top[:, :, :chunk]
        for i in range(n_sq):
            pb = p.astype(BF16)
            if i + 1 < n_sq:
                res = _bdot(pb, jnp.concatenate([w.astype(BF16), pb], axis=2))
                w = w + res[:, :, :2 * hd]
                p = res[:, :, 2 * hd:]
            else:
                w = w + _bdot(pb, w.astype(BF16))
        xv = jnp.concatenate([w.astype(BF16), v_b], axis=1)
        ry = _bdot(g_bot, xv)
        mp = _bdot_tn(xv, bhkh)
        s_old = s_ref[...]
        s_b = s_old.astype(BF16)
        y_nt = _bdot_nt((ry + r_f).astype(BF16), s_b)
        wc = jnp.stack([w_c[b * nc + c][:, (h // 2) * 2 * hd:(h // 2 + 1) * 2 * hd]
                        for h in range(nh) for b in range(nb)])
        psi = jnp.concatenate(
            [mp[h * nb:(h + 1) * nb, (1 - h % 2) * hd:(2 - h % 2) * hd, :] for h in range(nh)], axis=0)
        s_ref[...] = s_old * wc + _bdot(s_b, mp.astype(BF16)) + psi
        for b in range(nb):
            ys = [y_nt[h * nb + b] + ry[h * nb + b][:, (1 - h % 2) * hd:(2 - h % 2) * hd] for h in range(nh)]
            yacc_ref[b * tb + c * chunk:b * tb + (c + 1) * chunk, :] = jnp.concatenate(ys, axis=1)

    y = yacc_ref[...]
    avg_blk = _block_matrix(gw, hd, 1.0 / hd)
    m = _split_dot_right(y, avg_blk, 2)
    d = y - m
    var = _split_dot_right(d * d, avg_blk, 2)
    yn = d * lax.rsqrt(var + GN_EPS) * gn_g + gn_b
    out = (yn + bonus) * gate
    for b in range(nb):
        y_ref[b] = out[b * tb:(b + 1) * tb]
    sn_ref[...] = s_ref[...]


def _rwkv(ub, shift_prev, state, mu, wl, vec, chunk):
    b, l, _ = ub.shape
    tb = min(l, 256)
    assert l % tb == 0 and tb % chunk == 0
    hd = RWKV_HEAD
    n = RWKV_HEADS * b
    st = jnp.swapaxes(state, 0, 1)
    zero = jnp.zeros_like(st)
    st = jnp.stack([jnp.concatenate([st[h], zero[h]] if h % 2 == 0 else [zero[h], st[h]], axis=-1)
                    for h in range(RWKV_HEADS)]).reshape(n, hd, 2 * hd)
    y, sn = pl.pallas_call(
        functools.partial(_rwkv_body, nb=b, tb=tb, chunk=chunk),
        grid=(l // tb,),
        in_specs=[
            pl.BlockSpec((b, tb, B_COLS), lambda j: (0, j, 0)),
            pl.BlockSpec((b, 1, B_COLS), lambda j: (0, 0, 0)),
            pl.BlockSpec((n, hd, 2 * hd), lambda j: (0, 0, 0)),
            pl.BlockSpec((1, B_COLS), lambda j: (0, 0)),
            pl.BlockSpec((128, 3 * GROUP_WIDTH), lambda j: (0, 0)),
            pl.BlockSpec((8, GROUP_WIDTH), lambda j: (0, 0)),
        ],
        out_specs=[
            pl.BlockSpec((b, tb, GROUP_WIDTH), lambda j: (0, j, 0)),
            pl.BlockSpec((n, hd, 2 * hd), lambda j: (0, 0, 0)),
        ],
        out_shape=[
            jax.ShapeDtypeStruct((b, l, GROUP_WIDTH), F32),
            jax.ShapeDtypeStruct((n, hd, 2 * hd), F32),
        ],
        scratch_shapes=[
            pltpu.VMEM((b, 1, B_COLS), F32),
            pltpu.VMEM((n, hd, 2 * hd), F32),
            pltpu.VMEM((b * tb, GROUP_WIDTH), F32),
        ],
        compiler_params=_params(("arbitrary",)),
        name="rwkv",
    )(ub, shift_prev, st, mu, wl, vec)
    sn = sn.reshape(RWKV_HEADS, b, hd, 2 * hd)
    sn = jnp.stack([sn[h, :, :, (h % 2) * hd:(h % 2 + 1) * hd] for h in range(RWKV_HEADS)], axis=1)
    return y, sn


def _outffn_body(x_ref, ya_ref, yb_ref, yc_ref, yd_ref, wo_ref, g_ref, wg_ref, wu_ref, wd_ref, o_ref):
    ycat = jnp.concatenate([ya_ref[...], yb_ref[...], yc_ref[...], yd_ref[...]], axis=-1).astype(BF16)
    x1 = x_ref[...] + _dot(ycat, wo_ref[...])
    ms = jnp.mean(x1 * x1, axis=-1, keepdims=True)
    hn = ((x1 * lax.rsqrt(ms + RMS_EPS)) * g_ref[...]).astype(BF16)
    hg = _dot(hn, wg_ref[...])
    hu = _dot(hn, wu_ref[...])
    act = (hg * _sigmoid(hg) * hu).astype(BF16)
    o_ref[...] = x1 + _dot(act, wd_ref[...])


def _outffn(x2d, ya, yb, yc, yd, wo, g, wg, wu, wd):
    t = x2d.shape[0]
    tm = min(t, 256)
    assert t % tm == 0
    row = lambda w: pl.BlockSpec((tm, w), lambda i: (i, 0))
    const = lambda shape: pl.BlockSpec(shape, lambda i: (0, 0), pipeline_mode=pl.Buffered(1))
    return pl.pallas_call(
        _outffn_body,
        grid=(t // tm,),
        in_specs=[
            row(D_MODEL), row(GROUP_WIDTH), row(GROUP_WIDTH), row(GROUP_WIDTH), row(GROUP_WIDTH),
            const((4 * GROUP_WIDTH, D_MODEL)),
            const((1, D_MODEL)),
            const((D_MODEL, D_FF)),
            const((D_MODEL, D_FF)),
            const((D_FF, D_MODEL)),
        ],
        out_specs=row(D_MODEL),
        out_shape=jax.ShapeDtypeStruct((t, D_MODEL), F32),
        compiler_params=_params(("parallel",)),
        name="outffn",
    )(x2d, ya, yb, yc, yd, wo, g, wg, wu, wd)


def _pad_rows(a, rows):
    return jnp.pad(a, ((0, rows - a.shape[0]), (0, 0)))


def _layer_weights(l, p):
    gw = GROUP_WIDTH
    wl = jnp.zeros((128, 3 * gw), F32)
    wl = wl.at[0:32, 0:gw].set(p['rwkv_w2'][l])
    wl = wl.at[32:64, gw:2 * gw].set(p['rwkv_a2'][l])
    wl = wl.at[64:128, 2 * gw:3 * gw].set(p['rwkv_g2'][l])
    pool_w = p['pool_w'][l]
    wbd = jnp.zeros((gw, gw), F32)
    pc = gw // len(POOL_WINDOWS)
    for g in range(len(POOL_WINDOWS)):
        wbd = wbd.at[g * pc:(g + 1) * pc, g * pc:(g + 1) * pc].set(pool_w[g])
    return dict(
        norm_mix_g=p['norm_mix_g'][l][None, :],
        w_in=p['w_in'][l].astype(BF16),
        conv_w=_pad_rows(p['conv_w'][l], CONV_PAD),
        conv_vec=_pad_rows(jnp.stack([p['conv_b'][l], p['conv_ln_g'][l], p['conv_ln_b'][l]]), 8),
        rwkv_mu=p['rwkv_mu'][l][None, :],
        rwkv_wl=wl.astype(BF16),
        rwkv_vec=_pad_rows(jnp.stack([
            p['rwkv_w0'][l], p['rwkv_a0'][l], p['rwkv_k_k'][l], p['rwkv_k_a'][l],
            p['rwkv_r_k'][l].reshape(gw), p['rwkv_gn_g'][l], p['rwkv_gn_b'][l]]), 8),
        pool_wbd=wbd.astype(BF16),
        pool_scale=p['pool_scale'][l][None, :],
        attn_qg=jnp.tile(p['attn_q_norm'][l], N_Q_HEADS)[None, :],
        attn_kg=jnp.tile(p['attn_k_norm'][l], N_KV_HEADS)[None, :],
        attn_sinks=p['attn_sinks'][l],
        w_out=p['w_out'][l].astype(BF16),
        norm_ffn_g=p['norm_ffn_g'][l][None, :],
        ffn_w_gate=p['ffn_w_gate'][l].astype(BF16),
        ffn_w_up=p['ffn_w_up'][l].astype(BF16),
        ffn_w_down=p['ffn_w_down'][l].astype(BF16),
    )


def _trunk(x, conv_hist, rwkv_state, shift_prev, pool_hist, k_cache, v_cache, pos0, chunk, weights):
    b, l, _ = x.shape
    has_past = conv_hist is not None
    kvw = N_KV_HEADS * HEAD_DIM
    x2d = x.reshape(b * l, D_MODEL)
    new = [[] for _ in range(6)]
    for li in range(DEPTH):
        w = weights[li]
        if has_past:
            ch = jnp.pad(conv_hist[li], ((0, 0), (CONV_PAD - CONV_HIST, 0), (0, 0)))
            rs = rwkv_state[li]
            sp = shift_prev[li][:, None, :]
            ph = jnp.pad(pool_hist[li], ((0, 0), (POOL_PAD - POOL_HIST, 0), (0, 0)))
            hk = k_cache[li].reshape(b, WINDOW, kvw)
            hv = v_cache[li].reshape(b, WINDOW, kvw)
        else:
            ch = jnp.zeros((b, CONV_PAD, GROUP_WIDTH), F32)
            rs = jnp.zeros((b, RWKV_HEADS, RWKV_HEAD, RWKV_HEAD), F32)
            sp = jnp.zeros((b, 1, B_COLS), F32)
            ph = jnp.zeros((b, POOL_PAD, GROUP_WIDTH), F32)
            hk = jnp.zeros((b, WINDOW, kvw), F32)
            hv = jnp.zeros((b, WINDOW, kvw), F32)
        ua, ub, uc, ud = _proj(x2d, w['norm_mix_g'], w['w_in'])
        ua = ua.reshape(b, l, A_COLS)
        ub = ub.reshape(b, l, B_COLS)
        uc = uc.reshape(b, l, C_COLS)
        ud = ud.reshape(b, l, D_COLS)
        ya, conv_tail = _conv(ua, ch, w['conv_w'], w['conv_vec'])
        yb, s_new = _rwkv(ub, sp, rs, w['rwkv_mu'], w['rwkv_wl'], w['rwkv_vec'], min(CHUNK, l))
        yc = _pool(uc, ph, w['pool_wbd'], w['pool_scale'], pos0)
        yd, k_tail, v_tail = _attn(ud, hk, hv, w['attn_qg'], w['attn_kg'], w['attn_sinks'], chunk, has_past)
        flat = lambda y: y.reshape(b * l, GROUP_WIDTH)
        x2d = _outffn(x2d, flat(ya), flat(yb), flat(yc), flat(yd), w['w_out'], w['norm_ffn_g'],
                      w['ffn_w_gate'], w['ffn_w_up'], w['ffn_w_down'])
        new[0].append(conv_tail[:, CONV_PAD - CONV_HIST:, :])
        new[1].append(s_new)
        new[2].append(ub[:, l - 1, :])
        new[3].append(uc[:, l - POOL_HIST:, :])
        new[4].append(k_tail.reshape(b, WINDOW, N_KV_HEADS, HEAD_DIM))
        new[5].append(v_tail.reshape(b, WINDOW, N_KV_HEADS, HEAD_DIM))
    return x2d.reshape(b, l, D_MODEL), tuple(jnp.stack(n) for n in new)


def kernel(x_prompt, x_sample, cache_conv, state_rwkv, state_rwkv_shift, cache_pool, cache_k, cache_v, norm_mix_g, w_in, conv_w, conv_b, conv_ln_g, conv_ln_b, rwkv_mu, rwkv_w0, rwkv_w2, rwkv_a0, rwkv_a2, rwkv_g2, rwkv_k_k, rwkv_k_a, rwkv_r_k, rwkv_gn_g, rwkv_gn_b, pool_w, pool_scale, attn_q_norm, attn_k_norm, attn_sinks, w_out, norm_ffn_g, ffn_w_gate, ffn_w_up, ffn_w_down):
    p = dict(
        norm_mix_g=norm_mix_g, w_in=w_in, conv_w=conv_w, conv_b=conv_b, conv_ln_g=conv_ln_g,
        conv_ln_b=conv_ln_b, rwkv_mu=rwkv_mu, rwkv_w0=rwkv_w0, rwkv_w2=rwkv_w2, rwkv_a0=rwkv_a0,
        rwkv_a2=rwkv_a2, rwkv_g2=rwkv_g2, rwkv_k_k=rwkv_k_k, rwkv_k_a=rwkv_k_a, rwkv_r_k=rwkv_r_k,
        rwkv_gn_g=rwkv_gn_g, rwkv_gn_b=rwkv_gn_b, pool_w=pool_w, pool_scale=pool_scale,
        attn_q_norm=attn_q_norm, attn_k_norm=attn_k_norm, attn_sinks=attn_sinks, w_out=w_out,
        norm_ffn_g=norm_ffn_g, ffn_w_gate=ffn_w_gate, ffn_w_up=ffn_w_up, ffn_w_down=ffn_w_down)
    weights = [_layer_weights(l, p) for l in range(DEPTH)]
    y_p, (conv_p, rwkv_p, shift_p, pool_p, k_p, v_p) = _trunk(
        x_prompt, None, None, None, None, None, None, 0, CHUNK, weights)
    y_s, (conv_s, rwkv_s, shift_s, pool_s, k_s, v_s) = _trunk(
        x_sample, cache_conv, state_rwkv, state_rwkv_shift, cache_pool, cache_k, cache_v,
        PAST_LEN, x_sample.shape[1], weights)
    return (y_p, y_s, conv_p, conv_s, rwkv_p, rwkv_s, shift_p, shift_s,
            pool_p, pool_s, k_p, k_s, v_p, v_s)
```

```python
import functools
import math

import jax
import jax.numpy as jnp
from jax import lax
from jax.experimental import pallas as pl
from jax.experimental.pallas import tpu as pltpu

F32 = jnp.float32
BF16 = jnp.bfloat16

D_MODEL = 1024
DEPTH = 2
PAST_LEN = 1024
CHUNK = 64
GROUP_WIDTH = 256
CONV_WIDTH = 31
CONV_HIST = CONV_WIDTH - 1
CONV_PAD = 32
RWKV_HEAD = 64
RWKV_HEADS = 4
POOL_WINDOWS = (2, 4, 8, 16)
POOL_HIST = 15
POOL_PAD = 16
HEAD_DIM = 64
N_Q_HEADS = 4
N_KV_HEADS = 2
WINDOW = 128
D_FF = 2816
A_COLS = 512
B_COLS = 896
C_COLS = 256
D_COLS = 512
IN_COLS = A_COLS + B_COLS + C_COLS + D_COLS
RMS_EPS = 1e-6
LN_EPS = 1e-5
GN_EPS = 64e-5
ATTN_SCALE = HEAD_DIM ** -0.5
NEG_INF = -1e30

VMEM_LIMIT_BYTES = 56 * 1024 * 1024
SUBLANES = 8


def _dot(a, b):
    return jnp.dot(a, b, preferred_element_type=F32)


def _bdot(a, b):
    return lax.dot_general(a, b, (((2,), (1,)), ((0,), (0,))), preferred_element_type=F32)


def _bdot_nt(a, b):
    return lax.dot_general(a, b, (((2,), (2,)), ((0,), (0,))), preferred_element_type=F32)


def _bdot_tn(a, b):
    return lax.dot_general(a, b, (((1,), (1,)), ((0,), (0,))), preferred_element_type=F32)


def _sigmoid(x):
    return 1.0 / (1.0 + jnp.exp(-x))


def _split_dot_right(x, m_bf16, terms):
    acc = None
    rem = x
    for i in range(terms):
        hi = rem.astype(BF16)
        d = _dot(hi, m_bf16)
        acc = d if acc is None else acc + d
        if i + 1 < terms:
            rem = rem - hi.astype(F32)
    return acc


def _split_dot_left(m_bf16, x, terms):
    acc = None
    rem = x
    for i in range(terms):
        hi = rem.astype(BF16)
        d = _dot(m_bf16, hi)
        acc = d if acc is None else acc + d
        if i + 1 < terms:
            rem = rem - hi.astype(F32)
    return acc


def _block_matrix(n, blk, value):
    sh = int(math.log2(blk))
    r = lax.shift_right_logical(lax.broadcasted_iota(jnp.int32, (n, n), 0), sh)
    c = lax.shift_right_logical(lax.broadcasted_iota(jnp.int32, (n, n), 1), sh)
    return jnp.where(r == c, value, 0.0).astype(BF16)


def _params(sem):
    return pltpu.CompilerParams(dimension_semantics=sem, vmem_limit_bytes=VMEM_LIMIT_BYTES)


def _layer_block(shape, layer, n_grid):
    zeros = (0,) * len(shape)
    if n_grid == 1:
        index_map = lambda i: (layer,) + zeros
    else:
        index_map = lambda i, j: (layer,) + zeros
    return pl.BlockSpec((None,) + tuple(shape), index_map, pipeline_mode=pl.Buffered(1))


def _phase_copies(ext_ref, sh_ref):
    n = sh_ref.shape[1]
    for s in range(1, SUBLANES):
        sh_ref[s - 1] = ext_ref[s:s + n, :]


def _shifted_rows(ext_ref, sh_ref, start, rb):
    a, s = divmod(start, SUBLANES)
    base = a * SUBLANES
    return ext_ref[base:base + rb, :] if s == 0 else sh_ref[s - 1, base:base + rb, :]


def _conv_rows(ext_ref, sh_ref, w_ref, vec_ref, r0, rb):
    off = CONV_PAD - CONV_HIST
    acc = jnp.zeros((rb, GROUP_WIDTH), F32)
    for j in range(CONV_WIDTH):
        acc = acc + w_ref[j:j + 1, :] * _shifted_rows(ext_ref, sh_ref, r0 + off + j, rb)
    acc = acc + vec_ref[0:1, :]
    mu = jnp.mean(acc, axis=-1, keepdims=True)
    d = acc - mu
    var = jnp.mean(d * d, axis=-1, keepdims=True)
    yn = d * lax.rsqrt(var + LN_EPS) * vec_ref[1:2, :] + vec_ref[2:3, :]
    return yn * _sigmoid(yn)


def _pool_rows(ext_ref, sh_ref, wbd_ref, sc_ref, r0, rb, pos_start):
    base = POOL_PAD + r0
    sums = []
    acc = None
    for i in range(max(POOL_WINDOWS)):
        sh = _shifted_rows(ext_ref, sh_ref, base - i, rb)
        acc = sh if acc is None else acc + sh
        if i + 1 in POOL_WINDOWS:
            sums.append(acc)
    pos = pos_start + r0 + lax.broadcasted_iota(jnp.int32, (rb, 1), 0)
    means = [s / jnp.minimum(w, pos + 1).astype(F32) for s, w in zip(sums, POOL_WINDOWS)]
    lane = lax.broadcasted_iota(jnp.int32, (1, GROUP_WIDTH), 1)
    mean = jnp.where(lane < 64, means[0],
                     jnp.where(lane < 128, means[1], jnp.where(lane < 192, means[2], means[3])))
    d = mean - ext_ref[base:base + rb, :]
    return _dot(d.astype(BF16), wbd_ref[...]) * sc_ref[...]


def _projmix_body(x_ref, g_ref, w_ref, chist_ref, cw_ref, cvec_ref, phist_ref, pwbd_ref, psc_ref,
                  ub_ref, ud_ref, ya_ref, yc_ref, ctail_ref, ptail_ref,
                  cext_ref, csh_ref, pext_ref, psh_ref, *, nb, tl, rb, pos0):
    li = pl.program_id(1)

    @pl.when(li == 0)
    def _():
        cext_ref[:, 0:CONV_PAD, :] = chist_ref[...]
        pext_ref[:, 0:POOL_PAD, :] = phist_ref[...]

    x = x_ref[...].reshape(nb * tl, D_MODEL)
    ms = jnp.mean(x * x, axis=-1, keepdims=True)
    xn = ((x * lax.rsqrt(ms + RMS_EPS)) * g_ref[...]).astype(BF16)
    b0 = A_COLS
    c0 = A_COLS + B_COLS
    d0 = c0 + C_COLS
    ua = _dot(xn, w_ref[:, 0:A_COLS])
    uc = _dot(xn, w_ref[:, c0:d0])
    ub_ref[...] = _dot(xn, w_ref[:, b0:c0]).reshape(nb, tl, B_COLS)
    ud_ref[...] = _dot(xn, w_ref[:, d0:IN_COLS]).reshape(nb, tl, D_COLS)
    glu = ua[:, :GROUP_WIDTH] * _sigmoid(ua[:, GROUP_WIDTH:])
    for b in range(nb):
        cext = cext_ref.at[b]
        pext = pext_ref.at[b]
        cext[CONV_PAD:CONV_PAD + tl, :] = glu[b * tl:(b + 1) * tl]
        pext[POOL_PAD:POOL_PAD + tl, :] = uc[b * tl:(b + 1) * tl]
        _phase_copies(cext, csh_ref)
        _phase_copies(pext, psh_ref)
        for r0 in range(0, tl, rb):
            ya_ref[b, r0:r0 + rb, :] = _conv_rows(cext, csh_ref, cw_ref, cvec_ref, r0, rb).astype(BF16)
            yc_ref[b, r0:r0 + rb, :] = _pool_rows(pext, psh_ref, pwbd_ref, psc_ref, r0, rb,
                                                  pos0 + li * tl).astype(BF16)
        ctail = cext[tl:tl + CONV_PAD, :]
        ptail = pext[tl:tl + POOL_PAD, :]
        ctail_ref[b] = ctail
        ptail_ref[b] = ptail
        cext[0:CONV_PAD, :] = ctail
        pext[0:POOL_PAD, :] = ptail


def _projmix(x, layer, g, w_in, chist, conv_w, conv_vec, phist, pool_wbd, pool_scale, pos0):
    b, l, _ = x.shape
    nb = b if l <= 64 else 1
    tl = min(l, 512)
    rb = min(tl, 64)
    assert l % tl == 0 and tl % rb == 0 and tl >= CONV_PAD and b % nb == 0
    gw = GROUP_WIDTH
    tile = lambda w: pl.BlockSpec((nb, tl, w), lambda i, j: (i, j, 0))
    per_seq = lambda r: pl.BlockSpec((nb, r, gw), lambda i, j: (i, 0, 0))
    const = lambda shape: pl.BlockSpec(shape, lambda i, j: (0,) * len(shape))
    return pl.pallas_call(
        functools.partial(_projmix_body, nb=nb, tl=tl, rb=rb, pos0=pos0),
        grid=(b // nb, l // tl),
        in_specs=[
            tile(D_MODEL),
            const((1, D_MODEL)),
            _layer_block((D_MODEL, IN_COLS), layer, 2),
            per_seq(CONV_PAD), const((CONV_PAD, gw)), const((8, gw)),
            per_seq(POOL_PAD), const((gw, gw)), const((1, gw)),
        ],
        out_specs=[tile(B_COLS), tile(D_COLS), tile(gw), tile(gw), per_seq(CONV_PAD), per_seq(POOL_PAD)],
        out_shape=[
            jax.ShapeDtypeStruct((b, l, B_COLS), F32),
            jax.ShapeDtypeStruct((b, l, D_COLS), F32),
            jax.ShapeDtypeStruct((b, l, gw), BF16),
            jax.ShapeDtypeStruct((b, l, gw), BF16),
            jax.ShapeDtypeStruct((b, CONV_PAD, gw), F32),
            jax.ShapeDtypeStruct((b, POOL_PAD, gw), F32),
        ],
        scratch_shapes=[
            pltpu.VMEM((nb, CONV_PAD + tl, gw), F32),
            pltpu.VMEM((SUBLANES - 1, CONV_PAD + tl - SUBLANES, gw), F32),
            pltpu.VMEM((nb, POOL_PAD + tl, gw), F32),
            pltpu.VMEM((SUBLANES - 1, POOL_PAD + tl - SUBLANES, gw), F32),
        ],
        compiler_params=_params(("parallel", "arbitrary")),
        name="projmix",
    )(x, g, w_in, chist, conv_w, conv_vec, phist, pool_wbd, pool_scale)


def _attn_body(sink_ref, u_ref, hk_ref, hv_ref, qg_ref, kg_ref, y_ref, kt_ref, vt_ref,
               kh_ref, vh_ref, *, nb, tq, chunk, hist_valid):
    li = pl.program_id(0)
    hd = HEAD_DIM
    rows = nb * tq
    ncq = tq // chunk
    kw = WINDOW + chunk

    @pl.when(li == 0)
    def _():
        kh_ref[...] = hk_ref[...]
        vh_ref[...] = hv_ref[...]

    u = u_ref[...].reshape(rows, D_COLS)
    q = u[:, 0:256]
    k = u[:, 256:384]
    v = u[:, 384:512]
    inv = 1.0 / hd
    qms = _split_dot_right(q * q, _block_matrix(256, hd, inv), 2)
    kms = _split_dot_right(k * k, _block_matrix(128, hd, inv), 2)
    qn = (q * lax.rsqrt(qms + RMS_EPS)) * (qg_ref[...] * ATTN_SCALE)
    kn = (k * lax.rsqrt(kms + RMS_EPS)) * kg_ref[...]
    kcat = jnp.concatenate([kh_ref[...], kn.reshape(nb, tq, 2 * hd)], axis=1)
    vcat = jnp.concatenate([vh_ref[...], v.reshape(nb, tq, 2 * hd)], axis=1)
    ktail = kcat[:, tq:tq + WINDOW, :]
    vtail = vcat[:, tq:tq + WINDOW, :]
    kt_ref[...] = ktail
    vt_ref[...] = vtail
    kh_ref[...] = ktail
    vh_ref[...] = vtail
    kcat_b = kcat.astype(BF16)
    vcat_b = vcat.astype(BF16)

    slot_hi = lax.broadcasted_iota(jnp.int32, (1, 2 * hd), 1) >= hd
    q_tiles = []
    for h in range(N_Q_HEADS):
        col = qn[:, (h // 2) * 2 * hd:(h // 2 + 1) * 2 * hd]
        g = h // (N_Q_HEADS // N_KV_HEADS)
        if h % 2 != g:
            col = pltpu.roll(col, hd, axis=1)
        q_tiles.append(jnp.where(slot_hi if g == 1 else jnp.logical_not(slot_hi), col, 0.0).astype(BF16))
    pairs = [(b, c) for b in range(nb) for c in range(ncq)]
    qs = jnp.stack([jnp.concatenate([t[b * tq + c * chunk:b * tq + (c + 1) * chunk] for t in q_tiles], axis=0)
                    for b, c in pairs])
    ks = jnp.stack([kcat_b[b, c * chunk:c * chunk + kw] for b, c in pairs])
    vs = jnp.stack([vcat_b[b, c * chunk:c * chunk + kw] for b, c in pairs])
    s = _bdot_nt(qs, ks)
    if not hist_valid:
        assert ncq & (ncq - 1) == 0
        cidx = jnp.bitwise_and(lax.broadcasted_iota(jnp.int32, (len(pairs), 1, kw), 0), ncq - 1) * chunk
        kpos = lax.broadcasted_iota(jnp.int32, (len(pairs), 1, kw), 2) + cidx + (li * tq - WINDOW)
        s = jnp.where(kpos >= 0, s, NEG_INF)
    hrow = lax.broadcasted_iota(jnp.int32, (1, N_Q_HEADS * chunk, 1), 1)
    sk = jnp.full((1, N_Q_HEADS * chunk, 1), sink_ref[N_Q_HEADS - 1], F32)
    for h in range(N_Q_HEADS - 2, -1, -1):
        sk = jnp.where(hrow < (h + 1) * chunk, sink_ref[h], sk)
    m = jnp.maximum(jnp.max(s, axis=-1, keepdims=True), sk)
    p = jnp.exp(s - m)
    den = jnp.sum(p, axis=-1, keepdims=True) + jnp.exp(sk - m)
    o = _bdot(p.astype(BF16), vs) / den
    lo = jnp.logical_not(slot_hi)
    for i, (b, c) in enumerate(pairs):
        oc = o[i]
        col0 = jnp.where(lo, oc[0:chunk], pltpu.roll(oc[chunk:2 * chunk], hd, axis=1))
        col1 = jnp.where(lo, pltpu.roll(oc[2 * chunk:3 * chunk], hd, axis=1), oc[3 * chunk:4 * chunk])
        y_ref[b, c * chunk:(c + 1) * chunk, :] = jnp.concatenate([col0, col1], axis=1).astype(BF16)


def _attn(ud, hk, hv, qg, kg, sinks, chunk, hist_valid):
    b, l, _ = ud.shape
    tq = min(l, 256)
    assert l % tq == 0 and tq % chunk == 0
    kvw = N_KV_HEADS * HEAD_DIM
    return pl.pallas_call(
        functools.partial(_attn_body, nb=b, tq=tq, chunk=chunk, hist_valid=hist_valid),
        grid=(l // tq,),
        in_specs=[
            pl.BlockSpec(memory_space=pltpu.SMEM),
            pl.BlockSpec((b, tq, D_COLS), lambda j: (0, j, 0)),
            pl.BlockSpec((b, WINDOW, kvw), lambda j: (0, 0, 0)),
            pl.BlockSpec((b, WINDOW, kvw), lambda j: (0, 0, 0)),
            pl.BlockSpec((1, GROUP_WIDTH), lambda j: (0, 0)),
            pl.BlockSpec((1, kvw), lambda j: (0, 0)),
        ],
        out_specs=[
            pl.BlockSpec((b, tq, GROUP_WIDTH), lambda j: (0, j, 0)),
            pl.BlockSpec((b, WINDOW, kvw), lambda j: (0, 0, 0)),
            pl.BlockSpec((b, WINDOW, kvw), lambda j: (0, 0, 0)),
        ],
        out_shape=[
            jax.ShapeDtypeStruct((b, l, GROUP_WIDTH), BF16),
            jax.ShapeDtypeStruct((b, WINDOW, kvw), F32),
            jax.ShapeDtypeStruct((b, WINDOW, kvw), F32),
        ],
        scratch_shapes=[pltpu.VMEM((b, WINDOW, kvw), F32), pltpu.VMEM((b, WINDOW, kvw), F32)],
        compiler_params=_params(("arbitrary",)),
        name="attn",
    )(sinks, ud, hk, hv, qg, kg)


def _rwkv_body(u_ref, sp_ref, s0_ref, mu_ref, wl_ref, vec_ref, y_ref, sn_ref,
               prev_ref, s_ref, yacc_ref, *, nb, tb, chunk):
    li = pl.program_id(0)
    gw = GROUP_WIDTH
    hd = RWKV_HEAD
    nh = RWKV_HEADS
    rows = nb * tb
    nc = tb // chunk

    @pl.when(li == 0)
    def _():
        prev_ref[...] = sp_ref[...]
        s_ref[...] = s0_ref[...]

    row = lax.broadcasted_iota(jnp.int32, (tb, 1), 0)
    mu = mu_ref[...]
    xs_parts = []
    for b in range(nb):
        ub = u_ref[b]
        prev = jnp.where(row == 0, prev_ref[b], pltpu.roll(ub, 1, axis=0))
        prev_ref[b] = ub[tb - 1:tb, :]
        xs_parts.append(ub + mu * (prev - ub))
    xs = jnp.concatenate(xs_parts, axis=0)
    r = xs[:, 0:gw]
    k = xs[:, gw:2 * gw]
    v = xs[:, 2 * gw:3 * gw]
    lat = xs[:, 3 * gw:B_COLS]
    lane_lat = lax.broadcasted_iota(jnp.int32, (1, B_COLS - 3 * gw), 1)
    act = jnp.where(lane_lat < 32, jnp.tanh(lat), jnp.where(lane_lat < 64, lat, _sigmoid(lat)))
    lo = _dot(act.astype(BF16), wl_ref[...])
    w0 = vec_ref[0:1, :]
    a0 = vec_ref[1:2, :]
    k_k = vec_ref[2:3, :]
    k_a = vec_ref[3:4, :]
    r_k = vec_ref[4:5, :]
    gn_g = vec_ref[5:6, :]
    gn_b = vec_ref[6:7, :]
    z = -(w0 + lo[:, 0:gw])
    softplus = jnp.maximum(z, 0.0) + jnp.log(1.0 + jnp.exp(-jnp.abs(z)))
    logw = -jnp.exp(-softplus - 0.5)
    a_rate = _sigmoid(a0 + lo[:, gw:2 * gw])
    gate = lo[:, 2 * gw:3 * gw]
    ones_blk = _block_matrix(gw, hd, 1.0)
    kk = k * k_k
    kk = kk / jnp.maximum(jnp.sqrt(_split_dot_right(kk * kk, ones_blk, 2)), 1e-12)
    k_mod = k * (1.0 + (a_rate - 1.0) * k_a)
    b_v = kk * a_rate
    bonus = _split_dot_right(r * k_mod * r_k, ones_blk, 2) * v

    grp = min(rows, 256)
    gi = lax.broadcasted_iota(jnp.int32, (grp, grp), 0)
    gj = lax.broadcasted_iota(jnp.int32, (grp, grp), 1)
    csh = int(math.log2(chunk))
    tri = jnp.where(jnp.logical_and(lax.shift_right_logical(gi, csh) == lax.shift_right_logical(gj, csh),
                                    gj <= gi), 1.0, 0.0).astype(BF16)
    cum = jnp.concatenate([_split_dot_left(tri, logw[g0:g0 + grp], 3) for g0 in range(0, rows, grp)], axis=0)
    cum3 = cum.reshape(nb * nc, chunk, gw)
    cum_c = cum3[:, chunk - 1:chunk, :]
    e_end = jnp.exp(cum_c - cum3).reshape(rows, gw)
    w_c = jnp.exp(cum_c)
    e_neg = jnp.exp(-cum)
    dense = dict(
        a=-kk * jnp.exp(cum - logw),
        r=r * jnp.exp(cum),
        bt=b_v * e_neg,
        kt=k_mod * e_neg,
        bh=b_v * e_end,
        kh=k_mod * e_end,
    )
    slot_hi = lax.broadcasted_iota(jnp.int32, (1, 2 * hd), 1) >= hd

    def head_tile(x, h, own_slot):
        col = x[:, (h // 2) * 2 * hd:(h // 2 + 1) * 2 * hd]
        keep = slot_hi if (h % 2 == 1) == own_slot else jnp.logical_not(slot_hi)
        return jnp.where(keep, col, 0.0)

    v_sw = jnp.concatenate([pltpu.roll(v[:, j * 2 * hd:(j + 1) * 2 * hd], hd, axis=1) for j in range(nh // 2)],
                           axis=1)
    tiles = {name: [head_tile(x, h, True) for h in range(nh)] for name, x in dense.items()}
    tiles_b = {name: [t.astype(BF16) for t in tiles[name]] for name in ("a", "bt", "kt", "bh", "kh")}
    v_tiles = [head_tile(v_sw, h, False).astype(BF16) for h in range(nh)]

    def blocks(per_head, c):
        return jnp.stack([per_head[h][b * tb + c * chunk:b * tb + (c + 1) * chunk]
                          for h in range(nh) for b in range(nb)])

    n = nh * nb
    ri = lax.broadcasted_iota(jnp.int32, (2 * chunk, 2 * chunk), 0)
    ci = jnp.bitwise_and(lax.broadcasted_iota(jnp.int32, (2 * chunk, 2 * chunk), 1), chunk - 1)
    gmask = ci < jnp.bitwise_and(ri, chunk - 1) + lax.shift_right_logical(ri, csh)
    zeros_c = jnp.zeros((n, chunk, 2 * hd), BF16)
    n_sq = int(math.log2(chunk))
    for c in range(nc):
        a_b = blocks(tiles_b["a"], c)
        r_f = blocks(tiles["r"], c)
        v_b = blocks(v_tiles, c)
        ar = jnp.concatenate([a_b, r_f.astype(BF16)], axis=1)
        bk = jnp.concatenate([blocks(tiles_b["bt"], c), blocks(tiles_b["kt"], c)], axis=1)
        bhkh = jnp.concatenate([blocks(tiles_b["bh"], c), blocks(tiles_b["kh"], c)], axis=1)
        g = jnp.where(gmask, _bdot_nt(ar, bk), 0.0)
        g_top = g[:, :chunk, :]
        g_bot = g[:, chunk:, :].astype(BF16)
        w = a_b.astype(F32) + _bdot(g_top.astype(BF16), jnp.concatenate([zeros_c, v_b], axis=1))
        p = g_top[:, :, :chunk]
        for i in range(n_sq):
            pb = p.astype(BF16)
            if i + 1 < n_sq:
                res = _bdot(pb, jnp.concatenate([w.astype(BF16), pb], axis=2))
                w = w + res[:, :, :2 * hd]
                p = res[:, :, 2 * hd:]
            else:
                w = w + _bdot(pb, w.astype(BF16))
        xv = jnp.concatenate([w.astype(BF16), v_b], axis=1)
        ry = _bdot(g_bot, xv)
        mp = _bdot_tn(xv, bhkh)
        s_old = s_ref[...]
        s_b = s_old.astype(BF16)
        y_nt = _bdot_nt((ry + r_f).astype(BF16), s_b)
        wc = jnp.stack([w_c[b * nc + c][:, (h // 2) * 2 * hd:(h // 2 + 1) * 2 * hd]
                        for h in range(nh) for b in range(nb)])
        psi = jnp.concatenate(
            [mp[h * nb:(h + 1) * nb, (1 - h % 2) * hd:(2 - h % 2) * hd, :] for h in range(nh)], axis=0)
        s_ref[...] = s_old * wc + _bdot(s_b, mp.astype(BF16)) + psi
        for b in range(nb):
            ys = [y_nt[h * nb + b] + ry[h * nb + b][:, (1 - h % 2) * hd:(2 - h % 2) * hd] for h in range(nh)]
            yacc_ref[b * tb + c * chunk:b * tb + (c + 1) * chunk, :] = jnp.concatenate(ys, axis=1)

    y = yacc_ref[...]
    avg_blk = _block_matrix(gw, hd, 1.0 / hd)
    m = _split_dot_right(y, avg_blk, 2)
    d = y - m
    var = _split_dot_right(d * d, avg_blk, 2)
    yn = d * lax.rsqrt(var + GN_EPS) * gn_g + gn_b
    out = ((yn + bonus) * gate).astype(BF16)
    for b in range(nb):
        y_ref[b] = out[b * tb:(b + 1) * tb]
    sn_ref[...] = s_ref[...]


def _rwkv(ub, shift_prev, state, mu, wl, vec, chunk):
    b, l, _ = ub.shape
    tb = min(l, 256)
    assert l % tb == 0 and tb % chunk == 0
    hd = RWKV_HEAD
    n = RWKV_HEADS * b
    st = jnp.swapaxes(state, 0, 1)
    zero = jnp.zeros_like(st)
    st = jnp.stack([jnp.concatenate([st[h], zero[h]] if h % 2 == 0 else [zero[h], st[h]], axis=-1)
                    for h in range(RWKV_HEADS)]).reshape(n, hd, 2 * hd)
    y, sn = pl.pallas_call(
        functools.partial(_rwkv_body, nb=b, tb=tb, chunk=chunk),
        grid=(l // tb,),
        in_specs=[
            pl.BlockSpec((b, tb, B_COLS), lambda j: (0, j, 0)),
            pl.BlockSpec((b, 1, B_COLS), lambda j: (0, 0, 0)),
            pl.BlockSpec((n, hd, 2 * hd), lambda j: (0, 0, 0)),
            pl.BlockSpec((1, B_COLS), lambda j: (0, 0)),
            pl.BlockSpec((128, 3 * GROUP_WIDTH), lambda j: (0, 0)),
            pl.BlockSpec((8, GROUP_WIDTH), lambda j: (0, 0)),
        ],
        out_specs=[
            pl.BlockSpec((b, tb, GROUP_WIDTH), lambda j: (0, j, 0)),
            pl.BlockSpec((n, hd, 2 * hd), lambda j: (0, 0, 0)),
        ],
        out_shape=[
            jax.ShapeDtypeStruct((b, l, GROUP_WIDTH), BF16),
            jax.ShapeDtypeStruct((n, hd, 2 * hd), F32),
        ],
        scratch_shapes=[
            pltpu.VMEM((b, 1, B_COLS), F32),
            pltpu.VMEM((n, hd, 2 * hd), F32),
            pltpu.VMEM((b * tb, GROUP_WIDTH), F32),
        ],
        compiler_params=_params(("arbitrary",)),
        name="rwkv",
    )(ub, shift_prev, st, mu, wl, vec)
    sn = sn.reshape(RWKV_HEADS, b, hd, 2 * hd)
    sn = jnp.stack([sn[h, :, :, (h % 2) * hd:(h % 2 + 1) * hd] for h in range(RWKV_HEADS)], axis=1)
    return y, sn


def _outffn_body(x_ref, ya_ref, yb_ref, yc_ref, yd_ref, wo_ref, g_ref, wg_ref, wu_ref, wd_ref, o_ref):
    ycat = jnp.concatenate([ya_ref[...], yb_ref[...], yc_ref[...], yd_ref[...]], axis=-1)
    x1 = x_ref[...] + _dot(ycat, wo_ref[...])
    ms = jnp.mean(x1 * x1, axis=-1, keepdims=True)
    hn = ((x1 * lax.rsqrt(ms + RMS_EPS)) * g_ref[...]).astype(BF16)
    hg = _dot(hn, wg_ref[...])
    hu = _dot(hn, wu_ref[...])
    act = (hg * _sigmoid(hg) * hu).astype(BF16)
    o_ref[...] = x1 + _dot(act, wd_ref[...])


def _outffn(x2d, ya, yb, yc, yd, layer, wo, g, wg, wu, wd):
    t = x2d.shape[0]
    tm = min(t, 256)
    assert t % tm == 0
    row = lambda w: pl.BlockSpec((tm, w), lambda i: (i, 0))
    return pl.pallas_call(
        _outffn_body,
        grid=(t // tm,),
        in_specs=[
            row(D_MODEL), row(GROUP_WIDTH), row(GROUP_WIDTH), row(GROUP_WIDTH), row(GROUP_WIDTH),
            _layer_block((4 * GROUP_WIDTH, D_MODEL), layer, 1),
            pl.BlockSpec((1, D_MODEL), lambda i: (0, 0)),
            _layer_block((D_MODEL, D_FF), layer, 1),
            _layer_block((D_MODEL, D_FF), layer, 1),
            _layer_block((D_FF, D_MODEL), layer, 1),
        ],
        out_specs=row(D_MODEL),
        out_shape=jax.ShapeDtypeStruct((t, D_MODEL), F32),
        compiler_params=_params(("parallel",)),
        name="outffn",
    )(x2d, ya, yb, yc, yd, wo, g, wg, wu, wd)


def _pad_rows(a, rows):
    return jnp.pad(a, ((0, rows - a.shape[0]), (0, 0)))


def _layer_weights(l, p):
    gw = GROUP_WIDTH
    wl = jnp.zeros((128, 3 * gw), F32)
    wl = wl.at[0:32, 0:gw].set(p['rwkv_w2'][l])
    wl = wl.at[32:64, gw:2 * gw].set(p['rwkv_a2'][l])
    wl = wl.at[64:128, 2 * gw:3 * gw].set(p['rwkv_g2'][l])
    pool_w = p['pool_w'][l]
    wbd = jnp.zeros((gw, gw), F32)
    pc = gw // len(POOL_WINDOWS)
    for g in range(len(POOL_WINDOWS)):
        wbd = wbd.at[g * pc:(g + 1) * pc, g * pc:(g + 1) * pc].set(pool_w[g])
    return dict(
        norm_mix_g=p['norm_mix_g'][l][None, :],
        conv_w=_pad_rows(p['conv_w'][l], CONV_PAD),
        conv_vec=_pad_rows(jnp.stack([p['conv_b'][l], p['conv_ln_g'][l], p['conv_ln_b'][l]]), 8),
        rwkv_mu=p['rwkv_mu'][l][None, :],
        rwkv_wl=wl.astype(BF16),
        rwkv_vec=_pad_rows(jnp.stack([
            p['rwkv_w0'][l], p['rwkv_a0'][l], p['rwkv_k_k'][l], p['rwkv_k_a'][l],
            p['rwkv_r_k'][l].reshape(gw), p['rwkv_gn_g'][l], p['rwkv_gn_b'][l]]), 8),
        pool_wbd=wbd.astype(BF16),
        pool_scale=p['pool_scale'][l][None, :],
        attn_qg=jnp.tile(p['attn_q_norm'][l], N_Q_HEADS)[None, :],
        attn_kg=jnp.tile(p['attn_k_norm'][l], N_KV_HEADS)[None, :],
        attn_sinks=p['attn_sinks'][l],
        norm_ffn_g=p['norm_ffn_g'][l][None, :],
    )


def _trunk(x, conv_hist, rwkv_state, shift_prev, pool_hist, k_cache, v_cache, pos0, chunk, weights, big):
    b, l, _ = x.shape
    has_past = conv_hist is not None
    kvw = N_KV_HEADS * HEAD_DIM
    new = [[] for _ in range(6)]
    for li in range(DEPTH):
        w = weights[li]
        if has_past:
            ch = jnp.pad(conv_hist[li], ((0, 0), (CONV_PAD - CONV_HIST, 0), (0, 0)))
            rs = rwkv_state[li]
            sp = shift_prev[li][:, None, :]
            ph = jnp.pad(pool_hist[li], ((0, 0), (POOL_PAD - POOL_HIST, 0), (0, 0)))
            hk = k_cache[li].reshape(b, WINDOW, kvw)
            hv = v_cache[li].reshape(b, WINDOW, kvw)
        else:
            ch = jnp.zeros((b, CONV_PAD, GROUP_WIDTH), F32)
            rs = jnp.zeros((b, RWKV_HEADS, RWKV_HEAD, RWKV_HEAD), F32)
            sp = jnp.zeros((b, 1, B_COLS), F32)
            ph = jnp.zeros((b, POOL_PAD, GROUP_WIDTH), F32)
            hk = jnp.zeros((b, WINDOW, kvw), F32)
            hv = jnp.zeros((b, WINDOW, kvw), F32)
        ub, ud, ya, yc, conv_tail, pool_tail = _projmix(
            x, li, w['norm_mix_g'], big['w_in'], ch, w['conv_w'], w['conv_vec'],
            ph, w['pool_wbd'], w['pool_scale'], pos0)
        yb, s_new = _rwkv(ub, sp, rs, w['rwkv_mu'], w['rwkv_wl'], w['rwkv_vec'], min(CHUNK, l))
        yd, k_tail, v_tail = _attn(ud, hk, hv, w['attn_qg'], w['attn_kg'], w['attn_sinks'], chunk, has_past)
        flat = lambda y: y.reshape(b * l, GROUP_WIDTH)
        x = _outffn(x.reshape(b * l, D_MODEL), flat(ya), flat(yb), flat(yc), flat(yd), li, big['w_out'],
                    w['norm_ffn_g'], big['ffn_w_gate'], big['ffn_w_up'], big['ffn_w_down']
                    ).reshape(b, l, D_MODEL)
        new[0].append(conv_tail[:, CONV_PAD - CONV_HIST:, :])
        new[1].append(s_new)
        new[2].append(ub[:, l - 1, :])
        new[3].append(pool_tail[:, POOL_PAD - POOL_HIST:, :])
        new[4].append(k_tail.reshape(b, WINDOW, N_KV_HEADS, HEAD_DIM))
        new[5].append(v_tail.reshape(b, WINDOW, N_KV_HEADS, HEAD_DIM))
    return x, tuple(jnp.stack(n) for n in new)


def kernel(x_prompt, x_sample, cache_conv, state_rwkv, state_rwkv_shift, cache_pool, cache_k, cache_v, norm_mix_g, w_in, conv_w, conv_b, conv_ln_g, conv_ln_b, rwkv_mu, rwkv_w0, rwkv_w2, rwkv_a0, rwkv_a2, rwkv_g2, rwkv_k_k, rwkv_k_a, rwkv_r_k, rwkv_gn_g, rwkv_gn_b, pool_w, pool_scale, attn_q_norm, attn_k_norm, attn_sinks, w_out, norm_ffn_g, ffn_w_gate, ffn_w_up, ffn_w_down):
    p = dict(
        norm_mix_g=norm_mix_g, conv_w=conv_w, conv_b=conv_b, conv_ln_g=conv_ln_g,
        conv_ln_b=conv_ln_b, rwkv_mu=rwkv_mu, rwkv_w0=rwkv_w0, rwkv_w2=rwkv_w2, rwkv_a0=rwkv_a0,
        rwkv_a2=rwkv_a2, rwkv_g2=rwkv_g2, rwkv_k_k=rwkv_k_k, rwkv_k_a=rwkv_k_a, rwkv_r_k=rwkv_r_k,
        rwkv_gn_g=rwkv_gn_g, rwkv_gn_b=rwkv_gn_b, pool_w=pool_w, pool_scale=pool_scale,
        attn_q_norm=attn_q_norm, attn_k_norm=attn_k_norm, attn_sinks=attn_sinks, norm_ffn_g=norm_ffn_g)
    weights = [_layer_weights(l, p) for l in range(DEPTH)]
    big = dict(w_in=w_in.astype(BF16), w_out=w_out.astype(BF16), ffn_w_gate=ffn_w_gate.astype(BF16),
               ffn_w_up=ffn_w_up.astype(BF16), ffn_w_down=ffn_w_down.astype(BF16))
    y_p, (conv_p, rwkv_p, shift_p, pool_p, k_p, v_p) = _trunk(
        x_prompt, None, None, None, None, None, None, 0, CHUNK, weights, big)
    y_s, (conv_s, rwkv_s, shift_s, pool_s, k_s, v_s) = _trunk(
        x_sample, cache_conv, state_rwkv, state_rwkv_shift, cache_pool, cache_k, cache_v,
        PAST_LEN, x_sample.shape[1], weights, big)
    return (y_p, y_s, conv_p, conv_s, rwkv_p, rwkv_s, shift_p, shift_s,
            pool_p, pool_s, k_p, k_s, v_p, v_s)
```

```python
import functools
import math

import jax
import jax.numpy as jnp
from jax import lax
from jax.experimental import pallas as pl
from jax.experimental.pallas import tpu as pltpu

F32 = jnp.float32
BF16 = jnp.bfloat16

D_MODEL = 1024
DEPTH = 2
PAST_LEN = 1024
CHUNK = 64
GROUP_WIDTH = 256
CONV_WIDTH = 31
CONV_HIST = CONV_WIDTH - 1
CONV_PAD = 32
RWKV_HEAD = 64
RWKV_HEADS = 4
POOL_WINDOWS = (2, 4, 8, 16)
POOL_HIST = 15
POOL_PAD = 16
HEAD_DIM = 64
N_Q_HEADS = 4
N_KV_HEADS = 2
WINDOW = 128
D_FF = 2816
A_COLS = 512
B_COLS = 896
C_COLS = 256
D_COLS = 512
IN_COLS = A_COLS + B_COLS + C_COLS + D_COLS
RMS_EPS = 1e-6
LN_EPS = 1e-5
GN_EPS = 64e-5
ATTN_SCALE = HEAD_DIM ** -0.5
NEG_INF = -1e30

VMEM_LIMIT_BYTES = 56 * 1024 * 1024
SUBLANES = 8


def _dot(a, b):
    return jnp.dot(a, b, preferred_element_type=F32)


def _bdot(a, b):
    return lax.dot_general(a, b, (((2,), (1,)), ((0,), (0,))), preferred_element_type=F32)


def _bdot_nt(a, b):
    return lax.dot_general(a, b, (((2,), (2,)), ((0,), (0,))), preferred_element_type=F32)


def _bdot_tn(a, b):
    return lax.dot_general(a, b, (((1,), (1,)), ((0,), (0,))), preferred_element_type=F32)


def _sigmoid(x):
    return 1.0 / (1.0 + jnp.exp(-x))


def _split_dot_right(x, m_bf16, terms):
    acc = None
    rem = x
    for i in range(terms):
        hi = rem.astype(BF16)
        d = _dot(hi, m_bf16)
        acc = d if acc is None else acc + d
        if i + 1 < terms:
            rem = rem - hi.astype(F32)
    return acc


def _split_dot_left(m_bf16, x, terms):
    acc = None
    rem = x
    for i in range(terms):
        hi = rem.astype(BF16)
        d = _dot(m_bf16, hi)
        acc = d if acc is None else acc + d
        if i + 1 < terms:
            rem = rem - hi.astype(F32)
    return acc


def _block_matrix(n, blk, value):
    sh = int(math.log2(blk))
    r = lax.shift_right_logical(lax.broadcasted_iota(jnp.int32, (n, n), 0), sh)
    c = lax.shift_right_logical(lax.broadcasted_iota(jnp.int32, (n, n), 1), sh)
    return jnp.where(r == c, value, 0.0).astype(BF16)


def _params(sem):
    return pltpu.CompilerParams(dimension_semantics=sem, vmem_limit_bytes=VMEM_LIMIT_BYTES)


def _layer_block(shape, layer, n_grid):
    zeros = (0,) * len(shape)
    if n_grid == 1:
        index_map = lambda i: (layer,) + zeros
    else:
        index_map = lambda i, j: (layer,) + zeros
    return pl.BlockSpec((None,) + tuple(shape), index_map, pipeline_mode=pl.Buffered(1))


def _phase_copies(ext_ref, sh_ref):
    n = sh_ref.shape[1]
    for s in range(1, SUBLANES):
        sh_ref[s - 1] = ext_ref[s:s + n, :]


def _shifted_rows(ext_ref, sh_ref, start, rb):
    a, s = divmod(start, SUBLANES)
    base = a * SUBLANES
    return ext_ref[base:base + rb, :] if s == 0 else sh_ref[s - 1, base:base + rb, :]


def _conv_rows(ext_ref, sh_ref, w_ref, vec_ref, r0, rb):
    off = CONV_PAD - CONV_HIST
    acc = jnp.zeros((rb, GROUP_WIDTH), F32)
    for j in range(CONV_WIDTH):
        acc = acc + w_ref[j:j + 1, :] * _shifted_rows(ext_ref, sh_ref, r0 + off + j, rb)
    acc = acc + vec_ref[0:1, :]
    mu = jnp.mean(acc, axis=-1, keepdims=True)
    d = acc - mu
    var = jnp.mean(d * d, axis=-1, keepdims=True)
    yn = d * lax.rsqrt(var + LN_EPS) * vec_ref[1:2, :] + vec_ref[2:3, :]
    return yn * _sigmoid(yn)


def _pool_rows(ext_ref, sh_ref, wbd_ref, sc_ref, r0, rb, pos_start):
    base = POOL_PAD + r0
    sums = []
    acc = None
    for i in range(max(POOL_WINDOWS)):
        sh = _shifted_rows(ext_ref, sh_ref, base - i, rb)
        acc = sh if acc is None else acc + sh
        if i + 1 in POOL_WINDOWS:
            sums.append(acc)
    pos = pos_start + r0 + lax.broadcasted_iota(jnp.int32, (rb, 1), 0)
    means = [s / jnp.minimum(w, pos + 1).astype(F32) for s, w in zip(sums, POOL_WINDOWS)]
    lane = lax.broadcasted_iota(jnp.int32, (1, GROUP_WIDTH), 1)
    mean = jnp.where(lane < 64, means[0],
                     jnp.where(lane < 128, means[1], jnp.where(lane < 192, means[2], means[3])))
    d = mean - ext_ref[base:base + rb, :]
    return _dot(d.astype(BF16), wbd_ref[...]) * sc_ref[...]


def _projmix_body(x_ref, g_ref, w_ref, chist_ref, cw_ref, cvec_ref, phist_ref, pwbd_ref, psc_ref,
                  ub_ref, ud_ref, ya_ref, yc_ref, ctail_ref, ptail_ref,
                  cext_ref, csh_ref, pext_ref, psh_ref, *, nb, tl, rb, pos0):
    li = pl.program_id(1)

    @pl.when(li == 0)
    def _():
        cext_ref[:, 0:CONV_PAD, :] = chist_ref[...]
        pext_ref[:, 0:POOL_PAD, :] = phist_ref[...]

    x = x_ref[...].reshape(nb * tl, D_MODEL)
    ms = jnp.mean(x * x, axis=-1, keepdims=True)
    xn = ((x * lax.rsqrt(ms + RMS_EPS)) * g_ref[...]).astype(BF16)
    b0 = A_COLS
    c0 = A_COLS + B_COLS
    d0 = c0 + C_COLS
    ua = _dot(xn, w_ref[:, 0:A_COLS])
    uc = _dot(xn, w_ref[:, c0:d0])
    ub_ref[...] = _dot(xn, w_ref[:, b0:c0]).reshape(nb, tl, B_COLS)
    ud_ref[...] = _dot(xn, w_ref[:, d0:IN_COLS]).reshape(nb, tl, D_COLS)
    glu = ua[:, :GROUP_WIDTH] * _sigmoid(ua[:, GROUP_WIDTH:])
    for b in range(nb):
        cext = cext_ref.at[b]
        pext = pext_ref.at[b]
        cext[CONV_PAD:CONV_PAD + tl, :] = glu[b * tl:(b + 1) * tl]
        pext[POOL_PAD:POOL_PAD + tl, :] = uc[b * tl:(b + 1) * tl]
        _phase_copies(cext, csh_ref)
        _phase_copies(pext, psh_ref)
        for r0 in range(0, tl, rb):
            ya_ref[b, r0:r0 + rb, :] = _conv_rows(cext, csh_ref, cw_ref, cvec_ref, r0, rb).astype(BF16)
            yc_ref[b, r0:r0 + rb, :] = _pool_rows(pext, psh_ref, pwbd_ref, psc_ref, r0, rb,
                                                  pos0 + li * tl).astype(BF16)
        ctail = cext[tl:tl + CONV_PAD, :]
        ptail = pext[tl:tl + POOL_PAD, :]
        ctail_ref[b] = ctail
        ptail_ref[b] = ptail
        cext[0:CONV_PAD, :] = ctail
        pext[0:POOL_PAD, :] = ptail


def _projmix(x, layer, g, w_in, chist, conv_w, conv_vec, phist, pool_wbd, pool_scale, pos0):
    b, l, _ = x.shape
    nb = b if l <= 64 else 1
    tl = min(l, 512)
    rb = min(tl, 64)
    assert l % tl == 0 and tl % rb == 0 and tl >= CONV_PAD and b % nb == 0
    gw = GROUP_WIDTH
    tile = lambda w: pl.BlockSpec((nb, tl, w), lambda i, j: (i, j, 0))
    per_seq = lambda r: pl.BlockSpec((nb, r, gw), lambda i, j: (i, 0, 0))
    const = lambda shape: pl.BlockSpec(shape, lambda i, j: (0,) * len(shape))
    return pl.pallas_call(
        functools.partial(_projmix_body, nb=nb, tl=tl, rb=rb, pos0=pos0),
        grid=(b // nb, l // tl),
        in_specs=[
            tile(D_MODEL),
            const((1, D_MODEL)),
            _layer_block((D_MODEL, IN_COLS), layer, 2),
            per_seq(CONV_PAD), const((CONV_PAD, gw)), const((8, gw)),
            per_seq(POOL_PAD), const((gw, gw)), const((1, gw)),
        ],
        out_specs=[tile(B_COLS), tile(D_COLS), tile(gw), tile(gw), per_seq(CONV_PAD), per_seq(POOL_PAD)],
        out_shape=[
            jax.ShapeDtypeStruct((b, l, B_COLS), F32),
            jax.ShapeDtypeStruct((b, l, D_COLS), F32),
            jax.ShapeDtypeStruct((b, l, gw), BF16),
            jax.ShapeDtypeStruct((b, l, gw), BF16),
            jax.ShapeDtypeStruct((b, CONV_PAD, gw), F32),
            jax.ShapeDtypeStruct((b, POOL_PAD, gw), F32),
        ],
        scratch_shapes=[
            pltpu.VMEM((nb, CONV_PAD + tl, gw), F32),
            pltpu.VMEM((SUBLANES - 1, CONV_PAD + tl - SUBLANES, gw), F32),
            pltpu.VMEM((nb, POOL_PAD + tl, gw), F32),
            pltpu.VMEM((SUBLANES - 1, POOL_PAD + tl - SUBLANES, gw), F32),
        ],
        compiler_params=_params(("parallel", "arbitrary")),
        name="projmix",
    )(x, g, w_in, chist, conv_w, conv_vec, phist, pool_wbd, pool_scale)


def _attn_body(sink_ref, u_ref, hk_ref, hv_ref, qg_ref, kg_ref, y_ref, kt_ref, vt_ref,
               kh_ref, vh_ref, *, nb, tq, chunk, hist_valid):
    li = pl.program_id(0)
    hd = HEAD_DIM
    rows = nb * tq
    ncq = tq // chunk
    kw = WINDOW + chunk

    @pl.when(li == 0)
    def _():
        kh_ref[...] = hk_ref[...]
        vh_ref[...] = hv_ref[...]

    u = u_ref[...].reshape(rows, D_COLS)
    q = u[:, 0:256]
    k = u[:, 256:384]
    v = u[:, 384:512]
    inv = 1.0 / hd
    qms = _split_dot_right(q * q, _block_matrix(256, hd, inv), 2)
    kms = _split_dot_right(k * k, _block_matrix(128, hd, inv), 2)
    qn = (q * lax.rsqrt(qms + RMS_EPS)) * (qg_ref[...] * ATTN_SCALE)
    kn = (k * lax.rsqrt(kms + RMS_EPS)) * kg_ref[...]
    kcat = jnp.concatenate([kh_ref[...], kn.reshape(nb, tq, 2 * hd)], axis=1)
    vcat = jnp.concatenate([vh_ref[...], v.reshape(nb, tq, 2 * hd)], axis=1)
    ktail = kcat[:, tq:tq + WINDOW, :]
    vtail = vcat[:, tq:tq + WINDOW, :]
    kt_ref[...] = ktail
    vt_ref[...] = vtail
    kh_ref[...] = ktail
    vh_ref[...] = vtail
    kcat_b = kcat.astype(BF16)
    vcat_b = vcat.astype(BF16)

    slot_hi = lax.broadcasted_iota(jnp.int32, (1, 2 * hd), 1) >= hd
    q_tiles = []
    for h in range(N_Q_HEADS):
        col = qn[:, (h // 2) * 2 * hd:(h // 2 + 1) * 2 * hd]
        g = h // (N_Q_HEADS // N_KV_HEADS)
        if h % 2 != g:
            col = pltpu.roll(col, hd, axis=1)
        q_tiles.append(jnp.where(slot_hi if g == 1 else jnp.logical_not(slot_hi), col, 0.0).astype(BF16))
    pairs = [(b, c) for c in range(ncq) for b in range(nb)]
    qs = jnp.stack([jnp.concatenate([t[b * tq + c * chunk:b * tq + (c + 1) * chunk] for t in q_tiles], axis=0)
                    for b, c in pairs])
    ks = jnp.stack([kcat_b[b, c * chunk:c * chunk + kw] for b, c in pairs])
    vs = jnp.stack([vcat_b[b, c * chunk:c * chunk + kw] for b, c in pairs])
    st = _bdot_nt(ks, qs)
    nq = N_Q_HEADS * chunk
    if not hist_valid:
        n_edge = min(ncq, WINDOW // chunk) * nb
        cpos = [c * chunk for _, c in pairs[:n_edge]]
        kpos = lax.broadcasted_iota(jnp.int32, (1, kw, nq), 1) + (li * tq - WINDOW)
        edge = jnp.concatenate([jnp.where(kpos + cp >= 0, st[i:i + 1], NEG_INF) for i, cp in enumerate(cpos)],
                               axis=0)
        st = jnp.concatenate([edge, st[n_edge:]], axis=0) if n_edge < len(pairs) else edge
    hlane = lax.broadcasted_iota(jnp.int32, (1, 1, nq), 2)
    sk = jnp.full((1, 1, nq), sink_ref[N_Q_HEADS - 1], F32)
    for h in range(N_Q_HEADS - 2, -1, -1):
        sk = jnp.where(hlane < (h + 1) * chunk, sink_ref[h], sk)
    m = jnp.maximum(jnp.max(st, axis=1, keepdims=True), sk)
    p = jnp.exp(st - m)
    den = jnp.sum(p, axis=1, keepdims=True) + jnp.exp(sk - m)
    pn = (p * (1.0 / den)).astype(BF16)
    o = _bdot_tn(pn, vs)
    lo = jnp.logical_not(slot_hi)
    for i, (b, c) in enumerate(pairs):
        oc = o[i]
        col0 = jnp.where(lo, oc[0:chunk], pltpu.roll(oc[chunk:2 * chunk], hd, axis=1))
        col1 = jnp.where(lo, pltpu.roll(oc[2 * chunk:3 * chunk], hd, axis=1), oc[3 * chunk:4 * chunk])
        y_ref[b, c * chunk:(c + 1) * chunk, :] = jnp.concatenate([col0, col1], axis=1).astype(BF16)


def _attn(ud, hk, hv, qg, kg, sinks, chunk, hist_valid):
    b, l, _ = ud.shape
    tq = min(l, 256)
    assert l % tq == 0 and tq % chunk == 0
    kvw = N_KV_HEADS * HEAD_DIM
    return pl.pallas_call(
        functools.partial(_attn_body, nb=b, tq=tq, chunk=chunk, hist_valid=hist_valid),
        grid=(l // tq,),
        in_specs=[
            pl.BlockSpec(memory_space=pltpu.SMEM),
            pl.BlockSpec((b, tq, D_COLS), lambda j: (0, j, 0)),
            pl.BlockSpec((b, WINDOW, kvw), lambda j: (0, 0, 0)),
            pl.BlockSpec((b, WINDOW, kvw), lambda j: (0, 0, 0)),
            pl.BlockSpec((1, GROUP_WIDTH), lambda j: (0, 0)),
            pl.BlockSpec((1, kvw), lambda j: (0, 0)),
        ],
        out_specs=[
            pl.BlockSpec((b, tq, GROUP_WIDTH), lambda j: (0, j, 0)),
            pl.BlockSpec((b, WINDOW, kvw), lambda j: (0, 0, 0)),
            pl.BlockSpec((b, WINDOW, kvw), lambda j: (0, 0, 0)),
        ],
        out_shape=[
            jax.ShapeDtypeStruct((b, l, GROUP_WIDTH), BF16),
            jax.ShapeDtypeStruct((b, WINDOW, kvw), F32),
            jax.ShapeDtypeStruct((b, WINDOW, kvw), F32),
        ],
        scratch_shapes=[pltpu.VMEM((b, WINDOW, kvw), F32), pltpu.VMEM((b, WINDOW, kvw), F32)],
        compiler_params=_params(("arbitrary",)),
        name="attn",
    )(sinks, ud, hk, hv, qg, kg)


def _rwkv_body(u_ref, sp_ref, s0_ref, mu_ref, wl_ref, vec_ref, y_ref, sn_ref,
               prev_ref, s_ref, yacc_ref, *, nb, tb, chunk):
    li = pl.program_id(0)
    gw = GROUP_WIDTH
    hd = RWKV_HEAD
    nh = RWKV_HEADS
    rows = nb * tb
    nc = tb // chunk

    @pl.when(li == 0)
    def _():
        prev_ref[...] = sp_ref[...]
        s_ref[...] = s0_ref[...]

    row = lax.broadcasted_iota(jnp.int32, (tb, 1), 0)
    mu = mu_ref[...]
    xs_parts = []
    for b in range(nb):
        ub = u_ref[b]
        prev = jnp.where(row == 0, prev_ref[b], pltpu.roll(ub, 1, axis=0))
        prev_ref[b] = ub[tb - 1:tb, :]
        xs_parts.append(ub + mu * (prev - ub))
    xs = jnp.concatenate(xs_parts, axis=0)
    r = xs[:, 0:gw]
    k = xs[:, gw:2 * gw]
    v = xs[:, 2 * gw:3 * gw]
    lat = xs[:, 3 * gw:B_COLS]
    lane_lat = lax.broadcasted_iota(jnp.int32, (1, B_COLS - 3 * gw), 1)
    act = jnp.where(lane_lat < 32, jnp.tanh(lat), jnp.where(lane_lat < 64, lat, _sigmoid(lat)))
    lo = _dot(act.astype(BF16), wl_ref[...])
    w0 = vec_ref[0:1, :]
    a0 = vec_ref[1:2, :]
    k_k = vec_ref[2:3, :]
    k_a = vec_ref[3:4, :]
    r_k = vec_ref[4:5, :]
    gn_g = vec_ref[5:6, :]
    gn_b = vec_ref[6:7, :]
    z = -(w0 + lo[:, 0:gw])
    softplus = jnp.maximum(z, 0.0) + jnp.log(1.0 + jnp.exp(-jnp.abs(z)))
    logw = -jnp.exp(-softplus - 0.5)
    a_rate = _sigmoid(a0 + lo[:, gw:2 * gw])
    gate = lo[:, 2 * gw:3 * gw]
    ones_blk = _block_matrix(gw, hd, 1.0)
    kk = k * k_k
    kk = kk * lax.rsqrt(jnp.maximum(_split_dot_right(kk * kk, ones_blk, 2), 1e-24))
    k_mod = k * (1.0 + (a_rate - 1.0) * k_a)
    b_v = kk * a_rate
    bonus = _split_dot_right(r * k_mod * r_k, ones_blk, 2) * v

    grp = min(rows, 256)
    gi = lax.broadcasted_iota(jnp.int32, (grp, grp), 0)
    gj = lax.broadcasted_iota(jnp.int32, (grp, grp), 1)
    csh = int(math.log2(chunk))
    tri = jnp.where(jnp.logical_and(lax.shift_right_logical(gi, csh) == lax.shift_right_logical(gj, csh),
                                    gj <= gi), 1.0, 0.0).astype(BF16)
    cum = jnp.concatenate([_split_dot_left(tri, logw[g0:g0 + grp], 3) for g0 in range(0, rows, grp)], axis=0)
    cum3 = cum.reshape(nb * nc, chunk, gw)
    cum_c = cum3[:, chunk - 1:chunk, :]
    e_end = jnp.exp(cum_c - cum3).reshape(rows, gw)
    w_c = jnp.exp(cum_c)
    e_neg = jnp.exp(-cum)
    dense = dict(
        a=-kk * jnp.exp(cum - logw),
        r=r * jnp.exp(cum),
        bt=b_v * e_neg,
        kt=k_mod * e_neg,
        bh=b_v * e_end,
        kh=k_mod * e_end,
    )
    slot_hi = lax.broadcasted_iota(jnp.int32, (1, 2 * hd), 1) >= hd

    def head_tile(x, h, own_slot):
        col = x[:, (h // 2) * 2 * hd:(h // 2 + 1) * 2 * hd]
        keep = slot_hi if (h % 2 == 1) == own_slot else jnp.logical_not(slot_hi)
        return jnp.where(keep, col, 0.0)

    v_sw = jnp.concatenate([pltpu.roll(v[:, j * 2 * hd:(j + 1) * 2 * hd], hd, axis=1) for j in range(nh // 2)],
                           axis=1)
    tiles = {name: [head_tile(x, h, True) for h in range(nh)] for name, x in dense.items()}
    tiles_b = {name: [t.astype(BF16) for t in tiles[name]] for name in ("a", "bt", "kt", "bh", "kh")}
    v_tiles = [head_tile(v_sw, h, False).astype(BF16) for h in range(nh)]

    def blocks(per_head, c):
        return jnp.stack([per_head[h][b * tb + c * chunk:b * tb + (c + 1) * chunk]
                          for h in range(nh) for b in range(nb)])

    n = nh * nb
    ri = lax.broadcasted_iota(jnp.int32, (2 * chunk, 2 * chunk), 0)
    ci = jnp.bitwise_and(lax.broadcasted_iota(jnp.int32, (2 * chunk, 2 * chunk), 1), chunk - 1)
    gmask = ci < jnp.bitwise_and(ri, chunk - 1) + lax.shift_right_logical(ri, csh)
    zeros_c = jnp.zeros((n, chunk, 2 * hd), BF16)
    n_sq = int(math.log2(chunk))
    for c in range(nc):
        a_b = blocks(tiles_b["a"], c)
        r_f = blocks(tiles["r"], c)
        v_b = blocks(v_tiles, c)
        ar = jnp.concatenate([a_b, r_f.astype(BF16)], axis=1)
        bk = jnp.concatenate([blocks(tiles_b["bt"], c), blocks(tiles_b["kt"], c)], axis=1)
        bhkh = jnp.concatenate([blocks(tiles_b["bh"], c), blocks(tiles_b["kh"], c)], axis=1)
        g = jnp.where(gmask, _bdot_nt(ar, bk), 0.0)
        g_top = g[:, :chunk, :]
        g_bot = g[:, chunk:, :].astype(BF16)
        w = a_b.astype(F32) + _bdot(g_top.astype(BF16), jnp.concatenate([zeros_c, v_b], axis=1))
        p = g_top[:, :, :chunk]
        for i in range(n_sq):
            pb = p.astype(BF16)
            if i + 1 < n_sq:
                res = _bdot(pb, jnp.concatenate([w.astype(BF16), pb], axis=2))
                w = w + res[:, :, :2 * hd]
                p = res[:, :, 2 * hd:]
            else:
                w = w + _bdot(pb, w.astype(BF16))
        xv = jnp.concatenate([w.astype(BF16), v_b], axis=1)
        ry = _bdot(g_bot, xv)
        mp = _bdot_tn(xv, bhkh)
        s_old = s_ref[...]
        s_b = s_old.astype(BF16)
        y_nt = _bdot_nt((ry + r_f).astype(BF16), s_b)
        wc = jnp.stack([w_c[b * nc + c][:, (h // 2) * 2 * hd:(h // 2 + 1) * 2 * hd]
                        for h in range(nh) for b in range(nb)])
        psi = jnp.concatenate(
            [mp[h * nb:(h + 1) * nb, (1 - h % 2) * hd:(2 - h % 2) * hd, :] for h in range(nh)], axis=0)
        s_ref[...] = s_old * wc + _bdot(s_b, mp.astype(BF16)) + psi
        for b in range(nb):
            ys = [y_nt[h * nb + b] + ry[h * nb + b][:, (1 - h % 2) * hd:(2 - h % 2) * hd] for h in range(nh)]
            yacc_ref[b * tb + c * chunk:b * tb + (c + 1) * chunk, :] = jnp.concatenate(ys, axis=1)

    y = yacc_ref[...]
    avg_blk = _block_matrix(gw, hd, 1.0 / hd)
    m = _split_dot_right(y, avg_blk, 2)
    d = y - m
    var = _split_dot_right(d * d, avg_blk, 2)
    yn = d * lax.rsqrt(var + GN_EPS) * gn_g + gn_b
    out = ((yn + bonus) * gate).astype(BF16)
    for b in range(nb):
        y_ref[b] = out[b * tb:(b + 1) * tb]
    sn_ref[...] = s_ref[...]


def _rwkv(ub, shift_prev, state, mu, wl, vec, chunk):
    b, l, _ = ub.shape
    tb = min(l, 256)
    assert l % tb == 0 and tb % chunk == 0
    hd = RWKV_HEAD
    n = RWKV_HEADS * b
    st = jnp.swapaxes(state, 0, 1)
    zero = jnp.zeros_like(st)
    st = jnp.stack([jnp.concatenate([st[h], zero[h]] if h % 2 == 0 else [zero[h], st[h]], axis=-1)
                    for h in range(RWKV_HEADS)]).reshape(n, hd, 2 * hd)
    y, sn = pl.pallas_call(
        functools.partial(_rwkv_body, nb=b, tb=tb, chunk=chunk),
        grid=(l // tb,),
        in_specs=[
            pl.BlockSpec((b, tb, B_COLS), lambda j: (0, j, 0)),
            pl.BlockSpec((b, 1, B_COLS), lambda j: (0, 0, 0)),
            pl.BlockSpec((n, hd, 2 * hd), lambda j: (0, 0, 0)),
            pl.BlockSpec((1, B_COLS), lambda j: (0, 0)),
            pl.BlockSpec((128, 3 * GROUP_WIDTH), lambda j: (0, 0)),
            pl.BlockSpec((8, GROUP_WIDTH), lambda j: (0, 0)),
        ],
        out_specs=[
            pl.BlockSpec((b, tb, GROUP_WIDTH), lambda j: (0, j, 0)),
            pl.BlockSpec((n, hd, 2 * hd), lambda j: (0, 0, 0)),
        ],
        out_shape=[
            jax.ShapeDtypeStruct((b, l, GROUP_WIDTH), BF16),
            jax.ShapeDtypeStruct((n, hd, 2 * hd), F32),
        ],
        scratch_shapes=[
            pltpu.VMEM((b, 1, B_COLS), F32),
            pltpu.VMEM((n, hd, 2 * hd), F32),
            pltpu.VMEM((b * tb, GROUP_WIDTH), F32),
        ],
        compiler_params=_params(("arbitrary",)),
        name="rwkv",
    )(ub, shift_prev, st, mu, wl, vec)
    sn = sn.reshape(RWKV_HEADS, b, hd, 2 * hd)
    sn = jnp.stack([sn[h, :, :, (h % 2) * hd:(h % 2 + 1) * hd] for h in range(RWKV_HEADS)], axis=1)
    return y, sn


def _outffn_body(x_ref, ya_ref, yb_ref, yc_ref, yd_ref, wo_ref, g_ref, wg_ref, wu_ref, wd_ref, o_ref):
    ycat = jnp.concatenate([ya_ref[...], yb_ref[...], yc_ref[...], yd_ref[...]], axis=-1)
    x1 = x_ref[...] + _dot(ycat, wo_ref[...])
    ms = jnp.mean(x1 * x1, axis=-1, keepdims=True)
    hn = ((x1 * lax.rsqrt(ms + RMS_EPS)) * g_ref[...]).astype(BF16)
    hg = _dot(hn, wg_ref[...])
    hu = _dot(hn, wu_ref[...])
    act = (hg * _sigmoid(hg) * hu).astype(BF16)
    o_ref[...] = x1 + _dot(act, wd_ref[...])


def _outffn(x2d, ya, yb, yc, yd, layer, wo, g, wg, wu, wd):
    t = x2d.shape[0]
    tm = min(t, 512)
    assert t % tm == 0
    row = lambda w: pl.BlockSpec((tm, w), lambda i: (i, 0))
    return pl.pallas_call(
        _outffn_body,
        grid=(t // tm,),
        in_specs=[
            row(D_MODEL), row(GROUP_WIDTH), row(GROUP_WIDTH), row(GROUP_WIDTH), row(GROUP_WIDTH),
            _layer_block((4 * GROUP_WIDTH, D_MODEL), layer, 1),
            pl.BlockSpec((1, D_MODEL), lambda i: (0, 0)),
            _layer_block((D_MODEL, D_FF), layer, 1),
            _layer_block((D_MODEL, D_FF), layer, 1),
            _layer_block((D_FF, D_MODEL), layer, 1),
        ],
        out_specs=row(D_MODEL),
        out_shape=jax.ShapeDtypeStruct((t, D_MODEL), F32),
        compiler_params=_params(("parallel",)),
        name="outffn",
    )(x2d, ya, yb, yc, yd, wo, g, wg, wu, wd)


def _pad_rows(a, rows):
    return jnp.pad(a, ((0, rows - a.shape[0]), (0, 0)))


def _layer_weights(l, p):
    gw = GROUP_WIDTH
    wl = jnp.zeros((128, 3 * gw), F32)
    wl = wl.at[0:32, 0:gw].set(p['rwkv_w2'][l])
    wl = wl.at[32:64, gw:2 * gw].set(p['rwkv_a2'][l])
    wl = wl.at[64:128, 2 * gw:3 * gw].set(p['rwkv_g2'][l])
    pool_w = p['pool_w'][l]
    wbd = jnp.zeros((gw, gw), F32)
    pc = gw // len(POOL_WINDOWS)
    for g in range(len(POOL_WINDOWS)):
        wbd = wbd.at[g * pc:(g + 1) * pc, g * pc:(g + 1) * pc].set(pool_w[g])
    return dict(
        norm_mix_g=p['norm_mix_g'][l][None, :],
        conv_w=_pad_rows(p['conv_w'][l], CONV_PAD),
        conv_vec=_pad_rows(jnp.stack([p['conv_b'][l], p['conv_ln_g'][l], p['conv_ln_b'][l]]), 8),
        rwkv_mu=p['rwkv_mu'][l][None, :],
        rwkv_wl=wl.astype(BF16),
        rwkv_vec=_pad_rows(jnp.stack([
            p['rwkv_w0'][l], p['rwkv_a0'][l], p['rwkv_k_k'][l], p['rwkv_k_a'][l],
            p['rwkv_r_k'][l].reshape(gw), p['rwkv_gn_g'][l], p['rwkv_gn_b'][l]]), 8),
        pool_wbd=wbd.astype(BF16),
        pool_scale=p['pool_scale'][l][None, :],
        attn_qg=jnp.tile(p['attn_q_norm'][l], N_Q_HEADS)[None, :],
        attn_kg=jnp.tile(p['attn_k_norm'][l], N_KV_HEADS)[None, :],
        attn_sinks=p['attn_sinks'][l],
        norm_ffn_g=p['norm_ffn_g'][l][None, :],
    )


def _trunk(x, conv_hist, rwkv_state, shift_prev, pool_hist, k_cache, v_cache, pos0, chunk, weights, big):
    b, l, _ = x.shape
    has_past = conv_hist is not None
    kvw = N_KV_HEADS * HEAD_DIM
    new = [[] for _ in range(6)]
    for li in range(DEPTH):
        w = weights[li]
        if has_past:
            ch = jnp.pad(conv_hist[li], ((0, 0), (CONV_PAD - CONV_HIST, 0), (0, 0)))
            rs = rwkv_state[li]
            sp = shift_prev[li][:, None, :]
            ph = jnp.pad(pool_hist[li], ((0, 0), (POOL_PAD - POOL_HIST, 0), (0, 0)))
            hk = k_cache[li].reshape(b, WINDOW, kvw)
            hv = v_cache[li].reshape(b, WINDOW, kvw)
        else:
            ch = jnp.zeros((b, CONV_PAD, GROUP_WIDTH), F32)
            rs = jnp.zeros((b, RWKV_HEADS, RWKV_HEAD, RWKV_HEAD), F32)
            sp = jnp.zeros((b, 1, B_COLS), F32)
            ph = jnp.zeros((b, POOL_PAD, GROUP_WIDTH), F32)
            hk = jnp.zeros((b, WINDOW, kvw), F32)
            hv = jnp.zeros((b, WINDOW, kvw), F32)
        ub, ud, ya, yc, conv_tail, pool_tail = _projmix(
            x, li, w['norm_mix_g'], big['w_in'], ch, w['conv_w'], w['conv_vec'],
            ph, w['pool_wbd'], w['pool_scale'], pos0)
        yb, s_new = _rwkv(ub, sp, rs, w['rwkv_mu'], w['rwkv_wl'], w['rwkv_vec'], min(CHUNK, l))
        yd, k_tail, v_tail = _attn(ud, hk, hv, w['attn_qg'], w['attn_kg'], w['attn_sinks'], chunk, has_past)
        flat = lambda y: y.reshape(b * l, GROUP_WIDTH)
        x = _outffn(x.reshape(b * l, D_MODEL), flat(ya), flat(yb), flat(yc), flat(yd), li, big['w_out'],
                    w['norm_ffn_g'], big['ffn_w_gate'], big['ffn_w_up'], big['ffn_w_down']
                    ).reshape(b, l, D_MODEL)
        new[0].append(conv_tail[:, CONV_PAD - CONV_HIST:, :])
        new[1].append(s_new)
        new[2].append(ub[:, l - 1, :])
        new[3].append(pool_tail[:, POOL_PAD - POOL_HIST:, :])
        new[4].append(k_tail.reshape(b, WINDOW, N_KV_HEADS, HEAD_DIM))
        new[5].append(v_tail.reshape(b, WINDOW, N_KV_HEADS, HEAD_DIM))
    return x, tuple(jnp.stack(n) for n in new)


def kernel(x_prompt, x_sample, cache_conv, state_rwkv, state_rwkv_shift, cache_pool, cache_k, cache_v, norm_mix_g, w_in, conv_w, conv_b, conv_ln_g, conv_ln_b, rwkv_mu, rwkv_w0, rwkv_w2, rwkv_a0, rwkv_a2, rwkv_g2, rwkv_k_k, rwkv_k_a, rwkv_r_k, rwkv_gn_g, rwkv_gn_b, pool_w, pool_scale, attn_q_norm, attn_k_norm, attn_sinks, w_out, norm_ffn_g, ffn_w_gate, ffn_w_up, ffn_w_down):
    p = dict(
        norm_mix_g=norm_mix_g, conv_w=conv_w, conv_b=conv_b, conv_ln_g=conv_ln_g,
        conv_ln_b=conv_ln_b, rwkv_mu=rwkv_mu, rwkv_w0=rwkv_w0, rwkv_w2=rwkv_w2, rwkv_a0=rwkv_a0,
        rwkv_a2=rwkv_a2, rwkv_g2=rwkv_g2, rwkv_k_k=rwkv_k_k, rwkv_k_a=rwkv_k_a, rwkv_r_k=rwkv_r_k,
        rwkv_gn_g=rwkv_gn_g, rwkv_gn_b=rwkv_gn_b, pool_w=pool_w, pool_scale=pool_scale,
        attn_q_norm=attn_q_norm, attn_k_norm=attn_k_norm, attn_sinks=attn_sinks, norm_ffn_g=norm_ffn_g)
    weights = [_layer_weights(l, p) for l in range(DEPTH)]
    big = dict(w_in=w_in.astype(BF16), w_out=w_out.astype(BF16), ffn_w_gate=ffn_w_gate.astype(BF16),
               ffn_w_up=ffn_w_up.astype(BF16), ffn_w_down=ffn_w_down.astype(BF16))
    y_p, (conv_p, rwkv_p, shift_p, pool_p, k_p, v_p) = _trunk(
        x_prompt, None, None, None, None, None, None, 0, CHUNK, weights, big)
    y_s, (conv_s, rwkv_s, shift_s, pool_s, k_s, v_s) = _trunk(
        x_sample, cache_conv, state_rwkv, state_rwkv_shift, cache_pool, cache_k, cache_v,
        PAST_LEN, x_sample.shape[1], weights, big)
    return (y_p, y_s, conv_p, conv_s, rwkv_p, rwkv_s, shift_p, shift_s,
            pool_p, pool_s, k_p, k_s, v_p, v_s)
```

```python
import functools
import math

import jax
import jax.numpy as jnp
from jax import lax
from jax.experimental import pallas as pl
from jax.experimental.pallas import tpu as pltpu

F32 = jnp.float32
BF16 = jnp.bfloat16

D_MODEL = 1024
DEPTH = 2
PAST_LEN = 1024
CHUNK = 64
GROUP_WIDTH = 256
CONV_WIDTH = 31
CONV_HIST = CONV_WIDTH - 1
CONV_PAD = 32
RWKV_HEAD = 64
RWKV_HEADS = 4
POOL_WINDOWS = (2, 4, 8, 16)
POOL_HIST = 15
POOL_PAD = 16
HEAD_DIM = 64
N_Q_HEADS = 4
N_KV_HEADS = 2
WINDOW = 128
D_FF = 2816
A_COLS = 512
B_COLS = 896
C_COLS = 256
D_COLS = 512
IN_COLS = A_COLS + B_COLS + C_COLS + D_COLS
RMS_EPS = 1e-6
LN_EPS = 1e-5
GN_EPS = 64e-5
ATTN_SCALE = HEAD_DIM ** -0.5
NEG_INF = -1e30

VMEM_LIMIT_BYTES = 56 * 1024 * 1024
SUBLANES = 8


def _dot(a, b):
    return jnp.dot(a, b, preferred_element_type=F32)


def _bdot(a, b):
    return lax.dot_general(a, b, (((2,), (1,)), ((0,), (0,))), preferred_element_type=F32)


def _bdot_nt(a, b):
    return lax.dot_general(a, b, (((2,), (2,)), ((0,), (0,))), preferred_element_type=F32)


def _bdot_tn(a, b):
    return lax.dot_general(a, b, (((1,), (1,)), ((0,), (0,))), preferred_element_type=F32)


def _sigmoid(x):
    return 1.0 / (1.0 + jnp.exp(-x))


def _split_dot_right(x, m_bf16, terms):
    acc = None
    rem = x
    for i in range(terms):
        hi = rem.astype(BF16)
        d = _dot(hi, m_bf16)
        acc = d if acc is None else acc + d
        if i + 1 < terms:
            rem = rem - hi.astype(F32)
    return acc


def _split_dot_left(m_bf16, x, terms):
    acc = None
    rem = x
    for i in range(terms):
        hi = rem.astype(BF16)
        d = _dot(m_bf16, hi)
        acc = d if acc is None else acc + d
        if i + 1 < terms:
            rem = rem - hi.astype(F32)
    return acc


def _block_matrix(n, blk, value):
    sh = int(math.log2(blk))
    r = lax.shift_right_logical(lax.broadcasted_iota(jnp.int32, (n, n), 0), sh)
    c = lax.shift_right_logical(lax.broadcasted_iota(jnp.int32, (n, n), 1), sh)
    return jnp.where(r == c, value, 0.0).astype(BF16)


def _params(sem):
    return pltpu.CompilerParams(dimension_semantics=sem, vmem_limit_bytes=VMEM_LIMIT_BYTES)


def _layer_block(shape, layer, n_grid):
    zeros = (0,) * len(shape)
    if n_grid == 1:
        index_map = lambda i: (layer,) + zeros
    else:
        index_map = lambda i, j: (layer,) + zeros
    return pl.BlockSpec((None,) + tuple(shape), index_map, pipeline_mode=pl.Buffered(1))


def _phase_copies(ext_ref, sh_ref):
    n = sh_ref.shape[1]
    for s in range(1, SUBLANES):
        sh_ref[s - 1] = ext_ref[s:s + n, :]


def _shifted_rows(ext_ref, sh_ref, start, rb):
    a, s = divmod(start, SUBLANES)
    base = a * SUBLANES
    return ext_ref[base:base + rb, :] if s == 0 else sh_ref[s - 1, base:base + rb, :]


def _conv_rows(ext_ref, sh_ref, w_ref, vec_ref, r0, rb):
    off = CONV_PAD - CONV_HIST
    acc = jnp.zeros((rb, GROUP_WIDTH), F32)
    for j in range(CONV_WIDTH):
        acc = acc + w_ref[j:j + 1, :] * _shifted_rows(ext_ref, sh_ref, r0 + off + j, rb)
    acc = acc + vec_ref[0:1, :]
    mu = jnp.mean(acc, axis=-1, keepdims=True)
    d = acc - mu
    var = jnp.mean(d * d, axis=-1, keepdims=True)
    yn = d * lax.rsqrt(var + LN_EPS) * vec_ref[1:2, :] + vec_ref[2:3, :]
    return yn * _sigmoid(yn)


def _pool_rows(ext_ref, sh_ref, wbd_ref, sc_ref, r0, rb, pos_start):
    base = POOL_PAD + r0
    sums = []
    acc = None
    for i in range(max(POOL_WINDOWS)):
        sh = _shifted_rows(ext_ref, sh_ref, base - i, rb)
        acc = sh if acc is None else acc + sh
        if i + 1 in POOL_WINDOWS:
            sums.append(acc)
    pos = pos_start + r0 + lax.broadcasted_iota(jnp.int32, (rb, 1), 0)
    means = [s / jnp.minimum(w, pos + 1).astype(F32) for s, w in zip(sums, POOL_WINDOWS)]
    lane = lax.broadcasted_iota(jnp.int32, (1, GROUP_WIDTH), 1)
    mean = jnp.where(lane < 64, means[0],
                     jnp.where(lane < 128, means[1], jnp.where(lane < 192, means[2], means[3])))
    d = mean - ext_ref[base:base + rb, :]
    return _dot(d.astype(BF16), wbd_ref[...]) * sc_ref[...]


def _projmix_body(x_ref, g_ref, w_ref, chist_ref, cw_ref, cvec_ref, phist_ref, pwbd_ref, psc_ref,
                  ub_ref, ud_ref, ya_ref, yc_ref, ctail_ref, ptail_ref,
                  cext_ref, csh_ref, pext_ref, psh_ref, *, nb, tl, rb, pos0):
    li = pl.program_id(1)

    @pl.when(li == 0)
    def _():
        cext_ref[:, 0:CONV_PAD, :] = chist_ref[...]
        pext_ref[:, 0:POOL_PAD, :] = phist_ref[...]

    x = x_ref[...].reshape(nb * tl, D_MODEL)
    ms = jnp.mean(x * x, axis=-1, keepdims=True)
    xn = ((x * lax.rsqrt(ms + RMS_EPS)) * g_ref[...]).astype(BF16)
    b0 = A_COLS
    c0 = A_COLS + B_COLS
    d0 = c0 + C_COLS
    ua = _dot(xn, w_ref[:, 0:A_COLS])
    uc = _dot(xn, w_ref[:, c0:d0])
    ub_ref[...] = _dot(xn, w_ref[:, b0:c0]).reshape(nb, tl, B_COLS)
    ud_ref[...] = _dot(xn, w_ref[:, d0:IN_COLS]).reshape(nb, tl, D_COLS)
    glu = ua[:, :GROUP_WIDTH] * _sigmoid(ua[:, GROUP_WIDTH:])
    for b in range(nb):
        cext = cext_ref.at[b]
        pext = pext_ref.at[b]
        cext[CONV_PAD:CONV_PAD + tl, :] = glu[b * tl:(b + 1) * tl]
        pext[POOL_PAD:POOL_PAD + tl, :] = uc[b * tl:(b + 1) * tl]
        _phase_copies(cext, csh_ref)
        _phase_copies(pext, psh_ref)
        for r0 in range(0, tl, rb):
            ya_ref[b, r0:r0 + rb, :] = _conv_rows(cext, csh_ref, cw_ref, cvec_ref, r0, rb).astype(BF16)
            yc_ref[b, r0:r0 + rb, :] = _pool_rows(pext, psh_ref, pwbd_ref, psc_ref, r0, rb,
                                                  pos0 + li * tl).astype(BF16)
        ctail = cext[tl:tl + CONV_PAD, :]
        ptail = pext[tl:tl + POOL_PAD, :]
        ctail_ref[b] = ctail
        ptail_ref[b] = ptail
        cext[0:CONV_PAD, :] = ctail
        pext[0:POOL_PAD, :] = ptail


def _projmix(x, layer, g, w_in, chist, conv_w, conv_vec, phist, pool_wbd, pool_scale, pos0):
    b, l, _ = x.shape
    nb = b if l <= 64 else 1
    tl = min(l, 512)
    rb = min(tl, 64)
    assert l % tl == 0 and tl % rb == 0 and tl >= CONV_PAD and b % nb == 0
    gw = GROUP_WIDTH
    tile = lambda w: pl.BlockSpec((nb, tl, w), lambda i, j: (i, j, 0))
    per_seq = lambda r: pl.BlockSpec((nb, r, gw), lambda i, j: (i, 0, 0))
    return pl.pallas_call(
        functools.partial(_projmix_body, nb=nb, tl=tl, rb=rb, pos0=pos0),
        grid=(b // nb, l // tl),
        in_specs=[
            tile(D_MODEL),
            _layer_block((1, D_MODEL), layer, 2),
            _layer_block((D_MODEL, IN_COLS), layer, 2),
            per_seq(CONV_PAD), _layer_block((CONV_PAD, gw), layer, 2), _layer_block((8, gw), layer, 2),
            per_seq(POOL_PAD), _layer_block((gw, gw), layer, 2), _layer_block((1, gw), layer, 2),
        ],
        out_specs=[tile(B_COLS), tile(D_COLS), tile(gw), tile(gw), per_seq(CONV_PAD), per_seq(POOL_PAD)],
        out_shape=[
            jax.ShapeDtypeStruct((b, l, B_COLS), F32),
            jax.ShapeDtypeStruct((b, l, D_COLS), F32),
            jax.ShapeDtypeStruct((b, l, gw), BF16),
            jax.ShapeDtypeStruct((b, l, gw), BF16),
            jax.ShapeDtypeStruct((b, CONV_PAD, gw), F32),
            jax.ShapeDtypeStruct((b, POOL_PAD, gw), F32),
        ],
        scratch_shapes=[
            pltpu.VMEM((nb, CONV_PAD + tl, gw), F32),
            pltpu.VMEM((SUBLANES - 1, CONV_PAD + tl - SUBLANES, gw), F32),
            pltpu.VMEM((nb, POOL_PAD + tl, gw), F32),
            pltpu.VMEM((SUBLANES - 1, POOL_PAD + tl - SUBLANES, gw), F32),
        ],
        compiler_params=_params(("parallel", "arbitrary")),
        name="projmix",
    )(x, g, w_in, chist, conv_w, conv_vec, phist, pool_wbd, pool_scale)


def _attn_body(sink_ref, u_ref, hk_ref, hv_ref, qg_ref, kg_ref, y_ref, kt_ref, vt_ref,
               kh_ref, vh_ref, *, layer, nb, tq, chunk, hist_valid):
    li = pl.program_id(0)
    hd = HEAD_DIM
    rows = nb * tq
    ncq = tq // chunk
    kw = WINDOW + chunk

    @pl.when(li == 0)
    def _():
        kh_ref[...] = hk_ref[...]
        vh_ref[...] = hv_ref[...]

    u = u_ref[...].reshape(rows, D_COLS)
    q = u[:, 0:256]
    k = u[:, 256:384]
    v = u[:, 384:512]
    inv = 1.0 / hd
    qms = _split_dot_right(q * q, _block_matrix(256, hd, inv), 1)
    kms = _split_dot_right(k * k, _block_matrix(128, hd, inv), 2)
    qn = (q * lax.rsqrt(qms + RMS_EPS)) * (qg_ref[...] * ATTN_SCALE)
    kn = (k * lax.rsqrt(kms + RMS_EPS)) * kg_ref[...]
    kcat = jnp.concatenate([kh_ref[...], kn.reshape(nb, tq, 2 * hd)], axis=1)
    vcat = jnp.concatenate([vh_ref[...], v.reshape(nb, tq, 2 * hd)], axis=1)
    ktail = kcat[:, tq:tq + WINDOW, :]
    vtail = vcat[:, tq:tq + WINDOW, :]
    kt_ref[...] = ktail
    vt_ref[...] = vtail
    kh_ref[...] = ktail
    vh_ref[...] = vtail
    kcat_b = kcat.astype(BF16)
    vcat_b = vcat.astype(BF16)

    slot_hi = lax.broadcasted_iota(jnp.int32, (1, 2 * hd), 1) >= hd
    q_tiles = []
    for h in range(N_Q_HEADS):
        col = qn[:, (h // 2) * 2 * hd:(h // 2 + 1) * 2 * hd]
        g = h // (N_Q_HEADS // N_KV_HEADS)
        if h % 2 != g:
            col = pltpu.roll(col, hd, axis=1)
        q_tiles.append(jnp.where(slot_hi if g == 1 else jnp.logical_not(slot_hi), col, 0.0).astype(BF16))
    pairs = [(b, c) for c in range(ncq) for b in range(nb)]
    qs = jnp.stack([jnp.concatenate([t[b * tq + c * chunk:b * tq + (c + 1) * chunk] for t in q_tiles], axis=0)
                    for b, c in pairs])
    ks = jnp.stack([kcat_b[b, c * chunk:c * chunk + kw] for b, c in pairs])
    vs = jnp.stack([vcat_b[b, c * chunk:c * chunk + kw] for b, c in pairs])
    st = _bdot_nt(ks, qs)
    nq = N_Q_HEADS * chunk
    if not hist_valid:
        n_edge = min(ncq, WINDOW // chunk) * nb
        cpos = [c * chunk for _, c in pairs[:n_edge]]
        kpos = lax.broadcasted_iota(jnp.int32, (1, kw, nq), 1) + (li * tq - WINDOW)
        edge = jnp.concatenate([jnp.where(kpos + cp >= 0, st[i:i + 1], NEG_INF) for i, cp in enumerate(cpos)],
                               axis=0)
        st = jnp.concatenate([edge, st[n_edge:]], axis=0) if n_edge < len(pairs) else edge
    hlane = lax.broadcasted_iota(jnp.int32, (1, 1, nq), 2)
    sk = jnp.full((1, 1, nq), sink_ref[layer, N_Q_HEADS - 1], F32)
    for h in range(N_Q_HEADS - 2, -1, -1):
        sk = jnp.where(hlane < (h + 1) * chunk, sink_ref[layer, h], sk)
    m = jnp.maximum(jnp.max(st, axis=1, keepdims=True), sk)
    p = jnp.exp(st - m)
    den = jnp.sum(p, axis=1, keepdims=True) + jnp.exp(sk - m)
    pn = (p * (1.0 / den)).astype(BF16)
    o = _bdot_tn(pn, vs)
    lo = jnp.logical_not(slot_hi)
    for i, (b, c) in enumerate(pairs):
        oc = o[i]
        col0 = jnp.where(lo, oc[0:chunk], pltpu.roll(oc[chunk:2 * chunk], hd, axis=1))
        col1 = jnp.where(lo, pltpu.roll(oc[2 * chunk:3 * chunk], hd, axis=1), oc[3 * chunk:4 * chunk])
        y_ref[b, c * chunk:(c + 1) * chunk, :] = jnp.concatenate([col0, col1], axis=1).astype(BF16)


def _attn(ud, hk, hv, layer, qg, kg, sinks, chunk, hist_valid):
    b, l, _ = ud.shape
    tq = min(l, 256)
    assert l % tq == 0 and tq % chunk == 0
    kvw = N_KV_HEADS * HEAD_DIM
    return pl.pallas_call(
        functools.partial(_attn_body, layer=layer, nb=b, tq=tq, chunk=chunk, hist_valid=hist_valid),
        grid=(l // tq,),
        in_specs=[
            pl.BlockSpec(memory_space=pltpu.SMEM),
            pl.BlockSpec((b, tq, D_COLS), lambda j: (0, j, 0)),
            pl.BlockSpec((b, WINDOW, kvw), lambda j: (0, 0, 0)),
            pl.BlockSpec((b, WINDOW, kvw), lambda j: (0, 0, 0)),
            _layer_block((1, GROUP_WIDTH), layer, 1),
            _layer_block((1, kvw), layer, 1),
        ],
        out_specs=[
            pl.BlockSpec((b, tq, GROUP_WIDTH), lambda j: (0, j, 0)),
            pl.BlockSpec((b, WINDOW, kvw), lambda j: (0, 0, 0)),
            pl.BlockSpec((b, WINDOW, kvw), lambda j: (0, 0, 0)),
        ],
        out_shape=[
            jax.ShapeDtypeStruct((b, l, GROUP_WIDTH), BF16),
            jax.ShapeDtypeStruct((b, WINDOW, kvw), F32),
            jax.ShapeDtypeStruct((b, WINDOW, kvw), F32),
        ],
        scratch_shapes=[pltpu.VMEM((b, WINDOW, kvw), F32), pltpu.VMEM((b, WINDOW, kvw), F32)],
        compiler_params=_params(("arbitrary",)),
        name="attn",
    )(sinks, ud, hk, hv, qg, kg)


def _rwkv_body(u_ref, sp_ref, s0_ref, mu_ref, wl_ref, vec_ref, y_ref, sn_ref,
               prev_ref, s_ref, yacc_ref, *, nb, tb, chunk):
    li = pl.program_id(0)
    gw = GROUP_WIDTH
    hd = RWKV_HEAD
    nh = RWKV_HEADS
    rows = nb * tb
    nc = tb // chunk

    @pl.when(li == 0)
    def _():
        prev_ref[...] = sp_ref[...]
        zero = jnp.zeros((nb, hd, hd), F32)
        for h in range(nh):
            sh = s0_ref[:, h]
            s_ref[h * nb:(h + 1) * nb] = jnp.concatenate([sh, zero] if h % 2 == 0 else [zero, sh], axis=-1)

    row = lax.broadcasted_iota(jnp.int32, (tb, 1), 0)
    mu = mu_ref[...]
    xs_parts = []
    for b in range(nb):
        ub = u_ref[b]
        prev = jnp.where(row == 0, prev_ref[b], pltpu.roll(ub, 1, axis=0))
        prev_ref[b] = ub[tb - 1:tb, :]
        xs_parts.append(ub + mu * (prev - ub))
    xs = jnp.concatenate(xs_parts, axis=0)
    r = xs[:, 0:gw]
    k = xs[:, gw:2 * gw]
    v = xs[:, 2 * gw:3 * gw]
    lat = xs[:, 3 * gw:B_COLS]
    lane_lat = lax.broadcasted_iota(jnp.int32, (1, B_COLS - 3 * gw), 1)
    act = jnp.where(lane_lat < 32, jnp.tanh(lat), jnp.where(lane_lat < 64, lat, _sigmoid(lat)))
    lo = _dot(act.astype(BF16), wl_ref[...])
    w0 = vec_ref[0:1, :]
    a0 = vec_ref[1:2, :]
    k_k = vec_ref[2:3, :]
    k_a = vec_ref[3:4, :]
    r_k = vec_ref[4:5, :]
    gn_g = vec_ref[5:6, :]
    gn_b = vec_ref[6:7, :]
    z = -(w0 + lo[:, 0:gw])
    softplus = jnp.maximum(z, 0.0) + jnp.log(1.0 + jnp.exp(-jnp.abs(z)))
    logw = -jnp.exp(-softplus - 0.5)
    a_rate = _sigmoid(a0 + lo[:, gw:2 * gw])
    gate = lo[:, 2 * gw:3 * gw]
    ones_blk = _block_matrix(gw, hd, 1.0)
    kk = k * k_k
    kk = kk * lax.rsqrt(jnp.maximum(_split_dot_right(kk * kk, ones_blk, 2), 1e-24))
    k_mod = k * (1.0 + (a_rate - 1.0) * k_a)
    b_v = kk * a_rate
    bonus = _split_dot_right(r * k_mod * r_k, ones_blk, 1) * v

    grp = min(rows, 256)
    gi = lax.broadcasted_iota(jnp.int32, (grp, grp), 0)
    gj = lax.broadcasted_iota(jnp.int32, (grp, grp), 1)
    csh = int(math.log2(chunk))
    tri = jnp.where(jnp.logical_and(lax.shift_right_logical(gi, csh) == lax.shift_right_logical(gj, csh),
                                    gj <= gi), 1.0, 0.0).astype(BF16)
    cum = jnp.concatenate([_split_dot_left(tri, logw[g0:g0 + grp], 3) for g0 in range(0, rows, grp)], axis=0)
    cum3 = cum.reshape(nb * nc, chunk, gw)
    cum_c = cum3[:, chunk - 1:chunk, :]
    e_end = jnp.exp(cum_c - cum3).reshape(rows, gw)
    w_c = jnp.exp(cum_c)
    e_neg = jnp.exp(-cum)
    dense = dict(
        a=-kk * jnp.exp(cum - logw),
        r=r * jnp.exp(cum),
        bt=b_v * e_neg,
        kt=k_mod * e_neg,
        bh=b_v * e_end,
        kh=k_mod * e_end,
    )
    slot_hi = lax.broadcasted_iota(jnp.int32, (1, 2 * hd), 1) >= hd

    def head_tile(x, h, own_slot):
        col = x[:, (h // 2) * 2 * hd:(h // 2 + 1) * 2 * hd]
        keep = slot_hi if (h % 2 == 1) == own_slot else jnp.logical_not(slot_hi)
        return jnp.where(keep, col, 0.0)

    v_sw = jnp.concatenate([pltpu.roll(v[:, j * 2 * hd:(j + 1) * 2 * hd], hd, axis=1) for j in range(nh // 2)],
                           axis=1)
    tiles = {name: [head_tile(x, h, True) for h in range(nh)] for name, x in dense.items()}
    tiles_b = {name: [t.astype(BF16) for t in tiles[name]] for name in ("a", "bt", "kt", "bh", "kh")}
    v_tiles = [head_tile(v_sw, h, False).astype(BF16) for h in range(nh)]

    def blocks(per_head, c):
        return jnp.stack([per_head[h][b * tb + c * chunk:b * tb + (c + 1) * chunk]
                          for h in range(nh) for b in range(nb)])

    n = nh * nb
    ri = lax.broadcasted_iota(jnp.int32, (2 * chunk, 2 * chunk), 0)
    ci = jnp.bitwise_and(lax.broadcasted_iota(jnp.int32, (2 * chunk, 2 * chunk), 1), chunk - 1)
    gmask = ci < jnp.bitwise_and(ri, chunk - 1) + lax.shift_right_logical(ri, csh)
    zeros_c = jnp.zeros((n, chunk, 2 * hd), BF16)
    n_sq = int(math.log2(chunk))
    for c in range(nc):
        a_b = blocks(tiles_b["a"], c)
        r_f = blocks(tiles["r"], c)
        v_b = blocks(v_tiles, c)
        ar = jnp.concatenate([a_b, r_f.astype(BF16)], axis=1)
        bk = jnp.concatenate([blocks(tiles_b["bt"], c), blocks(tiles_b["kt"], c)], axis=1)
        bhkh = jnp.concatenate([blocks(tiles_b["bh"], c), blocks(tiles_b["kh"], c)], axis=1)
        g = jnp.where(gmask, _bdot_nt(ar, bk), 0.0)
        g_top = g[:, :chunk, :]
        g_bot = g[:, chunk:, :].astype(BF16)
        w = a_b.astype(F32) + _bdot(g_top.astype(BF16), jnp.concatenate([zeros_c, v_b], axis=1))
        p = g_top[:, :, :chunk]
        for i in range(n_sq):
            pb = p.astype(BF16)
            if i + 1 < n_sq:
                res = _bdot(pb, jnp.concatenate([w.astype(BF16), pb], axis=2))
                w = w + res[:, :, :2 * hd]
                p = res[:, :, 2 * hd:]
            else:
                w = w + _bdot(pb, w.astype(BF16))
        xv = jnp.concatenate([w.astype(BF16), v_b], axis=1)
        ry = _bdot(g_bot, xv)
        mp = _bdot_tn(xv, bhkh)
        s_old = s_ref[...]
        s_b = s_old.astype(BF16)
        y_nt = _bdot_nt((ry + r_f).astype(BF16), s_b)
        wc = jnp.stack([w_c[b * nc + c][:, (h // 2) * 2 * hd:(h // 2 + 1) * 2 * hd]
                        for h in range(nh) for b in range(nb)])
        psi = jnp.concatenate(
            [mp[h * nb:(h + 1) * nb, (1 - h % 2) * hd:(2 - h % 2) * hd, :] for h in range(nh)], axis=0)
        s_ref[...] = s_old * wc + _bdot(s_b, mp.astype(BF16)) + psi
        for b in range(nb):
            ys = [y_nt[h * nb + b] + ry[h * nb + b][:, (1 - h % 2) * hd:(2 - h % 2) * hd] for h in range(nh)]
            yacc_ref[b * tb + c * chunk:b * tb + (c + 1) * chunk, :] = jnp.concatenate(ys, axis=1)

    y = yacc_ref[...]
    avg_blk = _block_matrix(gw, hd, 1.0 / hd)
    m = _split_dot_right(y, avg_blk, 1)
    d = y - m
    var = _split_dot_right(d * d, avg_blk, 1)
    yn = d * lax.rsqrt(var + GN_EPS) * gn_g + gn_b
    out = ((yn + bonus) * gate).astype(BF16)
    for b in range(nb):
        y_ref[b] = out[b * tb:(b + 1) * tb]

    @pl.when(li == pl.num_programs(0) - 1)
    def _():
        for h in range(nh):
            sn_ref[:, h] = s_ref[h * nb:(h + 1) * nb, :, (h % 2) * hd:(h % 2 + 1) * hd]


def _rwkv(ub, shift_prev, state, layer, mu, wl, vec, chunk):
    b, l, _ = ub.shape
    tb = min(l, 256)
    assert l % tb == 0 and tb % chunk == 0
    hd = RWKV_HEAD
    n = RWKV_HEADS * b
    sshape = (b, RWKV_HEADS, hd, hd)
    return pl.pallas_call(
        functools.partial(_rwkv_body, nb=b, tb=tb, chunk=chunk),
        grid=(l // tb,),
        in_specs=[
            pl.BlockSpec((b, tb, B_COLS), lambda j: (0, j, 0)),
            pl.BlockSpec((b, 1, B_COLS), lambda j: (0, 0, 0)),
            pl.BlockSpec(sshape, lambda j: (0, 0, 0, 0)),
            _layer_block((1, B_COLS), layer, 1),
            _layer_block((128, 3 * GROUP_WIDTH), layer, 1),
            _layer_block((8, GROUP_WIDTH), layer, 1),
        ],
        out_specs=[
            pl.BlockSpec((b, tb, GROUP_WIDTH), lambda j: (0, j, 0)),
            pl.BlockSpec(sshape, lambda j: (0, 0, 0, 0)),
        ],
        out_shape=[
            jax.ShapeDtypeStruct((b, l, GROUP_WIDTH), BF16),
            jax.ShapeDtypeStruct(sshape, F32),
        ],
        scratch_shapes=[
            pltpu.VMEM((b, 1, B_COLS), F32),
            pltpu.VMEM((n, hd, 2 * hd), F32),
            pltpu.VMEM((b * tb, GROUP_WIDTH), F32),
        ],
        compiler_params=_params(("arbitrary",)),
        name="rwkv",
    )(ub, shift_prev, state, mu, wl, vec)


def _outffn_body(x_ref, ya_ref, yb_ref, yc_ref, yd_ref, wo_ref, g_ref, wg_ref, wu_ref, wd_ref, o_ref):
    ycat = jnp.concatenate([ya_ref[...], yb_ref[...], yc_ref[...], yd_ref[...]], axis=-1)
    x1 = x_ref[...] + _dot(ycat, wo_ref[...])
    ms = jnp.mean(x1 * x1, axis=-1, keepdims=True)
    hn = ((x1 * lax.rsqrt(ms + RMS_EPS)) * g_ref[...]).astype(BF16)
    hg = _dot(hn, wg_ref[...])
    hu = _dot(hn, wu_ref[...])
    act = (hg * _sigmoid(hg) * hu).astype(BF16)
    o_ref[...] = x1 + _dot(act, wd_ref[...])


def _outffn(x2d, ya, yb, yc, yd, layer, wo, g, wg, wu, wd):
    t = x2d.shape[0]
    tm = min(t, 512)
    assert t % tm == 0
    row = lambda w: pl.BlockSpec((tm, w), lambda i: (i, 0))
    return pl.pallas_call(
        _outffn_body,
        grid=(t // tm,),
        in_specs=[
            row(D_MODEL), row(GROUP_WIDTH), row(GROUP_WIDTH), row(GROUP_WIDTH), row(GROUP_WIDTH),
            _layer_block((4 * GROUP_WIDTH, D_MODEL), layer, 1),
            _layer_block((1, D_MODEL), layer, 1),
            _layer_block((D_MODEL, D_FF), layer, 1),
            _layer_block((D_MODEL, D_FF), layer, 1),
            _layer_block((D_FF, D_MODEL), layer, 1),
        ],
        out_specs=row(D_MODEL),
        out_shape=jax.ShapeDtypeStruct((t, D_MODEL), F32),
        compiler_params=_params(("parallel",)),
        name="outffn",
    )(x2d, ya, yb, yc, yd, wo, g, wg, wu, wd)


def _rows8(vectors):
    stacked = jnp.stack(vectors, axis=1)
    return jnp.pad(stacked, ((0, 0), (0, SUBLANES - stacked.shape[1]), (0, 0)))


def _stacked_weights(p):
    gw = GROUP_WIDTH
    depth = p['w_in'].shape[0]
    place = lambda a, before, after: jnp.pad(a, ((0, 0), (0, 0), (before, after)))
    wl = jnp.concatenate([place(p['rwkv_w2'], 0, 2 * gw), place(p['rwkv_a2'], gw, gw),
                          place(p['rwkv_g2'], 2 * gw, 0)], axis=1)
    pc = gw // len(POOL_WINDOWS)
    wbd = jnp.concatenate([place(p['pool_w'][:, g], g * pc, gw - (g + 1) * pc)
                           for g in range(len(POOL_WINDOWS))], axis=1)
    return dict(
        norm_mix_g=p['norm_mix_g'][:, None, :],
        w_in=p['w_in'].astype(BF16),
        conv_w=jnp.pad(p['conv_w'], ((0, 0), (0, CONV_PAD - CONV_WIDTH), (0, 0))),
        conv_vec=_rows8([p['conv_b'], p['conv_ln_g'], p['conv_ln_b']]),
        rwkv_mu=p['rwkv_mu'][:, None, :],
        rwkv_wl=wl.astype(BF16),
        rwkv_vec=_rows8([p['rwkv_w0'], p['rwkv_a0'], p['rwkv_k_k'], p['rwkv_k_a'],
                         p['rwkv_r_k'].reshape(depth, gw), p['rwkv_gn_g'], p['rwkv_gn_b']]),
        pool_wbd=wbd.astype(BF16),
        pool_scale=p['pool_scale'][:, None, :],
        attn_qg=jnp.tile(p['attn_q_norm'], (1, N_Q_HEADS))[:, None, :],
        attn_kg=jnp.tile(p['attn_k_norm'], (1, N_KV_HEADS))[:, None, :],
        attn_sinks=p['attn_sinks'],
        w_out=p['w_out'].astype(BF16),
        norm_ffn_g=p['norm_ffn_g'][:, None, :],
        ffn_w_gate=p['ffn_w_gate'].astype(BF16),
        ffn_w_up=p['ffn_w_up'].astype(BF16),
        ffn_w_down=p['ffn_w_down'].astype(BF16),
    )


def _trunk(x, conv_hist, rwkv_state, shift_prev, pool_hist, k_cache, v_cache, pos0, chunk, w):
    b, l, _ = x.shape
    has_past = conv_hist is not None
    kvw = N_KV_HEADS * HEAD_DIM
    new = [[] for _ in range(6)]
    for li in range(DEPTH):
        if has_past:
            ch = jnp.pad(conv_hist[li], ((0, 0), (CONV_PAD - CONV_HIST, 0), (0, 0)))
            rs = rwkv_state[li]
            sp = shift_prev[li][:, None, :]
            ph = jnp.pad(pool_hist[li], ((0, 0), (POOL_PAD - POOL_HIST, 0), (0, 0)))
            hk = k_cache[li].reshape(b, WINDOW, kvw)
            hv = v_cache[li].reshape(b, WINDOW, kvw)
        else:
            ch = jnp.zeros((b, CONV_PAD, GROUP_WIDTH), F32)
            rs = jnp.zeros((b, RWKV_HEADS, RWKV_HEAD, RWKV_HEAD), F32)
            sp = jnp.zeros((b, 1, B_COLS), F32)
            ph = jnp.zeros((b, POOL_PAD, GROUP_WIDTH), F32)
            hk = jnp.zeros((b, WINDOW, kvw), F32)
            hv = jnp.zeros((b, WINDOW, kvw), F32)
        ub, ud, ya, yc, conv_tail, pool_tail = _projmix(
            x, li, w['norm_mix_g'], w['w_in'], ch, w['conv_w'], w['conv_vec'],
            ph, w['pool_wbd'], w['pool_scale'], pos0)
        yb, s_new = _rwkv(ub, sp, rs, li, w['rwkv_mu'], w['rwkv_wl'], w['rwkv_vec'], min(CHUNK, l))
        yd, k_tail, v_tail = _attn(ud, hk, hv, li, w['attn_qg'], w['attn_kg'], w['attn_sinks'], chunk,
                                   has_past)
        flat = lambda y: y.reshape(b * l, GROUP_WIDTH)
        x = _outffn(x.reshape(b * l, D_MODEL), flat(ya), flat(yb), flat(yc), flat(yd), li, w['w_out'],
                    w['norm_ffn_g'], w['ffn_w_gate'], w['ffn_w_up'], w['ffn_w_down']
                    ).reshape(b, l, D_MODEL)
        new[0].append(conv_tail[:, CONV_PAD - CONV_HIST:, :])
        new[1].append(s_new)
        new[2].append(ub[:, l - 1, :])
        new[3].append(pool_tail[:, POOL_PAD - POOL_HIST:, :])
        new[4].append(k_tail.reshape(b, WINDOW, N_KV_HEADS, HEAD_DIM))
        new[5].append(v_tail.reshape(b, WINDOW, N_KV_HEADS, HEAD_DIM))
    return x, tuple(jnp.stack(n) for n in new)


def kernel(x_prompt, x_sample, cache_conv, state_rwkv, state_rwkv_shift, cache_pool, cache_k, cache_v, norm_mix_g, w_in, conv_w, conv_b, conv_ln_g, conv_ln_b, rwkv_mu, rwkv_w0, rwkv_w2, rwkv_a0, rwkv_a2, rwkv_g2, rwkv_k_k, rwkv_k_a, rwkv_r_k, rwkv_gn_g, rwkv_gn_b, pool_w, pool_scale, attn_q_norm, attn_k_norm, attn_sinks, w_out, norm_ffn_g, ffn_w_gate, ffn_w_up, ffn_w_down):
    w = _stacked_weights(dict(
        norm_mix_g=norm_mix_g, w_in=w_in, conv_w=conv_w, conv_b=conv_b, conv_ln_g=conv_ln_g,
        conv_ln_b=conv_ln_b, rwkv_mu=rwkv_mu, rwkv_w0=rwkv_w0, rwkv_w2=rwkv_w2, rwkv_a0=rwkv_a0,
        rwkv_a2=rwkv_a2, rwkv_g2=rwkv_g2, rwkv_k_k=rwkv_k_k, rwkv_k_a=rwkv_k_a, rwkv_r_k=rwkv_r_k,
        rwkv_gn_g=rwkv_gn_g, rwkv_gn_b=rwkv_gn_b, pool_w=pool_w, pool_scale=pool_scale,
        attn_q_norm=attn_q_norm, attn_k_norm=attn_k_norm, attn_sinks=attn_sinks, w_out=w_out,
        norm_ffn_g=norm_ffn_g, ffn_w_gate=ffn_w_gate, ffn_w_up=ffn_w_up, ffn_w_down=ffn_w_down))
    y_p, (conv_p, rwkv_p, shift_p, pool_p, k_p, v_p) = _trunk(
        x_prompt, None, None, None, None, None, None, 0, CHUNK, w)
    y_s, (conv_s, rwkv_s, shift_s, pool_s, k_s, v_s) = _trunk(
        x_sample, cache_conv, state_rwkv, state_rwkv_shift, cache_pool, cache_k, cache_v,
        PAST_LEN, x_sample.shape[1], w)
    return (y_p, y_s, conv_p, conv_s, rwkv_p, rwkv_s, shift_p, shift_s,
            pool_p, pool_s, k_p, k_s, v_p, v_s)
```

```python
import functools
import math

import jax
import jax.numpy as jnp
from jax import lax
from jax.experimental import pallas as pl
from jax.experimental.pallas import tpu as pltpu

F32 = jnp.float32
BF16 = jnp.bfloat16

D_MODEL = 1024
DEPTH = 2
PAST_LEN = 1024
CHUNK = 64
GROUP_WIDTH = 256
CONV_WIDTH = 31
CONV_HIST = CONV_WIDTH - 1
CONV_PAD = 32
RWKV_HEAD = 64
RWKV_HEADS = 4
POOL_WINDOWS = (2, 4, 8, 16)
POOL_HIST = 15
POOL_PAD = 16
HEAD_DIM = 64
N_Q_HEADS = 4
N_KV_HEADS = 2
WINDOW = 128
D_FF = 2816
A_COLS = 512
B_COLS = 896
C_COLS = 256
D_COLS = 512
IN_COLS = A_COLS + B_COLS + C_COLS + D_COLS
RMS_EPS = 1e-6
LN_EPS = 1e-5
GN_EPS = 64e-5
ATTN_SCALE = HEAD_DIM ** -0.5
NEG_INF = -1e30

VMEM_LIMIT_BYTES = 56 * 1024 * 1024
SUBLANES = 8


def _dot(a, b):
    return jnp.dot(a, b, preferred_element_type=F32)


def _bdot(a, b):
    return lax.dot_general(a, b, (((2,), (1,)), ((0,), (0,))), preferred_element_type=F32)


def _bdot_nt(a, b):
    return lax.dot_general(a, b, (((2,), (2,)), ((0,), (0,))), preferred_element_type=F32)


def _bdot_tn(a, b):
    return lax.dot_general(a, b, (((1,), (1,)), ((0,), (0,))), preferred_element_type=F32)


def _sigmoid(x):
    return 1.0 / (1.0 + jnp.exp(-x))


def _split_dot_right(x, m_bf16, terms):
    acc = None
    rem = x
    for i in range(terms):
        hi = rem.astype(BF16)
        d = _dot(hi, m_bf16)
        acc = d if acc is None else acc + d
        if i + 1 < terms:
            rem = rem - hi.astype(F32)
    return acc


def _split_dot_left(m_bf16, x, terms):
    acc = None
    rem = x
    for i in range(terms):
        hi = rem.astype(BF16)
        d = _dot(m_bf16, hi)
        acc = d if acc is None else acc + d
        if i + 1 < terms:
            rem = rem - hi.astype(F32)
    return acc


def _block_matrix(n, blk, value):
    sh = int(math.log2(blk))
    r = lax.shift_right_logical(lax.broadcasted_iota(jnp.int32, (n, n), 0), sh)
    c = lax.shift_right_logical(lax.broadcasted_iota(jnp.int32, (n, n), 1), sh)
    return jnp.where(r == c, value, 0.0).astype(BF16)


def _params(sem):
    return pltpu.CompilerParams(dimension_semantics=sem, vmem_limit_bytes=VMEM_LIMIT_BYTES)


def _layer_block(shape, layer, n_grid):
    zeros = (0,) * len(shape)
    if n_grid == 1:
        index_map = lambda i: (layer,) + zeros
    else:
        index_map = lambda i, j: (layer,) + zeros
    return pl.BlockSpec((None,) + tuple(shape), index_map, pipeline_mode=pl.Buffered(1))


def _phase_copies(ext_ref, sh_ref):
    n = sh_ref.shape[1]
    for s in range(1, SUBLANES):
        sh_ref[s - 1] = ext_ref[s:s + n, :]


def _shifted_rows(ext_ref, sh_ref, start, rb):
    a, s = divmod(start, SUBLANES)
    base = a * SUBLANES
    return ext_ref[base:base + rb, :] if s == 0 else sh_ref[s - 1, base:base + rb, :]


def _conv_rows(ext_ref, sh_ref, w_ref, vec_ref, r0, rb):
    off = CONV_PAD - CONV_HIST
    acc = jnp.zeros((rb, GROUP_WIDTH), F32)
    for j in range(CONV_WIDTH):
        acc = acc + w_ref[j:j + 1, :] * _shifted_rows(ext_ref, sh_ref, r0 + off + j, rb)
    acc = acc + vec_ref[0:1, :]
    mu = jnp.mean(acc, axis=-1, keepdims=True)
    d = acc - mu
    var = jnp.mean(d * d, axis=-1, keepdims=True)
    yn = d * lax.rsqrt(var + LN_EPS) * vec_ref[1:2, :] + vec_ref[2:3, :]
    return yn * _sigmoid(yn)


def _pool_rows(ext_ref, sh_ref, wbd_ref, sc_ref, r0, rb, pos_start):
    base = POOL_PAD + r0
    sums = []
    acc = None
    for i in range(max(POOL_WINDOWS)):
        sh = _shifted_rows(ext_ref, sh_ref, base - i, rb)
        acc = sh if acc is None else acc + sh
        if i + 1 in POOL_WINDOWS:
            sums.append(acc)
    pos = pos_start + r0 + lax.broadcasted_iota(jnp.int32, (rb, 1), 0)
    means = [s / jnp.minimum(w, pos + 1).astype(F32) for s, w in zip(sums, POOL_WINDOWS)]
    lane = lax.broadcasted_iota(jnp.int32, (1, GROUP_WIDTH), 1)
    mean = jnp.where(lane < 64, means[0],
                     jnp.where(lane < 128, means[1], jnp.where(lane < 192, means[2], means[3])))
    d = mean - ext_ref[base:base + rb, :]
    return _dot(d.astype(BF16), wbd_ref[...]) * sc_ref[...]


def _projmix_body(x_ref, g_ref, w_ref, chist_ref, cw_ref, cvec_ref, phist_ref, pwbd_ref, psc_ref,
                  ub_ref, ud_ref, ya_ref, yc_ref, ctail_ref, ptail_ref,
                  cext_ref, csh_ref, pext_ref, psh_ref, *, nb, tl, rb, pos0):
    li = pl.program_id(1)

    @pl.when(li == 0)
    def _():
        cext_ref[:, 0:CONV_PAD, :] = chist_ref[...]
        pext_ref[:, 0:POOL_PAD, :] = phist_ref[...]

    x = x_ref[...].reshape(nb * tl, D_MODEL)
    ms = jnp.mean(x * x, axis=-1, keepdims=True)
    xn = ((x * lax.rsqrt(ms + RMS_EPS)) * g_ref[...]).astype(BF16)
    b0 = A_COLS
    c0 = A_COLS + B_COLS
    d0 = c0 + C_COLS
    ua = _dot(xn, w_ref[:, 0:A_COLS])
    uc = _dot(xn, w_ref[:, c0:d0])
    ub_ref[...] = _dot(xn, w_ref[:, b0:c0]).reshape(nb, tl, B_COLS)
    ud_ref[...] = _dot(xn, w_ref[:, d0:IN_COLS]).reshape(nb, tl, D_COLS)
    glu = ua[:, :GROUP_WIDTH] * _sigmoid(ua[:, GROUP_WIDTH:])
    for b in range(nb):
        cext = cext_ref.at[b]
        pext = pext_ref.at[b]
        cext[CONV_PAD:CONV_PAD + tl, :] = glu[b * tl:(b + 1) * tl]
        pext[POOL_PAD:POOL_PAD + tl, :] = uc[b * tl:(b + 1) * tl]
        _phase_copies(cext, csh_ref)
        _phase_copies(pext, psh_ref)
        for r0 in range(0, tl, rb):
            ya_ref[b, r0:r0 + rb, :] = _conv_rows(cext, csh_ref, cw_ref, cvec_ref, r0, rb).astype(BF16)
            yc_ref[b, r0:r0 + rb, :] = _pool_rows(pext, psh_ref, pwbd_ref, psc_ref, r0, rb,
                                                  pos0 + li * tl).astype(BF16)
        ctail = cext[tl:tl + CONV_PAD, :]
        ptail = pext[tl:tl + POOL_PAD, :]
        ctail_ref[b] = ctail
        ptail_ref[b] = ptail
        cext[0:CONV_PAD, :] = ctail
        pext[0:POOL_PAD, :] = ptail


def _projmix(x, layer, g, w_in, chist, conv_w, conv_vec, phist, pool_wbd, pool_scale, pos0):
    b, l, _ = x.shape
    nb = b if l <= 64 else 1
    tl = min(l, 512)
    rb = min(tl, 64)
    assert l % tl == 0 and tl % rb == 0 and tl >= CONV_PAD and b % nb == 0
    gw = GROUP_WIDTH
    tile = lambda w: pl.BlockSpec((nb, tl, w), lambda i, j: (i, j, 0))
    per_seq = lambda r: pl.BlockSpec((nb, r, gw), lambda i, j: (i, 0, 0))
    return pl.pallas_call(
        functools.partial(_projmix_body, nb=nb, tl=tl, rb=rb, pos0=pos0),
        grid=(b // nb, l // tl),
        in_specs=[
            tile(D_MODEL),
            _layer_block((1, D_MODEL), layer, 2),
            _layer_block((D_MODEL, IN_COLS), layer, 2),
            per_seq(CONV_PAD), _layer_block((CONV_PAD, gw), layer, 2), _layer_block((8, gw), layer, 2),
            per_seq(POOL_PAD), _layer_block((gw, gw), layer, 2), _layer_block((1, gw), layer, 2),
        ],
        out_specs=[tile(B_COLS), tile(D_COLS), tile(gw), tile(gw), per_seq(CONV_PAD), per_seq(POOL_PAD)],
        out_shape=[
            jax.ShapeDtypeStruct((b, l, B_COLS), F32),
            jax.ShapeDtypeStruct((b, l, D_COLS), F32),
            jax.ShapeDtypeStruct((b, l, gw), BF16),
            jax.ShapeDtypeStruct((b, l, gw), BF16),
            jax.ShapeDtypeStruct((b, CONV_PAD, gw), F32),
            jax.ShapeDtypeStruct((b, POOL_PAD, gw), F32),
        ],
        scratch_shapes=[
            pltpu.VMEM((nb, CONV_PAD + tl, gw), F32),
            pltpu.VMEM((SUBLANES - 1, CONV_PAD + tl - SUBLANES, gw), F32),
            pltpu.VMEM((nb, POOL_PAD + tl, gw), F32),
            pltpu.VMEM((SUBLANES - 1, POOL_PAD + tl - SUBLANES, gw), F32),
        ],
        compiler_params=_params(("parallel", "arbitrary")),
        name="projmix",
    )(x, g, w_in, chist, conv_w, conv_vec, phist, pool_wbd, pool_scale)


def _attn_body(sink_ref, u_ref, hk_ref, hv_ref, qg_ref, kg_ref, y_ref, kt_ref, vt_ref,
               kh_ref, vh_ref, *, layer, nb, tq, chunk, hist_valid):
    li = pl.program_id(0)
    hd = HEAD_DIM
    rows = nb * tq
    ncq = tq // chunk
    kw = WINDOW + chunk

    @pl.when(li == 0)
    def _():
        kh_ref[...] = hk_ref[...]
        vh_ref[...] = hv_ref[...]

    u = u_ref[...].reshape(rows, D_COLS)
    q = u[:, 0:256]
    k = u[:, 256:384]
    v = u[:, 384:512]
    inv = 1.0 / hd
    qms = _split_dot_right(q * q, _block_matrix(256, hd, inv), 1)
    kms = _split_dot_right(k * k, _block_matrix(128, hd, inv), 2)
    qn = (q * lax.rsqrt(qms + RMS_EPS)) * (qg_ref[...] * ATTN_SCALE)
    kn = (k * lax.rsqrt(kms + RMS_EPS)) * kg_ref[...]
    kcat = jnp.concatenate([kh_ref[...], kn.reshape(nb, tq, 2 * hd)], axis=1)
    vcat = jnp.concatenate([vh_ref[...], v.reshape(nb, tq, 2 * hd)], axis=1)
    ktail = kcat[:, tq:tq + WINDOW, :]
    vtail = vcat[:, tq:tq + WINDOW, :]
    kt_ref[...] = ktail
    vt_ref[...] = vtail
    kh_ref[...] = ktail
    vh_ref[...] = vtail
    kcat_b = kcat.astype(BF16)
    vcat_b = vcat.astype(BF16)

    slot_hi = lax.broadcasted_iota(jnp.int32, (1, 2 * hd), 1) >= hd
    q_tiles = []
    for h in range(N_Q_HEADS):
        col = qn[:, (h // 2) * 2 * hd:(h // 2 + 1) * 2 * hd]
        g = h // (N_Q_HEADS // N_KV_HEADS)
        if h % 2 != g:
            col = pltpu.roll(col, hd, axis=1)
        q_tiles.append(jnp.where(slot_hi if g == 1 else jnp.logical_not(slot_hi), col, 0.0).astype(BF16))
    pairs = [(b, c) for c in range(ncq) for b in range(nb)]
    qs = jnp.stack([jnp.concatenate([t[b * tq + c * chunk:b * tq + (c + 1) * chunk] for t in q_tiles], axis=0)
                    for b, c in pairs])
    ks = jnp.stack([kcat_b[b, c * chunk:c * chunk + kw] for b, c in pairs])
    vs = jnp.stack([vcat_b[b, c * chunk:c * chunk + kw] for b, c in pairs])
    st = _bdot_nt(ks, qs)
    nq = N_Q_HEADS * chunk
    if not hist_valid:
        n_edge = min(ncq, WINDOW // chunk) * nb
        cpos = [c * chunk for _, c in pairs[:n_edge]]
        kpos = lax.broadcasted_iota(jnp.int32, (1, kw, nq), 1) + (li * tq - WINDOW)
        edge = jnp.concatenate([jnp.where(kpos + cp >= 0, st[i:i + 1], NEG_INF) for i, cp in enumerate(cpos)],
                               axis=0)
        st = jnp.concatenate([edge, st[n_edge:]], axis=0) if n_edge < len(pairs) else edge
    hlane = lax.broadcasted_iota(jnp.int32, (1, 1, nq), 2)
    sk = jnp.full((1, 1, nq), sink_ref[layer, N_Q_HEADS - 1], F32)
    for h in range(N_Q_HEADS - 2, -1, -1):
        sk = jnp.where(hlane < (h + 1) * chunk, sink_ref[layer, h], sk)
    m = jnp.maximum(jnp.max(st, axis=1, keepdims=True), sk)
    p = jnp.exp(st - m)
    den = jnp.sum(p, axis=1, keepdims=True) + jnp.exp(sk - m)
    pn = (p * (1.0 / den)).astype(BF16)
    o = _bdot_tn(pn, vs)
    lo = jnp.logical_not(slot_hi)
    for i, (b, c) in enumerate(pairs):
        oc = o[i]
        col0 = jnp.where(lo, oc[0:chunk], pltpu.roll(oc[chunk:2 * chunk], hd, axis=1))
        col1 = jnp.where(lo, pltpu.roll(oc[2 * chunk:3 * chunk], hd, axis=1), oc[3 * chunk:4 * chunk])
        y_ref[b, c * chunk:(c + 1) * chunk, :] = jnp.concatenate([col0, col1], axis=1).astype(BF16)


def _attn(ud, hk, hv, layer, qg, kg, sinks, chunk, hist_valid):
    b, l, _ = ud.shape
    tq = min(l, 256)
    assert l % tq == 0 and tq % chunk == 0
    kvw = N_KV_HEADS * HEAD_DIM
    return pl.pallas_call(
        functools.partial(_attn_body, layer=layer, nb=b, tq=tq, chunk=chunk, hist_valid=hist_valid),
        grid=(l // tq,),
        in_specs=[
            pl.BlockSpec(memory_space=pltpu.SMEM),
            pl.BlockSpec((b, tq, D_COLS), lambda j: (0, j, 0)),
            pl.BlockSpec((b, WINDOW, kvw), lambda j: (0, 0, 0)),
            pl.BlockSpec((b, WINDOW, kvw), lambda j: (0, 0, 0)),
            _layer_block((1, GROUP_WIDTH), layer, 1),
            _layer_block((1, kvw), layer, 1),
        ],
        out_specs=[
            pl.BlockSpec((b, tq, GROUP_WIDTH), lambda j: (0, j, 0)),
            pl.BlockSpec((b, WINDOW, kvw), lambda j: (0, 0, 0)),
            pl.BlockSpec((b, WINDOW, kvw), lambda j: (0, 0, 0)),
        ],
        out_shape=[
            jax.ShapeDtypeStruct((b, l, GROUP_WIDTH), BF16),
            jax.ShapeDtypeStruct((b, WINDOW, kvw), F32),
            jax.ShapeDtypeStruct((b, WINDOW, kvw), F32),
        ],
        scratch_shapes=[pltpu.VMEM((b, WINDOW, kvw), F32), pltpu.VMEM((b, WINDOW, kvw), F32)],
        compiler_params=_params(("arbitrary",)),
        name="attn",
    )(sinks, ud, hk, hv, qg, kg)


def _rwkv_body(u_ref, sp_ref, s0_ref, mu_ref, wl_ref, vec_ref, y_ref, sn_ref,
               prev_ref, s_ref, yacc_ref, *, nb, tb, chunk):
    li = pl.program_id(0)
    gw = GROUP_WIDTH
    hd = RWKV_HEAD
    nh = RWKV_HEADS
    rows = nb * tb
    nc = tb // chunk

    @pl.when(li == 0)
    def _():
        prev_ref[...] = sp_ref[...]
        zero = jnp.zeros((nb, hd, hd), F32)
        for h in range(nh):
            sh = s0_ref[:, h]
            s_ref[h * nb:(h + 1) * nb] = jnp.concatenate([sh, zero] if h % 2 == 0 else [zero, sh], axis=-1)

    row = lax.broadcasted_iota(jnp.int32, (tb, 1), 0)
    mu = mu_ref[...]
    xs_parts = []
    for b in range(nb):
        ub = u_ref[b]
        prev = jnp.where(row == 0, prev_ref[b], pltpu.roll(ub, 1, axis=0))
        prev_ref[b] = ub[tb - 1:tb, :]
        xs_parts.append(ub + mu * (prev - ub))
    xs = jnp.concatenate(xs_parts, axis=0)
    r = xs[:, 0:gw]
    k = xs[:, gw:2 * gw]
    v = xs[:, 2 * gw:3 * gw]
    lat = xs[:, 3 * gw:B_COLS]
    lane_lat = lax.broadcasted_iota(jnp.int32, (1, B_COLS - 3 * gw), 1)
    act = jnp.where(lane_lat < 32, jnp.tanh(lat), jnp.where(lane_lat < 64, lat, _sigmoid(lat)))
    lo = _dot(act.astype(BF16), wl_ref[...])
    w0 = vec_ref[0:1, :]
    a0 = vec_ref[1:2, :]
    k_k = vec_ref[2:3, :]
    k_a = vec_ref[3:4, :]
    r_k = vec_ref[4:5, :]
    gn_g = vec_ref[5:6, :]
    gn_b = vec_ref[6:7, :]
    z = -(w0 + lo[:, 0:gw])
    softplus = jnp.maximum(z, 0.0) + jnp.log(1.0 + jnp.exp(-jnp.abs(z)))
    logw = -jnp.exp(-softplus - 0.5)
    a_rate = _sigmoid(a0 + lo[:, gw:2 * gw])
    gate = lo[:, 2 * gw:3 * gw]
    ones_blk = _block_matrix(gw, hd, 1.0)
    kk = k * k_k
    kk = kk * lax.rsqrt(jnp.maximum(_split_dot_right(kk * kk, ones_blk, 1), 1e-24))
    k_mod = k * (1.0 + (a_rate - 1.0) * k_a)
    b_v = kk * a_rate
    bonus = _split_dot_right(r * k_mod * r_k, ones_blk, 1) * v

    grp = min(rows, 256)
    gi = lax.broadcasted_iota(jnp.int32, (grp, grp), 0)
    gj = lax.broadcasted_iota(jnp.int32, (grp, grp), 1)
    csh = int(math.log2(chunk))
    tri = jnp.where(jnp.logical_and(lax.shift_right_logical(gi, csh) == lax.shift_right_logical(gj, csh),
                                    gj <= gi), 1.0, 0.0).astype(BF16)
    cum = jnp.concatenate([_split_dot_left(tri, logw[g0:g0 + grp], 2) for g0 in range(0, rows, grp)], axis=0)
    cum3 = cum.reshape(nb * nc, chunk, gw)
    cum_c = cum3[:, chunk - 1:chunk, :]
    e_end = jnp.exp(cum_c - cum3).reshape(rows, gw)
    w_c = jnp.exp(cum_c)
    e_neg = jnp.exp(-cum)
    dense = dict(
        a=-kk * jnp.exp(cum - logw),
        r=r * jnp.exp(cum),
        bt=b_v * e_neg,
        kt=k_mod * e_neg,
        bh=b_v * e_end,
        kh=k_mod * e_end,
    )
    slot_hi = lax.broadcasted_iota(jnp.int32, (1, 2 * hd), 1) >= hd

    def head_tile(x, h, own_slot):
        col = x[:, (h // 2) * 2 * hd:(h // 2 + 1) * 2 * hd]
        keep = slot_hi if (h % 2 == 1) == own_slot else jnp.logical_not(slot_hi)
        if x.dtype == BF16:
            return col * jnp.where(keep, 1.0, 0.0).astype(BF16)
        return jnp.where(keep, col, 0.0)

    v_sw = jnp.concatenate([pltpu.roll(v[:, j * 2 * hd:(j + 1) * 2 * hd], hd, axis=1) for j in range(nh // 2)],
                           axis=1)
    tiles = {"r": [head_tile(dense["r"], h, True) for h in range(nh)]}
    dense_b = {name: dense[name].astype(BF16) for name in ("a", "bt", "kt", "bh", "kh")}
    tiles_b = {name: [head_tile(dense_b[name], h, True) for h in range(nh)] for name in ("a", "bh", "kh")}
    for name in ("bt", "kt"):
        tiles_b[name] = [dense_b[name][:, (h // 2) * 2 * hd:(h // 2 + 1) * 2 * hd] for h in range(nh)]
    v_sw_b = v_sw.astype(BF16)
    v_tiles = [head_tile(v_sw_b, h, False) for h in range(nh)]

    def blocks(per_head, c):
        return jnp.stack([per_head[h][b * tb + c * chunk:b * tb + (c + 1) * chunk]
                          for h in range(nh) for b in range(nb)])

    n = nh * nb
    ri = lax.broadcasted_iota(jnp.int32, (2 * chunk, 2 * chunk), 0)
    ci = jnp.bitwise_and(lax.broadcasted_iota(jnp.int32, (2 * chunk, 2 * chunk), 1), chunk - 1)
    gmask = ci < jnp.bitwise_and(ri, chunk - 1) + lax.shift_right_logical(ri, csh)
    zeros_c = jnp.zeros((n, chunk, 2 * hd), BF16)
    n_sq = int(math.log2(chunk))
    for c in range(nc):
        a_b = blocks(tiles_b["a"], c)
        r_f = blocks(tiles["r"], c)
        v_b = blocks(v_tiles, c)
        ar = jnp.concatenate([a_b, r_f.astype(BF16)], axis=1)
        bk = jnp.concatenate([blocks(tiles_b["bt"], c), blocks(tiles_b["kt"], c)], axis=1)
        bhkh = jnp.concatenate([blocks(tiles_b["bh"], c), blocks(tiles_b["kh"], c)], axis=1)
        g = jnp.where(gmask, _bdot_nt(ar, bk), 0.0)
        g_top = g[:, :chunk, :]
        g_bot = g[:, chunk:, :].astype(BF16)
        w = a_b.astype(F32) + _bdot(g_top.astype(BF16), jnp.concatenate([zeros_c, v_b], axis=1))
        p = g_top[:, :, :chunk]
        for i in range(n_sq):
            pb = p.astype(BF16)
            if i + 1 < n_sq:
                res = _bdot(pb, jnp.concatenate([w.astype(BF16), pb], axis=2))
                w = w + res[:, :, :2 * hd]
                p = res[:, :, 2 * hd:]
            else:
                hc = chunk // 2
                low = w[:, hc:, :] + _bdot(pb[:, hc:, :hc], w[:, :hc, :].astype(BF16))
                w = jnp.concatenate([w[:, :hc, :], low], axis=1)
        xv = jnp.concatenate([w.astype(BF16), v_b], axis=1)
        ry = _bdot(g_bot, xv)
        mp = _bdot_tn(xv, bhkh)
        s_old = s_ref[...]
        s_b = s_old.astype(BF16)
        y_nt = _bdot_nt((ry + r_f).astype(BF16), s_b)
        wc = jnp.stack([w_c[b * nc + c][:, (h // 2) * 2 * hd:(h // 2 + 1) * 2 * hd]
                        for h in range(nh) for b in range(nb)])
        psi = jnp.concatenate(
            [mp[h * nb:(h + 1) * nb, (1 - h % 2) * hd:(2 - h % 2) * hd, :] for h in range(nh)], axis=0)
        s_ref[...] = s_old * wc + _bdot(s_b, mp.astype(BF16)) + psi
        for b in range(nb):
            ys = [y_nt[h * nb + b] + ry[h * nb + b][:, (1 - h % 2) * hd:(2 - h % 2) * hd] for h in range(nh)]
            yacc_ref[b * tb + c * chunk:b * tb + (c + 1) * chunk, :] = jnp.concatenate(ys, axis=1)

    y = yacc_ref[...]
    avg_blk = _block_matrix(gw, hd, 1.0 / hd)
    m = _split_dot_right(y, avg_blk, 1)
    d = y - m
    var = _split_dot_right(d * d, avg_blk, 1)
    yn = d * lax.rsqrt(var + GN_EPS) * gn_g + gn_b
    out = ((yn + bonus) * gate).astype(BF16)
    for b in range(nb):
        y_ref[b] = out[b * tb:(b + 1) * tb]

    @pl.when(li == pl.num_programs(0) - 1)
    def _():
        for h in range(nh):
            sn_ref[:, h] = s_ref[h * nb:(h + 1) * nb, :, (h % 2) * hd:(h % 2 + 1) * hd]


def _rwkv(ub, shift_prev, state, layer, mu, wl, vec, chunk):
    b, l, _ = ub.shape
    tb = min(l, 256)
    assert l % tb == 0 and tb % chunk == 0
    hd = RWKV_HEAD
    n = RWKV_HEADS * b
    sshape = (b, RWKV_HEADS, hd, hd)
    return pl.pallas_call(
        functools.partial(_rwkv_body, nb=b, tb=tb, chunk=chunk),
        grid=(l // tb,),
        in_specs=[
            pl.BlockSpec((b, tb, B_COLS), lambda j: (0, j, 0)),
            pl.BlockSpec((b, 1, B_COLS), lambda j: (0, 0, 0)),
            pl.BlockSpec(sshape, lambda j: (0, 0, 0, 0)),
            _layer_block((1, B_COLS), layer, 1),
            _layer_block((128, 3 * GROUP_WIDTH), layer, 1),
            _layer_block((8, GROUP_WIDTH), layer, 1),
        ],
        out_specs=[
            pl.BlockSpec((b, tb, GROUP_WIDTH), lambda j: (0, j, 0)),
            pl.BlockSpec(sshape, lambda j: (0, 0, 0, 0)),
        ],
        out_shape=[
            jax.ShapeDtypeStruct((b, l, GROUP_WIDTH), BF16),
            jax.ShapeDtypeStruct(sshape, F32),
        ],
        scratch_shapes=[
            pltpu.VMEM((b, 1, B_COLS), F32),
            pltpu.VMEM((n, hd, 2 * hd), F32),
            pltpu.VMEM((b * tb, GROUP_WIDTH), F32),
        ],
        compiler_params=_params(("arbitrary",)),
        name="rwkv",
    )(ub, shift_prev, state, mu, wl, vec)


def _outffn_body(x_ref, ya_ref, yb_ref, yc_ref, yd_ref, wo_ref, g_ref, wg_ref, wu_ref, wd_ref, o_ref):
    ycat = jnp.concatenate([ya_ref[...], yb_ref[...], yc_ref[...], yd_ref[...]], axis=-1)
    x1 = x_ref[...] + _dot(ycat, wo_ref[...])
    ms = jnp.mean(x1 * x1, axis=-1, keepdims=True)
    hn = ((x1 * lax.rsqrt(ms + RMS_EPS)) * g_ref[...]).astype(BF16)
    hg = _dot(hn, wg_ref[...])
    hu = _dot(hn, wu_ref[...])
    act = (hg * _sigmoid(hg) * hu).astype(BF16)
    o_ref[...] = x1 + _dot(act, wd_ref[...])


def _outffn(x2d, ya, yb, yc, yd, layer, wo, g, wg, wu, wd):
    t = x2d.shape[0]
    tm = min(t, 512)
    assert t % tm == 0
    row = lambda w: pl.BlockSpec((tm, w), lambda i: (i, 0))
    return pl.pallas_call(
        _outffn_body,
        grid=(t // tm,),
        in_specs=[
            row(D_MODEL), row(GROUP_WIDTH), row(GROUP_WIDTH), row(GROUP_WIDTH), row(GROUP_WIDTH),
            _layer_block((4 * GROUP_WIDTH, D_MODEL), layer, 1),
            _layer_block((1, D_MODEL), layer, 1),
            _layer_block((D_MODEL, D_FF), layer, 1),
            _layer_block((D_MODEL, D_FF), layer, 1),
            _layer_block((D_FF, D_MODEL), layer, 1),
        ],
        out_specs=row(D_MODEL),
        out_shape=jax.ShapeDtypeStruct((t, D_MODEL), F32),
        compiler_params=_params(("parallel",)),
        name="outffn",
    )(x2d, ya, yb, yc, yd, wo, g, wg, wu, wd)


def _rows8(vectors):
    stacked = jnp.stack(vectors, axis=1)
    return jnp.pad(stacked, ((0, 0), (0, SUBLANES - stacked.shape[1]), (0, 0)))


def _stacked_weights(p):
    gw = GROUP_WIDTH
    depth = p['w_in'].shape[0]
    place = lambda a, before, after: jnp.pad(a, ((0, 0), (0, 0), (before, after)))
    wl = jnp.concatenate([place(p['rwkv_w2'], 0, 2 * gw), place(p['rwkv_a2'], gw, gw),
                          place(p['rwkv_g2'], 2 * gw, 0)], axis=1)
    pc = gw // len(POOL_WINDOWS)
    wbd = jnp.concatenate([place(p['pool_w'][:, g], g * pc, gw - (g + 1) * pc)
                           for g in range(len(POOL_WINDOWS))], axis=1)
    return dict(
        norm_mix_g=p['norm_mix_g'][:, None, :],
        w_in=p['w_in'].astype(BF16),
        conv_w=jnp.pad(p['conv_w'], ((0, 0), (0, CONV_PAD - CONV_WIDTH), (0, 0))),
        conv_vec=_rows8([p['conv_b'], p['conv_ln_g'], p['conv_ln_b']]),
        rwkv_mu=p['rwkv_mu'][:, None, :],
        rwkv_wl=wl.astype(BF16),
        rwkv_vec=_rows8([p['rwkv_w0'], p['rwkv_a0'], p['rwkv_k_k'], p['rwkv_k_a'],
                         p['rwkv_r_k'].reshape(depth, gw), p['rwkv_gn_g'], p['rwkv_gn_b']]),
        pool_wbd=wbd.astype(BF16),
        pool_scale=p['pool_scale'][:, None, :],
        attn_qg=jnp.tile(p['attn_q_norm'], (1, N_Q_HEADS))[:, None, :],
        attn_kg=jnp.tile(p['attn_k_norm'], (1, N_KV_HEADS))[:, None, :],
        attn_sinks=p['attn_sinks'],
        w_out=p['w_out'].astype(BF16),
        norm_ffn_g=p['norm_ffn_g'][:, None, :],
        ffn_w_gate=p['ffn_w_gate'].astype(BF16),
        ffn_w_up=p['ffn_w_up'].astype(BF16),
        ffn_w_down=p['ffn_w_down'].astype(BF16),
    )


def _trunk(x, conv_hist, rwkv_state, shift_prev, pool_hist, k_cache, v_cache, pos0, chunk, w):
    b, l, _ = x.shape
    has_past = conv_hist is not None
    kvw = N_KV_HEADS * HEAD_DIM
    new = [[] for _ in range(6)]
    for li in range(DEPTH):
        if has_past:
            ch = jnp.pad(conv_hist[li], ((0, 0), (CONV_PAD - CONV_HIST, 0), (0, 0)))
            rs = rwkv_state[li]
            sp = shift_prev[li][:, None, :]
            ph = jnp.pad(pool_hist[li], ((0, 0), (POOL_PAD - POOL_HIST, 0), (0, 0)))
            hk = k_cache[li].reshape(b, WINDOW, kvw)
            hv = v_cache[li].reshape(b, WINDOW, kvw)
        else:
            ch = jnp.zeros((b, CONV_PAD, GROUP_WIDTH), F32)
            rs = jnp.zeros((b, RWKV_HEADS, RWKV_HEAD, RWKV_HEAD), F32)
            sp = jnp.zeros((b, 1, B_COLS), F32)
            ph = jnp.zeros((b, POOL_PAD, GROUP_WIDTH), F32)
            hk = jnp.zeros((b, WINDOW, kvw), F32)
            hv = jnp.zeros((b, WINDOW, kvw), F32)
        ub, ud, ya, yc, conv_tail, pool_tail = _projmix(
            x, li, w['norm_mix_g'], w['w_in'], ch, w['conv_w'], w['conv_vec'],
            ph, w['pool_wbd'], w['pool_scale'], pos0)
        yb, s_new = _rwkv(ub, sp, rs, li, w['rwkv_mu'], w['rwkv_wl'], w['rwkv_vec'], min(CHUNK, l))
        yd, k_tail, v_tail = _attn(ud, hk, hv, li, w['attn_qg'], w['attn_kg'], w['attn_sinks'], chunk,
                                   has_past)
        flat = lambda y: y.reshape(b * l, GROUP_WIDTH)
        x = _outffn(x.reshape(b * l, D_MODEL), flat(ya), flat(yb), flat(yc), flat(yd), li, w['w_out'],
                    w['norm_ffn_g'], w['ffn_w_gate'], w['ffn_w_up'], w['ffn_w_down']
                    ).reshape(b, l, D_MODEL)
        new[0].append(conv_tail[:, CONV_PAD - CONV_HIST:, :])
        new[1].append(s_new)
        new[2].append(ub[:, l - 1, :])
        new[3].append(pool_tail[:, POOL_PAD - POOL_HIST:, :])
        new[4].append(k_tail.reshape(b, WINDOW, N_KV_HEADS, HEAD_DIM))
        new[5].append(v_tail.reshape(b, WINDOW, N_KV_HEADS, HEAD_DIM))
    return x, tuple(jnp.stack(n) for n in new)


def kernel(x_prompt, x_sample, cache_conv, state_rwkv, state_rwkv_shift, cache_pool, cache_k, cache_v, norm_mix_g, w_in, conv_w, conv_b, conv_ln_g, conv_ln_b, rwkv_mu, rwkv_w0, rwkv_w2, rwkv_a0, rwkv_a2, rwkv_g2, rwkv_k_k, rwkv_k_a, rwkv_r_k, rwkv_gn_g, rwkv_gn_b, pool_w, pool_scale, attn_q_norm, attn_k_norm, attn_sinks, w_out, norm_ffn_g, ffn_w_gate, ffn_w_up, ffn_w_down):
    w = _stacked_weights(dict(
        norm_mix_g=norm_mix_g, w_in=w_in, conv_w=conv_w, conv_b=conv_b, conv_ln_g=conv_ln_g,
        conv_ln_b=conv_ln_b, rwkv_mu=rwkv_mu, rwkv_w0=rwkv_w0, rwkv_w2=rwkv_w2, rwkv_a0=rwkv_a0,
        rwkv_a2=rwkv_a2, rwkv_g2=rwkv_g2, rwkv_k_k=rwkv_k_k, rwkv_k_a=rwkv_k_a, rwkv_r_k=rwkv_r_k,
        rwkv_gn_g=rwkv_gn_g, rwkv_gn_b=rwkv_gn_b, pool_w=pool_w, pool_scale=pool_scale,
        attn_q_norm=attn_q_norm, attn_k_norm=attn_k_norm, attn_sinks=attn_sinks, w_out=w_out,
        norm_ffn_g=norm_ffn_g, ffn_w_gate=ffn_w_gate, ffn_w_up=ffn_w_up, ffn_w_down=ffn_w_down))
    y_p, (conv_p, rwkv_p, shift_p, pool_p, k_p, v_p) = _trunk(
        x_prompt, None, None, None, None, None, None, 0, CHUNK, w)
    y_s, (conv_s, rwkv_s, shift_s, pool_s, k_s, v_s) = _trunk(
        x_sample, cache_conv, state_rwkv, state_rwkv_shift, cache_pool, cache_k, cache_v,
        PAST_LEN, x_sample.shape[1], w)
    return (y_p, y_s, conv_p, conv_s, rwkv_p, rwkv_s, shift_p, shift_s,
            pool_p, pool_s, k_p, k_s, v_p, v_s)
```

```python
import functools
import math

import jax
import jax.numpy as jnp
from jax import lax
from jax.experimental import pallas as pl
from jax.experimental.pallas import tpu as pltpu

F32 = jnp.float32
BF16 = jnp.bfloat16

D_MODEL = 1024
DEPTH = 2
PAST_LEN = 1024
CHUNK = 64
GROUP_WIDTH = 256
CONV_WIDTH = 31
CONV_HIST = CONV_WIDTH - 1
CONV_PAD = 32
RWKV_HEAD = 64
RWKV_HEADS = 4
POOL_WINDOWS = (2, 4, 8, 16)
POOL_HIST = 15
POOL_PAD = 16
HEAD_DIM = 64
N_Q_HEADS = 4
N_KV_HEADS = 2
WINDOW = 128
D_FF = 2816
A_COLS = 512
B_COLS = 896
C_COLS = 256
D_COLS = 512
IN_COLS = A_COLS + B_COLS + C_COLS + D_COLS
RMS_EPS = 1e-6
LN_EPS = 1e-5
GN_EPS = 64e-5
ATTN_SCALE = HEAD_DIM ** -0.5
NEG_INF = -1e30

DECAY_RANK = 32
AAA_RANK = 32
Q_COLS = N_Q_HEADS * HEAD_DIM
KV_COLS = N_KV_HEADS * HEAD_DIM

VMEM_LIMIT_BYTES = 56 * 1024 * 1024
SUBLANES = 8
MATMUL_ROWS = 512
MIXER_ROWS = 256
LOCAL_ROWS = 128


def _dot(a, b):
    return jnp.dot(a, b, preferred_element_type=F32)


def _bdot(a, b):
    return lax.dot_general(a, b, (((2,), (1,)), ((0,), (0,))), preferred_element_type=F32)


def _bdot_nt(a, b):
    return lax.dot_general(a, b, (((2,), (2,)), ((0,), (0,))), preferred_element_type=F32)


def _bdot_tn(a, b):
    return lax.dot_general(a, b, (((1,), (1,)), ((0,), (0,))), preferred_element_type=F32)


def _sigmoid(x):
    return 1.0 / (1.0 + jnp.exp(-x))


def _split_dot_right(x, m_bf16, terms):
    acc = None
    rem = x
    for i in range(terms):
        hi = rem.astype(BF16)
        d = _dot(hi, m_bf16)
        acc = d if acc is None else acc + d
        if i + 1 < terms:
            rem = rem - hi.astype(F32)
    return acc


def _split_dot_left(m_bf16, x, terms):
    acc = None
    rem = x
    for i in range(terms):
        hi = rem.astype(BF16)
        d = _dot(m_bf16, hi)
        acc = d if acc is None else acc + d
        if i + 1 < terms:
            rem = rem - hi.astype(F32)
    return acc


def _block_matrix(n, blk, value):
    sh = int(math.log2(blk))
    r = lax.shift_right_logical(lax.broadcasted_iota(jnp.int32, (n, n), 0), sh)
    c = lax.shift_right_logical(lax.broadcasted_iota(jnp.int32, (n, n), 1), sh)
    return jnp.where(r == c, value, 0.0).astype(BF16)


def _params(sem):
    return pltpu.CompilerParams(dimension_semantics=sem, vmem_limit_bytes=VMEM_LIMIT_BYTES)


def _layer_block(shape, layer, n_grid):
    zeros = (0,) * len(shape)
    if n_grid == 1:
        index_map = lambda i: (layer,) + zeros
    else:
        index_map = lambda i, j: (layer,) + zeros
    return pl.BlockSpec((None,) + tuple(shape), index_map, pipeline_mode=pl.Buffered(1))


def _phase_copies(ext_ref, sh_ref):
    n = sh_ref.shape[1]
    for s in range(1, SUBLANES):
        sh_ref[s - 1] = ext_ref[s:s + n, :]


def _shifted_rows(ext_ref, sh_ref, start, rb):
    a, s = divmod(start, SUBLANES)
    base = a * SUBLANES
    return ext_ref[base:base + rb, :] if s == 0 else sh_ref[s - 1, base:base + rb, :]


def _conv_rows(ext_ref, sh_ref, w_ref, vec_ref, r0, rb):
    off = CONV_PAD - CONV_HIST
    acc = jnp.zeros((rb, GROUP_WIDTH), F32)
    for j in range(CONV_WIDTH):
        acc = acc + w_ref[j:j + 1, :] * _shifted_rows(ext_ref, sh_ref, r0 + off + j, rb)
    acc = acc + vec_ref[0:1, :]
    mu = jnp.mean(acc, axis=-1, keepdims=True)
    d = acc - mu
    var = jnp.mean(d * d, axis=-1, keepdims=True)
    yn = d * lax.rsqrt(var + LN_EPS) * vec_ref[1:2, :] + vec_ref[2:3, :]
    return yn * _sigmoid(yn)


def _pool_rows(ext_ref, sh_ref, wbd_ref, sc_ref, r0, rb, pos_start):
    base = POOL_PAD + r0
    sums = []
    acc = None
    for i in range(max(POOL_WINDOWS)):
        sh = _shifted_rows(ext_ref, sh_ref, base - i, rb)
        acc = sh if acc is None else acc + sh
        if i + 1 in POOL_WINDOWS:
            sums.append(acc)
    pos = pos_start + r0 + lax.broadcasted_iota(jnp.int32, (rb, 1), 0)
    means = [s / jnp.minimum(w, pos + 1).astype(F32) for s, w in zip(sums, POOL_WINDOWS)]
    lane = lax.broadcasted_iota(jnp.int32, (1, GROUP_WIDTH), 1)
    pc = GROUP_WIDTH // len(POOL_WINDOWS)
    mean = means[-1]
    for g in range(len(POOL_WINDOWS) - 2, -1, -1):
        mean = jnp.where(lane < (g + 1) * pc, means[g], mean)
    d = mean - ext_ref[base:base + rb, :]
    return _dot(d.astype(BF16), wbd_ref[...]) * sc_ref[...]


def _projmix_body(x_ref, g_ref, w_ref, chist_ref, cw_ref, cvec_ref, phist_ref, pwbd_ref, psc_ref,
                  ub_ref, ud_ref, ya_ref, yc_ref, ctail_ref, ptail_ref,
                  cext_ref, csh_ref, pext_ref, psh_ref, *, nb, tl, rb, pos0):
    li = pl.program_id(1)

    @pl.when(li == 0)
    def _():
        cext_ref[:, 0:CONV_PAD, :] = chist_ref[...]
        pext_ref[:, 0:POOL_PAD, :] = phist_ref[...]

    x = x_ref[...].reshape(nb * tl, D_MODEL)
    ms = jnp.mean(x * x, axis=-1, keepdims=True)
    xn = ((x * lax.rsqrt(ms + RMS_EPS)) * g_ref[...]).astype(BF16)
    b0 = A_COLS
    c0 = A_COLS + B_COLS
    d0 = c0 + C_COLS
    ua = _dot(xn, w_ref[:, 0:A_COLS])
    uc = _dot(xn, w_ref[:, c0:d0])
    ub_ref[...] = _dot(xn, w_ref[:, b0:c0]).reshape(nb, tl, B_COLS)
    ud_ref[...] = _dot(xn, w_ref[:, d0:IN_COLS]).reshape(nb, tl, D_COLS)
    glu = ua[:, :GROUP_WIDTH] * _sigmoid(ua[:, GROUP_WIDTH:])
    for b in range(nb):
        cext = cext_ref.at[b]
        pext = pext_ref.at[b]
        cext[CONV_PAD:CONV_PAD + tl, :] = glu[b * tl:(b + 1) * tl]
        pext[POOL_PAD:POOL_PAD + tl, :] = uc[b * tl:(b + 1) * tl]
        _phase_copies(cext, csh_ref)
        _phase_copies(pext, psh_ref)
        for r0 in range(0, tl, rb):
            ya_ref[b, r0:r0 + rb, :] = _conv_rows(cext, csh_ref, cw_ref, cvec_ref, r0, rb).astype(BF16)
            yc_ref[b, r0:r0 + rb, :] = _pool_rows(pext, psh_ref, pwbd_ref, psc_ref, r0, rb,
                                                  pos0 + li * tl).astype(BF16)
        ctail = cext[tl:tl + CONV_PAD, :]
        ptail = pext[tl:tl + POOL_PAD, :]
        ctail_ref[b] = ctail
        ptail_ref[b] = ptail
        cext[0:CONV_PAD, :] = ctail
        pext[0:POOL_PAD, :] = ptail


def _projmix(x, layer, g, w_in, chist, conv_w, conv_vec, phist, pool_wbd, pool_scale, pos0):
    b, l, _ = x.shape
    nb = b if l <= 64 else 1
    tl = min(l, MATMUL_ROWS)
    rb = min(tl, LOCAL_ROWS)
    assert l % tl == 0 and tl % rb == 0 and tl >= CONV_PAD and b % nb == 0
    gw = GROUP_WIDTH
    tile = lambda w: pl.BlockSpec((nb, tl, w), lambda i, j: (i, j, 0))
    per_seq = lambda r: pl.BlockSpec((nb, r, gw), lambda i, j: (i, 0, 0))
    return pl.pallas_call(
        functools.partial(_projmix_body, nb=nb, tl=tl, rb=rb, pos0=pos0),
        grid=(b // nb, l // tl),
        in_specs=[
            tile(D_MODEL),
            _layer_block((1, D_MODEL), layer, 2),
            _layer_block((D_MODEL, IN_COLS), layer, 2),
            per_seq(CONV_PAD), _layer_block((CONV_PAD, gw), layer, 2), _layer_block((8, gw), layer, 2),
            per_seq(POOL_PAD), _layer_block((gw, gw), layer, 2), _layer_block((1, gw), layer, 2),
        ],
        out_specs=[tile(B_COLS), tile(D_COLS), tile(gw), tile(gw), per_seq(CONV_PAD), per_seq(POOL_PAD)],
        out_shape=[
            jax.ShapeDtypeStruct((b, l, B_COLS), F32),
            jax.ShapeDtypeStruct((b, l, D_COLS), F32),
            jax.ShapeDtypeStruct((b, l, gw), BF16),
            jax.ShapeDtypeStruct((b, l, gw), BF16),
            jax.ShapeDtypeStruct((b, CONV_PAD, gw), F32),
            jax.ShapeDtypeStruct((b, POOL_PAD, gw), F32),
        ],
        scratch_shapes=[
            pltpu.VMEM((nb, CONV_PAD + tl, gw), F32),
            pltpu.VMEM((SUBLANES - 1, CONV_PAD + tl - SUBLANES, gw), F32),
            pltpu.VMEM((nb, POOL_PAD + tl, gw), F32),
            pltpu.VMEM((SUBLANES - 1, POOL_PAD + tl - SUBLANES, gw), F32),
        ],
        compiler_params=_params(("parallel", "arbitrary")),
        name="projmix",
    )(x, g, w_in, chist, conv_w, conv_vec, phist, pool_wbd, pool_scale)


def _attn_body(sink_ref, u_ref, hk_ref, hv_ref, qg_ref, kg_ref, y_ref, kt_ref, vt_ref,
               kh_ref, vh_ref, *, layer, nb, tq, chunk, hist_valid):
    li = pl.program_id(0)
    hd = HEAD_DIM
    rows = nb * tq
    ncq = tq // chunk
    kw = WINDOW + chunk

    @pl.when(li == 0)
    def _():
        kh_ref[...] = hk_ref[...]
        vh_ref[...] = hv_ref[...]

    u = u_ref[...].reshape(rows, D_COLS)
    q = u[:, 0:Q_COLS]
    k = u[:, Q_COLS:Q_COLS + KV_COLS]
    v = u[:, Q_COLS + KV_COLS:D_COLS]
    inv = 1.0 / hd
    qms = _split_dot_right(q * q, _block_matrix(Q_COLS, hd, inv), 1)
    kms = _split_dot_right(k * k, _block_matrix(KV_COLS, hd, inv), 2)
    qn = (q * lax.rsqrt(qms + RMS_EPS)) * (qg_ref[...] * ATTN_SCALE)
    kn = (k * lax.rsqrt(kms + RMS_EPS)) * kg_ref[...]
    kcat = jnp.concatenate([kh_ref[...], kn.reshape(nb, tq, 2 * hd)], axis=1)
    vcat = jnp.concatenate([vh_ref[...], v.reshape(nb, tq, 2 * hd)], axis=1)
    ktail = kcat[:, tq:tq + WINDOW, :]
    vtail = vcat[:, tq:tq + WINDOW, :]
    kt_ref[...] = ktail
    vt_ref[...] = vtail
    kh_ref[...] = ktail
    vh_ref[...] = vtail
    kcat_b = kcat.astype(BF16)
    vcat_b = vcat.astype(BF16)

    slot_hi = lax.broadcasted_iota(jnp.int32, (1, 2 * hd), 1) >= hd
    q_tiles = []
    for h in range(N_Q_HEADS):
        col = qn[:, (h // 2) * 2 * hd:(h // 2 + 1) * 2 * hd]
        g = h // (N_Q_HEADS // N_KV_HEADS)
        if h % 2 != g:
            col = pltpu.roll(col, hd, axis=1)
        q_tiles.append(jnp.where(slot_hi if g == 1 else jnp.logical_not(slot_hi), col, 0.0).astype(BF16))
    pairs = [(b, c) for c in range(ncq) for b in range(nb)]
    qs = jnp.stack([jnp.concatenate([t[b * tq + c * chunk:b * tq + (c + 1) * chunk] for t in q_tiles], axis=0)
                    for b, c in pairs])
    ks = jnp.stack([kcat_b[b, c * chunk:c * chunk + kw] for b, c in pairs])
    vs = jnp.stack([vcat_b[b, c * chunk:c * chunk + kw] for b, c in pairs])
    st = _bdot_nt(ks, qs)
    nq = N_Q_HEADS * chunk
    if not hist_valid:
        n_edge = min(ncq, WINDOW // chunk) * nb
        cpos = [c * chunk for _, c in pairs[:n_edge]]
        kpos = lax.broadcasted_iota(jnp.int32, (1, kw, nq), 1) + (li * tq - WINDOW)
        edge = jnp.concatenate([jnp.where(kpos + cp >= 0, st[i:i + 1], NEG_INF) for i, cp in enumerate(cpos)],
                               axis=0)
        st = jnp.concatenate([edge, st[n_edge:]], axis=0) if n_edge < len(pairs) else edge
    hlane = lax.broadcasted_iota(jnp.int32, (1, 1, nq), 2)
    sk = jnp.full((1, 1, nq), sink_ref[layer, N_Q_HEADS - 1], F32)
    for h in range(N_Q_HEADS - 2, -1, -1):
        sk = jnp.where(hlane < (h + 1) * chunk, sink_ref[layer, h], sk)
    m = jnp.maximum(jnp.max(st, axis=1, keepdims=True), sk)
    p = jnp.exp(st - m)
    den = jnp.sum(p, axis=1, keepdims=True) + jnp.exp(sk - m)
    pn = (p * (1.0 / den)).astype(BF16)
    o = _bdot_tn(pn, vs)
    lo = jnp.logical_not(slot_hi)
    for i, (b, c) in enumerate(pairs):
        oc = o[i]
        col0 = jnp.where(lo, oc[0:chunk], pltpu.roll(oc[chunk:2 * chunk], hd, axis=1))
        col1 = jnp.where(lo, pltpu.roll(oc[2 * chunk:3 * chunk], hd, axis=1), oc[3 * chunk:4 * chunk])
        y_ref[b, c * chunk:(c + 1) * chunk, :] = jnp.concatenate([col0, col1], axis=1).astype(BF16)


def _attn(ud, hk, hv, layer, qg, kg, sinks, chunk, hist_valid):
    b, l, _ = ud.shape
    tq = min(l, MIXER_ROWS)
    assert l % tq == 0 and tq % chunk == 0
    kvw = N_KV_HEADS * HEAD_DIM
    return pl.pallas_call(
        functools.partial(_attn_body, layer=layer, nb=b, tq=tq, chunk=chunk, hist_valid=hist_valid),
        grid=(l // tq,),
        in_specs=[
            pl.BlockSpec(memory_space=pltpu.SMEM),
            pl.BlockSpec((b, tq, D_COLS), lambda j: (0, j, 0)),
            pl.BlockSpec((b, WINDOW, kvw), lambda j: (0, 0, 0)),
            pl.BlockSpec((b, WINDOW, kvw), lambda j: (0, 0, 0)),
            _layer_block((1, GROUP_WIDTH), layer, 1),
            _layer_block((1, kvw), layer, 1),
        ],
        out_specs=[
            pl.BlockSpec((b, tq, GROUP_WIDTH), lambda j: (0, j, 0)),
            pl.BlockSpec((b, WINDOW, kvw), lambda j: (0, 0, 0)),
            pl.BlockSpec((b, WINDOW, kvw), lambda j: (0, 0, 0)),
        ],
        out_shape=[
            jax.ShapeDtypeStruct((b, l, GROUP_WIDTH), BF16),
            jax.ShapeDtypeStruct((b, WINDOW, kvw), F32),
            jax.ShapeDtypeStruct((b, WINDOW, kvw), F32),
        ],
        scratch_shapes=[pltpu.VMEM((b, WINDOW, kvw), F32), pltpu.VMEM((b, WINDOW, kvw), F32)],
        compiler_params=_params(("arbitrary",)),
        name="attn",
    )(sinks, ud, hk, hv, qg, kg)


def _rwkv_body(u_ref, sp_ref, s0_ref, mu_ref, wl_ref, vec_ref, y_ref, sn_ref,
               prev_ref, s_ref, yacc_ref, *, nb, tb, chunk):
    li = pl.program_id(0)
    gw = GROUP_WIDTH
    hd = RWKV_HEAD
    nh = RWKV_HEADS
    rows = nb * tb
    nc = tb // chunk

    @pl.when(li == 0)
    def _():
        prev_ref[...] = sp_ref[...]
        zero = jnp.zeros((nb, hd, hd), F32)
        for h in range(nh):
            sh = s0_ref[:, h]
            s_ref[h * nb:(h + 1) * nb] = jnp.concatenate([sh, zero] if h % 2 == 0 else [zero, sh], axis=-1)

    row = lax.broadcasted_iota(jnp.int32, (tb, 1), 0)
    mu = mu_ref[...]
    xs_parts = []
    for b in range(nb):
        ub = u_ref[b]
        prev = jnp.where(row == 0, prev_ref[b], pltpu.roll(ub, 1, axis=0))
        prev_ref[b] = ub[tb - 1:tb, :]
        xs_parts.append(ub + mu * (prev - ub))
    xs = jnp.concatenate(xs_parts, axis=0)
    r = xs[:, 0:gw]
    k = xs[:, gw:2 * gw]
    v = xs[:, 2 * gw:3 * gw]
    lat = xs[:, 3 * gw:B_COLS]
    lane_lat = lax.broadcasted_iota(jnp.int32, (1, B_COLS - 3 * gw), 1)
    act = jnp.where(lane_lat < DECAY_RANK, jnp.tanh(lat),
                    jnp.where(lane_lat < DECAY_RANK + AAA_RANK, lat, _sigmoid(lat)))
    lo = _dot(act.astype(BF16), wl_ref[...])
    w0 = vec_ref[0:1, :]
    a0 = vec_ref[1:2, :]
    k_k = vec_ref[2:3, :]
    k_a = vec_ref[3:4, :]
    r_k = vec_ref[4:5, :]
    gn_g = vec_ref[5:6, :]
    gn_b = vec_ref[6:7, :]
    z = -(w0 + lo[:, 0:gw])
    softplus = jnp.maximum(z, 0.0) + jnp.log(1.0 + jnp.exp(-jnp.abs(z)))
    logw = -jnp.exp(-softplus - 0.5)
    a_rate = _sigmoid(a0 + lo[:, gw:2 * gw])
    gate = lo[:, 2 * gw:3 * gw]
    ones_blk = _block_matrix(gw, hd, 1.0)
    kk = k * k_k
    kk = kk * lax.rsqrt(jnp.maximum(_split_dot_right(kk * kk, ones_blk, 1), 1e-24))
    k_mod = k * (1.0 + (a_rate - 1.0) * k_a)
    b_v = kk * a_rate
    bonus = _split_dot_right(r * k_mod * r_k, ones_blk, 1) * v

    grp = min(rows, 256)
    gi = lax.broadcasted_iota(jnp.int32, (grp, grp), 0)
    gj = lax.broadcasted_iota(jnp.int32, (grp, grp), 1)
    csh = int(math.log2(chunk))
    tri = jnp.where(jnp.logical_and(lax.shift_right_logical(gi, csh) == lax.shift_right_logical(gj, csh),
                                    gj <= gi), 1.0, 0.0).astype(BF16)
    cum = jnp.concatenate([_split_dot_left(tri, logw[g0:g0 + grp], 2) for g0 in range(0, rows, grp)], axis=0)
    cum3 = cum.reshape(nb * nc, chunk, gw)
    cum_c = cum3[:, chunk - 1:chunk, :]
    e_end = jnp.exp(cum_c - cum3).reshape(rows, gw)
    w_c = jnp.exp(cum_c)
    e_neg = jnp.exp(-cum)
    dense = dict(
        a=-kk * jnp.exp(cum - logw),
        r=r * jnp.exp(cum),
        bt=b_v * e_neg,
        kt=k_mod * e_neg,
        bh=b_v * e_end,
        kh=k_mod * e_end,
    )
    slot_hi = lax.broadcasted_iota(jnp.int32, (1, 2 * hd), 1) >= hd

    def head_tile(x, h, own_slot):
        col = x[:, (h // 2) * 2 * hd:(h // 2 + 1) * 2 * hd]
        keep = slot_hi if (h % 2 == 1) == own_slot else jnp.logical_not(slot_hi)
        if x.dtype == BF16:
            return col * jnp.where(keep, 1.0, 0.0).astype(BF16)
        return jnp.where(keep, col, 0.0)

    v_sw = jnp.concatenate([pltpu.roll(v[:, j * 2 * hd:(j + 1) * 2 * hd], hd, axis=1) for j in range(nh // 2)],
                           axis=1)
    tiles = {"r": [head_tile(dense["r"], h, True) for h in range(nh)]}
    dense_b = {name: dense[name].astype(BF16) for name in ("a", "bt", "kt", "bh", "kh")}
    tiles_b = {name: [head_tile(dense_b[name], h, True) for h in range(nh)] for name in ("a", "bh", "kh")}
    for name in ("bt", "kt"):
        tiles_b[name] = [dense_b[name][:, (h // 2) * 2 * hd:(h // 2 + 1) * 2 * hd] for h in range(nh)]
    v_sw_b = v_sw.astype(BF16)
    v_tiles = [head_tile(v_sw_b, h, False) for h in range(nh)]

    def blocks(per_head, c):
        return jnp.stack([per_head[h][b * tb + c * chunk:b * tb + (c + 1) * chunk]
                          for h in range(nh) for b in range(nb)])

    n = nh * nb
    ri = lax.broadcasted_iota(jnp.int32, (2 * chunk, 2 * chunk), 0)
    ci = jnp.bitwise_and(lax.broadcasted_iota(jnp.int32, (2 * chunk, 2 * chunk), 1), chunk - 1)
    gmask = ci < jnp.bitwise_and(ri, chunk - 1) + lax.shift_right_logical(ri, csh)
    zeros_c = jnp.zeros((n, chunk, 2 * hd), BF16)
    n_sq = int(math.log2(chunk))
    for c in range(nc):
        a_b = blocks(tiles_b["a"], c)
        r_f = blocks(tiles["r"], c)
        v_b = blocks(v_tiles, c)
        ar = jnp.concatenate([a_b, r_f.astype(BF16)], axis=1)
        bk = jnp.concatenate([blocks(tiles_b["bt"], c), blocks(tiles_b["kt"], c)], axis=1)
        bhkh = jnp.concatenate([blocks(tiles_b["bh"], c), blocks(tiles_b["kh"], c)], axis=1)
        g = jnp.where(gmask, _bdot_nt(ar, bk), 0.0)
        g_top = g[:, :chunk, :]
        g_bot = g[:, chunk:, :].astype(BF16)
        w = a_b.astype(F32) + _bdot(g_top.astype(BF16), jnp.concatenate([zeros_c, v_b], axis=1))
        p = g_top[:, :, :chunk]
        for i in range(n_sq):
            pb = p.astype(BF16)
            if i + 1 < n_sq:
                res = _bdot(pb, jnp.concatenate([w.astype(BF16), pb], axis=2))
                w = w + res[:, :, :2 * hd]
                p = res[:, :, 2 * hd:]
            else:
                hc = chunk // 2
                low = w[:, hc:, :] + _bdot(pb[:, hc:, :hc], w[:, :hc, :].astype(BF16))
                w = jnp.concatenate([w[:, :hc, :], low], axis=1)
        xv = jnp.concatenate([w.astype(BF16), v_b], axis=1)
        ry = _bdot(g_bot, xv)
        mp = _bdot_tn(xv, bhkh)
        s_old = s_ref[...]
        s_b = s_old.astype(BF16)
        y_nt = _bdot_nt((ry + r_f).astype(BF16), s_b)
        wc = jnp.stack([w_c[b * nc + c][:, (h // 2) * 2 * hd:(h // 2 + 1) * 2 * hd]
                        for h in range(nh) for b in range(nb)])
        psi = jnp.concatenate(
            [mp[h * nb:(h + 1) * nb, (1 - h % 2) * hd:(2 - h % 2) * hd, :] for h in range(nh)], axis=0)
        s_ref[...] = s_old * wc + _bdot(s_b, mp.astype(BF16)) + psi
        for b in range(nb):
            ys = [y_nt[h * nb + b] + ry[h * nb + b][:, (1 - h % 2) * hd:(2 - h % 2) * hd] for h in range(nh)]
            yacc_ref[b * tb + c * chunk:b * tb + (c + 1) * chunk, :] = jnp.concatenate(ys, axis=1)

    y = yacc_ref[...]
    avg_blk = _block_matrix(gw, hd, 1.0 / hd)
    m = _split_dot_right(y, avg_blk, 1)
    d = y - m
    var = _split_dot_right(d * d, avg_blk, 1)
    yn = d * lax.rsqrt(var + GN_EPS) * gn_g + gn_b
    out = ((yn + bonus) * gate).astype(BF16)
    for b in range(nb):
        y_ref[b] = out[b * tb:(b + 1) * tb]

    @pl.when(li == pl.num_programs(0) - 1)
    def _():
        for h in range(nh):
            sn_ref[:, h] = s_ref[h * nb:(h + 1) * nb, :, (h % 2) * hd:(h % 2 + 1) * hd]


def _rwkv(ub, shift_prev, state, layer, mu, wl, vec, chunk):
    b, l, _ = ub.shape
    tb = min(l, MIXER_ROWS)
    assert l % tb == 0 and tb % chunk == 0
    hd = RWKV_HEAD
    n = RWKV_HEADS * b
    sshape = (b, RWKV_HEADS, hd, hd)
    return pl.pallas_call(
        functools.partial(_rwkv_body, nb=b, tb=tb, chunk=chunk),
        grid=(l // tb,),
        in_specs=[
            pl.BlockSpec((b, tb, B_COLS), lambda j: (0, j, 0)),
            pl.BlockSpec((b, 1, B_COLS), lambda j: (0, 0, 0)),
            pl.BlockSpec(sshape, lambda j: (0, 0, 0, 0)),
            _layer_block((1, B_COLS), layer, 1),
            _layer_block((128, 3 * GROUP_WIDTH), layer, 1),
            _layer_block((8, GROUP_WIDTH), layer, 1),
        ],
        out_specs=[
            pl.BlockSpec((b, tb, GROUP_WIDTH), lambda j: (0, j, 0)),
            pl.BlockSpec(sshape, lambda j: (0, 0, 0, 0)),
        ],
        out_shape=[
            jax.ShapeDtypeStruct((b, l, GROUP_WIDTH), BF16),
            jax.ShapeDtypeStruct(sshape, F32),
        ],
        scratch_shapes=[
            pltpu.VMEM((b, 1, B_COLS), F32),
            pltpu.VMEM((n, hd, 2 * hd), F32),
            pltpu.VMEM((b * tb, GROUP_WIDTH), F32),
        ],
        compiler_params=_params(("arbitrary",)),
        name="rwkv",
    )(ub, shift_prev, state, mu, wl, vec)


def _outffn_body(x_ref, ya_ref, yb_ref, yc_ref, yd_ref, wo_ref, g_ref, wg_ref, wu_ref, wd_ref, o_ref):
    ycat = jnp.concatenate([ya_ref[...], yb_ref[...], yc_ref[...], yd_ref[...]], axis=-1)
    x1 = x_ref[...] + _dot(ycat, wo_ref[...])
    ms = jnp.mean(x1 * x1, axis=-1, keepdims=True)
    hn = ((x1 * lax.rsqrt(ms + RMS_EPS)) * g_ref[...]).astype(BF16)
    hg = _dot(hn, wg_ref[...])
    hu = _dot(hn, wu_ref[...])
    act = (hg * _sigmoid(hg) * hu).astype(BF16)
    o_ref[...] = x1 + _dot(act, wd_ref[...])


def _outffn(x2d, ya, yb, yc, yd, layer, wo, g, wg, wu, wd):
    t = x2d.shape[0]
    tm = min(t, MATMUL_ROWS)
    assert t % tm == 0
    row = lambda w: pl.BlockSpec((tm, w), lambda i: (i, 0))
    return pl.pallas_call(
        _outffn_body,
        grid=(t // tm,),
        in_specs=[
            row(D_MODEL), row(GROUP_WIDTH), row(GROUP_WIDTH), row(GROUP_WIDTH), row(GROUP_WIDTH),
            _layer_block((4 * GROUP_WIDTH, D_MODEL), layer, 1),
            _layer_block((1, D_MODEL), layer, 1),
            _layer_block((D_MODEL, D_FF), layer, 1),
            _layer_block((D_MODEL, D_FF), layer, 1),
            _layer_block((D_FF, D_MODEL), layer, 1),
        ],
        out_specs=row(D_MODEL),
        out_shape=jax.ShapeDtypeStruct((t, D_MODEL), F32),
        compiler_params=_params(("parallel",)),
        name="outffn",
    )(x2d, ya, yb, yc, yd, wo, g, wg, wu, wd)


def _rows8(vectors):
    stacked = jnp.stack(vectors, axis=1)
    return jnp.pad(stacked, ((0, 0), (0, SUBLANES - stacked.shape[1]), (0, 0)))


def _stacked_weights(p):
    gw = GROUP_WIDTH
    depth = p['w_in'].shape[0]
    place = lambda a, before, after: jnp.pad(a, ((0, 0), (0, 0), (before, after)))
    wl = jnp.concatenate([place(p['rwkv_w2'], 0, 2 * gw), place(p['rwkv_a2'], gw, gw),
                          place(p['rwkv_g2'], 2 * gw, 0)], axis=1)
    pc = gw // len(POOL_WINDOWS)
    wbd = jnp.concatenate([place(p['pool_w'][:, g], g * pc, gw - (g + 1) * pc)
                           for g in range(len(POOL_WINDOWS))], axis=1)
    return dict(
        norm_mix_g=p['norm_mix_g'][:, None, :],
        w_in=p['w_in'].astype(BF16),
        conv_w=jnp.pad(p['conv_w'], ((0, 0), (0, CONV_PAD - CONV_WIDTH), (0, 0))),
        conv_vec=_rows8([p['conv_b'], p['conv_ln_g'], p['conv_ln_b']]),
        rwkv_mu=p['rwkv_mu'][:, None, :],
        rwkv_wl=wl.astype(BF16),
        rwkv_vec=_rows8([p['rwkv_w0'], p['rwkv_a0'], p['rwkv_k_k'], p['rwkv_k_a'],
                         p['rwkv_r_k'].reshape(depth, gw), p['rwkv_gn_g'], p['rwkv_gn_b']]),
        pool_wbd=wbd.astype(BF16),
        pool_scale=p['pool_scale'][:, None, :],
        attn_qg=jnp.tile(p['attn_q_norm'], (1, N_Q_HEADS))[:, None, :],
        attn_kg=jnp.tile(p['attn_k_norm'], (1, N_KV_HEADS))[:, None, :],
        attn_sinks=p['attn_sinks'],
        w_out=p['w_out'].astype(BF16),
        norm_ffn_g=p['norm_ffn_g'][:, None, :],
        ffn_w_gate=p['ffn_w_gate'].astype(BF16),
        ffn_w_up=p['ffn_w_up'].astype(BF16),
        ffn_w_down=p['ffn_w_down'].astype(BF16),
    )


def _trunk(x, conv_hist, rwkv_state, shift_prev, pool_hist, k_cache, v_cache, pos0, chunk, w):
    b, l, _ = x.shape
    has_past = conv_hist is not None
    kvw = N_KV_HEADS * HEAD_DIM
    new = [[] for _ in range(6)]
    for li in range(DEPTH):
        if has_past:
            ch = jnp.pad(conv_hist[li], ((0, 0), (CONV_PAD - CONV_HIST, 0), (0, 0)))
            rs = rwkv_state[li]
            sp = shift_prev[li][:, None, :]
            ph = jnp.pad(pool_hist[li], ((0, 0), (POOL_PAD - POOL_HIST, 0), (0, 0)))
            hk = k_cache[li].reshape(b, WINDOW, kvw)
            hv = v_cache[li].reshape(b, WINDOW, kvw)
        else:
            ch = jnp.zeros((b, CONV_PAD, GROUP_WIDTH), F32)
            rs = jnp.zeros((b, RWKV_HEADS, RWKV_HEAD, RWKV_HEAD), F32)
            sp = jnp.zeros((b, 1, B_COLS), F32)
            ph = jnp.zeros((b, POOL_PAD, GROUP_WIDTH), F32)
            hk = jnp.zeros((b, WINDOW, kvw), F32)
            hv = jnp.zeros((b, WINDOW, kvw), F32)
        ub, ud, ya, yc, conv_tail, pool_tail = _projmix(
            x, li, w['norm_mix_g'], w['w_in'], ch, w['conv_w'], w['conv_vec'],
            ph, w['pool_wbd'], w['pool_scale'], pos0)
        yb, s_new = _rwkv(ub, sp, rs, li, w['rwkv_mu'], w['rwkv_wl'], w['rwkv_vec'], min(CHUNK, l))
        yd, k_tail, v_tail = _attn(ud, hk, hv, li, w['attn_qg'], w['attn_kg'], w['attn_sinks'], chunk,
                                   has_past)
        flat = lambda y: y.reshape(b * l, GROUP_WIDTH)
        x = _outffn(x.reshape(b * l, D_MODEL), flat(ya), flat(yb), flat(yc), flat(yd), li, w['w_out'],
                    w['norm_ffn_g'], w['ffn_w_gate'], w['ffn_w_up'], w['ffn_w_down']
                    ).reshape(b, l, D_MODEL)
        new[0].append(conv_tail[:, CONV_PAD - CONV_HIST:, :])
        new[1].append(s_new)
        new[2].append(ub[:, l - 1, :])
        new[3].append(pool_tail[:, POOL_PAD - POOL_HIST:, :])
        new[4].append(k_tail.reshape(b, WINDOW, N_KV_HEADS, HEAD_DIM))
        new[5].append(v_tail.reshape(b, WINDOW, N_KV_HEADS, HEAD_DIM))
    return x, tuple(jnp.stack(n) for n in new)


def kernel(x_prompt, x_sample, cache_conv, state_rwkv, state_rwkv_shift, cache_pool, cache_k, cache_v, norm_mix_g, w_in, conv_w, conv_b, conv_ln_g, conv_ln_b, rwkv_mu, rwkv_w0, rwkv_w2, rwkv_a0, rwkv_a2, rwkv_g2, rwkv_k_k, rwkv_k_a, rwkv_r_k, rwkv_gn_g, rwkv_gn_b, pool_w, pool_scale, attn_q_norm, attn_k_norm, attn_sinks, w_out, norm_ffn_g, ffn_w_gate, ffn_w_up, ffn_w_down):
    w = _stacked_weights(dict(
        norm_mix_g=norm_mix_g, w_in=w_in, conv_w=conv_w, conv_b=conv_b, conv_ln_g=conv_ln_g,
        conv_ln_b=conv_ln_b, rwkv_mu=rwkv_mu, rwkv_w0=rwkv_w0, rwkv_w2=rwkv_w2, rwkv_a0=rwkv_a0,
        rwkv_a2=rwkv_a2, rwkv_g2=rwkv_g2, rwkv_k_k=rwkv_k_k, rwkv_k_a=rwkv_k_a, rwkv_r_k=rwkv_r_k,
        rwkv_gn_g=rwkv_gn_g, rwkv_gn_b=rwkv_gn_b, pool_w=pool_w, pool_scale=pool_scale,
        attn_q_norm=attn_q_norm, attn_k_norm=attn_k_norm, attn_sinks=attn_sinks, w_out=w_out,
        norm_ffn_g=norm_ffn_g, ffn_w_gate=ffn_w_gate, ffn_w_up=ffn_w_up, ffn_w_down=ffn_w_down))
    y_p, (conv_p, rwkv_p, shift_p, pool_p, k_p, v_p) = _trunk(
        x_prompt, None, None, None, None, None, None, 0, CHUNK, w)
    y_s, (conv_s, rwkv_s, shift_s, pool_s, k_s, v_s) = _trunk(
        x_sample, cache_conv, state_rwkv, state_rwkv_shift, cache_pool, cache_k, cache_v,
        PAST_LEN, x_sample.shape[1], w)
    return (y_p, y_s, conv_p, conv_s, rwkv_p, rwkv_s, shift_p, shift_s,
            pool_p, pool_s, k_p, k_s, v_p, v_s)
```

```python
import functools
import math

import jax
import jax.numpy as jnp
from jax import lax
from jax.experimental import pallas as pl
from jax.experimental.pallas import tpu as pltpu

F32 = jnp.float32
BF16 = jnp.bfloat16

D_MODEL = 1024
DEPTH = 2
PAST_LEN = 1024
CHUNK = 64
GROUP_WIDTH = 256
CONV_WIDTH = 31
CONV_HIST = CONV_WIDTH - 1
CONV_PAD = 32
RWKV_HEAD = 64
RWKV_HEADS = 4
POOL_WINDOWS = (2, 4, 8, 16)
POOL_HIST = 15
POOL_PAD = 16
HEAD_DIM = 64
N_Q_HEADS = 4
N_KV_HEADS = 2
WINDOW = 128
D_FF = 2816
A_COLS = 512
B_COLS = 896
C_COLS = 256
D_COLS = 512
IN_COLS = A_COLS + B_COLS + C_COLS + D_COLS
RMS_EPS = 1e-6
LN_EPS = 1e-5
GN_EPS = 64e-5
ATTN_SCALE = HEAD_DIM ** -0.5
NEG_INF = -1e30

DECAY_RANK = 32
AAA_RANK = 32
Q_COLS = N_Q_HEADS * HEAD_DIM
KV_COLS = N_KV_HEADS * HEAD_DIM

VMEM_LIMIT_BYTES = 56 * 1024 * 1024
SUBLANES = 8
MATMUL_ROWS = 512
MIXER_ROWS = 256
LOCAL_ROWS = 128


def _dot(a, b):
    return jnp.dot(a, b, preferred_element_type=F32)


def _bdot(a, b):
    return lax.dot_general(a, b, (((2,), (1,)), ((0,), (0,))), preferred_element_type=F32)


def _bdot_nt(a, b):
    return lax.dot_general(a, b, (((2,), (2,)), ((0,), (0,))), preferred_element_type=F32)


def _bdot_tn(a, b):
    return lax.dot_general(a, b, (((1,), (1,)), ((0,), (0,))), preferred_element_type=F32)


def _sigmoid(x):
    return 1.0 / (1.0 + jnp.exp(-x))


def _split_dot_right(x, m_bf16, terms):
    acc = None
    rem = x
    for i in range(terms):
        hi = rem.astype(BF16)
        d = _dot(hi, m_bf16)
        acc = d if acc is None else acc + d
        if i + 1 < terms:
            rem = rem - hi.astype(F32)
    return acc


def _split_dot_left(m_bf16, x, terms):
    acc = None
    rem = x
    for i in range(terms):
        hi = rem.astype(BF16)
        d = _dot(m_bf16, hi)
        acc = d if acc is None else acc + d
        if i + 1 < terms:
            rem = rem - hi.astype(F32)
    return acc


def _block_matrix(n, blk, value):
    sh = int(math.log2(blk))
    r = lax.shift_right_logical(lax.broadcasted_iota(jnp.int32, (n, n), 0), sh)
    c = lax.shift_right_logical(lax.broadcasted_iota(jnp.int32, (n, n), 1), sh)
    return jnp.where(r == c, value, 0.0).astype(BF16)


def _params(sem):
    return pltpu.CompilerParams(dimension_semantics=sem, vmem_limit_bytes=VMEM_LIMIT_BYTES)


def _layer_block(shape, layer, n_grid):
    zeros = (0,) * len(shape)
    if n_grid == 1:
        index_map = lambda i: (layer,) + zeros
    else:
        index_map = lambda i, j: (layer,) + zeros
    return pl.BlockSpec((None,) + tuple(shape), index_map, pipeline_mode=pl.Buffered(1))


def _phase_copies(ext_ref, sh_ref):
    n = sh_ref.shape[1]
    for s in range(1, SUBLANES):
        sh_ref[s - 1] = ext_ref[s:s + n, :]


def _shifted_rows(ext_ref, sh_ref, start, rb):
    a, s = divmod(start, SUBLANES)
    base = a * SUBLANES
    return ext_ref[base:base + rb, :] if s == 0 else sh_ref[s - 1, base:base + rb, :]


def _conv_rows(ext_ref, sh_ref, w_ref, vec_ref, r0, rb):
    off = CONV_PAD - CONV_HIST
    acc = jnp.zeros((rb, GROUP_WIDTH), F32)
    for j in range(CONV_WIDTH):
        acc = acc + w_ref[j:j + 1, :] * _shifted_rows(ext_ref, sh_ref, r0 + off + j, rb)
    acc = acc + vec_ref[0:1, :]
    mu = jnp.mean(acc, axis=-1, keepdims=True)
    d = acc - mu
    var = jnp.mean(d * d, axis=-1, keepdims=True)
    yn = d * lax.rsqrt(var + LN_EPS) * vec_ref[1:2, :] + vec_ref[2:3, :]
    return yn * _sigmoid(yn)


def _pool_rows(ext_ref, sh_ref, wbd_ref, sc_ref, r0, rb, pos_start):
    base = POOL_PAD + r0
    sums = []
    acc = None
    for i in range(max(POOL_WINDOWS)):
        sh = _shifted_rows(ext_ref, sh_ref, base - i, rb)
        acc = sh if acc is None else acc + sh
        if i + 1 in POOL_WINDOWS:
            sums.append(acc)
    pos = pos_start + r0 + lax.broadcasted_iota(jnp.int32, (rb, 1), 0)
    means = [s / jnp.minimum(w, pos + 1).astype(F32) for s, w in zip(sums, POOL_WINDOWS)]
    lane = lax.broadcasted_iota(jnp.int32, (1, GROUP_WIDTH), 1)
    pc = GROUP_WIDTH // len(POOL_WINDOWS)
    mean = means[-1]
    for g in range(len(POOL_WINDOWS) - 2, -1, -1):
        mean = jnp.where(lane < (g + 1) * pc, means[g], mean)
    d = mean - ext_ref[base:base + rb, :]
    return _dot(d.astype(BF16), wbd_ref[...]) * sc_ref[...]


def _projmix_body(x_ref, g_ref, w_ref, chist_ref, cw_ref, cvec_ref, phist_ref, pwbd_ref, psc_ref,
                  ub_ref, ud_ref, ya_ref, yc_ref, ctail_ref, ptail_ref,
                  cext_ref, csh_ref, pext_ref, psh_ref, *, nb, tl, rb, pos0):
    li = pl.program_id(1)

    @pl.when(li == 0)
    def _():
        cext_ref[:, 0:CONV_PAD, :] = chist_ref[...]
        pext_ref[:, 0:POOL_PAD, :] = phist_ref[...]

    x = x_ref[...].reshape(nb * tl, D_MODEL)
    ms = jnp.mean(x * x, axis=-1, keepdims=True)
    xn = ((x * lax.rsqrt(ms + RMS_EPS)) * g_ref[...]).astype(BF16)
    b0 = A_COLS
    c0 = A_COLS + B_COLS
    d0 = c0 + C_COLS
    ua = _dot(xn, w_ref[:, 0:A_COLS])
    uc = _dot(xn, w_ref[:, c0:d0])
    ub_ref[...] = _dot(xn, w_ref[:, b0:c0]).reshape(nb, tl, B_COLS)
    ud_ref[...] = _dot(xn, w_ref[:, d0:IN_COLS]).reshape(nb, tl, D_COLS)
    glu = ua[:, :GROUP_WIDTH] * _sigmoid(ua[:, GROUP_WIDTH:])
    for b in range(nb):
        cext = cext_ref.at[b]
        pext = pext_ref.at[b]
        cext[CONV_PAD:CONV_PAD + tl, :] = glu[b * tl:(b + 1) * tl]
        pext[POOL_PAD:POOL_PAD + tl, :] = uc[b * tl:(b + 1) * tl]
        _phase_copies(cext, csh_ref)
        _phase_copies(pext, psh_ref)
        for r0 in range(0, tl, rb):
            ya_ref[b, r0:r0 + rb, :] = _conv_rows(cext, csh_ref, cw_ref, cvec_ref, r0, rb).astype(BF16)
            yc_ref[b, r0:r0 + rb, :] = _pool_rows(pext, psh_ref, pwbd_ref, psc_ref, r0, rb,
                                                  pos0 + li * tl).astype(BF16)
        ctail = cext[tl:tl + CONV_PAD, :]
        ptail = pext[tl:tl + POOL_PAD, :]
        ctail_ref[b] = ctail
        ptail_ref[b] = ptail
        cext[0:CONV_PAD, :] = ctail
        pext[0:POOL_PAD, :] = ptail


def _projmix(x, layer, g, w_in, chist, conv_w, conv_vec, phist, pool_wbd, pool_scale, pos0):
    b, l, _ = x.shape
    nb = b if l <= 64 else 1
    tl = min(l, MATMUL_ROWS)
    rb = min(tl, LOCAL_ROWS)
    assert l % tl == 0 and tl % rb == 0 and tl >= CONV_PAD and b % nb == 0
    gw = GROUP_WIDTH
    tile = lambda w: pl.BlockSpec((nb, tl, w), lambda i, j: (i, j, 0))
    per_seq = lambda r: pl.BlockSpec((nb, r, gw), lambda i, j: (i, 0, 0))
    return pl.pallas_call(
        functools.partial(_projmix_body, nb=nb, tl=tl, rb=rb, pos0=pos0),
        grid=(b // nb, l // tl),
        in_specs=[
            tile(D_MODEL),
            _layer_block((1, D_MODEL), layer, 2),
            _layer_block((D_MODEL, IN_COLS), layer, 2),
            per_seq(CONV_PAD), _layer_block((CONV_PAD, gw), layer, 2), _layer_block((8, gw), layer, 2),
            per_seq(POOL_PAD), _layer_block((gw, gw), layer, 2), _layer_block((1, gw), layer, 2),
        ],
        out_specs=[tile(B_COLS), tile(D_COLS), tile(gw), tile(gw), per_seq(CONV_PAD), per_seq(POOL_PAD)],
        out_shape=[
            jax.ShapeDtypeStruct((b, l, B_COLS), F32),
            jax.ShapeDtypeStruct((b, l, D_COLS), F32),
            jax.ShapeDtypeStruct((b, l, gw), BF16),
            jax.ShapeDtypeStruct((b, l, gw), BF16),
            jax.ShapeDtypeStruct((b, CONV_PAD, gw), F32),
            jax.ShapeDtypeStruct((b, POOL_PAD, gw), F32),
        ],
        scratch_shapes=[
            pltpu.VMEM((nb, CONV_PAD + tl, gw), F32),
            pltpu.VMEM((SUBLANES - 1, CONV_PAD + tl - SUBLANES, gw), F32),
            pltpu.VMEM((nb, POOL_PAD + tl, gw), F32),
            pltpu.VMEM((SUBLANES - 1, POOL_PAD + tl - SUBLANES, gw), F32),
        ],
        compiler_params=_params(("parallel", "arbitrary")),
        name="projmix",
    )(x, g, w_in, chist, conv_w, conv_vec, phist, pool_wbd, pool_scale)


def _attn_body(sink_ref, u_ref, hk_ref, hv_ref, qg_ref, kg_ref, y_ref, kt_ref, vt_ref,
               kh_ref, vh_ref, *, layer, nb, tq, chunk, hist_valid):
    li = pl.program_id(0)
    hd = HEAD_DIM
    rows = nb * tq
    ncq = tq // chunk
    kw = WINDOW + chunk

    @pl.when(li == 0)
    def _():
        kh_ref[...] = hk_ref[...]
        vh_ref[...] = hv_ref[...]

    u = u_ref[...].reshape(rows, D_COLS)
    q = u[:, 0:Q_COLS]
    k = u[:, Q_COLS:Q_COLS + KV_COLS]
    v = u[:, Q_COLS + KV_COLS:D_COLS]
    inv = 1.0 / hd
    qms = _split_dot_right(q * q, _block_matrix(Q_COLS, hd, inv), 1)
    kms = _split_dot_right(k * k, _block_matrix(KV_COLS, hd, inv), 2)
    qn = (q * lax.rsqrt(qms + RMS_EPS)) * (qg_ref[...] * ATTN_SCALE)
    kn = (k * lax.rsqrt(kms + RMS_EPS)) * kg_ref[...]
    kcat = jnp.concatenate([kh_ref[...], kn.reshape(nb, tq, 2 * hd)], axis=1)
    vcat = jnp.concatenate([vh_ref[...], v.reshape(nb, tq, 2 * hd)], axis=1)
    ktail = kcat[:, tq:tq + WINDOW, :]
    vtail = vcat[:, tq:tq + WINDOW, :]
    kt_ref[...] = ktail
    vt_ref[...] = vtail
    kh_ref[...] = ktail
    vh_ref[...] = vtail
    kcat_b = kcat.astype(BF16)
    vcat_b = vcat.astype(BF16)

    slot_hi = lax.broadcasted_iota(jnp.int32, (1, 2 * hd), 1) >= hd
    q_tiles = []
    for h in range(N_Q_HEADS):
        col = qn[:, (h // 2) * 2 * hd:(h // 2 + 1) * 2 * hd]
        g = h // (N_Q_HEADS // N_KV_HEADS)
        if h % 2 != g:
            col = pltpu.roll(col, hd, axis=1)
        q_tiles.append(jnp.where(slot_hi if g == 1 else jnp.logical_not(slot_hi), col, 0.0).astype(BF16))
    pairs = [(b, c) for c in range(ncq) for b in range(nb)]
    qs = jnp.stack([jnp.concatenate([t[b * tq + c * chunk:b * tq + (c + 1) * chunk] for t in q_tiles], axis=0)
                    for b, c in pairs])
    ks = jnp.stack([kcat_b[b, c * chunk:c * chunk + kw] for b, c in pairs])
    vs = jnp.stack([vcat_b[b, c * chunk:c * chunk + kw] for b, c in pairs])
    st = _bdot_nt(ks, qs)
    nq = N_Q_HEADS * chunk
    if not hist_valid:
        n_edge = min(ncq, WINDOW // chunk) * nb
        cpos = [c * chunk for _, c in pairs[:n_edge]]
        kpos = lax.broadcasted_iota(jnp.int32, (1, kw, nq), 1) + (li * tq - WINDOW)
        edge = jnp.concatenate([jnp.where(kpos + cp >= 0, st[i:i + 1], NEG_INF) for i, cp in enumerate(cpos)],
                               axis=0)
        st = jnp.concatenate([edge, st[n_edge:]], axis=0) if n_edge < len(pairs) else edge
    hlane = lax.broadcasted_iota(jnp.int32, (1, 1, nq), 2)
    sk = jnp.full((1, 1, nq), sink_ref[layer, N_Q_HEADS - 1], F32)
    for h in range(N_Q_HEADS - 2, -1, -1):
        sk = jnp.where(hlane < (h + 1) * chunk, sink_ref[layer, h], sk)
    m = jnp.maximum(jnp.max(st, axis=1, keepdims=True), sk)
    p = jnp.exp(st - m)
    den = jnp.sum(p, axis=1, keepdims=True) + jnp.exp(sk - m)
    pn = (p * (1.0 / den)).astype(BF16)
    o = _bdot_tn(pn, vs)
    lo = jnp.logical_not(slot_hi)
    for i, (b, c) in enumerate(pairs):
        oc = o[i]
        col0 = jnp.where(lo, oc[0:chunk], pltpu.roll(oc[chunk:2 * chunk], hd, axis=1))
        col1 = jnp.where(lo, pltpu.roll(oc[2 * chunk:3 * chunk], hd, axis=1), oc[3 * chunk:4 * chunk])
        y_ref[b, c * chunk:(c + 1) * chunk, :] = jnp.concatenate([col0, col1], axis=1).astype(BF16)


def _attn(ud, hk, hv, layer, qg, kg, sinks, chunk, hist_valid):
    b, l, _ = ud.shape
    tq = min(l, MIXER_ROWS)
    assert l % tq == 0 and tq % chunk == 0
    kvw = N_KV_HEADS * HEAD_DIM
    return pl.pallas_call(
        functools.partial(_attn_body, layer=layer, nb=b, tq=tq, chunk=chunk, hist_valid=hist_valid),
        grid=(l // tq,),
        in_specs=[
            pl.BlockSpec(memory_space=pltpu.SMEM),
            pl.BlockSpec((b, tq, D_COLS), lambda j: (0, j, 0)),
            pl.BlockSpec((b, WINDOW, kvw), lambda j: (0, 0, 0)),
            pl.BlockSpec((b, WINDOW, kvw), lambda j: (0, 0, 0)),
            _layer_block((1, GROUP_WIDTH), layer, 1),
            _layer_block((1, kvw), layer, 1),
        ],
        out_specs=[
            pl.BlockSpec((b, tq, GROUP_WIDTH), lambda j: (0, j, 0)),
            pl.BlockSpec((b, WINDOW, kvw), lambda j: (0, 0, 0)),
            pl.BlockSpec((b, WINDOW, kvw), lambda j: (0, 0, 0)),
        ],
        out_shape=[
            jax.ShapeDtypeStruct((b, l, GROUP_WIDTH), BF16),
            jax.ShapeDtypeStruct((b, WINDOW, kvw), F32),
            jax.ShapeDtypeStruct((b, WINDOW, kvw), F32),
        ],
        scratch_shapes=[pltpu.VMEM((b, WINDOW, kvw), F32), pltpu.VMEM((b, WINDOW, kvw), F32)],
        compiler_params=_params(("arbitrary",)),
        name="attn",
    )(sinks, ud, hk, hv, qg, kg)


def _rwkv_body(u_ref, sp_ref, s0_ref, mu_ref, wl_ref, vec_ref, y_ref, sn_ref,
               prev_ref, s_ref, yacc_ref, *, nb, tb, chunk):
    li = pl.program_id(0)
    gw = GROUP_WIDTH
    hd = RWKV_HEAD
    nh = RWKV_HEADS
    rows = nb * tb
    nc = tb // chunk

    @pl.when(li == 0)
    def _():
        prev_ref[...] = sp_ref[...]
        zero = jnp.zeros((nb, hd, hd), F32)
        for h in range(nh):
            sh = s0_ref[:, h]
            s_ref[h * nb:(h + 1) * nb] = jnp.concatenate([sh, zero] if h % 2 == 0 else [zero, sh], axis=-1)

    row = lax.broadcasted_iota(jnp.int32, (tb, 1), 0)
    mu = mu_ref[...]
    xs_parts = []
    for b in range(nb):
        ub = u_ref[b]
        prev = jnp.where(row == 0, prev_ref[b], pltpu.roll(ub, 1, axis=0))
        prev_ref[b] = ub[tb - 1:tb, :]
        xs_parts.append(ub + mu * (prev - ub))
    xs = jnp.concatenate(xs_parts, axis=0)
    r = xs[:, 0:gw]
    k = xs[:, gw:2 * gw]
    v = xs[:, 2 * gw:3 * gw]
    lat = xs[:, 3 * gw:B_COLS]
    lane_lat = lax.broadcasted_iota(jnp.int32, (1, B_COLS - 3 * gw), 1)
    act = jnp.where(lane_lat < DECAY_RANK, jnp.tanh(lat),
                    jnp.where(lane_lat < DECAY_RANK + AAA_RANK, lat, _sigmoid(lat)))
    lo = _dot(act.astype(BF16), wl_ref[...])
    w0 = vec_ref[0:1, :]
    a0 = vec_ref[1:2, :]
    k_k = vec_ref[2:3, :]
    k_a = vec_ref[3:4, :]
    r_k = vec_ref[4:5, :]
    gn_g = vec_ref[5:6, :]
    gn_b = vec_ref[6:7, :]
    z = -(w0 + lo[:, 0:gw])
    softplus = jnp.maximum(z, 0.0) + jnp.log(1.0 + jnp.exp(-jnp.abs(z)))
    logw = -jnp.exp(-softplus - 0.5)
    a_rate = _sigmoid(a0 + lo[:, gw:2 * gw])
    gate = lo[:, 2 * gw:3 * gw]
    ones_blk = _block_matrix(gw, hd, 1.0)
    kk = k * k_k
    kk = kk * lax.rsqrt(jnp.maximum(_split_dot_right(kk * kk, ones_blk, 1), 1e-24))
    k_mod = k * (1.0 + (a_rate - 1.0) * k_a)
    b_v = kk * a_rate
    bonus = _split_dot_right(r * k_mod * r_k, ones_blk, 1) * v

    grp = min(rows, 256)
    gi = lax.broadcasted_iota(jnp.int32, (grp, grp), 0)
    gj = lax.broadcasted_iota(jnp.int32, (grp, grp), 1)
    csh = int(math.log2(chunk))
    tri = jnp.where(jnp.logical_and(lax.shift_right_logical(gi, csh) == lax.shift_right_logical(gj, csh),
                                    gj <= gi), 1.0, 0.0).astype(BF16)
    cum = jnp.concatenate([_split_dot_left(tri, logw[g0:g0 + grp], 2) for g0 in range(0, rows, grp)], axis=0)
    cum3 = cum.reshape(nb * nc, chunk, gw)
    cum_c = cum3[:, chunk - 1:chunk, :]
    e_end = jnp.exp(cum_c - cum3).reshape(rows, gw)
    w_c = jnp.exp(cum_c)
    e_neg = jnp.exp(-cum)
    dense = dict(
        a=-kk * jnp.exp(cum - logw),
        r=r * jnp.exp(cum),
        bt=b_v * e_neg,
        kt=k_mod * e_neg,
        bh=b_v * e_end,
        kh=k_mod * e_end,
    )
    slot_hi = lax.broadcasted_iota(jnp.int32, (1, 2 * hd), 1) >= hd

    def head_tile(x, h, own_slot):
        col = x[:, (h // 2) * 2 * hd:(h // 2 + 1) * 2 * hd]
        keep = slot_hi if (h % 2 == 1) == own_slot else jnp.logical_not(slot_hi)
        if x.dtype == BF16:
            return col * jnp.where(keep, 1.0, 0.0).astype(BF16)
        return jnp.where(keep, col, 0.0)

    v_sw = jnp.concatenate([pltpu.roll(v[:, j * 2 * hd:(j + 1) * 2 * hd], hd, axis=1) for j in range(nh // 2)],
                           axis=1)
    tiles = {"r": [head_tile(dense["r"], h, True) for h in range(nh)]}
    dense_b = {name: dense[name].astype(BF16) for name in ("a", "bt", "kt", "bh", "kh")}
    tiles_b = {name: [head_tile(dense_b[name], h, True) for h in range(nh)] for name in ("a", "bh", "kh")}
    for name in ("bt", "kt"):
        tiles_b[name] = [dense_b[name][:, (h // 2) * 2 * hd:(h // 2 + 1) * 2 * hd] for h in range(nh)]
    v_sw_b = v_sw.astype(BF16)
    v_tiles = [head_tile(v_sw_b, h, False) for h in range(nh)]

    def blocks(per_head, c):
        return jnp.stack([per_head[h][b * tb + c * chunk:b * tb + (c + 1) * chunk]
                          for h in range(nh) for b in range(nb)])

    n = nh * nb
    ri = lax.broadcasted_iota(jnp.int32, (2 * chunk, 2 * chunk), 0)
    ci = jnp.bitwise_and(lax.broadcasted_iota(jnp.int32, (2 * chunk, 2 * chunk), 1), chunk - 1)
    gmask = ci < jnp.bitwise_and(ri, chunk - 1) + lax.shift_right_logical(ri, csh)
    zeros_c = jnp.zeros((n, chunk, 2 * hd), BF16)
    n_sq = int(math.log2(chunk))
    for c in range(nc):
        a_b = blocks(tiles_b["a"], c)
        r_f = blocks(tiles["r"], c)
        v_b = blocks(v_tiles, c)
        ar = jnp.concatenate([a_b, r_f.astype(BF16)], axis=1)
        bk = jnp.concatenate([blocks(tiles_b["bt"], c), blocks(tiles_b["kt"], c)], axis=1)
        bhkh = jnp.concatenate([blocks(tiles_b["bh"], c), blocks(tiles_b["kh"], c)], axis=1)
        g = jnp.where(gmask, _bdot_nt(ar, bk), 0.0)
        g_top = g[:, :chunk, :]
        g_bot = g[:, chunk:, :].astype(BF16)
        w = a_b.astype(F32) + _bdot(g_top.astype(BF16), jnp.concatenate([zeros_c, v_b], axis=1))
        p = g_top[:, :, :chunk]
        for i in range(n_sq):
            pb = p.astype(BF16)
            if i + 1 < n_sq:
                res = _bdot(pb, jnp.concatenate([w.astype(BF16), pb], axis=2))
                w = w + res[:, :, :2 * hd]
                p = res[:, :, 2 * hd:]
            else:
                hc = chunk // 2
                low = w[:, hc:, :] + _bdot(pb[:, hc:, :hc], w[:, :hc, :].astype(BF16))
                w = jnp.concatenate([w[:, :hc, :], low], axis=1)
        xv = jnp.concatenate([w.astype(BF16), v_b], axis=1)
        ry = _bdot(g_bot, xv)
        mp = _bdot_tn(xv, bhkh)
        s_old = s_ref[...]
        s_b = s_old.astype(BF16)
        y_nt = _bdot_nt((ry + r_f).astype(BF16), s_b)
        wc = jnp.stack([w_c[b * nc + c][:, (h // 2) * 2 * hd:(h // 2 + 1) * 2 * hd]
                        for h in range(nh) for b in range(nb)])
        psi = jnp.concatenate(
            [mp[h * nb:(h + 1) * nb, (1 - h % 2) * hd:(2 - h % 2) * hd, :] for h in range(nh)], axis=0)
        s_ref[...] = s_old * wc + _bdot(s_b, mp.astype(BF16)) + psi
        for b in range(nb):
            ys = [y_nt[h * nb + b] + ry[h * nb + b][:, (1 - h % 2) * hd:(2 - h % 2) * hd] for h in range(nh)]
            yacc_ref[b * tb + c * chunk:b * tb + (c + 1) * chunk, :] = jnp.concatenate(ys, axis=1)

    y = yacc_ref[...]
    avg_blk = _block_matrix(gw, hd, 1.0 / hd)
    m = _split_dot_right(y, avg_blk, 1)
    d = y - m
    var = _split_dot_right(d * d, avg_blk, 1)
    yn = d * lax.rsqrt(var + GN_EPS) * gn_g + gn_b
    out = ((yn + bonus) * gate).astype(BF16)
    for b in range(nb):
        y_ref[b] = out[b * tb:(b + 1) * tb]

    @pl.when(li == pl.num_programs(0) - 1)
    def _():
        for h in range(nh):
            sn_ref[:, h] = s_ref[h * nb:(h + 1) * nb, :, (h % 2) * hd:(h % 2 + 1) * hd]


def _rwkv(ub, shift_prev, state, layer, mu, wl, vec, chunk):
    b, l, _ = ub.shape
    tb = min(l, MIXER_ROWS)
    assert l % tb == 0 and tb % chunk == 0
    hd = RWKV_HEAD
    n = RWKV_HEADS * b
    sshape = (b, RWKV_HEADS, hd, hd)
    return pl.pallas_call(
        functools.partial(_rwkv_body, nb=b, tb=tb, chunk=chunk),
        grid=(l // tb,),
        in_specs=[
            pl.BlockSpec((b, tb, B_COLS), lambda j: (0, j, 0)),
            pl.BlockSpec((b, 1, B_COLS), lambda j: (0, 0, 0)),
            pl.BlockSpec(sshape, lambda j: (0, 0, 0, 0)),
            _layer_block((1, B_COLS), layer, 1),
            _layer_block((128, 3 * GROUP_WIDTH), layer, 1),
            _layer_block((8, GROUP_WIDTH), layer, 1),
        ],
        out_specs=[
            pl.BlockSpec((b, tb, GROUP_WIDTH), lambda j: (0, j, 0)),
            pl.BlockSpec(sshape, lambda j: (0, 0, 0, 0)),
        ],
        out_shape=[
            jax.ShapeDtypeStruct((b, l, GROUP_WIDTH), BF16),
            jax.ShapeDtypeStruct(sshape, F32),
        ],
        scratch_shapes=[
            pltpu.VMEM((b, 1, B_COLS), F32),
            pltpu.VMEM((n, hd, 2 * hd), F32),
            pltpu.VMEM((b * tb, GROUP_WIDTH), F32),
        ],
        compiler_params=_params(("arbitrary",)),
        name="rwkv",
    )(ub, shift_prev, state, mu, wl, vec)


def _outffn_body(*refs, steps0):
    acts = (refs[0:5], refs[5:10])
    wo_ref, g_ref, wg_ref, wu_ref, wd_ref = refs[10:15]
    outs = refs[15:17]
    i = pl.program_id(0)

    def tile(x_ref, ya_ref, yb_ref, yc_ref, yd_ref, o_ref):
        ycat = jnp.concatenate([ya_ref[...], yb_ref[...], yc_ref[...], yd_ref[...]], axis=-1)
        x1 = x_ref[...] + _dot(ycat, wo_ref[...])
        ms = jnp.mean(x1 * x1, axis=-1, keepdims=True)
        hn = ((x1 * lax.rsqrt(ms + RMS_EPS)) * g_ref[...]).astype(BF16)
        hg = _dot(hn, wg_ref[...])
        hu = _dot(hn, wu_ref[...])
        act = (hg * _sigmoid(hg) * hu).astype(BF16)
        o_ref[...] = x1 + _dot(act, wd_ref[...])

    @pl.when(i < steps0)
    def _():
        tile(*acts[0], outs[0])

    @pl.when(i >= steps0)
    def _():
        tile(*acts[1], outs[1])


def _outffn(groups, layer, wo, g, wg, wu, wd):
    rows = [grp[0].shape[0] for grp in groups]
    tm = min(min(rows), MATMUL_ROWS)
    assert all(t % tm == 0 for t in rows)
    steps = [t // tm for t in rows]
    index_maps = (lambda i: (jnp.minimum(i, steps[0] - 1), 0), lambda i: (jnp.maximum(i - steps[0], 0), 0))
    row = lambda k, w: pl.BlockSpec((tm, w), index_maps[k])
    act_specs = [row(k, w) for k in range(2) for w in (D_MODEL,) + (GROUP_WIDTH,) * 4]
    return pl.pallas_call(
        functools.partial(_outffn_body, steps0=steps[0]),
        grid=(steps[0] + steps[1],),
        in_specs=act_specs + [
            _layer_block((4 * GROUP_WIDTH, D_MODEL), layer, 1),
            _layer_block((1, D_MODEL), layer, 1),
            _layer_block((D_MODEL, D_FF), layer, 1),
            _layer_block((D_MODEL, D_FF), layer, 1),
            _layer_block((D_FF, D_MODEL), layer, 1),
        ],
        out_specs=[row(0, D_MODEL), row(1, D_MODEL)],
        out_shape=[jax.ShapeDtypeStruct((t, D_MODEL), F32) for t in rows],
        compiler_params=_params(("arbitrary",)),
        name="outffn",
    )(*groups[0], *groups[1], wo, g, wg, wu, wd)


def _rows8(vectors):
    stacked = jnp.stack(vectors, axis=1)
    return jnp.pad(stacked, ((0, 0), (0, SUBLANES - stacked.shape[1]), (0, 0)))


def _stacked_weights(p):
    gw = GROUP_WIDTH
    depth = p['w_in'].shape[0]
    place = lambda a, before, after: jnp.pad(a, ((0, 0), (0, 0), (before, after)))
    wl = jnp.concatenate([place(p['rwkv_w2'], 0, 2 * gw), place(p['rwkv_a2'], gw, gw),
                          place(p['rwkv_g2'], 2 * gw, 0)], axis=1)
    pc = gw // len(POOL_WINDOWS)
    wbd = jnp.concatenate([place(p['pool_w'][:, g], g * pc, gw - (g + 1) * pc)
                           for g in range(len(POOL_WINDOWS))], axis=1)
    return dict(
        norm_mix_g=p['norm_mix_g'][:, None, :],
        w_in=p['w_in'].astype(BF16),
        conv_w=jnp.pad(p['conv_w'], ((0, 0), (0, CONV_PAD - CONV_WIDTH), (0, 0))),
        conv_vec=_rows8([p['conv_b'], p['conv_ln_g'], p['conv_ln_b']]),
        rwkv_mu=p['rwkv_mu'][:, None, :],
        rwkv_wl=wl.astype(BF16),
        rwkv_vec=_rows8([p['rwkv_w0'], p['rwkv_a0'], p['rwkv_k_k'], p['rwkv_k_a'],
                         p['rwkv_r_k'].reshape(depth, gw), p['rwkv_gn_g'], p['rwkv_gn_b']]),
        pool_wbd=wbd.astype(BF16),
        pool_scale=p['pool_scale'][:, None, :],
        attn_qg=jnp.tile(p['attn_q_norm'], (1, N_Q_HEADS))[:, None, :],
        attn_kg=jnp.tile(p['attn_k_norm'], (1, N_KV_HEADS))[:, None, :],
        attn_sinks=p['attn_sinks'],
        w_out=p['w_out'].astype(BF16),
        norm_ffn_g=p['norm_ffn_g'][:, None, :],
        ffn_w_gate=p['ffn_w_gate'].astype(BF16),
        ffn_w_up=p['ffn_w_up'].astype(BF16),
        ffn_w_down=p['ffn_w_down'].astype(BF16),
    )


def _mixers(x, li, past, pos0, chunk, w):
    b, l, _ = x.shape
    kvw = N_KV_HEADS * HEAD_DIM
    if past is not None:
        conv_hist, rwkv_state, shift_prev, pool_hist, k_cache, v_cache = past
        ch = jnp.pad(conv_hist[li], ((0, 0), (CONV_PAD - CONV_HIST, 0), (0, 0)))
        rs = rwkv_state[li]
        sp = shift_prev[li][:, None, :]
        ph = jnp.pad(pool_hist[li], ((0, 0), (POOL_PAD - POOL_HIST, 0), (0, 0)))
        hk = k_cache[li].reshape(b, WINDOW, kvw)
        hv = v_cache[li].reshape(b, WINDOW, kvw)
    else:
        ch = jnp.zeros((b, CONV_PAD, GROUP_WIDTH), F32)
        rs = jnp.zeros((b, RWKV_HEADS, RWKV_HEAD, RWKV_HEAD), F32)
        sp = jnp.zeros((b, 1, B_COLS), F32)
        ph = jnp.zeros((b, POOL_PAD, GROUP_WIDTH), F32)
        hk = jnp.zeros((b, WINDOW, kvw), F32)
        hv = jnp.zeros((b, WINDOW, kvw), F32)
    ub, ud, ya, yc, conv_tail, pool_tail = _projmix(
        x, li, w['norm_mix_g'], w['w_in'], ch, w['conv_w'], w['conv_vec'],
        ph, w['pool_wbd'], w['pool_scale'], pos0)
    yb, s_new = _rwkv(ub, sp, rs, li, w['rwkv_mu'], w['rwkv_wl'], w['rwkv_vec'], min(CHUNK, l))
    yd, k_tail, v_tail = _attn(ud, hk, hv, li, w['attn_qg'], w['attn_kg'], w['attn_sinks'], chunk,
                               past is not None)
    flat = lambda y: y.reshape(b * l, GROUP_WIDTH)
    acts = (x.reshape(b * l, D_MODEL), flat(ya), flat(yb), flat(yc), flat(yd))
    caches = (conv_tail[:, CONV_PAD - CONV_HIST:, :], s_new, ub[:, l - 1, :],
              pool_tail[:, POOL_PAD - POOL_HIST:, :],
              k_tail.reshape(b, WINDOW, N_KV_HEADS, HEAD_DIM), v_tail.reshape(b, WINDOW, N_KV_HEADS, HEAD_DIM))
    return acts, caches


def kernel(x_prompt, x_sample, cache_conv, state_rwkv, state_rwkv_shift, cache_pool, cache_k, cache_v, norm_mix_g, w_in, conv_w, conv_b, conv_ln_g, conv_ln_b, rwkv_mu, rwkv_w0, rwkv_w2, rwkv_a0, rwkv_a2, rwkv_g2, rwkv_k_k, rwkv_k_a, rwkv_r_k, rwkv_gn_g, rwkv_gn_b, pool_w, pool_scale, attn_q_norm, attn_k_norm, attn_sinks, w_out, norm_ffn_g, ffn_w_gate, ffn_w_up, ffn_w_down):
    w = _stacked_weights(dict(
        norm_mix_g=norm_mix_g, w_in=w_in, conv_w=conv_w, conv_b=conv_b, conv_ln_g=conv_ln_g,
        conv_ln_b=conv_ln_b, rwkv_mu=rwkv_mu, rwkv_w0=rwkv_w0, rwkv_w2=rwkv_w2, rwkv_a0=rwkv_a0,
        rwkv_a2=rwkv_a2, rwkv_g2=rwkv_g2, rwkv_k_k=rwkv_k_k, rwkv_k_a=rwkv_k_a, rwkv_r_k=rwkv_r_k,
        rwkv_gn_g=rwkv_gn_g, rwkv_gn_b=rwkv_gn_b, pool_w=pool_w, pool_scale=pool_scale,
        attn_q_norm=attn_q_norm, attn_k_norm=attn_k_norm, attn_sinks=attn_sinks, w_out=w_out,
        norm_ffn_g=norm_ffn_g, ffn_w_gate=ffn_w_gate, ffn_w_up=ffn_w_up, ffn_w_down=ffn_w_down))
    xs = [x_prompt, x_sample]
    pasts = (None, (cache_conv, state_rwkv, state_rwkv_shift, cache_pool, cache_k, cache_v))
    pos0s = (0, PAST_LEN)
    chunks = (CHUNK, x_sample.shape[1])
    new = ([], [])
    for li in range(DEPTH):
        acts = []
        for gi in range(2):
            a, caches = _mixers(xs[gi], li, pasts[gi], pos0s[gi], chunks[gi], w)
            acts.append(a)
            new[gi].append(caches)
        outs = _outffn(acts, li, w['w_out'], w['norm_ffn_g'], w['ffn_w_gate'], w['ffn_w_up'], w['ffn_w_down'])
        xs = [o.reshape(x.shape) for o, x in zip(outs, xs)]
    stacked = [[jnp.stack([new[gi][li][kind] for li in range(DEPTH)]) for gi in range(2)] for kind in range(6)]
    return (xs[0], xs[1]) + tuple(t for kind in stacked for t in kind)
```

```python
import functools
import math

import jax
import jax.numpy as jnp
from jax import lax
from jax.experimental import pallas as pl
from jax.experimental.pallas import tpu as pltpu

F32 = jnp.float32
BF16 = jnp.bfloat16

D_MODEL = 1024
DEPTH = 2
PAST_LEN = 1024
CHUNK = 64
GROUP_WIDTH = 256
CONV_WIDTH = 31
CONV_HIST = CONV_WIDTH - 1
CONV_PAD = 32
RWKV_HEAD = 64
RWKV_HEADS = 4
POOL_WINDOWS = (2, 4, 8, 16)
POOL_HIST = 15
POOL_PAD = 16
HEAD_DIM = 64
N_Q_HEADS = 4
N_KV_HEADS = 2
WINDOW = 128
D_FF = 2816
A_COLS = 512
B_COLS = 896
C_COLS = 256
D_COLS = 512
IN_COLS = A_COLS + B_COLS + C_COLS + D_COLS
RMS_EPS = 1e-6
LN_EPS = 1e-5
GN_EPS = 64e-5
ATTN_SCALE = HEAD_DIM ** -0.5
NEG_INF = -1e30

DECAY_RANK = 32
AAA_RANK = 32
Q_COLS = N_Q_HEADS * HEAD_DIM
KV_COLS = N_KV_HEADS * HEAD_DIM

VMEM_LIMIT_BYTES = 56 * 1024 * 1024
SUBLANES = 8
MATMUL_ROWS = 512
RWKV_ROWS = 256
ATTN_ROWS = 512
LOCAL_ROWS = 128


def _dot(a, b):
    return jnp.dot(a, b, preferred_element_type=F32)


def _bdot(a, b):
    return lax.dot_general(a, b, (((2,), (1,)), ((0,), (0,))), preferred_element_type=F32)


def _bdot_nt(a, b):
    return lax.dot_general(a, b, (((2,), (2,)), ((0,), (0,))), preferred_element_type=F32)


def _bdot_tn(a, b):
    return lax.dot_general(a, b, (((1,), (1,)), ((0,), (0,))), preferred_element_type=F32)


def _sigmoid(x):
    return 1.0 / (1.0 + jnp.exp(-x))


def _split_dot_right(x, m_bf16, terms):
    acc = None
    rem = x
    for i in range(terms):
        hi = rem.astype(BF16)
        d = _dot(hi, m_bf16)
        acc = d if acc is None else acc + d
        if i + 1 < terms:
            rem = rem - hi.astype(F32)
    return acc


def _split_dot_left(m_bf16, x, terms):
    acc = None
    rem = x
    for i in range(terms):
        hi = rem.astype(BF16)
        d = _dot(m_bf16, hi)
        acc = d if acc is None else acc + d
        if i + 1 < terms:
            rem = rem - hi.astype(F32)
    return acc


def _block_matrix(n, blk, value):
    sh = int(math.log2(blk))
    r = lax.shift_right_logical(lax.broadcasted_iota(jnp.int32, (n, n), 0), sh)
    c = lax.shift_right_logical(lax.broadcasted_iota(jnp.int32, (n, n), 1), sh)
    return jnp.where(r == c, value, 0.0).astype(BF16)


def _params(sem):
    return pltpu.CompilerParams(dimension_semantics=sem, vmem_limit_bytes=VMEM_LIMIT_BYTES)


def _layer_block(shape, layer, n_grid):
    zeros = (0,) * len(shape)
    if n_grid == 1:
        index_map = lambda i: (layer,) + zeros
    else:
        index_map = lambda i, j: (layer,) + zeros
    return pl.BlockSpec((None,) + tuple(shape), index_map, pipeline_mode=pl.Buffered(1))


def _phase_copies(ext_ref, sh_ref):
    n = sh_ref.shape[1]
    for s in range(1, SUBLANES):
        sh_ref[s - 1] = ext_ref[s:s + n, :]


def _shifted_rows(ext_ref, sh_ref, start, rb):
    a, s = divmod(start, SUBLANES)
    base = a * SUBLANES
    return ext_ref[base:base + rb, :] if s == 0 else sh_ref[s - 1, base:base + rb, :]


def _conv_rows(ext_ref, sh_ref, w_ref, vec_ref, r0, rb):
    off = CONV_PAD - CONV_HIST
    acc = jnp.zeros((rb, GROUP_WIDTH), F32)
    for j in range(CONV_WIDTH):
        acc = acc + w_ref[j:j + 1, :] * _shifted_rows(ext_ref, sh_ref, r0 + off + j, rb)
    acc = acc + vec_ref[0:1, :]
    mu = jnp.mean(acc, axis=-1, keepdims=True)
    d = acc - mu
    var = jnp.mean(d * d, axis=-1, keepdims=True)
    yn = d * lax.rsqrt(var + LN_EPS) * vec_ref[1:2, :] + vec_ref[2:3, :]
    return yn * _sigmoid(yn)


def _pool_rows(ext_ref, sh_ref, wbd_ref, sc_ref, r0, rb, pos_start):
    base = POOL_PAD + r0
    sums = []
    acc = None
    for i in range(max(POOL_WINDOWS)):
        sh = _shifted_rows(ext_ref, sh_ref, base - i, rb)
        acc = sh if acc is None else acc + sh
        if i + 1 in POOL_WINDOWS:
            sums.append(acc)
    pos = pos_start + r0 + lax.broadcasted_iota(jnp.int32, (rb, 1), 0)
    means = [s / jnp.minimum(w, pos + 1).astype(F32) for s, w in zip(sums, POOL_WINDOWS)]
    lane = lax.broadcasted_iota(jnp.int32, (1, GROUP_WIDTH), 1)
    pc = GROUP_WIDTH // len(POOL_WINDOWS)
    mean = means[-1]
    for g in range(len(POOL_WINDOWS) - 2, -1, -1):
        mean = jnp.where(lane < (g + 1) * pc, means[g], mean)
    d = mean - ext_ref[base:base + rb, :]
    return _dot(d.astype(BF16), wbd_ref[...]) * sc_ref[...]


def _projmix_body(x_ref, g_ref, w_ref, chist_ref, cw_ref, cvec_ref, phist_ref, pwbd_ref, psc_ref,
                  ub_ref, ud_ref, ya_ref, yc_ref, ctail_ref, ptail_ref,
                  cext_ref, csh_ref, pext_ref, psh_ref, *, nb, tl, rb, pos0):
    li = pl.program_id(1)

    @pl.when(li == 0)
    def _():
        cext_ref[:, 0:CONV_PAD, :] = chist_ref[...]
        pext_ref[:, 0:POOL_PAD, :] = phist_ref[...]

    x = x_ref[...].reshape(nb * tl, D_MODEL)
    ms = jnp.mean(x * x, axis=-1, keepdims=True)
    xn = ((x * lax.rsqrt(ms + RMS_EPS)) * g_ref[...]).astype(BF16)
    b0 = A_COLS
    c0 = A_COLS + B_COLS
    d0 = c0 + C_COLS
    ua = _dot(xn, w_ref[:, 0:A_COLS])
    uc = _dot(xn, w_ref[:, c0:d0])
    ub_ref[...] = _dot(xn, w_ref[:, b0:c0]).reshape(nb, tl, B_COLS)
    ud_ref[...] = _dot(xn, w_ref[:, d0:IN_COLS]).reshape(nb, tl, D_COLS)
    glu = ua[:, :GROUP_WIDTH] * _sigmoid(ua[:, GROUP_WIDTH:])
    for b in range(nb):
        cext = cext_ref.at[b]
        pext = pext_ref.at[b]
        cext[CONV_PAD:CONV_PAD + tl, :] = glu[b * tl:(b + 1) * tl]
        pext[POOL_PAD:POOL_PAD + tl, :] = uc[b * tl:(b + 1) * tl]
        _phase_copies(cext, csh_ref)
        _phase_copies(pext, psh_ref)
        for r0 in range(0, tl, rb):
            ya_ref[b, r0:r0 + rb, :] = _conv_rows(cext, csh_ref, cw_ref, cvec_ref, r0, rb).astype(BF16)
            yc_ref[b, r0:r0 + rb, :] = _pool_rows(pext, psh_ref, pwbd_ref, psc_ref, r0, rb,
                                                  pos0 + li * tl).astype(BF16)
        ctail = cext[tl:tl + CONV_PAD, :]
        ptail = pext[tl:tl + POOL_PAD, :]
        ctail_ref[b] = ctail
        ptail_ref[b] = ptail
        cext[0:CONV_PAD, :] = ctail
        pext[0:POOL_PAD, :] = ptail


def _projmix(x, layer, g, w_in, chist, conv_w, conv_vec, phist, pool_wbd, pool_scale, pos0):
    b, l, _ = x.shape
    nb = b if l <= 64 else 1
    tl = min(l, MATMUL_ROWS)
    rb = min(tl, LOCAL_ROWS)
    assert l % tl == 0 and tl % rb == 0 and tl >= CONV_PAD and b % nb == 0
    gw = GROUP_WIDTH
    tile = lambda w: pl.BlockSpec((nb, tl, w), lambda i, j: (i, j, 0))
    per_seq = lambda r: pl.BlockSpec((nb, r, gw), lambda i, j: (i, 0, 0))
    return pl.pallas_call(
        functools.partial(_projmix_body, nb=nb, tl=tl, rb=rb, pos0=pos0),
        grid=(b // nb, l // tl),
        in_specs=[
            tile(D_MODEL),
            _layer_block((1, D_MODEL), layer, 2),
            _layer_block((D_MODEL, IN_COLS), layer, 2),
            per_seq(CONV_PAD), _layer_block((CONV_PAD, gw), layer, 2), _layer_block((8, gw), layer, 2),
            per_seq(POOL_PAD), _layer_block((gw, gw), layer, 2), _layer_block((1, gw), layer, 2),
        ],
        out_specs=[tile(B_COLS), tile(D_COLS), tile(gw), tile(gw), per_seq(CONV_PAD), per_seq(POOL_PAD)],
        out_shape=[
            jax.ShapeDtypeStruct((b, l, B_COLS), F32),
            jax.ShapeDtypeStruct((b, l, D_COLS), F32),
            jax.ShapeDtypeStruct((b, l, gw), BF16),
            jax.ShapeDtypeStruct((b, l, gw), BF16),
            jax.ShapeDtypeStruct((b, CONV_PAD, gw), F32),
            jax.ShapeDtypeStruct((b, POOL_PAD, gw), F32),
        ],
        scratch_shapes=[
            pltpu.VMEM((nb, CONV_PAD + tl, gw), F32),
            pltpu.VMEM((SUBLANES - 1, CONV_PAD + tl - SUBLANES, gw), F32),
            pltpu.VMEM((nb, POOL_PAD + tl, gw), F32),
            pltpu.VMEM((SUBLANES - 1, POOL_PAD + tl - SUBLANES, gw), F32),
        ],
        compiler_params=_params(("parallel", "arbitrary")),
        name="projmix",
    )(x, g, w_in, chist, conv_w, conv_vec, phist, pool_wbd, pool_scale)


def _attn_body(sink_ref, u_ref, hk_ref, hv_ref, qg_ref, kg_ref, y_ref, kt_ref, vt_ref,
               kh_ref, vh_ref, *, layer, nb, tq, chunk, hist_valid):
    li = pl.program_id(0)
    hd = HEAD_DIM
    rows = nb * tq
    ncq = tq // chunk
    kw = WINDOW + chunk

    @pl.when(li == 0)
    def _():
        kh_ref[...] = hk_ref[...]
        vh_ref[...] = hv_ref[...]

    u = u_ref[...].reshape(rows, D_COLS)
    q = u[:, 0:Q_COLS]
    k = u[:, Q_COLS:Q_COLS + KV_COLS]
    v = u[:, Q_COLS + KV_COLS:D_COLS]
    inv = 1.0 / hd
    qms = _split_dot_right(q * q, _block_matrix(Q_COLS, hd, inv), 1)
    kms = _split_dot_right(k * k, _block_matrix(KV_COLS, hd, inv), 2)
    qn = (q * lax.rsqrt(qms + RMS_EPS)) * (qg_ref[...] * ATTN_SCALE)
    kn = (k * lax.rsqrt(kms + RMS_EPS)) * kg_ref[...]
    kcat = jnp.concatenate([kh_ref[...], kn.reshape(nb, tq, 2 * hd)], axis=1)
    vcat = jnp.concatenate([vh_ref[...], v.reshape(nb, tq, 2 * hd)], axis=1)
    ktail = kcat[:, tq:tq + WINDOW, :]
    vtail = vcat[:, tq:tq + WINDOW, :]
    kt_ref[...] = ktail
    vt_ref[...] = vtail
    kh_ref[...] = ktail
    vh_ref[...] = vtail
    kcat_b = kcat.astype(BF16)
    vcat_b = vcat.astype(BF16)

    slot_hi = lax.broadcasted_iota(jnp.int32, (1, 2 * hd), 1) >= hd
    q_tiles = []
    for h in range(N_Q_HEADS):
        col = qn[:, (h // 2) * 2 * hd:(h // 2 + 1) * 2 * hd]
        g = h // (N_Q_HEADS // N_KV_HEADS)
        if h % 2 != g:
            col = pltpu.roll(col, hd, axis=1)
        q_tiles.append(jnp.where(slot_hi if g == 1 else jnp.logical_not(slot_hi), col, 0.0).astype(BF16))
    pairs = [(b, c) for c in range(ncq) for b in range(nb)]
    qs = jnp.stack([jnp.concatenate([t[b * tq + c * chunk:b * tq + (c + 1) * chunk] for t in q_tiles], axis=0)
                    for b, c in pairs])
    ks = jnp.stack([kcat_b[b, c * chunk:c * chunk + kw] for b, c in pairs])
    vs = jnp.stack([vcat_b[b, c * chunk:c * chunk + kw] for b, c in pairs])
    st = _bdot_nt(ks, qs)
    nq = N_Q_HEADS * chunk
    if not hist_valid:
        n_edge = min(ncq, WINDOW // chunk) * nb
        cpos = [c * chunk for _, c in pairs[:n_edge]]
        kpos = lax.broadcasted_iota(jnp.int32, (1, kw, nq), 1) + (li * tq - WINDOW)
        edge = jnp.concatenate([jnp.where(kpos + cp >= 0, st[i:i + 1], NEG_INF) for i, cp in enumerate(cpos)],
                               axis=0)
        st = jnp.concatenate([edge, st[n_edge:]], axis=0) if n_edge < len(pairs) else edge
    hlane = lax.broadcasted_iota(jnp.int32, (1, 1, nq), 2)
    sk = jnp.full((1, 1, nq), sink_ref[layer, N_Q_HEADS - 1], F32)
    for h in range(N_Q_HEADS - 2, -1, -1):
        sk = jnp.where(hlane < (h + 1) * chunk, sink_ref[layer, h], sk)
    m = jnp.maximum(jnp.max(st, axis=1, keepdims=True), sk)
    p = jnp.exp(st - m)
    den = jnp.sum(p, axis=1, keepdims=True) + jnp.exp(sk - m)
    pn = (p * (1.0 / den)).astype(BF16)
    o = _bdot_tn(pn, vs)
    lo = jnp.logical_not(slot_hi)
    for i, (b, c) in enumerate(pairs):
        oc = o[i]
        col0 = jnp.where(lo, oc[0:chunk], pltpu.roll(oc[chunk:2 * chunk], hd, axis=1))
        col1 = jnp.where(lo, pltpu.roll(oc[2 * chunk:3 * chunk], hd, axis=1), oc[3 * chunk:4 * chunk])
        y_ref[b, c * chunk:(c + 1) * chunk, :] = jnp.concatenate([col0, col1], axis=1).astype(BF16)


def _attn(ud, hk, hv, layer, qg, kg, sinks, chunk, hist_valid):
    b, l, _ = ud.shape
    tq = min(l, ATTN_ROWS)
    assert l % tq == 0 and tq % chunk == 0
    kvw = N_KV_HEADS * HEAD_DIM
    return pl.pallas_call(
        functools.partial(_attn_body, layer=layer, nb=b, tq=tq, chunk=chunk, hist_valid=hist_valid),
        grid=(l // tq,),
        in_specs=[
            pl.BlockSpec(memory_space=pltpu.SMEM),
            pl.BlockSpec((b, tq, D_COLS), lambda j: (0, j, 0)),
            pl.BlockSpec((b, WINDOW, kvw), lambda j: (0, 0, 0)),
            pl.BlockSpec((b, WINDOW, kvw), lambda j: (0, 0, 0)),
            _layer_block((1, GROUP_WIDTH), layer, 1),
            _layer_block((1, kvw), layer, 1),
        ],
        out_specs=[
            pl.BlockSpec((b, tq, GROUP_WIDTH), lambda j: (0, j, 0)),
            pl.BlockSpec((b, WINDOW, kvw), lambda j: (0, 0, 0)),
            pl.BlockSpec((b, WINDOW, kvw), lambda j: (0, 0, 0)),
        ],
        out_shape=[
            jax.ShapeDtypeStruct((b, l, GROUP_WIDTH), BF16),
            jax.ShapeDtypeStruct((b, WINDOW, kvw), F32),
            jax.ShapeDtypeStruct((b, WINDOW, kvw), F32),
        ],
        scratch_shapes=[pltpu.VMEM((b, WINDOW, kvw), F32), pltpu.VMEM((b, WINDOW, kvw), F32)],
        compiler_params=_params(("arbitrary",)),
        name="attn",
    )(sinks, ud, hk, hv, qg, kg)


def _rwkv_body(u_ref, sp_ref, s0_ref, mu_ref, wl_ref, vec_ref, y_ref, sn_ref,
               prev_ref, s_ref, yacc_ref, *, nb, tb, chunk):
    li = pl.program_id(0)
    gw = GROUP_WIDTH
    hd = RWKV_HEAD
    nh = RWKV_HEADS
    rows = nb * tb
    nc = tb // chunk

    @pl.when(li == 0)
    def _():
        prev_ref[...] = sp_ref[...]
        zero = jnp.zeros((nb, hd, hd), F32)
        for h in range(nh):
            sh = s0_ref[:, h]
            s_ref[h * nb:(h + 1) * nb] = jnp.concatenate([sh, zero] if h % 2 == 0 else [zero, sh], axis=-1)

    row = lax.broadcasted_iota(jnp.int32, (tb, 1), 0)
    mu = mu_ref[...]
    xs_parts = []
    for b in range(nb):
        ub = u_ref[b]
        prev = jnp.where(row == 0, prev_ref[b], pltpu.roll(ub, 1, axis=0))
        prev_ref[b] = ub[tb - 1:tb, :]
        xs_parts.append(ub + mu * (prev - ub))
    xs = jnp.concatenate(xs_parts, axis=0)
    r = xs[:, 0:gw]
    k = xs[:, gw:2 * gw]
    v = xs[:, 2 * gw:3 * gw]
    lat = xs[:, 3 * gw:B_COLS]
    lane_lat = lax.broadcasted_iota(jnp.int32, (1, B_COLS - 3 * gw), 1)
    act = jnp.where(lane_lat < DECAY_RANK, jnp.tanh(lat),
                    jnp.where(lane_lat < DECAY_RANK + AAA_RANK, lat, _sigmoid(lat)))
    lo = _dot(act.astype(BF16), wl_ref[...])
    w0 = vec_ref[0:1, :]
    a0 = vec_ref[1:2, :]
    k_k = vec_ref[2:3, :]
    k_a = vec_ref[3:4, :]
    r_k = vec_ref[4:5, :]
    gn_g = vec_ref[5:6, :]
    gn_b = vec_ref[6:7, :]
    z = -(w0 + lo[:, 0:gw])
    softplus = jnp.maximum(z, 0.0) + jnp.log(1.0 + jnp.exp(-jnp.abs(z)))
    logw = -jnp.exp(-softplus - 0.5)
    a_rate = _sigmoid(a0 + lo[:, gw:2 * gw])
    gate = lo[:, 2 * gw:3 * gw]
    ones_blk = _block_matrix(gw, hd, 1.0)
    kk = k * k_k
    kk = kk * lax.rsqrt(jnp.maximum(_split_dot_right(kk * kk, ones_blk, 1), 1e-24))
    k_mod = k * (1.0 + (a_rate - 1.0) * k_a)
    b_v = kk * a_rate
    bonus = _split_dot_right(r * k_mod * r_k, ones_blk, 1) * v

    grp = min(rows, 256)
    gi = lax.broadcasted_iota(jnp.int32, (grp, grp), 0)
    gj = lax.broadcasted_iota(jnp.int32, (grp, grp), 1)
    csh = int(math.log2(chunk))
    tri = jnp.where(jnp.logical_and(lax.shift_right_logical(gi, csh) == lax.shift_right_logical(gj, csh),
                                    gj <= gi), 1.0, 0.0).astype(BF16)
    cum = jnp.concatenate([_split_dot_left(tri, logw[g0:g0 + grp], 2) for g0 in range(0, rows, grp)], axis=0)
    cum3 = cum.reshape(nb * nc, chunk, gw)
    cum_c = cum3[:, chunk - 1:chunk, :]
    e_end = jnp.exp(cum_c - cum3).reshape(rows, gw)
    w_c = jnp.exp(cum_c)
    e_neg = jnp.exp(-cum)
    dense = dict(
        a=-kk * jnp.exp(cum - logw),
        r=r * jnp.exp(cum),
        bt=b_v * e_neg,
        kt=k_mod * e_neg,
        bh=b_v * e_end,
        kh=k_mod * e_end,
    )
    slot_hi = lax.broadcasted_iota(jnp.int32, (1, 2 * hd), 1) >= hd

    def head_tile(x, h, own_slot):
        col = x[:, (h // 2) * 2 * hd:(h // 2 + 1) * 2 * hd]
        keep = slot_hi if (h % 2 == 1) == own_slot else jnp.logical_not(slot_hi)
        if x.dtype == BF16:
            return col * jnp.where(keep, 1.0, 0.0).astype(BF16)
        return jnp.where(keep, col, 0.0)

    v_sw = jnp.concatenate([pltpu.roll(v[:, j * 2 * hd:(j + 1) * 2 * hd], hd, axis=1) for j in range(nh // 2)],
                           axis=1)
    tiles = {"r": [head_tile(dense["r"], h, True) for h in range(nh)]}
    dense_b = {name: dense[name].astype(BF16) for name in ("a", "bt", "kt", "bh", "kh")}
    tiles_b = {name: [head_tile(dense_b[name], h, True) for h in range(nh)] for name in ("a", "bh", "kh")}
    for name in ("bt", "kt"):
        tiles_b[name] = [dense_b[name][:, (h // 2) * 2 * hd:(h // 2 + 1) * 2 * hd] for h in range(nh)]
    v_sw_b = v_sw.astype(BF16)
    v_tiles = [head_tile(v_sw_b, h, False) for h in range(nh)]

    def blocks(per_head, c):
        return jnp.stack([per_head[h][b * tb + c * chunk:b * tb + (c + 1) * chunk]
                          for h in range(nh) for b in range(nb)])

    n = nh * nb
    ri = lax.broadcasted_iota(jnp.int32, (2 * chunk, 2 * chunk), 0)
    ci = jnp.bitwise_and(lax.broadcasted_iota(jnp.int32, (2 * chunk, 2 * chunk), 1), chunk - 1)
    gmask = ci < jnp.bitwise_and(ri, chunk - 1) + lax.shift_right_logical(ri, csh)
    zeros_c = jnp.zeros((n, chunk, 2 * hd), BF16)
    n_sq = int(math.log2(chunk))
    for c in range(nc):
        a_b = blocks(tiles_b["a"], c)
        r_f = blocks(tiles["r"], c)
        v_b = blocks(v_tiles, c)
        ar = jnp.concatenate([a_b, r_f.astype(BF16)], axis=1)
        bk = jnp.concatenate([blocks(tiles_b["bt"], c), blocks(tiles_b["kt"], c)], axis=1)
        bhkh = jnp.concatenate([blocks(tiles_b["bh"], c), blocks(tiles_b["kh"], c)], axis=1)
        g = jnp.where(gmask, _bdot_nt(ar, bk), 0.0)
        g_top = g[:, :chunk, :]
        g_bot = g[:, chunk:, :].astype(BF16)
        w = a_b.astype(F32) + _bdot(g_top.astype(BF16), jnp.concatenate([zeros_c, v_b], axis=1))
        p = g_top[:, :, :chunk]
        for i in range(n_sq):
            pb = p.astype(BF16)
            if i + 1 < n_sq:
                res = _bdot(pb, jnp.concatenate([w.astype(BF16), pb], axis=2))
                w = w + res[:, :, :2 * hd]
                p = res[:, :, 2 * hd:]
            else:
                hc = chunk // 2
                low = w[:, hc:, :] + _bdot(pb[:, hc:, :hc], w[:, :hc, :].astype(BF16))
                w = jnp.concatenate([w[:, :hc, :], low], axis=1)
        xv = jnp.concatenate([w.astype(BF16), v_b], axis=1)
        ry = _bdot(g_bot, xv)
        mp = _bdot_tn(xv, bhkh)
        s_old = s_ref[...]
        s_b = s_old.astype(BF16)
        y_nt = _bdot_nt((ry + r_f).astype(BF16), s_b)
        wc = jnp.stack([w_c[b * nc + c][:, (h // 2) * 2 * hd:(h // 2 + 1) * 2 * hd]
                        for h in range(nh) for b in range(nb)])
        psi = jnp.concatenate(
            [mp[h * nb:(h + 1) * nb, (1 - h % 2) * hd:(2 - h % 2) * hd, :] for h in range(nh)], axis=0)
        s_ref[...] = s_old * wc + _bdot(s_b, mp.astype(BF16)) + psi
        for b in range(nb):
            ys = [y_nt[h * nb + b] + ry[h * nb + b][:, (1 - h % 2) * hd:(2 - h % 2) * hd] for h in range(nh)]
            yacc_ref[b * tb + c * chunk:b * tb + (c + 1) * chunk, :] = jnp.concatenate(ys, axis=1)

    y = yacc_ref[...]
    avg_blk = _block_matrix(gw, hd, 1.0 / hd)
    m = _split_dot_right(y, avg_blk, 1)
    d = y - m
    var = _split_dot_right(d * d, avg_blk, 1)
    yn = d * lax.rsqrt(var + GN_EPS) * gn_g + gn_b
    out = ((yn + bonus) * gate).astype(BF16)
    for b in range(nb):
        y_ref[b] = out[b * tb:(b + 1) * tb]

    @pl.when(li == pl.num_programs(0) - 1)
    def _():
        for h in range(nh):
            sn_ref[:, h] = s_ref[h * nb:(h + 1) * nb, :, (h % 2) * hd:(h % 2 + 1) * hd]


def _rwkv(ub, shift_prev, state, layer, mu, wl, vec, chunk):
    b, l, _ = ub.shape
    tb = min(l, RWKV_ROWS)
    assert l % tb == 0 and tb % chunk == 0
    hd = RWKV_HEAD
    n = RWKV_HEADS * b
    sshape = (b, RWKV_HEADS, hd, hd)
    return pl.pallas_call(
        functools.partial(_rwkv_body, nb=b, tb=tb, chunk=chunk),
        grid=(l // tb,),
        in_specs=[
            pl.BlockSpec((b, tb, B_COLS), lambda j: (0, j, 0)),
            pl.BlockSpec((b, 1, B_COLS), lambda j: (0, 0, 0)),
            pl.BlockSpec(sshape, lambda j: (0, 0, 0, 0)),
            _layer_block((1, B_COLS), layer, 1),
            _layer_block((128, 3 * GROUP_WIDTH), layer, 1),
            _layer_block((8, GROUP_WIDTH), layer, 1),
        ],
        out_specs=[
            pl.BlockSpec((b, tb, GROUP_WIDTH), lambda j: (0, j, 0)),
            pl.BlockSpec(sshape, lambda j: (0, 0, 0, 0)),
        ],
        out_shape=[
            jax.ShapeDtypeStruct((b, l, GROUP_WIDTH), BF16),
            jax.ShapeDtypeStruct(sshape, F32),
        ],
        scratch_shapes=[
            pltpu.VMEM((b, 1, B_COLS), F32),
            pltpu.VMEM((n, hd, 2 * hd), F32),
            pltpu.VMEM((b * tb, GROUP_WIDTH), F32),
        ],
        compiler_params=_params(("arbitrary",)),
        name="rwkv",
    )(ub, shift_prev, state, mu, wl, vec)


def _outffn_body(x_ref, ya_ref, yb_ref, yc_ref, yd_ref, wo_ref, g_ref, wg_ref, wu_ref, wd_ref, o_ref):
    ycat = jnp.concatenate([ya_ref[...], yb_ref[...], yc_ref[...], yd_ref[...]], axis=-1)
    x1 = x_ref[...] + _dot(ycat, wo_ref[...])
    ms = jnp.mean(x1 * x1, axis=-1, keepdims=True)
    hn = ((x1 * lax.rsqrt(ms + RMS_EPS)) * g_ref[...]).astype(BF16)
    hg = _dot(hn, wg_ref[...])
    hu = _dot(hn, wu_ref[...])
    act = (hg * _sigmoid(hg) * hu).astype(BF16)
    o_ref[...] = x1 + _dot(act, wd_ref[...])


def _outffn(x2d, ya, yb, yc, yd, layer, wo, g, wg, wu, wd):
    t = x2d.shape[0]
    tm = min(t, MATMUL_ROWS)
    assert t % tm == 0
    row = lambda w: pl.BlockSpec((tm, w), lambda i: (i, 0))
    return pl.pallas_call(
        _outffn_body,
        grid=(t // tm,),
        in_specs=[
            row(D_MODEL), row(GROUP_WIDTH), row(GROUP_WIDTH), row(GROUP_WIDTH), row(GROUP_WIDTH),
            _layer_block((4 * GROUP_WIDTH, D_MODEL), layer, 1),
            _layer_block((1, D_MODEL), layer, 1),
            _layer_block((D_MODEL, D_FF), layer, 1),
            _layer_block((D_MODEL, D_FF), layer, 1),
            _layer_block((D_FF, D_MODEL), layer, 1),
        ],
        out_specs=row(D_MODEL),
        out_shape=jax.ShapeDtypeStruct((t, D_MODEL), F32),
        compiler_params=_params(("parallel",)),
        name="outffn",
    )(x2d, ya, yb, yc, yd, wo, g, wg, wu, wd)


def _rows8(vectors):
    stacked = jnp.stack(vectors, axis=1)
    return jnp.pad(stacked, ((0, 0), (0, SUBLANES - stacked.shape[1]), (0, 0)))


def _stacked_weights(p):
    gw = GROUP_WIDTH
    depth = p['w_in'].shape[0]
    place = lambda a, before, after: jnp.pad(a, ((0, 0), (0, 0), (before, after)))
    wl = jnp.concatenate([place(p['rwkv_w2'], 0, 2 * gw), place(p['rwkv_a2'], gw, gw),
                          place(p['rwkv_g2'], 2 * gw, 0)], axis=1)
    pc = gw // len(POOL_WINDOWS)
    wbd = jnp.concatenate([place(p['pool_w'][:, g], g * pc, gw - (g + 1) * pc)
                           for g in range(len(POOL_WINDOWS))], axis=1)
    return dict(
        norm_mix_g=p['norm_mix_g'][:, None, :],
        w_in=p['w_in'].astype(BF16),
        conv_w=jnp.pad(p['conv_w'], ((0, 0), (0, CONV_PAD - CONV_WIDTH), (0, 0))),
        conv_vec=_rows8([p['conv_b'], p['conv_ln_g'], p['conv_ln_b']]),
        rwkv_mu=p['rwkv_mu'][:, None, :],
        rwkv_wl=wl.astype(BF16),
        rwkv_vec=_rows8([p['rwkv_w0'], p['rwkv_a0'], p['rwkv_k_k'], p['rwkv_k_a'],
                         p['rwkv_r_k'].reshape(depth, gw), p['rwkv_gn_g'], p['rwkv_gn_b']]),
        pool_wbd=wbd.astype(BF16),
        pool_scale=p['pool_scale'][:, None, :],
        attn_qg=jnp.tile(p['attn_q_norm'], (1, N_Q_HEADS))[:, None, :],
        attn_kg=jnp.tile(p['attn_k_norm'], (1, N_KV_HEADS))[:, None, :],
        attn_sinks=p['attn_sinks'],
        w_out=p['w_out'].astype(BF16),
        norm_ffn_g=p['norm_ffn_g'][:, None, :],
        ffn_w_gate=p['ffn_w_gate'].astype(BF16),
        ffn_w_up=p['ffn_w_up'].astype(BF16),
        ffn_w_down=p['ffn_w_down'].astype(BF16),
    )


def _trunk(x, conv_hist, rwkv_state, shift_prev, pool_hist, k_cache, v_cache, pos0, chunk, w):
    b, l, _ = x.shape
    has_past = conv_hist is not None
    kvw = N_KV_HEADS * HEAD_DIM
    new = [[] for _ in range(6)]
    for li in range(DEPTH):
        if has_past:
            ch = jnp.pad(conv_hist[li], ((0, 0), (CONV_PAD - CONV_HIST, 0), (0, 0)))
            rs = rwkv_state[li]
            sp = shift_prev[li][:, None, :]
            ph = jnp.pad(pool_hist[li], ((0, 0), (POOL_PAD - POOL_HIST, 0), (0, 0)))
            hk = k_cache[li].reshape(b, WINDOW, kvw)
            hv = v_cache[li].reshape(b, WINDOW, kvw)
        else:
            ch = jnp.zeros((b, CONV_PAD, GROUP_WIDTH), F32)
            rs = jnp.zeros((b, RWKV_HEADS, RWKV_HEAD, RWKV_HEAD), F32)
            sp = jnp.zeros((b, 1, B_COLS), F32)
            ph = jnp.zeros((b, POOL_PAD, GROUP_WIDTH), F32)
            hk = jnp.zeros((b, WINDOW, kvw), F32)
            hv = jnp.zeros((b, WINDOW, kvw), F32)
        ub, ud, ya, yc, conv_tail, pool_tail = _projmix(
            x, li, w['norm_mix_g'], w['w_in'], ch, w['conv_w'], w['conv_vec'],
            ph, w['pool_wbd'], w['pool_scale'], pos0)
        yb, s_new = _rwkv(ub, sp, rs, li, w['rwkv_mu'], w['rwkv_wl'], w['rwkv_vec'], min(CHUNK, l))
        yd, k_tail, v_tail = _attn(ud, hk, hv, li, w['attn_qg'], w['attn_kg'], w['attn_sinks'], chunk,
                                   has_past)
        flat = lambda y: y.reshape(b * l, GROUP_WIDTH)
        x = _outffn(x.reshape(b * l, D_MODEL), flat(ya), flat(yb), flat(yc), flat(yd), li, w['w_out'],
                    w['norm_ffn_g'], w['ffn_w_gate'], w['ffn_w_up'], w['ffn_w_down']
                    ).reshape(b, l, D_MODEL)
        new[0].append(conv_tail[:, CONV_PAD - CONV_HIST:, :])
        new[1].append(s_new)
        new[2].append(ub[:, l - 1, :])
        new[3].append(pool_tail[:, POOL_PAD - POOL_HIST:, :])
        new[4].append(k_tail.reshape(b, WINDOW, N_KV_HEADS, HEAD_DIM))
        new[5].append(v_tail.reshape(b, WINDOW, N_KV_HEADS, HEAD_DIM))
    return x, tuple(jnp.stack(n) for n in new)


def kernel(x_prompt, x_sample, cache_conv, state_rwkv, state_rwkv_shift, cache_pool, cache_k, cache_v, norm_mix_g, w_in, conv_w, conv_b, conv_ln_g, conv_ln_b, rwkv_mu, rwkv_w0, rwkv_w2, rwkv_a0, rwkv_a2, rwkv_g2, rwkv_k_k, rwkv_k_a, rwkv_r_k, rwkv_gn_g, rwkv_gn_b, pool_w, pool_scale, attn_q_norm, attn_k_norm, attn_sinks, w_out, norm_ffn_g, ffn_w_gate, ffn_w_up, ffn_w_down):
    w = _stacked_weights(dict(
        norm_mix_g=norm_mix_g, w_in=w_in, conv_w=conv_w, conv_b=conv_b, conv_ln_g=conv_ln_g,
        conv_ln_b=conv_ln_b, rwkv_mu=rwkv_mu, rwkv_w0=rwkv_w0, rwkv_w2=rwkv_w2, rwkv_a0=rwkv_a0,
        rwkv_a2=rwkv_a2, rwkv_g2=rwkv_g2, rwkv_k_k=rwkv_k_k, rwkv_k_a=rwkv_k_a, rwkv_r_k=rwkv_r_k,
        rwkv_gn_g=rwkv_gn_g, rwkv_gn_b=rwkv_gn_b, pool_w=pool_w, pool_scale=pool_scale,
        attn_q_norm=attn_q_norm, attn_k_norm=attn_k_norm, attn_sinks=attn_sinks, w_out=w_out,
        norm_ffn_g=norm_ffn_g, ffn_w_gate=ffn_w_gate, ffn_w_up=ffn_w_up, ffn_w_down=ffn_w_down))
    y_p, (conv_p, rwkv_p, shift_p, pool_p, k_p, v_p) = _trunk(
        x_prompt, None, None, None, None, None, None, 0, CHUNK, w)
    y_s, (conv_s, rwkv_s, shift_s, pool_s, k_s, v_s) = _trunk(
        x_sample, cache_conv, state_rwkv, state_rwkv_shift, cache_pool, cache_k, cache_v,
        PAST_LEN, x_sample.shape[1], w)
    return (y_p, y_s, conv_p, conv_s, rwkv_p, rwkv_s, shift_p, shift_s,
            pool_p, pool_s, k_p, k_s, v_p, v_s)
```

```python
import functools
import math

import jax
import jax.numpy as jnp
from jax import lax
from jax.experimental import pallas as pl
from jax.experimental.pallas import tpu as pltpu

F32 = jnp.float32
BF16 = jnp.bfloat16

D_MODEL = 1024
DEPTH = 2
PAST_LEN = 1024
CHUNK = 64
GROUP_WIDTH = 256
CONV_WIDTH = 31
CONV_HIST = CONV_WIDTH - 1
CONV_PAD = 32
RWKV_HEAD = 64
RWKV_HEADS = 4
POOL_WINDOWS = (2, 4, 8, 16)
POOL_HIST = 15
POOL_PAD = 16
HEAD_DIM = 64
N_Q_HEADS = 4
N_KV_HEADS = 2
WINDOW = 128
D_FF = 2816
A_COLS = 512
B_COLS = 896
C_COLS = 256
D_COLS = 512
IN_COLS = A_COLS + B_COLS + C_COLS + D_COLS
RMS_EPS = 1e-6
LN_EPS = 1e-5
GN_EPS = 64e-5
ATTN_SCALE = HEAD_DIM ** -0.5
NEG_INF = -1e30

DECAY_RANK = 32
AAA_RANK = 32
Q_COLS = N_Q_HEADS * HEAD_DIM
KV_COLS = N_KV_HEADS * HEAD_DIM

VMEM_LIMIT_BYTES = 56 * 1024 * 1024
SUBLANES = 8
MATMUL_ROWS = 512
MIXER_ROWS = 256
LOCAL_ROWS = 128


def _dot(a, b):
    return jnp.dot(a, b, preferred_element_type=F32)


def _bdot(a, b):
    return lax.dot_general(a, b, (((2,), (1,)), ((0,), (0,))), preferred_element_type=F32)


def _bdot_nt(a, b):
    return lax.dot_general(a, b, (((2,), (2,)), ((0,), (0,))), preferred_element_type=F32)


def _bdot_tn(a, b):
    return lax.dot_general(a, b, (((1,), (1,)), ((0,), (0,))), preferred_element_type=F32)


def _sigmoid(x):
    return 1.0 / (1.0 + jnp.exp(-x))


def _split_dot_right(x, m_bf16, terms):
    acc = None
    rem = x
    for i in range(terms):
        hi = rem.astype(BF16)
        d = _dot(hi, m_bf16)
        acc = d if acc is None else acc + d
        if i + 1 < terms:
            rem = rem - hi.astype(F32)
    return acc


def _split_dot_left(m_bf16, x, terms):
    acc = None
    rem = x
    for i in range(terms):
        hi = rem.astype(BF16)
        d = _dot(m_bf16, hi)
        acc = d if acc is None else acc + d
        if i + 1 < terms:
            rem = rem - hi.astype(F32)
    return acc


def _block_matrix(n, blk, value):
    sh = int(math.log2(blk))
    r = lax.shift_right_logical(lax.broadcasted_iota(jnp.int32, (n, n), 0), sh)
    c = lax.shift_right_logical(lax.broadcasted_iota(jnp.int32, (n, n), 1), sh)
    return jnp.where(r == c, value, 0.0).astype(BF16)


def _params(sem):
    return pltpu.CompilerParams(dimension_semantics=sem, vmem_limit_bytes=VMEM_LIMIT_BYTES)


def _layer_block(shape, layer, n_grid):
    zeros = (0,) * len(shape)
    if n_grid == 1:
        index_map = lambda i: (layer,) + zeros
    else:
        index_map = lambda i, j: (layer,) + zeros
    return pl.BlockSpec((None,) + tuple(shape), index_map, pipeline_mode=pl.Buffered(1))


def _phase_copies(ext_ref, sh_ref):
    n = sh_ref.shape[1]
    for s in range(1, SUBLANES):
        sh_ref[s - 1] = ext_ref[s:s + n, :]


def _shifted_rows(ext_ref, sh_ref, start, rb):
    a, s = divmod(start, SUBLANES)
    base = a * SUBLANES
    return ext_ref[base:base + rb, :] if s == 0 else sh_ref[s - 1, base:base + rb, :]


def _conv_rows(ext_ref, sh_ref, w_ref, vec_ref, r0, rb):
    off = CONV_PAD - CONV_HIST
    acc = jnp.zeros((rb, GROUP_WIDTH), F32)
    for j in range(CONV_WIDTH):
        acc = acc + w_ref[j:j + 1, :] * _shifted_rows(ext_ref, sh_ref, r0 + off + j, rb)
    acc = acc + vec_ref[0:1, :]
    mu = jnp.mean(acc, axis=-1, keepdims=True)
    d = acc - mu
    var = jnp.mean(d * d, axis=-1, keepdims=True)
    yn = d * lax.rsqrt(var + LN_EPS) * vec_ref[1:2, :] + vec_ref[2:3, :]
    return yn * _sigmoid(yn)


def _pool_rows(ext_ref, sh_ref, wbd_ref, sc_ref, r0, rb, pos_start):
    base = POOL_PAD + r0
    sums = []
    acc = None
    for i in range(max(POOL_WINDOWS)):
        sh = _shifted_rows(ext_ref, sh_ref, base - i, rb)
        acc = sh if acc is None else acc + sh
        if i + 1 in POOL_WINDOWS:
            sums.append(acc)
    pos = pos_start + r0 + lax.broadcasted_iota(jnp.int32, (rb, 1), 0)
    means = [s / jnp.minimum(w, pos + 1).astype(F32) for s, w in zip(sums, POOL_WINDOWS)]
    lane = lax.broadcasted_iota(jnp.int32, (1, GROUP_WIDTH), 1)
    pc = GROUP_WIDTH // len(POOL_WINDOWS)
    mean = means[-1]
    for g in range(len(POOL_WINDOWS) - 2, -1, -1):
        mean = jnp.where(lane < (g + 1) * pc, means[g], mean)
    d = mean - ext_ref[base:base + rb, :]
    return _dot(d.astype(BF16), wbd_ref[...]) * sc_ref[...]


def _projmix_body(x_ref, g_ref, w_ref, chist_ref, cw_ref, cvec_ref, phist_ref, pwbd_ref, psc_ref,
                  ub_ref, ud_ref, ya_ref, yc_ref, ctail_ref, ptail_ref,
                  cext_ref, csh_ref, pext_ref, psh_ref, *, nb, tl, rb, pos0):
    li = pl.program_id(1)

    @pl.when(li == 0)
    def _():
        cext_ref[:, 0:CONV_PAD, :] = chist_ref[...]
        pext_ref[:, 0:POOL_PAD, :] = phist_ref[...]

    x = x_ref[...].reshape(nb * tl, D_MODEL)
    ms = jnp.mean(x * x, axis=-1, keepdims=True)
    xn = ((x * lax.rsqrt(ms + RMS_EPS)) * g_ref[...]).astype(BF16)
    b0 = A_COLS
    c0 = A_COLS + B_COLS
    d0 = c0 + C_COLS
    ua = _dot(xn, w_ref[:, 0:A_COLS])
    uc = _dot(xn, w_ref[:, c0:d0])
    ub_ref[...] = _dot(xn, w_ref[:, b0:c0]).reshape(nb, tl, B_COLS)
    ud_ref[...] = _dot(xn, w_ref[:, d0:IN_COLS]).reshape(nb, tl, D_COLS)
    glu = ua[:, :GROUP_WIDTH] * _sigmoid(ua[:, GROUP_WIDTH:])
    for b in range(nb):
        cext = cext_ref.at[b]
        pext = pext_ref.at[b]
        cext[CONV_PAD:CONV_PAD + tl, :] = glu[b * tl:(b + 1) * tl]
        pext[POOL_PAD:POOL_PAD + tl, :] = uc[b * tl:(b + 1) * tl]
        _phase_copies(cext, csh_ref)
        _phase_copies(pext, psh_ref)
        for r0 in range(0, tl, rb):
            ya_ref[b, r0:r0 + rb, :] = _conv_rows(cext, csh_ref, cw_ref, cvec_ref, r0, rb).astype(BF16)
            yc_ref[b, r0:r0 + rb, :] = _pool_rows(pext, psh_ref, pwbd_ref, psc_ref, r0, rb,
                                                  pos0 + li * tl).astype(BF16)
        ctail = cext[tl:tl + CONV_PAD, :]
        ptail = pext[tl:tl + POOL_PAD, :]
        ctail_ref[b] = ctail
        ptail_ref[b] = ptail
        cext[0:CONV_PAD, :] = ctail
        pext[0:POOL_PAD, :] = ptail


def _projmix(x, layer, g, w_in, chist, conv_w, conv_vec, phist, pool_wbd, pool_scale, pos0):
    b, l, _ = x.shape
    nb = b if l <= 64 else 1
    tl = min(l, MATMUL_ROWS)
    rb = min(tl, LOCAL_ROWS)
    assert l % tl == 0 and tl % rb == 0 and tl >= CONV_PAD and b % nb == 0
    gw = GROUP_WIDTH
    tile = lambda w: pl.BlockSpec((nb, tl, w), lambda i, j: (i, j, 0))
    per_seq = lambda r: pl.BlockSpec((nb, r, gw), lambda i, j: (i, 0, 0))
    return pl.pallas_call(
        functools.partial(_projmix_body, nb=nb, tl=tl, rb=rb, pos0=pos0),
        grid=(b // nb, l // tl),
        in_specs=[
            tile(D_MODEL),
            _layer_block((1, D_MODEL), layer, 2),
            _layer_block((D_MODEL, IN_COLS), layer, 2),
            per_seq(CONV_PAD), _layer_block((CONV_PAD, gw), layer, 2), _layer_block((8, gw), layer, 2),
            per_seq(POOL_PAD), _layer_block((gw, gw), layer, 2), _layer_block((1, gw), layer, 2),
        ],
        out_specs=[tile(B_COLS), tile(D_COLS), tile(gw), tile(gw), per_seq(CONV_PAD), per_seq(POOL_PAD)],
        out_shape=[
            jax.ShapeDtypeStruct((b, l, B_COLS), F32),
            jax.ShapeDtypeStruct((b, l, D_COLS), F32),
            jax.ShapeDtypeStruct((b, l, gw), BF16),
            jax.ShapeDtypeStruct((b, l, gw), BF16),
            jax.ShapeDtypeStruct((b, CONV_PAD, gw), F32),
            jax.ShapeDtypeStruct((b, POOL_PAD, gw), F32),
        ],
        scratch_shapes=[
            pltpu.VMEM((nb, CONV_PAD + tl, gw), F32),
            pltpu.VMEM((SUBLANES - 1, CONV_PAD + tl - SUBLANES, gw), F32),
            pltpu.VMEM((nb, POOL_PAD + tl, gw), F32),
            pltpu.VMEM((SUBLANES - 1, POOL_PAD + tl - SUBLANES, gw), F32),
        ],
        compiler_params=_params(("parallel", "arbitrary")),
        name="projmix",
    )(x, g, w_in, chist, conv_w, conv_vec, phist, pool_wbd, pool_scale)


def _attn_body(sink_ref, u_ref, hk_ref, hv_ref, qg_ref, kg_ref, y_ref, kt_ref, vt_ref,
               kh_ref, vh_ref, *, layer, nb, tq, chunk, hist_valid):
    li = pl.program_id(0)
    hd = HEAD_DIM
    rows = nb * tq
    ncq = tq // chunk
    kw = WINDOW + chunk

    @pl.when(li == 0)
    def _():
        kh_ref[...] = hk_ref[...]
        vh_ref[...] = hv_ref[...]

    u = u_ref[...].reshape(rows, D_COLS)
    q = u[:, 0:Q_COLS]
    k = u[:, Q_COLS:Q_COLS + KV_COLS]
    v = u[:, Q_COLS + KV_COLS:D_COLS]
    inv = 1.0 / hd
    qms = _split_dot_right(q * q, _block_matrix(Q_COLS, hd, inv), 1)
    kms = _split_dot_right(k * k, _block_matrix(KV_COLS, hd, inv), 2)
    qn = (q * lax.rsqrt(qms + RMS_EPS)) * (qg_ref[...] * ATTN_SCALE)
    kn = (k * lax.rsqrt(kms + RMS_EPS)) * kg_ref[...]
    kcat = jnp.concatenate([kh_ref[...], kn.reshape(nb, tq, 2 * hd)], axis=1)
    vcat = jnp.concatenate([vh_ref[...], v.reshape(nb, tq, 2 * hd)], axis=1)
    ktail = kcat[:, tq:tq + WINDOW, :]
    vtail = vcat[:, tq:tq + WINDOW, :]
    kt_ref[...] = ktail
    vt_ref[...] = vtail
    kh_ref[...] = ktail
    vh_ref[...] = vtail
    kcat_b = kcat.astype(BF16)
    vcat_b = vcat.astype(BF16)

    slot_hi = lax.broadcasted_iota(jnp.int32, (1, 2 * hd), 1) >= hd
    q_tiles = []
    for h in range(N_Q_HEADS):
        col = qn[:, (h // 2) * 2 * hd:(h // 2 + 1) * 2 * hd]
        g = h // (N_Q_HEADS // N_KV_HEADS)
        if h % 2 != g:
            col = pltpu.roll(col, hd, axis=1)
        q_tiles.append(jnp.where(slot_hi if g == 1 else jnp.logical_not(slot_hi), col, 0.0).astype(BF16))
    pairs = [(b, c) for c in range(ncq) for b in range(nb)]
    qs = jnp.stack([jnp.concatenate([t[b * tq + c * chunk:b * tq + (c + 1) * chunk] for t in q_tiles], axis=0)
                    for b, c in pairs])
    ks = jnp.stack([kcat_b[b, c * chunk:c * chunk + kw] for b, c in pairs])
    vs = jnp.stack([vcat_b[b, c * chunk:c * chunk + kw] for b, c in pairs])
    st = _bdot_nt(ks, qs)
    nq = N_Q_HEADS * chunk
    if not hist_valid:
        n_edge = min(ncq, WINDOW // chunk) * nb
        cpos = [c * chunk for _, c in pairs[:n_edge]]
        kpos = lax.broadcasted_iota(jnp.int32, (1, kw, nq), 1) + (li * tq - WINDOW)
        edge = jnp.concatenate([jnp.where(kpos + cp >= 0, st[i:i + 1], NEG_INF) for i, cp in enumerate(cpos)],
                               axis=0)
        st = jnp.concatenate([edge, st[n_edge:]], axis=0) if n_edge < len(pairs) else edge
    hlane = lax.broadcasted_iota(jnp.int32, (1, 1, nq), 2)
    sk = jnp.full((1, 1, nq), sink_ref[layer, N_Q_HEADS - 1], F32)
    for h in range(N_Q_HEADS - 2, -1, -1):
        sk = jnp.where(hlane < (h + 1) * chunk, sink_ref[layer, h], sk)
    m = jnp.maximum(jnp.max(st, axis=1, keepdims=True), sk)
    p = jnp.exp(st - m)
    den = jnp.sum(p, axis=1, keepdims=True) + jnp.exp(sk - m)
    pn = (p * (1.0 / den)).astype(BF16)
    o = _bdot_tn(pn, vs)
    lo = jnp.logical_not(slot_hi)
    for i, (b, c) in enumerate(pairs):
        oc = o[i]
        col0 = jnp.where(lo, oc[0:chunk], pltpu.roll(oc[chunk:2 * chunk], hd, axis=1))
        col1 = jnp.where(lo, pltpu.roll(oc[2 * chunk:3 * chunk], hd, axis=1), oc[3 * chunk:4 * chunk])
        y_ref[b, c * chunk:(c + 1) * chunk, :] = jnp.concatenate([col0, col1], axis=1).astype(BF16)


def _attn(ud, hk, hv, layer, qg, kg, sinks, chunk, hist_valid):
    b, l, _ = ud.shape
    tq = min(l, MIXER_ROWS)
    assert l % tq == 0 and tq % chunk == 0
    kvw = N_KV_HEADS * HEAD_DIM
    return pl.pallas_call(
        functools.partial(_attn_body, layer=layer, nb=b, tq=tq, chunk=chunk, hist_valid=hist_valid),
        grid=(l // tq,),
        in_specs=[
            pl.BlockSpec(memory_space=pltpu.SMEM),
            pl.BlockSpec((b, tq, D_COLS), lambda j: (0, j, 0)),
            pl.BlockSpec((b, WINDOW, kvw), lambda j: (0, 0, 0)),
            pl.BlockSpec((b, WINDOW, kvw), lambda j: (0, 0, 0)),
            _layer_block((1, GROUP_WIDTH), layer, 1),
            _layer_block((1, kvw), layer, 1),
        ],
        out_specs=[
            pl.BlockSpec((b, tq, GROUP_WIDTH), lambda j: (0, j, 0)),
            pl.BlockSpec((b, WINDOW, kvw), lambda j: (0, 0, 0)),
            pl.BlockSpec((b, WINDOW, kvw), lambda j: (0, 0, 0)),
        ],
        out_shape=[
            jax.ShapeDtypeStruct((b, l, GROUP_WIDTH), BF16),
            jax.ShapeDtypeStruct((b, WINDOW, kvw), F32),
            jax.ShapeDtypeStruct((b, WINDOW, kvw), F32),
        ],
        scratch_shapes=[pltpu.VMEM((b, WINDOW, kvw), F32), pltpu.VMEM((b, WINDOW, kvw), F32)],
        compiler_params=_params(("arbitrary",)),
        name="attn",
    )(sinks, ud, hk, hv, qg, kg)


def _rwkv_body(*refs, nb, tb, chunk, n_cast):
    u_ref, sp_ref, s0_ref, mu_ref, wl_ref, vec_ref = refs[:6]
    cast_in = refs[6:6 + n_cast]
    y_ref, sn_ref = refs[6 + n_cast:8 + n_cast]
    cast_out = refs[8 + n_cast:8 + 2 * n_cast]
    prev_ref, s_ref, yacc_ref = refs[8 + 2 * n_cast:]
    li = pl.program_id(0)
    for src, dst in zip(cast_in, cast_out):
        dst[...] = src[...].astype(BF16)
    gw = GROUP_WIDTH
    hd = RWKV_HEAD
    nh = RWKV_HEADS
    rows = nb * tb
    nc = tb // chunk

    @pl.when(li == 0)
    def _():
        prev_ref[...] = sp_ref[...]
        zero = jnp.zeros((nb, hd, hd), F32)
        for h in range(nh):
            sh = s0_ref[:, h]
            s_ref[h * nb:(h + 1) * nb] = jnp.concatenate([sh, zero] if h % 2 == 0 else [zero, sh], axis=-1)

    row = lax.broadcasted_iota(jnp.int32, (tb, 1), 0)
    mu = mu_ref[...]
    xs_parts = []
    for b in range(nb):
        ub = u_ref[b]
        prev = jnp.where(row == 0, prev_ref[b], pltpu.roll(ub, 1, axis=0))
        prev_ref[b] = ub[tb - 1:tb, :]
        xs_parts.append(ub + mu * (prev - ub))
    xs = jnp.concatenate(xs_parts, axis=0)
    r = xs[:, 0:gw]
    k = xs[:, gw:2 * gw]
    v = xs[:, 2 * gw:3 * gw]
    lat = xs[:, 3 * gw:B_COLS]
    lane_lat = lax.broadcasted_iota(jnp.int32, (1, B_COLS - 3 * gw), 1)
    act = jnp.where(lane_lat < DECAY_RANK, jnp.tanh(lat),
                    jnp.where(lane_lat < DECAY_RANK + AAA_RANK, lat, _sigmoid(lat)))
    lo = _dot(act.astype(BF16), wl_ref[...])
    w0 = vec_ref[0:1, :]
    a0 = vec_ref[1:2, :]
    k_k = vec_ref[2:3, :]
    k_a = vec_ref[3:4, :]
    r_k = vec_ref[4:5, :]
    gn_g = vec_ref[5:6, :]
    gn_b = vec_ref[6:7, :]
    z = -(w0 + lo[:, 0:gw])
    softplus = jnp.maximum(z, 0.0) + jnp.log(1.0 + jnp.exp(-jnp.abs(z)))
    logw = -jnp.exp(-softplus - 0.5)
    a_rate = _sigmoid(a0 + lo[:, gw:2 * gw])
    gate = lo[:, 2 * gw:3 * gw]
    ones_blk = _block_matrix(gw, hd, 1.0)
    kk = k * k_k
    kk = kk * lax.rsqrt(jnp.maximum(_split_dot_right(kk * kk, ones_blk, 1), 1e-24))
    k_mod = k * (1.0 + (a_rate - 1.0) * k_a)
    b_v = kk * a_rate
    bonus = _split_dot_right(r * k_mod * r_k, ones_blk, 1) * v

    grp = min(rows, 256)
    gi = lax.broadcasted_iota(jnp.int32, (grp, grp), 0)
    gj = lax.broadcasted_iota(jnp.int32, (grp, grp), 1)
    csh = int(math.log2(chunk))
    tri = jnp.where(jnp.logical_and(lax.shift_right_logical(gi, csh) == lax.shift_right_logical(gj, csh),
                                    gj <= gi), 1.0, 0.0).astype(BF16)
    cum = jnp.concatenate([_split_dot_left(tri, logw[g0:g0 + grp], 2) for g0 in range(0, rows, grp)], axis=0)
    cum3 = cum.reshape(nb * nc, chunk, gw)
    cum_c = cum3[:, chunk - 1:chunk, :]
    e_end = jnp.exp(cum_c - cum3).reshape(rows, gw)
    w_c = jnp.exp(cum_c)
    e_neg = jnp.exp(-cum)
    dense = dict(
        a=-kk * jnp.exp(cum - logw),
        r=r * jnp.exp(cum),
        bt=b_v * e_neg,
        kt=k_mod * e_neg,
        bh=b_v * e_end,
        kh=k_mod * e_end,
    )
    slot_hi = lax.broadcasted_iota(jnp.int32, (1, 2 * hd), 1) >= hd

    def head_tile(x, h, own_slot):
        col = x[:, (h // 2) * 2 * hd:(h // 2 + 1) * 2 * hd]
        keep = slot_hi if (h % 2 == 1) == own_slot else jnp.logical_not(slot_hi)
        if x.dtype == BF16:
            return col * jnp.where(keep, 1.0, 0.0).astype(BF16)
        return jnp.where(keep, col, 0.0)

    v_sw = jnp.concatenate([pltpu.roll(v[:, j * 2 * hd:(j + 1) * 2 * hd], hd, axis=1) for j in range(nh // 2)],
                           axis=1)
    tiles = {"r": [head_tile(dense["r"], h, True) for h in range(nh)]}
    dense_b = {name: dense[name].astype(BF16) for name in ("a", "bt", "kt", "bh", "kh")}
    tiles_b = {name: [head_tile(dense_b[name], h, True) for h in range(nh)] for name in ("a", "bh", "kh")}
    for name in ("bt", "kt"):
        tiles_b[name] = [dense_b[name][:, (h // 2) * 2 * hd:(h // 2 + 1) * 2 * hd] for h in range(nh)]
    v_sw_b = v_sw.astype(BF16)
    v_tiles = [head_tile(v_sw_b, h, False) for h in range(nh)]

    def blocks(per_head, c):
        return jnp.stack([per_head[h][b * tb + c * chunk:b * tb + (c + 1) * chunk]
                          for h in range(nh) for b in range(nb)])

    n = nh * nb
    ri = lax.broadcasted_iota(jnp.int32, (2 * chunk, 2 * chunk), 0)
    ci = jnp.bitwise_and(lax.broadcasted_iota(jnp.int32, (2 * chunk, 2 * chunk), 1), chunk - 1)
    gmask = ci < jnp.bitwise_and(ri, chunk - 1) + lax.shift_right_logical(ri, csh)
    zeros_c = jnp.zeros((n, chunk, 2 * hd), BF16)
    n_sq = int(math.log2(chunk))
    for c in range(nc):
        a_b = blocks(tiles_b["a"], c)
        r_f = blocks(tiles["r"], c)
        v_b = blocks(v_tiles, c)
        ar = jnp.concatenate([a_b, r_f.astype(BF16)], axis=1)
        bk = jnp.concatenate([blocks(tiles_b["bt"], c), blocks(tiles_b["kt"], c)], axis=1)
        bhkh = jnp.concatenate([blocks(tiles_b["bh"], c), blocks(tiles_b["kh"], c)], axis=1)
        g = jnp.where(gmask, _bdot_nt(ar, bk), 0.0)
        g_top = g[:, :chunk, :]
        g_bot = g[:, chunk:, :].astype(BF16)
        w = a_b.astype(F32) + _bdot(g_top.astype(BF16), jnp.concatenate([zeros_c, v_b], axis=1))
        p = g_top[:, :, :chunk]
        for i in range(n_sq):
            pb = p.astype(BF16)
            if i + 1 < n_sq:
                res = _bdot(pb, jnp.concatenate([w.astype(BF16), pb], axis=2))
                w = w + res[:, :, :2 * hd]
                p = res[:, :, 2 * hd:]
            else:
                hc = chunk // 2
                low = w[:, hc:, :] + _bdot(pb[:, hc:, :hc], w[:, :hc, :].astype(BF16))
                w = jnp.concatenate([w[:, :hc, :], low], axis=1)
        xv = jnp.concatenate([w.astype(BF16), v_b], axis=1)
        ry = _bdot(g_bot, xv)
        mp = _bdot_tn(xv, bhkh)
        s_old = s_ref[...]
        s_b = s_old.astype(BF16)
        y_nt = _bdot_nt((ry + r_f).astype(BF16), s_b)
        wc = jnp.stack([w_c[b * nc + c][:, (h // 2) * 2 * hd:(h // 2 + 1) * 2 * hd]
                        for h in range(nh) for b in range(nb)])
        psi = jnp.concatenate(
            [mp[h * nb:(h + 1) * nb, (1 - h % 2) * hd:(2 - h % 2) * hd, :] for h in range(nh)], axis=0)
        s_ref[...] = s_old * wc + _bdot(s_b, mp.astype(BF16)) + psi
        for b in range(nb):
            ys = [y_nt[h * nb + b] + ry[h * nb + b][:, (1 - h % 2) * hd:(2 - h % 2) * hd] for h in range(nh)]
            yacc_ref[b * tb + c * chunk:b * tb + (c + 1) * chunk, :] = jnp.concatenate(ys, axis=1)

    y = yacc_ref[...]
    avg_blk = _block_matrix(gw, hd, 1.0 / hd)
    m = _split_dot_right(y, avg_blk, 1)
    d = y - m
    var = _split_dot_right(d * d, avg_blk, 1)
    yn = d * lax.rsqrt(var + GN_EPS) * gn_g + gn_b
    out = ((yn + bonus) * gate).astype(BF16)
    for b in range(nb):
        y_ref[b] = out[b * tb:(b + 1) * tb]

    @pl.when(li == pl.num_programs(0) - 1)
    def _():
        for h in range(nh):
            sn_ref[:, h] = s_ref[h * nb:(h + 1) * nb, :, (h % 2) * hd:(h % 2 + 1) * hd]


def _rwkv(ub, shift_prev, state, layer, mu, wl, vec, chunk, to_cast=()):
    b, l, _ = ub.shape
    tb = min(l, MIXER_ROWS)
    assert l % tb == 0 and tb % chunk == 0
    steps = l // tb
    hd = RWKV_HEAD
    n = RWKV_HEADS * b
    sshape = (b, RWKV_HEADS, hd, hd)
    cast_specs = []
    for wt in to_cast:
        depth, rows, cols = wt.shape
        slab = rows // steps
        assert rows % steps == 0 and slab % (2 * SUBLANES) == 0
        cast_specs.append(pl.BlockSpec((depth, slab, cols), lambda j: (0, j, 0)))
    outs = pl.pallas_call(
        functools.partial(_rwkv_body, nb=b, tb=tb, chunk=chunk, n_cast=len(to_cast)),
        grid=(steps,),
        in_specs=[
            pl.BlockSpec((b, tb, B_COLS), lambda j: (0, j, 0)),
            pl.BlockSpec((b, 1, B_COLS), lambda j: (0, 0, 0)),
            pl.BlockSpec(sshape, lambda j: (0, 0, 0, 0)),
            _layer_block((1, B_COLS), layer, 1),
            _layer_block((128, 3 * GROUP_WIDTH), layer, 1),
            _layer_block((8, GROUP_WIDTH), layer, 1),
        ] + cast_specs,
        out_specs=[
            pl.BlockSpec((b, tb, GROUP_WIDTH), lambda j: (0, j, 0)),
            pl.BlockSpec(sshape, lambda j: (0, 0, 0, 0)),
        ] + cast_specs,
        out_shape=[
            jax.ShapeDtypeStruct((b, l, GROUP_WIDTH), BF16),
            jax.ShapeDtypeStruct(sshape, F32),
        ] + [jax.ShapeDtypeStruct(wt.shape, BF16) for wt in to_cast],
        scratch_shapes=[
            pltpu.VMEM((b, 1, B_COLS), F32),
            pltpu.VMEM((n, hd, 2 * hd), F32),
            pltpu.VMEM((b * tb, GROUP_WIDTH), F32),
        ],
        compiler_params=_params(("arbitrary",)),
        name="rwkv",
    )(ub, shift_prev, state, mu, wl, vec, *to_cast)
    return outs[0], outs[1], tuple(outs[2:])


def _outffn_body(x_ref, ya_ref, yb_ref, yc_ref, yd_ref, wo_ref, g_ref, wg_ref, wu_ref, wd_ref, o_ref):
    ycat = jnp.concatenate([ya_ref[...], yb_ref[...], yc_ref[...], yd_ref[...]], axis=-1)
    x1 = x_ref[...] + _dot(ycat, wo_ref[...])
    ms = jnp.mean(x1 * x1, axis=-1, keepdims=True)
    hn = ((x1 * lax.rsqrt(ms + RMS_EPS)) * g_ref[...]).astype(BF16)
    hg = _dot(hn, wg_ref[...])
    hu = _dot(hn, wu_ref[...])
    act = (hg * _sigmoid(hg) * hu).astype(BF16)
    o_ref[...] = x1 + _dot(act, wd_ref[...])


def _outffn(x2d, ya, yb, yc, yd, layer, wo, g, wg, wu, wd):
    t = x2d.shape[0]
    tm = min(t, MATMUL_ROWS)
    assert t % tm == 0
    row = lambda w: pl.BlockSpec((tm, w), lambda i: (i, 0))
    return pl.pallas_call(
        _outffn_body,
        grid=(t // tm,),
        in_specs=[
            row(D_MODEL), row(GROUP_WIDTH), row(GROUP_WIDTH), row(GROUP_WIDTH), row(GROUP_WIDTH),
            _layer_block((4 * GROUP_WIDTH, D_MODEL), layer, 1),
            _layer_block((1, D_MODEL), layer, 1),
            _layer_block((D_MODEL, D_FF), layer, 1),
            _layer_block((D_MODEL, D_FF), layer, 1),
            _layer_block((D_FF, D_MODEL), layer, 1),
        ],
        out_specs=row(D_MODEL),
        out_shape=jax.ShapeDtypeStruct((t, D_MODEL), F32),
        compiler_params=_params(("parallel",)),
        name="outffn",
    )(x2d, ya, yb, yc, yd, wo, g, wg, wu, wd)


_LATE_WEIGHTS = ('w_out', 'ffn_w_gate', 'ffn_w_up', 'ffn_w_down')


def _rows8(vectors):
    stacked = jnp.stack(vectors, axis=1)
    return jnp.pad(stacked, ((0, 0), (0, SUBLANES - stacked.shape[1]), (0, 0)))


def _stacked_weights(p):
    gw = GROUP_WIDTH
    depth = p['w_in'].shape[0]
    place = lambda a, before, after: jnp.pad(a, ((0, 0), (0, 0), (before, after)))
    wl = jnp.concatenate([place(p['rwkv_w2'], 0, 2 * gw), place(p['rwkv_a2'], gw, gw),
                          place(p['rwkv_g2'], 2 * gw, 0)], axis=1)
    pc = gw // len(POOL_WINDOWS)
    wbd = jnp.concatenate([place(p['pool_w'][:, g], g * pc, gw - (g + 1) * pc)
                           for g in range(len(POOL_WINDOWS))], axis=1)
    return dict(
        norm_mix_g=p['norm_mix_g'][:, None, :],
        w_in=p['w_in'].astype(BF16),
        conv_w=jnp.pad(p['conv_w'], ((0, 0), (0, CONV_PAD - CONV_WIDTH), (0, 0))),
        conv_vec=_rows8([p['conv_b'], p['conv_ln_g'], p['conv_ln_b']]),
        rwkv_mu=p['rwkv_mu'][:, None, :],
        rwkv_wl=wl.astype(BF16),
        rwkv_vec=_rows8([p['rwkv_w0'], p['rwkv_a0'], p['rwkv_k_k'], p['rwkv_k_a'],
                         p['rwkv_r_k'].reshape(depth, gw), p['rwkv_gn_g'], p['rwkv_gn_b']]),
        pool_wbd=wbd.astype(BF16),
        pool_scale=p['pool_scale'][:, None, :],
        attn_qg=jnp.tile(p['attn_q_norm'], (1, N_Q_HEADS))[:, None, :],
        attn_kg=jnp.tile(p['attn_k_norm'], (1, N_KV_HEADS))[:, None, :],
        attn_sinks=p['attn_sinks'],
        norm_ffn_g=p['norm_ffn_g'][:, None, :],
        pending_casts={name: p[name] for name in _LATE_WEIGHTS},
    )


def _trunk(x, conv_hist, rwkv_state, shift_prev, pool_hist, k_cache, v_cache, pos0, chunk, w):
    b, l, _ = x.shape
    has_past = conv_hist is not None
    kvw = N_KV_HEADS * HEAD_DIM
    new = [[] for _ in range(6)]
    for li in range(DEPTH):
        if has_past:
            ch = jnp.pad(conv_hist[li], ((0, 0), (CONV_PAD - CONV_HIST, 0), (0, 0)))
            rs = rwkv_state[li]
            sp = shift_prev[li][:, None, :]
            ph = jnp.pad(pool_hist[li], ((0, 0), (POOL_PAD - POOL_HIST, 0), (0, 0)))
            hk = k_cache[li].reshape(b, WINDOW, kvw)
            hv = v_cache[li].reshape(b, WINDOW, kvw)
        else:
            ch = jnp.zeros((b, CONV_PAD, GROUP_WIDTH), F32)
            rs = jnp.zeros((b, RWKV_HEADS, RWKV_HEAD, RWKV_HEAD), F32)
            sp = jnp.zeros((b, 1, B_COLS), F32)
            ph = jnp.zeros((b, POOL_PAD, GROUP_WIDTH), F32)
            hk = jnp.zeros((b, WINDOW, kvw), F32)
            hv = jnp.zeros((b, WINDOW, kvw), F32)
        ub, ud, ya, yc, conv_tail, pool_tail = _projmix(
            x, li, w['norm_mix_g'], w['w_in'], ch, w['conv_w'], w['conv_vec'],
            ph, w['pool_wbd'], w['pool_scale'], pos0)
        pending = w.pop('pending_casts', {})
        yb, s_new, cast = _rwkv(ub, sp, rs, li, w['rwkv_mu'], w['rwkv_wl'], w['rwkv_vec'], min(CHUNK, l),
                                tuple(pending.values()))
        w.update(zip(pending.keys(), cast))
        yd, k_tail, v_tail = _attn(ud, hk, hv, li, w['attn_qg'], w['attn_kg'], w['attn_sinks'], chunk,
                                   has_past)
        flat = lambda y: y.reshape(b * l, GROUP_WIDTH)
        x = _outffn(x.reshape(b * l, D_MODEL), flat(ya), flat(yb), flat(yc), flat(yd), li, w['w_out'],
                    w['norm_ffn_g'], w['ffn_w_gate'], w['ffn_w_up'], w['ffn_w_down']
                    ).reshape(b, l, D_MODEL)
        new[0].append(conv_tail[:, CONV_PAD - CONV_HIST:, :])
        new[1].append(s_new)
        new[2].append(ub[:, l - 1, :])
        new[3].append(pool_tail[:, POOL_PAD - POOL_HIST:, :])
        new[4].append(k_tail.reshape(b, WINDOW, N_KV_HEADS, HEAD_DIM))
        new[5].append(v_tail.reshape(b, WINDOW, N_KV_HEADS, HEAD_DIM))
    return x, tuple(jnp.stack(n) for n in new)


def kernel(x_prompt, x_sample, cache_conv, state_rwkv, state_rwkv_shift, cache_pool, cache_k, cache_v, norm_mix_g, w_in, conv_w, conv_b, conv_ln_g, conv_ln_b, rwkv_mu, rwkv_w0, rwkv_w2, rwkv_a0, rwkv_a2, rwkv_g2, rwkv_k_k, rwkv_k_a, rwkv_r_k, rwkv_gn_g, rwkv_gn_b, pool_w, pool_scale, attn_q_norm, attn_k_norm, attn_sinks, w_out, norm_ffn_g, ffn_w_gate, ffn_w_up, ffn_w_down):
    w = _stacked_weights(dict(
        norm_mix_g=norm_mix_g, w_in=w_in, conv_w=conv_w, conv_b=conv_b, conv_ln_g=conv_ln_g,
        conv_ln_b=conv_ln_b, rwkv_mu=rwkv_mu, rwkv_w0=rwkv_w0, rwkv_w2=rwkv_w2, rwkv_a0=rwkv_a0,
        rwkv_a2=rwkv_a2, rwkv_g2=rwkv_g2, rwkv_k_k=rwkv_k_k, rwkv_k_a=rwkv_k_a, rwkv_r_k=rwkv_r_k,
        rwkv_gn_g=rwkv_gn_g, rwkv_gn_b=rwkv_gn_b, pool_w=pool_w, pool_scale=pool_scale,
        attn_q_norm=attn_q_norm, attn_k_norm=attn_k_norm, attn_sinks=attn_sinks, w_out=w_out,
        norm_ffn_g=norm_ffn_g, ffn_w_gate=ffn_w_gate, ffn_w_up=ffn_w_up, ffn_w_down=ffn_w_down))
    y_p, (conv_p, rwkv_p, shift_p, pool_p, k_p, v_p) = _trunk(
        x_prompt, None, None, None, None, None, None, 0, CHUNK, w)
    y_s, (conv_s, rwkv_s, shift_s, pool_s, k_s, v_s) = _trunk(
        x_sample, cache_conv, state_rwkv, state_rwkv_shift, cache_pool, cache_k, cache_v,
        PAST_LEN, x_sample.shape[1], w)
    return (y_p, y_s, conv_p, conv_s, rwkv_p, rwkv_s, shift_p, shift_s,
            pool_p, pool_s, k_p, k_s, v_p, v_s)
```

```python
import functools
import math

import jax
import jax.numpy as jnp
from jax import lax
from jax.experimental import pallas as pl
from jax.experimental.pallas import tpu as pltpu

F32 = jnp.float32
BF16 = jnp.bfloat16

D_MODEL = 1024
DEPTH = 2
PAST_LEN = 1024
CHUNK = 64
GROUP_WIDTH = 256
CONV_WIDTH = 31
CONV_HIST = CONV_WIDTH - 1
CONV_PAD = 32
RWKV_HEAD = 64
RWKV_HEADS = 4
POOL_WINDOWS = (2, 4, 8, 16)
POOL_HIST = 15
POOL_PAD = 16
HEAD_DIM = 64
N_Q_HEADS = 4
N_KV_HEADS = 2
WINDOW = 128
D_FF = 2816
A_COLS = 512
B_COLS = 896
C_COLS = 256
D_COLS = 512
IN_COLS = A_COLS + B_COLS + C_COLS + D_COLS
RMS_EPS = 1e-6
LN_EPS = 1e-5
GN_EPS = 64e-5
ATTN_SCALE = HEAD_DIM ** -0.5
NEG_INF = -1e30

DECAY_RANK = 32
AAA_RANK = 32
Q_COLS = N_Q_HEADS * HEAD_DIM
KV_COLS = N_KV_HEADS * HEAD_DIM

VMEM_LIMIT_BYTES = 56 * 1024 * 1024
SUBLANES = 8
MATMUL_ROWS = 512
MIXER_ROWS = 256
LOCAL_ROWS = 128


def _dot(a, b):
    return jnp.dot(a, b, preferred_element_type=F32)


def _bdot(a, b):
    return lax.dot_general(a, b, (((2,), (1,)), ((0,), (0,))), preferred_element_type=F32)


def _bdot_nt(a, b):
    return lax.dot_general(a, b, (((2,), (2,)), ((0,), (0,))), preferred_element_type=F32)


def _bdot_tn(a, b):
    return lax.dot_general(a, b, (((1,), (1,)), ((0,), (0,))), preferred_element_type=F32)


def _sigmoid(x):
    return 1.0 / (1.0 + jnp.exp(-x))


def _split_dot_right(x, m_bf16, terms):
    acc = None
    rem = x
    for i in range(terms):
        hi = rem.astype(BF16)
        d = _dot(hi, m_bf16)
        acc = d if acc is None else acc + d
        if i + 1 < terms:
            rem = rem - hi.astype(F32)
    return acc


def _split_dot_left(m_bf16, x, terms):
    acc = None
    rem = x
    for i in range(terms):
        hi = rem.astype(BF16)
        d = _dot(m_bf16, hi)
        acc = d if acc is None else acc + d
        if i + 1 < terms:
            rem = rem - hi.astype(F32)
    return acc


def _block_matrix(n, blk, value):
    sh = int(math.log2(blk))
    r = lax.shift_right_logical(lax.broadcasted_iota(jnp.int32, (n, n), 0), sh)
    c = lax.shift_right_logical(lax.broadcasted_iota(jnp.int32, (n, n), 1), sh)
    return jnp.where(r == c, value, 0.0).astype(BF16)


def _params(sem):
    return pltpu.CompilerParams(dimension_semantics=sem, vmem_limit_bytes=VMEM_LIMIT_BYTES)


def _layer_block(shape, layer, n_grid):
    zeros = (0,) * len(shape)
    if n_grid == 1:
        index_map = lambda i: (layer,) + zeros
    else:
        index_map = lambda i, j: (layer,) + zeros
    return pl.BlockSpec((None,) + tuple(shape), index_map, pipeline_mode=pl.Buffered(1))


def _phase_copies(ext_ref, sh_ref):
    n = sh_ref.shape[1]
    for s in range(1, SUBLANES):
        sh_ref[s - 1] = ext_ref[s:s + n, :]


def _shifted_rows(ext_ref, sh_ref, start, rb):
    a, s = divmod(start, SUBLANES)
    base = a * SUBLANES
    return ext_ref[base:base + rb, :] if s == 0 else sh_ref[s - 1, base:base + rb, :]


def _conv_rows(ext_ref, sh_ref, w_ref, vec_ref, r0, rb):
    off = CONV_PAD - CONV_HIST
    acc = jnp.zeros((rb, GROUP_WIDTH), F32)
    for j in range(CONV_WIDTH):
        acc = acc + w_ref[j:j + 1, :] * _shifted_rows(ext_ref, sh_ref, r0 + off + j, rb)
    acc = acc + vec_ref[0:1, :]
    mu = jnp.mean(acc, axis=-1, keepdims=True)
    d = acc - mu
    var = jnp.mean(d * d, axis=-1, keepdims=True)
    yn = d * lax.rsqrt(var + LN_EPS) * vec_ref[1:2, :] + vec_ref[2:3, :]
    return yn * _sigmoid(yn)


def _pool_rows(ext_ref, sh_ref, wbd_ref, sc_ref, r0, rb, pos_start):
    base = POOL_PAD + r0
    sums = []
    acc = None
    for i in range(max(POOL_WINDOWS)):
        sh = _shifted_rows(ext_ref, sh_ref, base - i, rb)
        acc = sh if acc is None else acc + sh
        if i + 1 in POOL_WINDOWS:
            sums.append(acc)
    pos = pos_start + r0 + lax.broadcasted_iota(jnp.int32, (rb, 1), 0)
    means = [s / jnp.minimum(w, pos + 1).astype(F32) for s, w in zip(sums, POOL_WINDOWS)]
    lane = lax.broadcasted_iota(jnp.int32, (1, GROUP_WIDTH), 1)
    pc = GROUP_WIDTH // len(POOL_WINDOWS)
    mean = means[-1]
    for g in range(len(POOL_WINDOWS) - 2, -1, -1):
        mean = jnp.where(lane < (g + 1) * pc, means[g], mean)
    d = mean - ext_ref[base:base + rb, :]
    return _dot(d.astype(BF16), wbd_ref[...]) * sc_ref[...]


def _projmix_body(*refs, nb, tl, rb, pos0, has_past):
    x_ref, g_ref, w_ref, cw_ref, cvec_ref, pwbd_ref, psc_ref = refs[:7]
    n_in = 9 if has_past else 7
    (ub_ref, ud_ref, ya_ref, yc_ref, ctail_ref, ptail_ref,
     cext_ref, csh_ref, pext_ref, psh_ref) = refs[n_in:]
    li = pl.program_id(1)

    @pl.when(li == 0)
    def _():
        if has_past:
            cext_ref[:, 0:CONV_PAD, :] = refs[7][...]
            pext_ref[:, 0:POOL_PAD, :] = refs[8][...]
        else:
            cext_ref[:, 0:CONV_PAD, :] = jnp.zeros((nb, CONV_PAD, GROUP_WIDTH), F32)
            pext_ref[:, 0:POOL_PAD, :] = jnp.zeros((nb, POOL_PAD, GROUP_WIDTH), F32)

    x = x_ref[...].reshape(nb * tl, D_MODEL)
    ms = jnp.mean(x * x, axis=-1, keepdims=True)
    xn = ((x * lax.rsqrt(ms + RMS_EPS)) * g_ref[...]).astype(BF16)
    b0 = A_COLS
    c0 = A_COLS + B_COLS
    d0 = c0 + C_COLS
    ua = _dot(xn, w_ref[:, 0:A_COLS])
    uc = _dot(xn, w_ref[:, c0:d0])
    ub_ref[...] = _dot(xn, w_ref[:, b0:c0]).reshape(nb, tl, B_COLS)
    ud_ref[...] = _dot(xn, w_ref[:, d0:IN_COLS]).reshape(nb, tl, D_COLS)
    glu = ua[:, :GROUP_WIDTH] * _sigmoid(ua[:, GROUP_WIDTH:])
    for b in range(nb):
        cext = cext_ref.at[b]
        pext = pext_ref.at[b]
        cext[CONV_PAD:CONV_PAD + tl, :] = glu[b * tl:(b + 1) * tl]
        pext[POOL_PAD:POOL_PAD + tl, :] = uc[b * tl:(b + 1) * tl]
        _phase_copies(cext, csh_ref)
        _phase_copies(pext, psh_ref)
        for r0 in range(0, tl, rb):
            ya_ref[b, r0:r0 + rb, :] = _conv_rows(cext, csh_ref, cw_ref, cvec_ref, r0, rb).astype(BF16)
            yc_ref[b, r0:r0 + rb, :] = _pool_rows(pext, psh_ref, pwbd_ref, psc_ref, r0, rb,
                                                  pos0 + li * tl).astype(BF16)
        ctail = cext[tl:tl + CONV_PAD, :]
        ptail = pext[tl:tl + POOL_PAD, :]
        ctail_ref[b] = ctail
        ptail_ref[b] = ptail
        cext[0:CONV_PAD, :] = ctail
        pext[0:POOL_PAD, :] = ptail


def _projmix(x, layer, g, w_in, conv_w, conv_vec, pool_wbd, pool_scale, pos0, hists=()):
    b, l, _ = x.shape
    nb = b if l <= 64 else 1
    tl = min(l, MATMUL_ROWS)
    rb = min(tl, LOCAL_ROWS)
    assert l % tl == 0 and tl % rb == 0 and tl >= CONV_PAD and b % nb == 0
    gw = GROUP_WIDTH
    tile = lambda w: pl.BlockSpec((nb, tl, w), lambda i, j: (i, j, 0))
    per_seq = lambda r: pl.BlockSpec((nb, r, gw), lambda i, j: (i, 0, 0))
    return pl.pallas_call(
        functools.partial(_projmix_body, nb=nb, tl=tl, rb=rb, pos0=pos0, has_past=bool(hists)),
        grid=(b // nb, l // tl),
        in_specs=[
            tile(D_MODEL),
            _layer_block((1, D_MODEL), layer, 2),
            _layer_block((D_MODEL, IN_COLS), layer, 2),
            _layer_block((CONV_PAD, gw), layer, 2), _layer_block((8, gw), layer, 2),
            _layer_block((gw, gw), layer, 2), _layer_block((1, gw), layer, 2),
        ] + ([per_seq(CONV_PAD), per_seq(POOL_PAD)] if hists else []),
        out_specs=[tile(B_COLS), tile(D_COLS), tile(gw), tile(gw), per_seq(CONV_PAD), per_seq(POOL_PAD)],
        out_shape=[
            jax.ShapeDtypeStruct((b, l, B_COLS), F32),
            jax.ShapeDtypeStruct((b, l, D_COLS), F32),
            jax.ShapeDtypeStruct((b, l, gw), BF16),
            jax.ShapeDtypeStruct((b, l, gw), BF16),
            jax.ShapeDtypeStruct((b, CONV_PAD, gw), F32),
            jax.ShapeDtypeStruct((b, POOL_PAD, gw), F32),
        ],
        scratch_shapes=[
            pltpu.VMEM((nb, CONV_PAD + tl, gw), F32),
            pltpu.VMEM((SUBLANES - 1, CONV_PAD + tl - SUBLANES, gw), F32),
            pltpu.VMEM((nb, POOL_PAD + tl, gw), F32),
            pltpu.VMEM((SUBLANES - 1, POOL_PAD + tl - SUBLANES, gw), F32),
        ],
        compiler_params=_params(("parallel", "arbitrary")),
        name="projmix",
    )(x, g, w_in, conv_w, conv_vec, pool_wbd, pool_scale, *hists)


def _attn_body(*refs, layer, nb, tq, chunk, hist_valid):
    sink_ref, u_ref, qg_ref, kg_ref = refs[:4]
    n_in = 6 if hist_valid else 4
    y_ref, kt_ref, vt_ref, kh_ref, vh_ref = refs[n_in:]
    li = pl.program_id(0)
    hd = HEAD_DIM
    rows = nb * tq
    ncq = tq // chunk
    kw = WINDOW + chunk

    @pl.when(li == 0)
    def _():
        if hist_valid:
            kh_ref[...] = refs[4][...]
            vh_ref[...] = refs[5][...]
        else:
            kh_ref[...] = jnp.zeros(kh_ref.shape, F32)
            vh_ref[...] = jnp.zeros(vh_ref.shape, F32)

    u = u_ref[...].reshape(rows, D_COLS)
    q = u[:, 0:Q_COLS]
    k = u[:, Q_COLS:Q_COLS + KV_COLS]
    v = u[:, Q_COLS + KV_COLS:D_COLS]
    inv = 1.0 / hd
    qms = _split_dot_right(q * q, _block_matrix(Q_COLS, hd, inv), 1)
    kms = _split_dot_right(k * k, _block_matrix(KV_COLS, hd, inv), 2)
    qn = (q * lax.rsqrt(qms + RMS_EPS)) * (qg_ref[...] * ATTN_SCALE)
    kn = (k * lax.rsqrt(kms + RMS_EPS)) * kg_ref[...]
    kcat = jnp.concatenate([kh_ref[...], kn.reshape(nb, tq, 2 * hd)], axis=1)
    vcat = jnp.concatenate([vh_ref[...], v.reshape(nb, tq, 2 * hd)], axis=1)
    ktail = kcat[:, tq:tq + WINDOW, :]
    vtail = vcat[:, tq:tq + WINDOW, :]
    kt_ref[...] = ktail
    vt_ref[...] = vtail
    kh_ref[...] = ktail
    vh_ref[...] = vtail
    kcat_b = kcat.astype(BF16)
    vcat_b = vcat.astype(BF16)

    slot_hi = lax.broadcasted_iota(jnp.int32, (1, 2 * hd), 1) >= hd
    q_tiles = []
    for h in range(N_Q_HEADS):
        col = qn[:, (h // 2) * 2 * hd:(h // 2 + 1) * 2 * hd]
        g = h // (N_Q_HEADS // N_KV_HEADS)
        if h % 2 != g:
            col = pltpu.roll(col, hd, axis=1)
        q_tiles.append(jnp.where(slot_hi if g == 1 else jnp.logical_not(slot_hi), col, 0.0).astype(BF16))
    pairs = [(b, c) for c in range(ncq) for b in range(nb)]
    qs = jnp.stack([jnp.concatenate([t[b * tq + c * chunk:b * tq + (c + 1) * chunk] for t in q_tiles], axis=0)
                    for b, c in pairs])
    ks = jnp.stack([kcat_b[b, c * chunk:c * chunk + kw] for b, c in pairs])
    vs = jnp.stack([vcat_b[b, c * chunk:c * chunk + kw] for b, c in pairs])
    st = _bdot_nt(ks, qs)
    nq = N_Q_HEADS * chunk
    if not hist_valid:
        n_edge = min(ncq, WINDOW // chunk) * nb
        cpos = [c * chunk for _, c in pairs[:n_edge]]
        kpos = lax.broadcasted_iota(jnp.int32, (1, kw, nq), 1) + (li * tq - WINDOW)
        edge = jnp.concatenate([jnp.where(kpos + cp >= 0, st[i:i + 1], NEG_INF) for i, cp in enumerate(cpos)],
                               axis=0)
        st = jnp.concatenate([edge, st[n_edge:]], axis=0) if n_edge < len(pairs) else edge
    hlane = lax.broadcasted_iota(jnp.int32, (1, 1, nq), 2)
    sk = jnp.full((1, 1, nq), sink_ref[layer, N_Q_HEADS - 1], F32)
    for h in range(N_Q_HEADS - 2, -1, -1):
        sk = jnp.where(hlane < (h + 1) * chunk, sink_ref[layer, h], sk)
    m = jnp.maximum(jnp.max(st, axis=1, keepdims=True), sk)
    p = jnp.exp(st - m)
    den = jnp.sum(p, axis=1, keepdims=True) + jnp.exp(sk - m)
    pn = (p * (1.0 / den)).astype(BF16)
    o = _bdot_tn(pn, vs)
    lo = jnp.logical_not(slot_hi)
    for i, (b, c) in enumerate(pairs):
        oc = o[i]
        col0 = jnp.where(lo, oc[0:chunk], pltpu.roll(oc[chunk:2 * chunk], hd, axis=1))
        col1 = jnp.where(lo, pltpu.roll(oc[2 * chunk:3 * chunk], hd, axis=1), oc[3 * chunk:4 * chunk])
        y_ref[b, c * chunk:(c + 1) * chunk, :] = jnp.concatenate([col0, col1], axis=1).astype(BF16)


def _attn(ud, layer, qg, kg, sinks, chunk, caches=()):
    b, l, _ = ud.shape
    tq = min(l, MIXER_ROWS)
    assert l % tq == 0 and tq % chunk == 0
    kvw = N_KV_HEADS * HEAD_DIM
    cache_spec = pl.BlockSpec((b, WINDOW, kvw), lambda j: (0, 0, 0))
    return pl.pallas_call(
        functools.partial(_attn_body, layer=layer, nb=b, tq=tq, chunk=chunk, hist_valid=bool(caches)),
        grid=(l // tq,),
        in_specs=[
            pl.BlockSpec(memory_space=pltpu.SMEM),
            pl.BlockSpec((b, tq, D_COLS), lambda j: (0, j, 0)),
            _layer_block((1, GROUP_WIDTH), layer, 1),
            _layer_block((1, kvw), layer, 1),
        ] + [cache_spec for _ in caches],
        out_specs=[
            pl.BlockSpec((b, tq, GROUP_WIDTH), lambda j: (0, j, 0)),
            pl.BlockSpec((b, WINDOW, kvw), lambda j: (0, 0, 0)),
            pl.BlockSpec((b, WINDOW, kvw), lambda j: (0, 0, 0)),
        ],
        out_shape=[
            jax.ShapeDtypeStruct((b, l, GROUP_WIDTH), BF16),
            jax.ShapeDtypeStruct((b, WINDOW, kvw), F32),
            jax.ShapeDtypeStruct((b, WINDOW, kvw), F32),
        ],
        scratch_shapes=[pltpu.VMEM((b, WINDOW, kvw), F32), pltpu.VMEM((b, WINDOW, kvw), F32)],
        compiler_params=_params(("arbitrary",)),
        name="attn",
    )(sinks, ud, qg, kg, *caches)


def _rwkv_body(*refs, nb, tb, chunk, n_cast, has_past):
    u_ref, mu_ref, wl_ref, vec_ref = refs[:4]
    n_in = 6 if has_past else 4
    cast_in = refs[n_in:n_in + n_cast]
    y_ref, sn_ref = refs[n_in + n_cast:n_in + n_cast + 2]
    cast_out = refs[n_in + n_cast + 2:n_in + 2 * n_cast + 2]
    prev_ref, s_ref, yacc_ref = refs[n_in + 2 * n_cast + 2:]
    li = pl.program_id(0)
    for src, dst in zip(cast_in, cast_out):
        dst[...] = src[...].astype(BF16)
    gw = GROUP_WIDTH
    hd = RWKV_HEAD
    nh = RWKV_HEADS
    rows = nb * tb
    nc = tb // chunk

    @pl.when(li == 0)
    def _():
        if not has_past:
            prev_ref[...] = jnp.zeros(prev_ref.shape, F32)
            s_ref[...] = jnp.zeros(s_ref.shape, F32)
            return
        sp_ref, s0_ref = refs[4:6]
        prev_ref[...] = sp_ref[...]
        zero = jnp.zeros((nb, hd, hd), F32)
        for h in range(nh):
            sh = s0_ref[:, h]
            s_ref[h * nb:(h + 1) * nb] = jnp.concatenate([sh, zero] if h % 2 == 0 else [zero, sh], axis=-1)

    row = lax.broadcasted_iota(jnp.int32, (tb, 1), 0)
    mu = mu_ref[...]
    xs_parts = []
    for b in range(nb):
        ub = u_ref[b]
        prev = jnp.where(row == 0, prev_ref[b], pltpu.roll(ub, 1, axis=0))
        prev_ref[b] = ub[tb - 1:tb, :]
        xs_parts.append(ub + mu * (prev - ub))
    xs = jnp.concatenate(xs_parts, axis=0)
    r = xs[:, 0:gw]
    k = xs[:, gw:2 * gw]
    v = xs[:, 2 * gw:3 * gw]
    lat = xs[:, 3 * gw:B_COLS]
    lane_lat = lax.broadcasted_iota(jnp.int32, (1, B_COLS - 3 * gw), 1)
    act = jnp.where(lane_lat < DECAY_RANK, jnp.tanh(lat),
                    jnp.where(lane_lat < DECAY_RANK + AAA_RANK, lat, _sigmoid(lat)))
    lo = _dot(act.astype(BF16), wl_ref[...])
    w0 = vec_ref[0:1, :]
    a0 = vec_ref[1:2, :]
    k_k = vec_ref[2:3, :]
    k_a = vec_ref[3:4, :]
    r_k = vec_ref[4:5, :]
    gn_g = vec_ref[5:6, :]
    gn_b = vec_ref[6:7, :]
    z = -(w0 + lo[:, 0:gw])
    softplus = jnp.maximum(z, 0.0) + jnp.log(1.0 + jnp.exp(-jnp.abs(z)))
    logw = -jnp.exp(-softplus - 0.5)
    a_rate = _sigmoid(a0 + lo[:, gw:2 * gw])
    gate = lo[:, 2 * gw:3 * gw]
    ones_blk = _block_matrix(gw, hd, 1.0)
    kk = k * k_k
    kk = kk * lax.rsqrt(jnp.maximum(_split_dot_right(kk * kk, ones_blk, 1), 1e-24))
    k_mod = k * (1.0 + (a_rate - 1.0) * k_a)
    b_v = kk * a_rate
    bonus = _split_dot_right(r * k_mod * r_k, ones_blk, 1) * v

    grp = min(rows, 256)
    gi = lax.broadcasted_iota(jnp.int32, (grp, grp), 0)
    gj = lax.broadcasted_iota(jnp.int32, (grp, grp), 1)
    csh = int(math.log2(chunk))
    tri = jnp.where(jnp.logical_and(lax.shift_right_logical(gi, csh) == lax.shift_right_logical(gj, csh),
                                    gj <= gi), 1.0, 0.0).astype(BF16)
    cum = jnp.concatenate([_split_dot_left(tri, logw[g0:g0 + grp], 2) for g0 in range(0, rows, grp)], axis=0)
    cum3 = cum.reshape(nb * nc, chunk, gw)
    cum_c = cum3[:, chunk - 1:chunk, :]
    e_end = jnp.exp(cum_c - cum3).reshape(rows, gw)
    w_c = jnp.exp(cum_c)
    e_neg = jnp.exp(-cum)
    dense = dict(
        a=-kk * jnp.exp(cum - logw),
        r=r * jnp.exp(cum),
        bt=b_v * e_neg,
        kt=k_mod * e_neg,
        bh=b_v * e_end,
        kh=k_mod * e_end,
    )
    slot_hi = lax.broadcasted_iota(jnp.int32, (1, 2 * hd), 1) >= hd

    def head_tile(x, h, own_slot):
        col = x[:, (h // 2) * 2 * hd:(h // 2 + 1) * 2 * hd]
        keep = slot_hi if (h % 2 == 1) == own_slot else jnp.logical_not(slot_hi)
        if x.dtype == BF16:
            return col * jnp.where(keep, 1.0, 0.0).astype(BF16)
        return jnp.where(keep, col, 0.0)

    v_sw = jnp.concatenate([pltpu.roll(v[:, j * 2 * hd:(j + 1) * 2 * hd], hd, axis=1) for j in range(nh // 2)],
                           axis=1)
    tiles = {"r": [head_tile(dense["r"], h, True) for h in range(nh)]}
    dense_b = {name: dense[name].astype(BF16) for name in ("a", "bt", "kt", "bh", "kh")}
    tiles_b = {name: [head_tile(dense_b[name], h, True) for h in range(nh)] for name in ("a", "bh", "kh")}
    for name in ("bt", "kt"):
        tiles_b[name] = [dense_b[name][:, (h // 2) * 2 * hd:(h // 2 + 1) * 2 * hd] for h in range(nh)]
    v_sw_b = v_sw.astype(BF16)
    v_tiles = [head_tile(v_sw_b, h, False) for h in range(nh)]

    def blocks(per_head, c):
        return jnp.stack([per_head[h][b * tb + c * chunk:b * tb + (c + 1) * chunk]
                          for h in range(nh) for b in range(nb)])

    n = nh * nb
    ri = lax.broadcasted_iota(jnp.int32, (2 * chunk, 2 * chunk), 0)
    ci = jnp.bitwise_and(lax.broadcasted_iota(jnp.int32, (2 * chunk, 2 * chunk), 1), chunk - 1)
    gmask = ci < jnp.bitwise_and(ri, chunk - 1) + lax.shift_right_logical(ri, csh)
    zeros_c = jnp.zeros((n, chunk, 2 * hd), BF16)
    n_sq = int(math.log2(chunk))
    for c in range(nc):
        a_b = blocks(tiles_b["a"], c)
        r_f = blocks(tiles["r"], c)
        v_b = blocks(v_tiles, c)
        ar = jnp.concatenate([a_b, r_f.astype(BF16)], axis=1)
        bk = jnp.concatenate([blocks(tiles_b["bt"], c), blocks(tiles_b["kt"], c)], axis=1)
        bhkh = jnp.concatenate([blocks(tiles_b["bh"], c), blocks(tiles_b["kh"], c)], axis=1)
        g = jnp.where(gmask, _bdot_nt(ar, bk), 0.0)
        g_top = g[:, :chunk, :]
        g_bot = g[:, chunk:, :].astype(BF16)
        w = a_b.astype(F32) + _bdot(g_top.astype(BF16), jnp.concatenate([zeros_c, v_b], axis=1))
        p = g_top[:, :, :chunk]
        for i in range(n_sq):
            pb = p.astype(BF16)
            if i + 1 < n_sq:
                res = _bdot(pb, jnp.concatenate([w.astype(BF16), pb], axis=2))
                w = w + res[:, :, :2 * hd]
                p = res[:, :, 2 * hd:]
            else:
                hc = chunk // 2
                low = w[:, hc:, :] + _bdot(pb[:, hc:, :hc], w[:, :hc, :].astype(BF16))
                w = jnp.concatenate([w[:, :hc, :], low], axis=1)
        xv = jnp.concatenate([w.astype(BF16), v_b], axis=1)
        ry = _bdot(g_bot, xv)
        mp = _bdot_tn(xv, bhkh)
        s_old = s_ref[...]
        s_b = s_old.astype(BF16)
        y_nt = _bdot_nt((ry + r_f).astype(BF16), s_b)
        wc = jnp.stack([w_c[b * nc + c][:, (h // 2) * 2 * hd:(h // 2 + 1) * 2 * hd]
                        for h in range(nh) for b in range(nb)])
        psi = jnp.concatenate(
            [mp[h * nb:(h + 1) * nb, (1 - h % 2) * hd:(2 - h % 2) * hd, :] for h in range(nh)], axis=0)
        s_ref[...] = s_old * wc + _bdot(s_b, mp.astype(BF16)) + psi
        for b in range(nb):
            ys = [y_nt[h * nb + b] + ry[h * nb + b][:, (1 - h % 2) * hd:(2 - h % 2) * hd] for h in range(nh)]
            yacc_ref[b * tb + c * chunk:b * tb + (c + 1) * chunk, :] = jnp.concatenate(ys, axis=1)

    y = yacc_ref[...]
    avg_blk = _block_matrix(gw, hd, 1.0 / hd)
    m = _split_dot_right(y, avg_blk, 1)
    d = y - m
    var = _split_dot_right(d * d, avg_blk, 1)
    yn = d * lax.rsqrt(var + GN_EPS) * gn_g + gn_b
    out = ((yn + bonus) * gate).astype(BF16)
    for b in range(nb):
        y_ref[b] = out[b * tb:(b + 1) * tb]

    @pl.when(li == pl.num_programs(0) - 1)
    def _():
        for h in range(nh):
            sn_ref[:, h] = s_ref[h * nb:(h + 1) * nb, :, (h % 2) * hd:(h % 2 + 1) * hd]


def _rwkv(ub, layer, mu, wl, vec, chunk, past=(), to_cast=()):
    b, l, _ = ub.shape
    tb = min(l, MIXER_ROWS)
    assert l % tb == 0 and tb % chunk == 0
    steps = l // tb
    hd = RWKV_HEAD
    n = RWKV_HEADS * b
    sshape = (b, RWKV_HEADS, hd, hd)
    cast_specs = []
    for wt in to_cast:
        depth, rows, cols = wt.shape
        slab = rows // steps
        assert rows % steps == 0 and slab % (2 * SUBLANES) == 0
        cast_specs.append(pl.BlockSpec((depth, slab, cols), lambda j: (0, j, 0)))
    outs = pl.pallas_call(
        functools.partial(_rwkv_body, nb=b, tb=tb, chunk=chunk, n_cast=len(to_cast), has_past=bool(past)),
        grid=(steps,),
        in_specs=[
            pl.BlockSpec((b, tb, B_COLS), lambda j: (0, j, 0)),
            _layer_block((1, B_COLS), layer, 1),
            _layer_block((128, 3 * GROUP_WIDTH), layer, 1),
            _layer_block((8, GROUP_WIDTH), layer, 1),
        ] + ([pl.BlockSpec((b, 1, B_COLS), lambda j: (0, 0, 0)),
              pl.BlockSpec(sshape, lambda j: (0, 0, 0, 0))] if past else []) + cast_specs,
        out_specs=[
            pl.BlockSpec((b, tb, GROUP_WIDTH), lambda j: (0, j, 0)),
            pl.BlockSpec(sshape, lambda j: (0, 0, 0, 0)),
        ] + cast_specs,
        out_shape=[
            jax.ShapeDtypeStruct((b, l, GROUP_WIDTH), BF16),
            jax.ShapeDtypeStruct(sshape, F32),
        ] + [jax.ShapeDtypeStruct(wt.shape, BF16) for wt in to_cast],
        scratch_shapes=[
            pltpu.VMEM((b, 1, B_COLS), F32),
            pltpu.VMEM((n, hd, 2 * hd), F32),
            pltpu.VMEM((b * tb, GROUP_WIDTH), F32),
        ],
        compiler_params=_params(("arbitrary",)),
        name="rwkv",
    )(ub, mu, wl, vec, *past, *to_cast)
    return outs[0], outs[1], tuple(outs[2:])


def _outffn_body(x_ref, ya_ref, yb_ref, yc_ref, yd_ref, wo_ref, g_ref, wg_ref, wu_ref, wd_ref, o_ref):
    ycat = jnp.concatenate([ya_ref[...], yb_ref[...], yc_ref[...], yd_ref[...]], axis=-1)
    x1 = x_ref[...] + _dot(ycat, wo_ref[...])
    ms = jnp.mean(x1 * x1, axis=-1, keepdims=True)
    hn = ((x1 * lax.rsqrt(ms + RMS_EPS)) * g_ref[...]).astype(BF16)
    hg = _dot(hn, wg_ref[...])
    hu = _dot(hn, wu_ref[...])
    act = (hg * _sigmoid(hg) * hu).astype(BF16)
    o_ref[...] = x1 + _dot(act, wd_ref[...])


def _outffn(x2d, ya, yb, yc, yd, layer, wo, g, wg, wu, wd):
    t = x2d.shape[0]
    tm = min(t, MATMUL_ROWS)
    assert t % tm == 0
    row = lambda w: pl.BlockSpec((tm, w), lambda i: (i, 0))
    return pl.pallas_call(
        _outffn_body,
        grid=(t // tm,),
        in_specs=[
            row(D_MODEL), row(GROUP_WIDTH), row(GROUP_WIDTH), row(GROUP_WIDTH), row(GROUP_WIDTH),
            _layer_block((4 * GROUP_WIDTH, D_MODEL), layer, 1),
            _layer_block((1, D_MODEL), layer, 1),
            _layer_block((D_MODEL, D_FF), layer, 1),
            _layer_block((D_MODEL, D_FF), layer, 1),
            _layer_block((D_FF, D_MODEL), layer, 1),
        ],
        out_specs=row(D_MODEL),
        out_shape=jax.ShapeDtypeStruct((t, D_MODEL), F32),
        compiler_params=_params(("parallel",)),
        name="outffn",
    )(x2d, ya, yb, yc, yd, wo, g, wg, wu, wd)


_LATE_WEIGHTS = ('w_out', 'ffn_w_gate', 'ffn_w_up', 'ffn_w_down')


def _rows8(vectors):
    stacked = jnp.stack(vectors, axis=1)
    return jnp.pad(stacked, ((0, 0), (0, SUBLANES - stacked.shape[1]), (0, 0)))


def _stacked_weights(p):
    gw = GROUP_WIDTH
    depth = p['w_in'].shape[0]
    place = lambda a, before, after: jnp.pad(a, ((0, 0), (0, 0), (before, after)))
    wl = jnp.concatenate([place(p['rwkv_w2'], 0, 2 * gw), place(p['rwkv_a2'], gw, gw),
                          place(p['rwkv_g2'], 2 * gw, 0)], axis=1)
    pc = gw // len(POOL_WINDOWS)
    wbd = jnp.concatenate([place(p['pool_w'][:, g], g * pc, gw - (g + 1) * pc)
                           for g in range(len(POOL_WINDOWS))], axis=1)
    return dict(
        norm_mix_g=p['norm_mix_g'][:, None, :],
        w_in=p['w_in'].astype(BF16),
        conv_w=jnp.pad(p['conv_w'], ((0, 0), (0, CONV_PAD - CONV_WIDTH), (0, 0))),
        conv_vec=_rows8([p['conv_b'], p['conv_ln_g'], p['conv_ln_b']]),
        rwkv_mu=p['rwkv_mu'][:, None, :],
        rwkv_wl=wl.astype(BF16),
        rwkv_vec=_rows8([p['rwkv_w0'], p['rwkv_a0'], p['rwkv_k_k'], p['rwkv_k_a'],
                         p['rwkv_r_k'].reshape(depth, gw), p['rwkv_gn_g'], p['rwkv_gn_b']]),
        pool_wbd=wbd.astype(BF16),
        pool_scale=p['pool_scale'][:, None, :],
        attn_qg=jnp.tile(p['attn_q_norm'], (1, N_Q_HEADS))[:, None, :],
        attn_kg=jnp.tile(p['attn_k_norm'], (1, N_KV_HEADS))[:, None, :],
        attn_sinks=p['attn_sinks'],
        norm_ffn_g=p['norm_ffn_g'][:, None, :],
        pending_casts={name: p[name] for name in _LATE_WEIGHTS},
    )


def _trunk(x, conv_hist, rwkv_state, shift_prev, pool_hist, k_cache, v_cache, pos0, chunk, w):
    b, l, _ = x.shape
    has_past = conv_hist is not None
    kvw = N_KV_HEADS * HEAD_DIM
    new = [[] for _ in range(6)]
    for li in range(DEPTH):
        local_hists = rwkv_past = kv_caches = ()
        if has_past:
            local_hists = (jnp.pad(conv_hist[li], ((0, 0), (CONV_PAD - CONV_HIST, 0), (0, 0))),
                           jnp.pad(pool_hist[li], ((0, 0), (POOL_PAD - POOL_HIST, 0), (0, 0))))
            rwkv_past = (shift_prev[li][:, None, :], rwkv_state[li])
            kv_caches = (k_cache[li].reshape(b, WINDOW, kvw), v_cache[li].reshape(b, WINDOW, kvw))
        ub, ud, ya, yc, conv_tail, pool_tail = _projmix(
            x, li, w['norm_mix_g'], w['w_in'], w['conv_w'], w['conv_vec'],
            w['pool_wbd'], w['pool_scale'], pos0, local_hists)
        pending = w.pop('pending_casts', {})
        yb, s_new, cast = _rwkv(ub, li, w['rwkv_mu'], w['rwkv_wl'], w['rwkv_vec'], min(CHUNK, l),
                                rwkv_past, tuple(pending.values()))
        w.update(zip(pending.keys(), cast))
        yd, k_tail, v_tail = _attn(ud, li, w['attn_qg'], w['attn_kg'], w['attn_sinks'], chunk, kv_caches)
        flat = lambda y: y.reshape(b * l, GROUP_WIDTH)
        x = _outffn(x.reshape(b * l, D_MODEL), flat(ya), flat(yb), flat(yc), flat(yd), li, w['w_out'],
                    w['norm_ffn_g'], w['ffn_w_gate'], w['ffn_w_up'], w['ffn_w_down']
                    ).reshape(b, l, D_MODEL)
        new[0].append(conv_tail[:, CONV_PAD - CONV_HIST:, :])
        new[1].append(s_new)
        new[2].append(ub[:, l - 1, :])
        new[3].append(pool_tail[:, POOL_PAD - POOL_HIST:, :])
        new[4].append(k_tail.reshape(b, WINDOW, N_KV_HEADS, HEAD_DIM))
        new[5].append(v_tail.reshape(b, WINDOW, N_KV_HEADS, HEAD_DIM))
    return x, tuple(jnp.stack(n) for n in new)


def kernel(x_prompt, x_sample, cache_conv, state_rwkv, state_rwkv_shift, cache_pool, cache_k, cache_v, norm_mix_g, w_in, conv_w, conv_b, conv_ln_g, conv_ln_b, rwkv_mu, rwkv_w0, rwkv_w2, rwkv_a0, rwkv_a2, rwkv_g2, rwkv_k_k, rwkv_k_a, rwkv_r_k, rwkv_gn_g, rwkv_gn_b, pool_w, pool_scale, attn_q_norm, attn_k_norm, attn_sinks, w_out, norm_ffn_g, ffn_w_gate, ffn_w_up, ffn_w_down):
    w = _stacked_weights(dict(
        norm_mix_g=norm_mix_g, w_in=w_in, conv_w=conv_w, conv_b=conv_b, conv_ln_g=conv_ln_g,
        conv_ln_b=conv_ln_b, rwkv_mu=rwkv_mu, rwkv_w0=rwkv_w0, rwkv_w2=rwkv_w2, rwkv_a0=rwkv_a0,
        rwkv_a2=rwkv_a2, rwkv_g2=rwkv_g2, rwkv_k_k=rwkv_k_k, rwkv_k_a=rwkv_k_a, rwkv_r_k=rwkv_r_k,
        rwkv_gn_g=rwkv_gn_g, rwkv_gn_b=rwkv_gn_b, pool_w=pool_w, pool_scale=pool_scale,
        attn_q_norm=attn_q_norm, attn_k_norm=attn_k_norm, attn_sinks=attn_sinks, w_out=w_out,
        norm_ffn_g=norm_ffn_g, ffn_w_gate=ffn_w_gate, ffn_w_up=ffn_w_up, ffn_w_down=ffn_w_down))
    y_p, (conv_p, rwkv_p, shift_p, pool_p, k_p, v_p) = _trunk(
        x_prompt, None, None, None, None, None, None, 0, CHUNK, w)
    y_s, (conv_s, rwkv_s, shift_s, pool_s, k_s, v_s) = _trunk(
        x_sample, cache_conv, state_rwkv, state_rwkv_shift, cache_pool, cache_k, cache_v,
        PAST_LEN, x_sample.shape[1], w)
    return (y_p, y_s, conv_p, conv_s, rwkv_p, rwkv_s, shift_p, shift_s,
            pool_p, pool_s, k_p, k_s, v_p, v_s)
```

```python
import functools
import math

import jax
import jax.numpy as jnp
from jax import lax
from jax.experimental import pallas as pl
from jax.experimental.pallas import tpu as pltpu

F32 = jnp.float32
BF16 = jnp.bfloat16

D_MODEL = 1024
DEPTH = 2
PAST_LEN = 1024
CHUNK = 64
GROUP_WIDTH = 256
CONV_WIDTH = 31
CONV_HIST = CONV_WIDTH - 1
CONV_PAD = 32
RWKV_HEAD = 64
RWKV_HEADS = 4
POOL_WINDOWS = (2, 4, 8, 16)
POOL_HIST = 15
POOL_PAD = 16
HEAD_DIM = 64
N_Q_HEADS = 4
N_KV_HEADS = 2
WINDOW = 128
D_FF = 2816
A_COLS = 512
B_COLS = 896
C_COLS = 256
D_COLS = 512
IN_COLS = A_COLS + B_COLS + C_COLS + D_COLS
RMS_EPS = 1e-6
LN_EPS = 1e-5
GN_EPS = 64e-5
ATTN_SCALE = HEAD_DIM ** -0.5
NEG_INF = -1e30

DECAY_RANK = 32
AAA_RANK = 32
Q_COLS = N_Q_HEADS * HEAD_DIM
KV_COLS = N_KV_HEADS * HEAD_DIM

VMEM_LIMIT_BYTES = 56 * 1024 * 1024
SUBLANES = 8
MATMUL_ROWS = 512
MIXER_ROWS = 256
LOCAL_ROWS = 128


def _dot(a, b):
    return jnp.dot(a, b, preferred_element_type=F32)


def _bdot(a, b):
    return lax.dot_general(a, b, (((2,), (1,)), ((0,), (0,))), preferred_element_type=F32)


def _bdot_nt(a, b):
    return lax.dot_general(a, b, (((2,), (2,)), ((0,), (0,))), preferred_element_type=F32)


def _bdot_tn(a, b):
    return lax.dot_general(a, b, (((1,), (1,)), ((0,), (0,))), preferred_element_type=F32)


def _sigmoid(x):
    return 1.0 / (1.0 + jnp.exp(-x))


def _split_dot_right(x, m_bf16, terms):
    acc = None
    rem = x
    for i in range(terms):
        hi = rem.astype(BF16)
        d = _dot(hi, m_bf16)
        acc = d if acc is None else acc + d
        if i + 1 < terms:
            rem = rem - hi.astype(F32)
    return acc


def _split_dot_left(m_bf16, x, terms):
    acc = None
    rem = x
    for i in range(terms):
        hi = rem.astype(BF16)
        d = _dot(m_bf16, hi)
        acc = d if acc is None else acc + d
        if i + 1 < terms:
            rem = rem - hi.astype(F32)
    return acc


def _block_matrix(n, blk, value):
    sh = int(math.log2(blk))
    r = lax.shift_right_logical(lax.broadcasted_iota(jnp.int32, (n, n), 0), sh)
    c = lax.shift_right_logical(lax.broadcasted_iota(jnp.int32, (n, n), 1), sh)
    return jnp.where(r == c, value, 0.0).astype(BF16)


def _params(sem):
    return pltpu.CompilerParams(dimension_semantics=sem, vmem_limit_bytes=VMEM_LIMIT_BYTES)


def _layer_block(shape, layer, n_grid):
    zeros = (0,) * len(shape)
    if n_grid == 1:
        index_map = lambda i: (layer,) + zeros
    else:
        index_map = lambda i, j: (layer,) + zeros
    return pl.BlockSpec((None,) + tuple(shape), index_map, pipeline_mode=pl.Buffered(1))


def _phase_copies(ext_ref, sh_ref):
    n = sh_ref.shape[1]
    for s in range(1, SUBLANES):
        sh_ref[s - 1] = ext_ref[s:s + n, :]


def _shifted_rows(ext_ref, sh_ref, start, rb):
    a, s = divmod(start, SUBLANES)
    base = a * SUBLANES
    return ext_ref[base:base + rb, :] if s == 0 else sh_ref[s - 1, base:base + rb, :]


def _conv_rows(ext_ref, sh_ref, w_ref, vec_ref, r0, rb):
    off = CONV_PAD - CONV_HIST
    acc = jnp.zeros((rb, GROUP_WIDTH), F32)
    for j in range(CONV_WIDTH):
        acc = acc + w_ref[j:j + 1, :] * _shifted_rows(ext_ref, sh_ref, r0 + off + j, rb)
    acc = acc + vec_ref[0:1, :]
    mu = jnp.mean(acc, axis=-1, keepdims=True)
    d = acc - mu
    var = jnp.mean(d * d, axis=-1, keepdims=True)
    yn = d * lax.rsqrt(var + LN_EPS) * vec_ref[1:2, :] + vec_ref[2:3, :]
    return yn * _sigmoid(yn)


def _pool_rows(ext_ref, sh_ref, wbd_ref, sc_ref, r0, rb, pos_start):
    base = POOL_PAD + r0
    sums = []
    acc = None
    for i in range(max(POOL_WINDOWS)):
        sh = _shifted_rows(ext_ref, sh_ref, base - i, rb)
        acc = sh if acc is None else acc + sh
        if i + 1 in POOL_WINDOWS:
            sums.append(acc)
    pos = pos_start + r0 + lax.broadcasted_iota(jnp.int32, (rb, 1), 0)
    means = [s / jnp.minimum(w, pos + 1).astype(F32) for s, w in zip(sums, POOL_WINDOWS)]
    lane = lax.broadcasted_iota(jnp.int32, (1, GROUP_WIDTH), 1)
    pc = GROUP_WIDTH // len(POOL_WINDOWS)
    mean = means[-1]
    for g in range(len(POOL_WINDOWS) - 2, -1, -1):
        mean = jnp.where(lane < (g + 1) * pc, means[g], mean)
    d = mean - ext_ref[base:base + rb, :]
    return _dot(d.astype(BF16), wbd_ref[...]) * sc_ref[...]


def _projmix_body(*refs, nb, tl, rb, pos0, has_past):
    x_ref, g_ref, w_ref, cw_ref, cvec_ref, pwbd_ref, psc_ref = refs[:7]
    n_in = 9 if has_past else 7
    (ub_ref, ud_ref, ya_ref, yc_ref, ctail_ref, ptail_ref,
     cext_ref, csh_ref, pext_ref, psh_ref) = refs[n_in:]
    li = pl.program_id(1)

    @pl.when(li == 0)
    def _():
        if has_past:
            cext_ref[:, 0:CONV_PAD, :] = refs[7][...]
            pext_ref[:, 0:POOL_PAD, :] = refs[8][...]
        else:
            cext_ref[:, 0:CONV_PAD, :] = jnp.zeros((nb, CONV_PAD, GROUP_WIDTH), F32)
            pext_ref[:, 0:POOL_PAD, :] = jnp.zeros((nb, POOL_PAD, GROUP_WIDTH), F32)

    x = x_ref[...].reshape(nb * tl, D_MODEL)
    ms = jnp.mean(x * x, axis=-1, keepdims=True)
    xn = ((x * lax.rsqrt(ms + RMS_EPS)) * g_ref[...]).astype(BF16)
    b0 = A_COLS
    c0 = A_COLS + B_COLS
    d0 = c0 + C_COLS
    ua = _dot(xn, w_ref[:, 0:A_COLS])
    uc = _dot(xn, w_ref[:, c0:d0])
    ub_ref[...] = _dot(xn, w_ref[:, b0:c0]).reshape(nb, tl, B_COLS)
    ud_ref[...] = _dot(xn, w_ref[:, d0:IN_COLS]).reshape(nb, tl, D_COLS)
    glu = ua[:, :GROUP_WIDTH] * _sigmoid(ua[:, GROUP_WIDTH:])
    for b in range(nb):
        cext = cext_ref.at[b]
        pext = pext_ref.at[b]
        cext[CONV_PAD:CONV_PAD + tl, :] = glu[b * tl:(b + 1) * tl]
        pext[POOL_PAD:POOL_PAD + tl, :] = uc[b * tl:(b + 1) * tl]
        _phase_copies(cext, csh_ref)
        _phase_copies(pext, psh_ref)
        for r0 in range(0, tl, rb):
            ya_ref[b, r0:r0 + rb, :] = _conv_rows(cext, csh_ref, cw_ref, cvec_ref, r0, rb).astype(BF16)
            yc_ref[b, r0:r0 + rb, :] = _pool_rows(pext, psh_ref, pwbd_ref, psc_ref, r0, rb,
                                                  pos0 + li * tl).astype(BF16)
        ctail = cext[tl:tl + CONV_PAD, :]
        ptail = pext[tl:tl + POOL_PAD, :]
        ctail_ref[b] = ctail
        ptail_ref[b] = ptail
        cext[0:CONV_PAD, :] = ctail
        pext[0:POOL_PAD, :] = ptail


def _projmix(x, layer, g, w_in, conv_w, conv_vec, pool_wbd, pool_scale, pos0, hists=()):
    b, l, _ = x.shape
    nb = b if l <= 64 else 1
    tl = min(l, MATMUL_ROWS)
    rb = min(tl, LOCAL_ROWS)
    assert l % tl == 0 and tl % rb == 0 and tl >= CONV_PAD and b % nb == 0
    gw = GROUP_WIDTH
    tile = lambda w: pl.BlockSpec((nb, tl, w), lambda i, j: (i, j, 0))
    per_seq = lambda r: pl.BlockSpec((nb, r, gw), lambda i, j: (i, 0, 0))
    return pl.pallas_call(
        functools.partial(_projmix_body, nb=nb, tl=tl, rb=rb, pos0=pos0, has_past=bool(hists)),
        grid=(b // nb, l // tl),
        in_specs=[
            tile(D_MODEL),
            _layer_block((1, D_MODEL), layer, 2),
            _layer_block((D_MODEL, IN_COLS), layer, 2),
            _layer_block((CONV_PAD, gw), layer, 2), _layer_block((8, gw), layer, 2),
            _layer_block((gw, gw), layer, 2), _layer_block((1, gw), layer, 2),
        ] + ([per_seq(CONV_PAD), per_seq(POOL_PAD)] if hists else []),
        out_specs=[tile(B_COLS), tile(D_COLS), tile(gw), tile(gw), per_seq(CONV_PAD), per_seq(POOL_PAD)],
        out_shape=[
            jax.ShapeDtypeStruct((b, l, B_COLS), F32),
            jax.ShapeDtypeStruct((b, l, D_COLS), F32),
            jax.ShapeDtypeStruct((b, l, gw), BF16),
            jax.ShapeDtypeStruct((b, l, gw), BF16),
            jax.ShapeDtypeStruct((b, CONV_PAD, gw), F32),
            jax.ShapeDtypeStruct((b, POOL_PAD, gw), F32),
        ],
        scratch_shapes=[
            pltpu.VMEM((nb, CONV_PAD + tl, gw), F32),
            pltpu.VMEM((SUBLANES - 1, CONV_PAD + tl - SUBLANES, gw), F32),
            pltpu.VMEM((nb, POOL_PAD + tl, gw), F32),
            pltpu.VMEM((SUBLANES - 1, POOL_PAD + tl - SUBLANES, gw), F32),
        ],
        compiler_params=_params(("parallel", "arbitrary")),
        name="projmix",
    )(x, g, w_in, conv_w, conv_vec, pool_wbd, pool_scale, *hists)


def _attn_body(*refs, layer, nb, tq, chunk, hist_valid, n_alias):
    sink_ref, u_ref, qg_ref, kg_ref = refs[:4]
    n_in = (6 if hist_valid else 4) + n_alias
    y_ref, kt_ref, vt_ref, kh_ref, vh_ref = refs[n_in:]
    li = pl.program_id(0)
    hd = HEAD_DIM
    rows = nb * tq
    ncq = tq // chunk
    kw = WINDOW + chunk

    @pl.when(li == 0)
    def _():
        if hist_valid:
            kh_ref[...] = refs[4][...]
            vh_ref[...] = refs[5][...]
        else:
            kh_ref[...] = jnp.zeros(kh_ref.shape, F32)
            vh_ref[...] = jnp.zeros(vh_ref.shape, F32)

    u = u_ref[...].reshape(rows, D_COLS)
    q = u[:, 0:Q_COLS]
    k = u[:, Q_COLS:Q_COLS + KV_COLS]
    v = u[:, Q_COLS + KV_COLS:D_COLS]
    inv = 1.0 / hd
    qms = _split_dot_right(q * q, _block_matrix(Q_COLS, hd, inv), 1)
    kms = _split_dot_right(k * k, _block_matrix(KV_COLS, hd, inv), 2)
    qn = (q * lax.rsqrt(qms + RMS_EPS)) * (qg_ref[...] * ATTN_SCALE)
    kn = (k * lax.rsqrt(kms + RMS_EPS)) * kg_ref[...]
    kcat = jnp.concatenate([kh_ref[...], kn.reshape(nb, tq, 2 * hd)], axis=1)
    vcat = jnp.concatenate([vh_ref[...], v.reshape(nb, tq, 2 * hd)], axis=1)
    ktail = kcat[:, tq:tq + WINDOW, :]
    vtail = vcat[:, tq:tq + WINDOW, :]
    if n_alias:
        kt_ref[...] = ktail
        vt_ref[...] = vtail
    else:
        for lyr in range(DEPTH):
            kt_ref[lyr] = ktail if lyr == layer else jnp.zeros(ktail.shape, F32)
            vt_ref[lyr] = vtail if lyr == layer else jnp.zeros(vtail.shape, F32)
    kh_ref[...] = ktail
    vh_ref[...] = vtail
    kcat_b = kcat.astype(BF16)
    vcat_b = vcat.astype(BF16)

    slot_hi = lax.broadcasted_iota(jnp.int32, (1, 2 * hd), 1) >= hd
    q_tiles = []
    for h in range(N_Q_HEADS):
        col = qn[:, (h // 2) * 2 * hd:(h // 2 + 1) * 2 * hd]
        g = h // (N_Q_HEADS // N_KV_HEADS)
        if h % 2 != g:
            col = pltpu.roll(col, hd, axis=1)
        q_tiles.append(jnp.where(slot_hi if g == 1 else jnp.logical_not(slot_hi), col, 0.0).astype(BF16))
    pairs = [(b, c) for c in range(ncq) for b in range(nb)]
    qs = jnp.stack([jnp.concatenate([t[b * tq + c * chunk:b * tq + (c + 1) * chunk] for t in q_tiles], axis=0)
                    for b, c in pairs])
    ks = jnp.stack([kcat_b[b, c * chunk:c * chunk + kw] for b, c in pairs])
    vs = jnp.stack([vcat_b[b, c * chunk:c * chunk + kw] for b, c in pairs])
    st = _bdot_nt(ks, qs)
    nq = N_Q_HEADS * chunk
    if not hist_valid:
        n_edge = min(ncq, WINDOW // chunk) * nb
        cpos = [c * chunk for _, c in pairs[:n_edge]]
        kpos = lax.broadcasted_iota(jnp.int32, (1, kw, nq), 1) + (li * tq - WINDOW)
        edge = jnp.concatenate([jnp.where(kpos + cp >= 0, st[i:i + 1], NEG_INF) for i, cp in enumerate(cpos)],
                               axis=0)
        st = jnp.concatenate([edge, st[n_edge:]], axis=0) if n_edge < len(pairs) else edge
    hlane = lax.broadcasted_iota(jnp.int32, (1, 1, nq), 2)
    sk = jnp.full((1, 1, nq), sink_ref[layer, N_Q_HEADS - 1], F32)
    for h in range(N_Q_HEADS - 2, -1, -1):
        sk = jnp.where(hlane < (h + 1) * chunk, sink_ref[layer, h], sk)
    m = jnp.maximum(jnp.max(st, axis=1, keepdims=True), sk)
    p = jnp.exp(st - m)
    den = jnp.sum(p, axis=1, keepdims=True) + jnp.exp(sk - m)
    pn = (p * (1.0 / den)).astype(BF16)
    o = _bdot_tn(pn, vs)
    lo = jnp.logical_not(slot_hi)
    for i, (b, c) in enumerate(pairs):
        oc = o[i]
        col0 = jnp.where(lo, oc[0:chunk], pltpu.roll(oc[chunk:2 * chunk], hd, axis=1))
        col1 = jnp.where(lo, pltpu.roll(oc[2 * chunk:3 * chunk], hd, axis=1), oc[3 * chunk:4 * chunk])
        y_ref[b, c * chunk:(c + 1) * chunk, :] = jnp.concatenate([col0, col1], axis=1).astype(BF16)


def _attn(ud, layer, qg, kg, sinks, chunk, caches=(), tails=()):
    b, l, _ = ud.shape
    tq = min(l, MIXER_ROWS)
    assert l % tq == 0 and tq % chunk == 0
    kvw = N_KV_HEADS * HEAD_DIM
    layer_spec = pl.BlockSpec((None, b, WINDOW, kvw), lambda j: (layer, 0, 0, 0))
    n_fixed = 4 + len(caches)
    return pl.pallas_call(
        functools.partial(_attn_body, layer=layer, nb=b, tq=tq, chunk=chunk, hist_valid=bool(caches),
                          n_alias=len(tails)),
        grid=(l // tq,),
        in_specs=[
            pl.BlockSpec(memory_space=pltpu.SMEM),
            pl.BlockSpec((b, tq, D_COLS), lambda j: (0, j, 0)),
            _layer_block((1, GROUP_WIDTH), layer, 1),
            _layer_block((1, kvw), layer, 1),
        ] + [layer_spec for _ in caches] + [pl.BlockSpec(memory_space=pl.ANY) for _ in tails],
        out_specs=[pl.BlockSpec((b, tq, GROUP_WIDTH), lambda j: (0, j, 0))] + 2 * [
            layer_spec if tails else pl.BlockSpec((DEPTH, b, WINDOW, kvw), lambda j: (0, 0, 0, 0))],
        out_shape=[
            jax.ShapeDtypeStruct((b, l, GROUP_WIDTH), BF16),
            jax.ShapeDtypeStruct((DEPTH, b, WINDOW, kvw), F32),
            jax.ShapeDtypeStruct((DEPTH, b, WINDOW, kvw), F32),
        ],
        input_output_aliases={n_fixed + i: 1 + i for i in range(len(tails))},
        scratch_shapes=[pltpu.VMEM((b, WINDOW, kvw), F32), pltpu.VMEM((b, WINDOW, kvw), F32)],
        compiler_params=_params(("arbitrary",)),
        name="attn",
    )(sinks, ud, qg, kg, *caches, *tails)


def _rwkv_body(*refs, nb, tb, chunk, n_cast, has_past):
    u_ref, mu_ref, wl_ref, vec_ref = refs[:4]
    n_in = 6 if has_past else 4
    cast_in = refs[n_in:n_in + n_cast]
    y_ref, sn_ref = refs[n_in + n_cast:n_in + n_cast + 2]
    cast_out = refs[n_in + n_cast + 2:n_in + 2 * n_cast + 2]
    prev_ref, s_ref, yacc_ref = refs[n_in + 2 * n_cast + 2:]
    li = pl.program_id(0)
    for src, dst in zip(cast_in, cast_out):
        dst[...] = src[...].astype(BF16)
    gw = GROUP_WIDTH
    hd = RWKV_HEAD
    nh = RWKV_HEADS
    rows = nb * tb
    nc = tb // chunk

    @pl.when(li == 0)
    def _():
        if not has_past:
            prev_ref[...] = jnp.zeros(prev_ref.shape, F32)
            s_ref[...] = jnp.zeros(s_ref.shape, F32)
            return
        sp_ref, s0_ref = refs[4:6]
        prev_ref[...] = sp_ref[...]
        zero = jnp.zeros((nb, hd, hd), F32)
        for h in range(nh):
            sh = s0_ref[:, h]
            s_ref[h * nb:(h + 1) * nb] = jnp.concatenate([sh, zero] if h % 2 == 0 else [zero, sh], axis=-1)

    row = lax.broadcasted_iota(jnp.int32, (tb, 1), 0)
    mu = mu_ref[...]
    xs_parts = []
    for b in range(nb):
        ub = u_ref[b]
        prev = jnp.where(row == 0, prev_ref[b], pltpu.roll(ub, 1, axis=0))
        prev_ref[b] = ub[tb - 1:tb, :]
        xs_parts.append(ub + mu * (prev - ub))
    xs = jnp.concatenate(xs_parts, axis=0)
    r = xs[:, 0:gw]
    k = xs[:, gw:2 * gw]
    v = xs[:, 2 * gw:3 * gw]
    lat = xs[:, 3 * gw:B_COLS]
    lane_lat = lax.broadcasted_iota(jnp.int32, (1, B_COLS - 3 * gw), 1)
    act = jnp.where(lane_lat < DECAY_RANK, jnp.tanh(lat),
                    jnp.where(lane_lat < DECAY_RANK + AAA_RANK, lat, _sigmoid(lat)))
    lo = _dot(act.astype(BF16), wl_ref[...])
    w0 = vec_ref[0:1, :]
    a0 = vec_ref[1:2, :]
    k_k = vec_ref[2:3, :]
    k_a = vec_ref[3:4, :]
    r_k = vec_ref[4:5, :]
    gn_g = vec_ref[5:6, :]
    gn_b = vec_ref[6:7, :]
    z = -(w0 + lo[:, 0:gw])
    softplus = jnp.maximum(z, 0.0) + jnp.log(1.0 + jnp.exp(-jnp.abs(z)))
    logw = -jnp.exp(-softplus - 0.5)
    a_rate = _sigmoid(a0 + lo[:, gw:2 * gw])
    gate = lo[:, 2 * gw:3 * gw]
    ones_blk = _block_matrix(gw, hd, 1.0)
    kk = k * k_k
    kk = kk * lax.rsqrt(jnp.maximum(_split_dot_right(kk * kk, ones_blk, 1), 1e-24))
    k_mod = k * (1.0 + (a_rate - 1.0) * k_a)
    b_v = kk * a_rate
    bonus = _split_dot_right(r * k_mod * r_k, ones_blk, 1) * v

    grp = min(rows, 256)
    gi = lax.broadcasted_iota(jnp.int32, (grp, grp), 0)
    gj = lax.broadcasted_iota(jnp.int32, (grp, grp), 1)
    csh = int(math.log2(chunk))
    tri = jnp.where(jnp.logical_and(lax.shift_right_logical(gi, csh) == lax.shift_right_logical(gj, csh),
                                    gj <= gi), 1.0, 0.0).astype(BF16)
    cum = jnp.concatenate([_split_dot_left(tri, logw[g0:g0 + grp], 2) for g0 in range(0, rows, grp)], axis=0)
    cum3 = cum.reshape(nb * nc, chunk, gw)
    cum_c = cum3[:, chunk - 1:chunk, :]
    e_end = jnp.exp(cum_c - cum3).reshape(rows, gw)
    w_c = jnp.exp(cum_c)
    e_neg = jnp.exp(-cum)
    dense = dict(
        a=-kk * jnp.exp(cum - logw),
        r=r * jnp.exp(cum),
        bt=b_v * e_neg,
        kt=k_mod * e_neg,
        bh=b_v * e_end,
        kh=k_mod * e_end,
    )
    slot_hi = lax.broadcasted_iota(jnp.int32, (1, 2 * hd), 1) >= hd

    def head_tile(x, h, own_slot):
        col = x[:, (h // 2) * 2 * hd:(h // 2 + 1) * 2 * hd]
        keep = slot_hi if (h % 2 == 1) == own_slot else jnp.logical_not(slot_hi)
        if x.dtype == BF16:
            return col * jnp.where(keep, 1.0, 0.0).astype(BF16)
        return jnp.where(keep, col, 0.0)

    v_sw = jnp.concatenate([pltpu.roll(v[:, j * 2 * hd:(j + 1) * 2 * hd], hd, axis=1) for j in range(nh // 2)],
                           axis=1)
    tiles = {"r": [head_tile(dense["r"], h, True) for h in range(nh)]}
    dense_b = {name: dense[name].astype(BF16) for name in ("a", "bt", "kt", "bh", "kh")}
    tiles_b = {name: [head_tile(dense_b[name], h, True) for h in range(nh)] for name in ("a", "bh", "kh")}
    for name in ("bt", "kt"):
        tiles_b[name] = [dense_b[name][:, (h // 2) * 2 * hd:(h // 2 + 1) * 2 * hd] for h in range(nh)]
    v_sw_b = v_sw.astype(BF16)
    v_tiles = [head_tile(v_sw_b, h, False) for h in range(nh)]

    def blocks(per_head, c):
        return jnp.stack([per_head[h][b * tb + c * chunk:b * tb + (c + 1) * chunk]
                          for h in range(nh) for b in range(nb)])

    n = nh * nb
    ri = lax.broadcasted_iota(jnp.int32, (2 * chunk, 2 * chunk), 0)
    ci = jnp.bitwise_and(lax.broadcasted_iota(jnp.int32, (2 * chunk, 2 * chunk), 1), chunk - 1)
    gmask = ci < jnp.bitwise_and(ri, chunk - 1) + lax.shift_right_logical(ri, csh)
    zeros_c = jnp.zeros((n, chunk, 2 * hd), BF16)
    n_sq = int(math.log2(chunk))
    for c in range(nc):
        a_b = blocks(tiles_b["a"], c)
        r_f = blocks(tiles["r"], c)
        v_b = blocks(v_tiles, c)
        ar = jnp.concatenate([a_b, r_f.astype(BF16)], axis=1)
        bk = jnp.concatenate([blocks(tiles_b["bt"], c), blocks(tiles_b["kt"], c)], axis=1)
        bhkh = jnp.concatenate([blocks(tiles_b["bh"], c), blocks(tiles_b["kh"], c)], axis=1)
        g = jnp.where(gmask, _bdot_nt(ar, bk), 0.0)
        g_top = g[:, :chunk, :]
        g_bot = g[:, chunk:, :].astype(BF16)
        w = a_b.astype(F32) + _bdot(g_top.astype(BF16), jnp.concatenate([zeros_c, v_b], axis=1))
        p = g_top[:, :, :chunk]
        for i in range(n_sq):
            pb = p.astype(BF16)
            if i + 1 < n_sq:
                res = _bdot(pb, jnp.concatenate([w.astype(BF16), pb], axis=2))
                w = w + res[:, :, :2 * hd]
                p = res[:, :, 2 * hd:]
            else:
                hc = chunk // 2
                low = w[:, hc:, :] + _bdot(pb[:, hc:, :hc], w[:, :hc, :].astype(BF16))
                w = jnp.concatenate([w[:, :hc, :], low], axis=1)
        xv = jnp.concatenate([w.astype(BF16), v_b], axis=1)
        ry = _bdot(g_bot, xv)
        mp = _bdot_tn(xv, bhkh)
        s_old = s_ref[...]
        s_b = s_old.astype(BF16)
        y_nt = _bdot_nt((ry + r_f).astype(BF16), s_b)
        wc = jnp.stack([w_c[b * nc + c][:, (h // 2) * 2 * hd:(h // 2 + 1) * 2 * hd]
                        for h in range(nh) for b in range(nb)])
        psi = jnp.concatenate(
            [mp[h * nb:(h + 1) * nb, (1 - h % 2) * hd:(2 - h % 2) * hd, :] for h in range(nh)], axis=0)
        s_ref[...] = s_old * wc + _bdot(s_b, mp.astype(BF16)) + psi
        for b in range(nb):
            ys = [y_nt[h * nb + b] + ry[h * nb + b][:, (1 - h % 2) * hd:(2 - h % 2) * hd] for h in range(nh)]
            yacc_ref[b * tb + c * chunk:b * tb + (c + 1) * chunk, :] = jnp.concatenate(ys, axis=1)

    y = yacc_ref[...]
    avg_blk = _block_matrix(gw, hd, 1.0 / hd)
    m = _split_dot_right(y, avg_blk, 1)
    d = y - m
    var = _split_dot_right(d * d, avg_blk, 1)
    yn = d * lax.rsqrt(var + GN_EPS) * gn_g + gn_b
    out = ((yn + bonus) * gate).astype(BF16)
    for b in range(nb):
        y_ref[b] = out[b * tb:(b + 1) * tb]

    @pl.when(li == pl.num_programs(0) - 1)
    def _():
        for h in range(nh):
            sn_ref[:, h] = s_ref[h * nb:(h + 1) * nb, :, (h % 2) * hd:(h % 2 + 1) * hd]


def _rwkv(ub, layer, mu, wl, vec, chunk, past=(), to_cast=()):
    b, l, _ = ub.shape
    tb = min(l, MIXER_ROWS)
    assert l % tb == 0 and tb % chunk == 0
    steps = l // tb
    hd = RWKV_HEAD
    n = RWKV_HEADS * b
    sshape = (b, RWKV_HEADS, hd, hd)
    cast_specs = []
    for wt in to_cast:
        depth, rows, cols = wt.shape
        slab = rows // steps
        assert rows % steps == 0 and slab % (2 * SUBLANES) == 0
        cast_specs.append(pl.BlockSpec((depth, slab, cols), lambda j: (0, j, 0)))
    outs = pl.pallas_call(
        functools.partial(_rwkv_body, nb=b, tb=tb, chunk=chunk, n_cast=len(to_cast), has_past=bool(past)),
        grid=(steps,),
        in_specs=[
            pl.BlockSpec((b, tb, B_COLS), lambda j: (0, j, 0)),
            _layer_block((1, B_COLS), layer, 1),
            _layer_block((128, 3 * GROUP_WIDTH), layer, 1),
            _layer_block((8, GROUP_WIDTH), layer, 1),
        ] + ([pl.BlockSpec((b, 1, B_COLS), lambda j: (0, 0, 0)),
              pl.BlockSpec(sshape, lambda j: (0, 0, 0, 0))] if past else []) + cast_specs,
        out_specs=[
            pl.BlockSpec((b, tb, GROUP_WIDTH), lambda j: (0, j, 0)),
            pl.BlockSpec(sshape, lambda j: (0, 0, 0, 0)),
        ] + cast_specs,
        out_shape=[
            jax.ShapeDtypeStruct((b, l, GROUP_WIDTH), BF16),
            jax.ShapeDtypeStruct(sshape, F32),
        ] + [jax.ShapeDtypeStruct(wt.shape, BF16) for wt in to_cast],
        scratch_shapes=[
            pltpu.VMEM((b, 1, B_COLS), F32),
            pltpu.VMEM((n, hd, 2 * hd), F32),
            pltpu.VMEM((b * tb, GROUP_WIDTH), F32),
        ],
        compiler_params=_params(("arbitrary",)),
        name="rwkv",
    )(ub, mu, wl, vec, *past, *to_cast)
    return outs[0], outs[1], tuple(outs[2:])


def _outffn_body(x_ref, ya_ref, yb_ref, yc_ref, yd_ref, wo_ref, g_ref, wg_ref, wu_ref, wd_ref, o_ref):
    ycat = jnp.concatenate([ya_ref[...], yb_ref[...], yc_ref[...], yd_ref[...]], axis=-1)
    x1 = x_ref[...] + _dot(ycat, wo_ref[...])
    ms = jnp.mean(x1 * x1, axis=-1, keepdims=True)
    hn = ((x1 * lax.rsqrt(ms + RMS_EPS)) * g_ref[...]).astype(BF16)
    hg = _dot(hn, wg_ref[...])
    hu = _dot(hn, wu_ref[...])
    act = (hg * _sigmoid(hg) * hu).astype(BF16)
    o_ref[...] = x1 + _dot(act, wd_ref[...])


def _outffn(x2d, ya, yb, yc, yd, layer, wo, g, wg, wu, wd):
    t = x2d.shape[0]
    tm = min(t, MATMUL_ROWS)
    assert t % tm == 0
    row = lambda w: pl.BlockSpec((tm, w), lambda i: (i, 0))
    return pl.pallas_call(
        _outffn_body,
        grid=(t // tm,),
        in_specs=[
            row(D_MODEL), row(GROUP_WIDTH), row(GROUP_WIDTH), row(GROUP_WIDTH), row(GROUP_WIDTH),
            _layer_block((4 * GROUP_WIDTH, D_MODEL), layer, 1),
            _layer_block((1, D_MODEL), layer, 1),
            _layer_block((D_MODEL, D_FF), layer, 1),
            _layer_block((D_MODEL, D_FF), layer, 1),
            _layer_block((D_FF, D_MODEL), layer, 1),
        ],
        out_specs=row(D_MODEL),
        out_shape=jax.ShapeDtypeStruct((t, D_MODEL), F32),
        compiler_params=_params(("parallel",)),
        name="outffn",
    )(x2d, ya, yb, yc, yd, wo, g, wg, wu, wd)


_LATE_WEIGHTS = ('w_out', 'ffn_w_gate', 'ffn_w_up', 'ffn_w_down')


def _rows8(vectors):
    stacked = jnp.stack(vectors, axis=1)
    return jnp.pad(stacked, ((0, 0), (0, SUBLANES - stacked.shape[1]), (0, 0)))


def _stacked_weights(p):
    gw = GROUP_WIDTH
    depth = p['w_in'].shape[0]
    place = lambda a, before, after: jnp.pad(a, ((0, 0), (0, 0), (before, after)))
    wl = jnp.concatenate([place(p['rwkv_w2'], 0, 2 * gw), place(p['rwkv_a2'], gw, gw),
                          place(p['rwkv_g2'], 2 * gw, 0)], axis=1)
    pc = gw // len(POOL_WINDOWS)
    wbd = jnp.concatenate([place(p['pool_w'][:, g], g * pc, gw - (g + 1) * pc)
                           for g in range(len(POOL_WINDOWS))], axis=1)
    return dict(
        norm_mix_g=p['norm_mix_g'][:, None, :],
        w_in=p['w_in'].astype(BF16),
        conv_w=jnp.pad(p['conv_w'], ((0, 0), (0, CONV_PAD - CONV_WIDTH), (0, 0))),
        conv_vec=_rows8([p['conv_b'], p['conv_ln_g'], p['conv_ln_b']]),
        rwkv_mu=p['rwkv_mu'][:, None, :],
        rwkv_wl=wl.astype(BF16),
        rwkv_vec=_rows8([p['rwkv_w0'], p['rwkv_a0'], p['rwkv_k_k'], p['rwkv_k_a'],
                         p['rwkv_r_k'].reshape(depth, gw), p['rwkv_gn_g'], p['rwkv_gn_b']]),
        pool_wbd=wbd.astype(BF16),
        pool_scale=p['pool_scale'][:, None, :],
        attn_qg=jnp.tile(p['attn_q_norm'], (1, N_Q_HEADS))[:, None, :],
        attn_kg=jnp.tile(p['attn_k_norm'], (1, N_KV_HEADS))[:, None, :],
        attn_sinks=p['attn_sinks'],
        norm_ffn_g=p['norm_ffn_g'][:, None, :],
        pending_casts={name: p[name] for name in _LATE_WEIGHTS},
    )


def _trunk(x, conv_hist, rwkv_state, shift_prev, pool_hist, k_cache, v_cache, pos0, chunk, w):
    b, l, _ = x.shape
    has_past = conv_hist is not None
    kvw = N_KV_HEADS * HEAD_DIM
    new = [[] for _ in range(4)]
    kv_caches = kv_tails = ()
    if has_past:
        kv_caches = (k_cache.reshape(DEPTH, b, WINDOW, kvw), v_cache.reshape(DEPTH, b, WINDOW, kvw))
    for li in range(DEPTH):
        local_hists = rwkv_past = ()
        if has_past:
            local_hists = (jnp.pad(conv_hist[li], ((0, 0), (CONV_PAD - CONV_HIST, 0), (0, 0))),
                           jnp.pad(pool_hist[li], ((0, 0), (POOL_PAD - POOL_HIST, 0), (0, 0))))
            rwkv_past = (shift_prev[li][:, None, :], rwkv_state[li])
        ub, ud, ya, yc, conv_tail, pool_tail = _projmix(
            x, li, w['norm_mix_g'], w['w_in'], w['conv_w'], w['conv_vec'],
            w['pool_wbd'], w['pool_scale'], pos0, local_hists)
        pending = w.pop('pending_casts', {})
        yb, s_new, cast = _rwkv(ub, li, w['rwkv_mu'], w['rwkv_wl'], w['rwkv_vec'], min(CHUNK, l),
                                rwkv_past, tuple(pending.values()))
        w.update(zip(pending.keys(), cast))
        yd, *kv_tails = _attn(ud, li, w['attn_qg'], w['attn_kg'], w['attn_sinks'], chunk, kv_caches,
                              tuple(kv_tails))
        flat = lambda y: y.reshape(b * l, GROUP_WIDTH)
        x = _outffn(x.reshape(b * l, D_MODEL), flat(ya), flat(yb), flat(yc), flat(yd), li, w['w_out'],
                    w['norm_ffn_g'], w['ffn_w_gate'], w['ffn_w_up'], w['ffn_w_down']
                    ).reshape(b, l, D_MODEL)
        new[0].append(conv_tail[:, CONV_PAD - CONV_HIST:, :])
        new[1].append(s_new)
        new[2].append(ub[:, l - 1, :])
        new[3].append(pool_tail[:, POOL_PAD - POOL_HIST:, :])
    kv_new = tuple(t.reshape(DEPTH, b, WINDOW, N_KV_HEADS, HEAD_DIM) for t in kv_tails)
    return x, tuple(jnp.stack(n) for n in new) + kv_new


def kernel(x_prompt, x_sample, cache_conv, state_rwkv, state_rwkv_shift, cache_pool, cache_k, cache_v, norm_mix_g, w_in, conv_w, conv_b, conv_ln_g, conv_ln_b, rwkv_mu, rwkv_w0, rwkv_w2, rwkv_a0, rwkv_a2, rwkv_g2, rwkv_k_k, rwkv_k_a, rwkv_r_k, rwkv_gn_g, rwkv_gn_b, pool_w, pool_scale, attn_q_norm, attn_k_norm, attn_sinks, w_out, norm_ffn_g, ffn_w_gate, ffn_w_up, ffn_w_down):
    w = _stacked_weights(dict(
        norm_mix_g=norm_mix_g, w_in=w_in, conv_w=conv_w, conv_b=conv_b, conv_ln_g=conv_ln_g,
        conv_ln_b=conv_ln_b, rwkv_mu=rwkv_mu, rwkv_w0=rwkv_w0, rwkv_w2=rwkv_w2, rwkv_a0=rwkv_a0,
        rwkv_a2=rwkv_a2, rwkv_g2=rwkv_g2, rwkv_k_k=rwkv_k_k, rwkv_k_a=rwkv_k_a, rwkv_r_k=rwkv_r_k,
        rwkv_gn_g=rwkv_gn_g, rwkv_gn_b=rwkv_gn_b, pool_w=pool_w, pool_scale=pool_scale,
        attn_q_norm=attn_q_norm, attn_k_norm=attn_k_norm, attn_sinks=attn_sinks, w_out=w_out,
        norm_ffn_g=norm_ffn_g, ffn_w_gate=ffn_w_gate, ffn_w_up=ffn_w_up, ffn_w_down=ffn_w_down))
    y_p, (conv_p, rwkv_p, shift_p, pool_p, k_p, v_p) = _trunk(
        x_prompt, None, None, None, None, None, None, 0, CHUNK, w)
    y_s, (conv_s, rwkv_s, shift_s, pool_s, k_s, v_s) = _trunk(
        x_sample, cache_conv, state_rwkv, state_rwkv_shift, cache_pool, cache_k, cache_v,
        PAST_LEN, x_sample.shape[1], w)
    return (y_p, y_s, conv_p, conv_s, rwkv_p, rwkv_s, shift_p, shift_s,
            pool_p, pool_s, k_p, k_s, v_p, v_s)
```

```python
import functools
import math

import jax
import jax.numpy as jnp
from jax import lax
from jax.experimental import pallas as pl
from jax.experimental.pallas import tpu as pltpu

F32 = jnp.float32
BF16 = jnp.bfloat16

D_MODEL = 1024
DEPTH = 2
PAST_LEN = 1024
CHUNK = 64
GROUP_WIDTH = 256
CONV_WIDTH = 31
CONV_HIST = CONV_WIDTH - 1
CONV_PAD = 32
RWKV_HEAD = 64
RWKV_HEADS = 4
POOL_WINDOWS = (2, 4, 8, 16)
POOL_HIST = 15
POOL_PAD = 16
HEAD_DIM = 64
N_Q_HEADS = 4
N_KV_HEADS = 2
WINDOW = 128
D_FF = 2816
A_COLS = 512
B_COLS = 896
C_COLS = 256
D_COLS = 512
IN_COLS = A_COLS + B_COLS + C_COLS + D_COLS
RMS_EPS = 1e-6
LN_EPS = 1e-5
GN_EPS = 64e-5
ATTN_SCALE = HEAD_DIM ** -0.5
NEG_INF = -1e30

DECAY_RANK = 32
AAA_RANK = 32
Q_COLS = N_Q_HEADS * HEAD_DIM
KV_COLS = N_KV_HEADS * HEAD_DIM

VMEM_LIMIT_BYTES = 56 * 1024 * 1024
SUBLANES = 8
MATMUL_ROWS = 512
MIXER_ROWS = 256
LOCAL_ROWS = 128


def _dot(a, b):
    return jnp.dot(a, b, preferred_element_type=F32)


def _bdot(a, b):
    return lax.dot_general(a, b, (((2,), (1,)), ((0,), (0,))), preferred_element_type=F32)


def _bdot_nt(a, b):
    return lax.dot_general(a, b, (((2,), (2,)), ((0,), (0,))), preferred_element_type=F32)


def _bdot_tn(a, b):
    return lax.dot_general(a, b, (((1,), (1,)), ((0,), (0,))), preferred_element_type=F32)


def _sigmoid(x):
    return 1.0 / (1.0 + jnp.exp(-x))


def _split_dot_right(x, m_bf16, terms):
    acc = None
    rem = x
    for i in range(terms):
        hi = rem.astype(BF16)
        d = _dot(hi, m_bf16)
        acc = d if acc is None else acc + d
        if i + 1 < terms:
            rem = rem - hi.astype(F32)
    return acc


def _split_dot_left(m_bf16, x, terms):
    acc = None
    rem = x
    for i in range(terms):
        hi = rem.astype(BF16)
        d = _dot(m_bf16, hi)
        acc = d if acc is None else acc + d
        if i + 1 < terms:
            rem = rem - hi.astype(F32)
    return acc


def _block_matrix(n, blk, value):
    sh = int(math.log2(blk))
    r = lax.shift_right_logical(lax.broadcasted_iota(jnp.int32, (n, n), 0), sh)
    c = lax.shift_right_logical(lax.broadcasted_iota(jnp.int32, (n, n), 1), sh)
    return jnp.where(r == c, value, 0.0).astype(BF16)


def _params(sem):
    return pltpu.CompilerParams(dimension_semantics=sem, vmem_limit_bytes=VMEM_LIMIT_BYTES)


def _layer_block(shape, layer, n_grid):
    zeros = (0,) * len(shape)
    if n_grid == 1:
        index_map = lambda i: (layer,) + zeros
    else:
        index_map = lambda i, j: (layer,) + zeros
    return pl.BlockSpec((None,) + tuple(shape), index_map, pipeline_mode=pl.Buffered(1))


def _phase_copies(ext_ref, sh_ref):
    n = sh_ref.shape[1]
    for s in range(1, SUBLANES):
        sh_ref[s - 1] = ext_ref[s:s + n, :]


def _shifted_rows(ext_ref, sh_ref, start, rb):
    a, s = divmod(start, SUBLANES)
    base = a * SUBLANES
    return ext_ref[base:base + rb, :] if s == 0 else sh_ref[s - 1, base:base + rb, :]


def _conv_rows(ext_ref, sh_ref, w_ref, vec_ref, r0, rb):
    off = CONV_PAD - CONV_HIST
    acc = jnp.zeros((rb, GROUP_WIDTH), F32)
    for j in range(CONV_WIDTH):
        acc = acc + w_ref[j:j + 1, :] * _shifted_rows(ext_ref, sh_ref, r0 + off + j, rb)
    acc = acc + vec_ref[0:1, :]
    mu = jnp.mean(acc, axis=-1, keepdims=True)
    d = acc - mu
    var = jnp.mean(d * d, axis=-1, keepdims=True)
    yn = d * lax.rsqrt(var + LN_EPS) * vec_ref[1:2, :] + vec_ref[2:3, :]
    return yn * _sigmoid(yn)


def _pool_rows(ext_ref, sh_ref, wbd_ref, sc_ref, r0, rb, pos_start):
    base = POOL_PAD + r0
    sums = []
    acc = None
    for i in range(max(POOL_WINDOWS)):
        sh = _shifted_rows(ext_ref, sh_ref, base - i, rb)
        acc = sh if acc is None else acc + sh
        if i + 1 in POOL_WINDOWS:
            sums.append(acc)
    pos = pos_start + r0 + lax.broadcasted_iota(jnp.int32, (rb, 1), 0)
    means = [s / jnp.minimum(w, pos + 1).astype(F32) for s, w in zip(sums, POOL_WINDOWS)]
    lane = lax.broadcasted_iota(jnp.int32, (1, GROUP_WIDTH), 1)
    pc = GROUP_WIDTH // len(POOL_WINDOWS)
    mean = means[-1]
    for g in range(len(POOL_WINDOWS) - 2, -1, -1):
        mean = jnp.where(lane < (g + 1) * pc, means[g], mean)
    d = mean - ext_ref[base:base + rb, :]
    return _dot(d.astype(BF16), wbd_ref[...]) * sc_ref[...]


def _projmix_body(*refs, nb, tl, rb, pos0, has_past):
    x_ref, g_ref, w_ref, cw_ref, cvec_ref, pwbd_ref, psc_ref = refs[:7]
    n_in = 9 if has_past else 7
    (ub_ref, ud_ref, ya_ref, yc_ref, ctail_ref, ptail_ref,
     cext_ref, csh_ref, pext_ref, psh_ref) = refs[n_in:]
    li = pl.program_id(1)

    @pl.when(li == 0)
    def _():
        if has_past:
            cext_ref[:, 0:CONV_PAD, :] = refs[7][...]
            pext_ref[:, 0:POOL_PAD, :] = refs[8][...]
        else:
            cext_ref[:, 0:CONV_PAD, :] = jnp.zeros((nb, CONV_PAD, GROUP_WIDTH), F32)
            pext_ref[:, 0:POOL_PAD, :] = jnp.zeros((nb, POOL_PAD, GROUP_WIDTH), F32)

    x = x_ref[...].reshape(nb * tl, D_MODEL)
    ms = jnp.mean(x * x, axis=-1, keepdims=True)
    xn = ((x * lax.rsqrt(ms + RMS_EPS)) * g_ref[...]).astype(BF16)
    b0 = A_COLS
    c0 = A_COLS + B_COLS
    d0 = c0 + C_COLS
    ua = _dot(xn, w_ref[:, 0:A_COLS])
    uc = _dot(xn, w_ref[:, c0:d0])
    ub_ref[...] = _dot(xn, w_ref[:, b0:c0]).reshape(nb, tl, B_COLS)
    ud_ref[...] = _dot(xn, w_ref[:, d0:IN_COLS]).reshape(nb, tl, D_COLS)
    glu = ua[:, :GROUP_WIDTH] * _sigmoid(ua[:, GROUP_WIDTH:])
    for b in range(nb):
        cext = cext_ref.at[b]
        pext = pext_ref.at[b]
        cext[CONV_PAD:CONV_PAD + tl, :] = glu[b * tl:(b + 1) * tl]
        pext[POOL_PAD:POOL_PAD + tl, :] = uc[b * tl:(b + 1) * tl]
        _phase_copies(cext, csh_ref)
        _phase_copies(pext, psh_ref)
        for r0 in range(0, tl, rb):
            ya_ref[b, r0:r0 + rb, :] = _conv_rows(cext, csh_ref, cw_ref, cvec_ref, r0, rb).astype(BF16)
            yc_ref[b, r0:r0 + rb, :] = _pool_rows(pext, psh_ref, pwbd_ref, psc_ref, r0, rb,
                                                  pos0 + li * tl).astype(BF16)
        ctail = cext[tl:tl + CONV_PAD, :]
        ptail = pext[tl:tl + POOL_PAD, :]
        ctail_ref[b] = ctail
        ptail_ref[b] = ptail
        cext[0:CONV_PAD, :] = ctail
        pext[0:POOL_PAD, :] = ptail


def _projmix(x, layer, g, w_in, conv_w, conv_vec, pool_wbd, pool_scale, pos0, hists=()):
    b, l, _ = x.shape
    nb = b if l <= 64 else 1
    tl = min(l, MATMUL_ROWS)
    rb = min(tl, LOCAL_ROWS)
    assert l % tl == 0 and tl % rb == 0 and tl >= CONV_PAD and b % nb == 0
    gw = GROUP_WIDTH
    tile = lambda w: pl.BlockSpec((nb, tl, w), lambda i, j: (i, j, 0))
    per_seq = lambda r: pl.BlockSpec((nb, r, gw), lambda i, j: (i, 0, 0))
    return pl.pallas_call(
        functools.partial(_projmix_body, nb=nb, tl=tl, rb=rb, pos0=pos0, has_past=bool(hists)),
        grid=(b // nb, l // tl),
        in_specs=[
            tile(D_MODEL),
            _layer_block((1, D_MODEL), layer, 2),
            _layer_block((D_MODEL, IN_COLS), layer, 2),
            _layer_block((CONV_PAD, gw), layer, 2), _layer_block((8, gw), layer, 2),
            _layer_block((gw, gw), layer, 2), _layer_block((1, gw), layer, 2),
        ] + ([per_seq(CONV_PAD), per_seq(POOL_PAD)] if hists else []),
        out_specs=[tile(B_COLS), tile(D_COLS), tile(gw), tile(gw), per_seq(CONV_PAD), per_seq(POOL_PAD)],
        out_shape=[
            jax.ShapeDtypeStruct((b, l, B_COLS), F32),
            jax.ShapeDtypeStruct((b, l, D_COLS), F32),
            jax.ShapeDtypeStruct((b, l, gw), BF16),
            jax.ShapeDtypeStruct((b, l, gw), BF16),
            jax.ShapeDtypeStruct((b, CONV_PAD, gw), F32),
            jax.ShapeDtypeStruct((b, POOL_PAD, gw), F32),
        ],
        scratch_shapes=[
            pltpu.VMEM((nb, CONV_PAD + tl, gw), F32),
            pltpu.VMEM((SUBLANES - 1, CONV_PAD + tl - SUBLANES, gw), F32),
            pltpu.VMEM((nb, POOL_PAD + tl, gw), F32),
            pltpu.VMEM((SUBLANES - 1, POOL_PAD + tl - SUBLANES, gw), F32),
        ],
        compiler_params=_params(("parallel", "arbitrary")),
        name="projmix",
    )(x, g, w_in, conv_w, conv_vec, pool_wbd, pool_scale, *hists)


def _attn_body(*refs, layer, nb, tq, chunk, hist_valid, n_alias):
    sink_ref, u_ref, qg_ref, kg_ref = refs[:4]
    n_in = (6 if hist_valid else 4) + n_alias
    y_ref, kt_ref, vt_ref, kh_ref, vh_ref = refs[n_in:]
    li = pl.program_id(0)
    hd = HEAD_DIM
    rows = nb * tq
    ncq = tq // chunk
    kw = WINDOW + chunk

    @pl.when(li == 0)
    def _():
        if hist_valid:
            kh_ref[...] = refs[4][...]
            vh_ref[...] = refs[5][...]
        else:
            kh_ref[...] = jnp.zeros(kh_ref.shape, F32)
            vh_ref[...] = jnp.zeros(vh_ref.shape, F32)

    u = u_ref[...].reshape(rows, D_COLS)
    q = u[:, 0:Q_COLS]
    k = u[:, Q_COLS:Q_COLS + KV_COLS]
    v = u[:, Q_COLS + KV_COLS:D_COLS]
    inv = 1.0 / hd
    qms = _split_dot_right(q * q, _block_matrix(Q_COLS, hd, inv), 1)
    kms = _split_dot_right(k * k, _block_matrix(KV_COLS, hd, inv), 2)
    qn = (q * lax.rsqrt(qms + RMS_EPS)) * (qg_ref[...] * ATTN_SCALE)
    kn = (k * lax.rsqrt(kms + RMS_EPS)) * kg_ref[...]
    kcat = jnp.concatenate([kh_ref[...], kn.reshape(nb, tq, 2 * hd)], axis=1)
    vcat = jnp.concatenate([vh_ref[...], v.reshape(nb, tq, 2 * hd)], axis=1)
    ktail = kcat[:, tq:tq + WINDOW, :]
    vtail = vcat[:, tq:tq + WINDOW, :]
    if n_alias:
        kt_ref[...] = ktail
        vt_ref[...] = vtail
    else:
        for lyr in range(DEPTH):
            kt_ref[lyr] = ktail if lyr == layer else jnp.zeros(ktail.shape, F32)
            vt_ref[lyr] = vtail if lyr == layer else jnp.zeros(vtail.shape, F32)
    kh_ref[...] = ktail
    vh_ref[...] = vtail
    kcat_b = kcat.astype(BF16)
    vcat_b = vcat.astype(BF16)

    slot_hi = lax.broadcasted_iota(jnp.int32, (1, 2 * hd), 1) >= hd
    q_tiles = []
    for h in range(N_Q_HEADS):
        col = qn[:, (h // 2) * 2 * hd:(h // 2 + 1) * 2 * hd]
        g = h // (N_Q_HEADS // N_KV_HEADS)
        if h % 2 != g:
            col = pltpu.roll(col, hd, axis=1)
        q_tiles.append(jnp.where(slot_hi if g == 1 else jnp.logical_not(slot_hi), col, 0.0).astype(BF16))
    pairs = [(b, c) for c in range(ncq) for b in range(nb)]
    qs = jnp.stack([jnp.concatenate([t[b * tq + c * chunk:b * tq + (c + 1) * chunk] for t in q_tiles], axis=0)
                    for b, c in pairs])
    ks = jnp.stack([kcat_b[b, c * chunk:c * chunk + kw] for b, c in pairs])
    vs = jnp.stack([vcat_b[b, c * chunk:c * chunk + kw] for b, c in pairs])
    st = _bdot_nt(ks, qs)
    nq = N_Q_HEADS * chunk
    if not hist_valid:
        n_edge = min(ncq, WINDOW // chunk) * nb
        cpos = [c * chunk for _, c in pairs[:n_edge]]
        kpos = lax.broadcasted_iota(jnp.int32, (1, kw, nq), 1) + (li * tq - WINDOW)
        edge = jnp.concatenate([jnp.where(kpos + cp >= 0, st[i:i + 1], NEG_INF) for i, cp in enumerate(cpos)],
                               axis=0)
        st = jnp.concatenate([edge, st[n_edge:]], axis=0) if n_edge < len(pairs) else edge
    hlane = lax.broadcasted_iota(jnp.int32, (1, 1, nq), 2)
    sk = jnp.full((1, 1, nq), sink_ref[layer, N_Q_HEADS - 1], F32)
    for h in range(N_Q_HEADS - 2, -1, -1):
        sk = jnp.where(hlane < (h + 1) * chunk, sink_ref[layer, h], sk)
    m = jnp.maximum(jnp.max(st, axis=1, keepdims=True), sk)
    p = jnp.exp(st - m)
    den = jnp.sum(p, axis=1, keepdims=True) + jnp.exp(sk - m)
    pn = (p * (1.0 / den)).astype(BF16)
    o = _bdot_tn(pn, vs)
    lo = jnp.logical_not(slot_hi)
    for i, (b, c) in enumerate(pairs):
        oc = o[i]
        col0 = jnp.where(lo, oc[0:chunk], pltpu.roll(oc[chunk:2 * chunk], hd, axis=1))
        col1 = jnp.where(lo, pltpu.roll(oc[2 * chunk:3 * chunk], hd, axis=1), oc[3 * chunk:4 * chunk])
        y_ref[b, c * chunk:(c + 1) * chunk, :] = jnp.concatenate([col0, col1], axis=1).astype(BF16)


def _attn(ud, layer, qg, kg, sinks, chunk, caches=(), tails=()):
    b, l, _ = ud.shape
    tq = min(l, MIXER_ROWS)
    assert l % tq == 0 and tq % chunk == 0
    kvw = N_KV_HEADS * HEAD_DIM
    layer_spec = pl.BlockSpec((None, b, WINDOW, kvw), lambda j: (layer, 0, 0, 0))
    n_fixed = 4 + len(caches)
    return pl.pallas_call(
        functools.partial(_attn_body, layer=layer, nb=b, tq=tq, chunk=chunk, hist_valid=bool(caches),
                          n_alias=len(tails)),
        grid=(l // tq,),
        in_specs=[
            pl.BlockSpec(memory_space=pltpu.SMEM),
            pl.BlockSpec((b, tq, D_COLS), lambda j: (0, j, 0)),
            _layer_block((1, GROUP_WIDTH), layer, 1),
            _layer_block((1, kvw), layer, 1),
        ] + [layer_spec for _ in caches] + [pl.BlockSpec(memory_space=pl.ANY) for _ in tails],
        out_specs=[pl.BlockSpec((b, tq, GROUP_WIDTH), lambda j: (0, j, 0))] + 2 * [
            layer_spec if tails else pl.BlockSpec((DEPTH, b, WINDOW, kvw), lambda j: (0, 0, 0, 0))],
        out_shape=[
            jax.ShapeDtypeStruct((b, l, GROUP_WIDTH), BF16),
            jax.ShapeDtypeStruct((DEPTH, b, WINDOW, kvw), F32),
            jax.ShapeDtypeStruct((DEPTH, b, WINDOW, kvw), F32),
        ],
        input_output_aliases={n_fixed + i: 1 + i for i in range(len(tails))},
        scratch_shapes=[pltpu.VMEM((b, WINDOW, kvw), F32), pltpu.VMEM((b, WINDOW, kvw), F32)],
        compiler_params=_params(("arbitrary",)),
        name="attn",
    )(sinks, ud, qg, kg, *caches, *tails)


def _rwkv_body(*refs, nb, tb, chunk, n_cast, has_past):
    u_ref, mu_ref, wl_ref, vec_ref = refs[:4]
    n_in = 6 if has_past else 4
    cast_in = refs[n_in:n_in + n_cast]
    y_ref, sn_ref = refs[n_in + n_cast:n_in + n_cast + 2]
    cast_out = refs[n_in + n_cast + 2:n_in + 2 * n_cast + 2]
    prev_ref, s_ref, yacc_ref = refs[n_in + 2 * n_cast + 2:]
    li = pl.program_id(0)
    for src, dst in zip(cast_in, cast_out):
        dst[...] = src[...].astype(BF16)
    gw = GROUP_WIDTH
    hd = RWKV_HEAD
    nh = RWKV_HEADS
    rows = nb * tb
    nc = tb // chunk

    @pl.when(li == 0)
    def _():
        if not has_past:
            prev_ref[...] = jnp.zeros(prev_ref.shape, F32)
            s_ref[...] = jnp.zeros(s_ref.shape, F32)
            return
        sp_ref, s0_ref = refs[4:6]
        prev_ref[...] = sp_ref[...]
        zero = jnp.zeros((nb, hd, hd), F32)
        for h in range(nh):
            sh = s0_ref[:, h]
            s_ref[h * nb:(h + 1) * nb] = jnp.concatenate([sh, zero] if h % 2 == 0 else [zero, sh], axis=-1)

    row = lax.broadcasted_iota(jnp.int32, (tb, 1), 0)
    mu = mu_ref[...]
    xs_parts = []
    for b in range(nb):
        ub = u_ref[b]
        prev = jnp.where(row == 0, prev_ref[b], pltpu.roll(ub, 1, axis=0))
        prev_ref[b] = ub[tb - 1:tb, :]
        xs_parts.append(ub + mu * (prev - ub))
    xs = jnp.concatenate(xs_parts, axis=0)
    r = xs[:, 0:gw]
    k = xs[:, gw:2 * gw]
    v = xs[:, 2 * gw:3 * gw]
    lat = xs[:, 3 * gw:B_COLS]
    lane_lat = lax.broadcasted_iota(jnp.int32, (1, B_COLS - 3 * gw), 1)
    act = jnp.where(lane_lat < DECAY_RANK, jnp.tanh(lat),
                    jnp.where(lane_lat < DECAY_RANK + AAA_RANK, lat, _sigmoid(lat)))
    lo = _dot(act.astype(BF16), wl_ref[...])
    w0 = vec_ref[0:1, :]
    a0 = vec_ref[1:2, :]
    k_k = vec_ref[2:3, :]
    k_a = vec_ref[3:4, :]
    r_k = vec_ref[4:5, :]
    gn_g = vec_ref[5:6, :]
    gn_b = vec_ref[6:7, :]
    z = -(w0 + lo[:, 0:gw])
    softplus = jnp.maximum(z, 0.0) + jnp.log(1.0 + jnp.exp(-jnp.abs(z)))
    logw = -jnp.exp(-softplus - 0.5)
    a_rate = _sigmoid(a0 + lo[:, gw:2 * gw])
    gate = lo[:, 2 * gw:3 * gw]
    ones_blk = _block_matrix(gw, hd, 1.0)
    kk = k * k_k
    kk = kk * lax.rsqrt(jnp.maximum(_split_dot_right(kk * kk, ones_blk, 1), 1e-24))
    k_mod = k * (1.0 + (a_rate - 1.0) * k_a)
    b_v = kk * a_rate
    bonus = _split_dot_right(r * k_mod * r_k, ones_blk, 1) * v

    grp = min(rows, 256)
    gi = lax.broadcasted_iota(jnp.int32, (grp, grp), 0)
    gj = lax.broadcasted_iota(jnp.int32, (grp, grp), 1)
    csh = int(math.log2(chunk))
    tri = jnp.where(jnp.logical_and(lax.shift_right_logical(gi, csh) == lax.shift_right_logical(gj, csh),
                                    gj <= gi), 1.0, 0.0).astype(BF16)
    cum = jnp.concatenate([_split_dot_left(tri, logw[g0:g0 + grp], 2) for g0 in range(0, rows, grp)], axis=0)
    cum3 = cum.reshape(nb * nc, chunk, gw)
    cum_c = cum3[:, chunk - 1:chunk, :]
    e_end = jnp.exp(cum_c - cum3).reshape(rows, gw)
    w_c = jnp.exp(cum_c)
    e_neg = jnp.exp(-cum)
    dense = dict(
        a=-kk * jnp.exp(cum - logw),
        r=r * jnp.exp(cum),
        bt=b_v * e_neg,
        kt=k_mod * e_neg,
        bh=b_v * e_end,
        kh=k_mod * e_end,
    )
    slot_hi = lax.broadcasted_iota(jnp.int32, (1, 2 * hd), 1) >= hd

    def head_tile(x, h, own_slot):
        col = x[:, (h // 2) * 2 * hd:(h // 2 + 1) * 2 * hd]
        keep = slot_hi if (h % 2 == 1) == own_slot else jnp.logical_not(slot_hi)
        if x.dtype == BF16:
            return col * jnp.where(keep, 1.0, 0.0).astype(BF16)
        return jnp.where(keep, col, 0.0)

    v_sw = jnp.concatenate([pltpu.roll(v[:, j * 2 * hd:(j + 1) * 2 * hd], hd, axis=1) for j in range(nh // 2)],
                           axis=1)
    tiles = {"r": [head_tile(dense["r"], h, True) for h in range(nh)]}
    dense_b = {name: dense[name].astype(BF16) for name in ("a", "bt", "kt", "bh", "kh")}
    tiles_b = {name: [head_tile(dense_b[name], h, True) for h in range(nh)] for name in ("a", "bh", "kh")}
    for name in ("bt", "kt"):
        tiles_b[name] = [dense_b[name][:, (h // 2) * 2 * hd:(h // 2 + 1) * 2 * hd] for h in range(nh)]
    v_sw_b = v_sw.astype(BF16)
    v_tiles = [head_tile(v_sw_b, h, False) for h in range(nh)]

    def blocks(per_head, c):
        return jnp.stack([per_head[h][b * tb + c * chunk:b * tb + (c + 1) * chunk]
                          for h in range(nh) for b in range(nb)])

    n = nh * nb
    ri = lax.broadcasted_iota(jnp.int32, (2 * chunk, 2 * chunk), 0)
    ci = jnp.bitwise_and(lax.broadcasted_iota(jnp.int32, (2 * chunk, 2 * chunk), 1), chunk - 1)
    gmask = ci < jnp.bitwise_and(ri, chunk - 1) + lax.shift_right_logical(ri, csh)
    zeros_c = jnp.zeros((n, chunk, 2 * hd), BF16)
    n_sq = int(math.log2(chunk))
    for c in range(nc):
        a_b = blocks(tiles_b["a"], c)
        r_f = blocks(tiles["r"], c)
        v_b = blocks(v_tiles, c)
        ar = jnp.concatenate([a_b, r_f.astype(BF16)], axis=1)
        bk = jnp.concatenate([blocks(tiles_b["bt"], c), blocks(tiles_b["kt"], c)], axis=1)
        bhkh = jnp.concatenate([blocks(tiles_b["bh"], c), blocks(tiles_b["kh"], c)], axis=1)
        g = jnp.where(gmask, _bdot_nt(ar, bk), 0.0)
        g_top = g[:, :chunk, :]
        g_bot = g[:, chunk:, :].astype(BF16)
        w = a_b.astype(F32) + _bdot(g_top.astype(BF16), jnp.concatenate([zeros_c, v_b], axis=1))
        p = g_top[:, :, :chunk]
        for i in range(n_sq):
            pb = p.astype(BF16)
            if i + 1 < n_sq:
                res = _bdot(pb, jnp.concatenate([w.astype(BF16), pb], axis=2))
                w = w + res[:, :, :2 * hd]
                p = res[:, :, 2 * hd:]
            else:
                hc = chunk // 2
                low = w[:, hc:, :] + _bdot(pb[:, hc:, :hc], w[:, :hc, :].astype(BF16))
                w = jnp.concatenate([w[:, :hc, :], low], axis=1)
        xv = jnp.concatenate([w.astype(BF16), v_b], axis=1)
        ry = _bdot(g_bot, xv)
        mp = _bdot_tn(xv, bhkh)
        s_old = s_ref[...]
        s_b = s_old.astype(BF16)
        y_nt = _bdot_nt((ry + r_f).astype(BF16), s_b)
        wc = jnp.stack([w_c[b * nc + c][:, (h // 2) * 2 * hd:(h // 2 + 1) * 2 * hd]
                        for h in range(nh) for b in range(nb)])
        psi = jnp.concatenate(
            [mp[h * nb:(h + 1) * nb, (1 - h % 2) * hd:(2 - h % 2) * hd, :] for h in range(nh)], axis=0)
        s_ref[...] = s_old * wc + _bdot(s_b, mp.astype(BF16)) + psi
        for b in range(nb):
            ys = [y_nt[h * nb + b] + ry[h * nb + b][:, (1 - h % 2) * hd:(2 - h % 2) * hd] for h in range(nh)]
            yacc_ref[b * tb + c * chunk:b * tb + (c + 1) * chunk, :] = jnp.concatenate(ys, axis=1)

    y = yacc_ref[...]
    avg_blk = _block_matrix(gw, hd, 1.0 / hd)
    m = _split_dot_right(y, avg_blk, 1)
    d = y - m
    var = _split_dot_right(d * d, avg_blk, 1)
    yn = d * lax.rsqrt(var + GN_EPS) * gn_g + gn_b
    out = ((yn + bonus) * gate).astype(BF16)
    for b in range(nb):
        y_ref[b] = out[b * tb:(b + 1) * tb]

    @pl.when(li == pl.num_programs(0) - 1)
    def _():
        for h in range(nh):
            sn_ref[:, h] = s_ref[h * nb:(h + 1) * nb, :, (h % 2) * hd:(h % 2 + 1) * hd]


def _rwkv(ub, layer, mu, wl, vec, chunk, past=(), to_cast=()):
    b, l, _ = ub.shape
    tb = min(l, MIXER_ROWS)
    assert l % tb == 0 and tb % chunk == 0
    steps = l // tb
    hd = RWKV_HEAD
    n = RWKV_HEADS * b
    sshape = (b, RWKV_HEADS, hd, hd)
    cast_specs = []
    for wt in to_cast:
        depth, rows, cols = wt.shape
        slab = rows // steps
        assert rows % steps == 0 and slab % (2 * SUBLANES) == 0
        cast_specs.append(pl.BlockSpec((depth, slab, cols), lambda j: (0, j, 0)))
    outs = pl.pallas_call(
        functools.partial(_rwkv_body, nb=b, tb=tb, chunk=chunk, n_cast=len(to_cast), has_past=bool(past)),
        grid=(steps,),
        in_specs=[
            pl.BlockSpec((b, tb, B_COLS), lambda j: (0, j, 0)),
            _layer_block((1, B_COLS), layer, 1),
            _layer_block((128, 3 * GROUP_WIDTH), layer, 1),
            _layer_block((8, GROUP_WIDTH), layer, 1),
        ] + ([pl.BlockSpec((b, 1, B_COLS), lambda j: (0, 0, 0)),
              pl.BlockSpec(sshape, lambda j: (0, 0, 0, 0))] if past else []) + cast_specs,
        out_specs=[
            pl.BlockSpec((b, tb, GROUP_WIDTH), lambda j: (0, j, 0)),
            pl.BlockSpec(sshape, lambda j: (0, 0, 0, 0)),
        ] + cast_specs,
        out_shape=[
            jax.ShapeDtypeStruct((b, l, GROUP_WIDTH), BF16),
            jax.ShapeDtypeStruct(sshape, F32),
        ] + [jax.ShapeDtypeStruct(wt.shape, BF16) for wt in to_cast],
        scratch_shapes=[
            pltpu.VMEM((b, 1, B_COLS), F32),
            pltpu.VMEM((n, hd, 2 * hd), F32),
            pltpu.VMEM((b * tb, GROUP_WIDTH), F32),
        ],
        compiler_params=_params(("arbitrary",)),
        name="rwkv",
    )(ub, mu, wl, vec, *past, *to_cast)
    return outs[0], outs[1], tuple(outs[2:])


def _outffn_body(x_ref, ya_ref, yb_ref, yc_ref, yd_ref, g_ref, wo_in, wg_in, wu_in, wd_in, o_ref,
                 *scratch, layer, staged):
    if staged:
        wo_ref, wg_ref, wu_ref, wd_ref, sems = scratch
        pairs = ((wo_in, wo_ref), (wg_in, wg_ref), (wu_in, wu_ref), (wd_in, wd_ref))
        copies = [pltpu.make_async_copy(src.at[layer], dst, sems.at[k]) for k, (src, dst) in enumerate(pairs)]
        for copy in copies:
            copy.start()
        wait = lambda *ks: [copies[k].wait() for k in ks]
    else:
        wo_ref, wg_ref, wu_ref, wd_ref = wo_in, wg_in, wu_in, wd_in
        wait = lambda *ks: None
    ycat = jnp.concatenate([ya_ref[...], yb_ref[...], yc_ref[...], yd_ref[...]], axis=-1)
    wait(0)
    x1 = x_ref[...] + _dot(ycat, wo_ref[...])
    ms = jnp.mean(x1 * x1, axis=-1, keepdims=True)
    hn = ((x1 * lax.rsqrt(ms + RMS_EPS)) * g_ref[...]).astype(BF16)
    wait(1, 2)
    hg = _dot(hn, wg_ref[...])
    hu = _dot(hn, wu_ref[...])
    act = (hg * _sigmoid(hg) * hu).astype(BF16)
    wait(3)
    o_ref[...] = x1 + _dot(act, wd_ref[...])


def _outffn(x2d, ya, yb, yc, yd, layer, wo, g, wg, wu, wd):
    t = x2d.shape[0]
    tm = min(t, MATMUL_ROWS)
    assert t % tm == 0
    staged = t == tm
    row = lambda w: pl.BlockSpec((tm, w), lambda i: (i, 0))
    weight_shapes = ((4 * GROUP_WIDTH, D_MODEL), (D_MODEL, D_FF), (D_MODEL, D_FF), (D_FF, D_MODEL))
    if staged:
        weight_specs = [pl.BlockSpec(memory_space=pl.ANY) for _ in weight_shapes]
        scratch = [pltpu.VMEM(s, BF16) for s in weight_shapes] + [pltpu.SemaphoreType.DMA((len(weight_shapes),))]
    else:
        weight_specs = [_layer_block(s, layer, 1) for s in weight_shapes]
        scratch = []
    return pl.pallas_call(
        functools.partial(_outffn_body, layer=layer, staged=staged),
        grid=(t // tm,),
        in_specs=[
            row(D_MODEL), row(GROUP_WIDTH), row(GROUP_WIDTH), row(GROUP_WIDTH), row(GROUP_WIDTH),
            _layer_block((1, D_MODEL), layer, 1),
        ] + weight_specs,
        out_specs=row(D_MODEL),
        out_shape=jax.ShapeDtypeStruct((t, D_MODEL), F32),
        scratch_shapes=scratch,
        compiler_params=_params(("parallel",)),
        name="outffn",
    )(x2d, ya, yb, yc, yd, g, wo, wg, wu, wd)


_LATE_WEIGHTS = ('w_out', 'ffn_w_gate', 'ffn_w_up', 'ffn_w_down')


def _rows8(vectors):
    stacked = jnp.stack(vectors, axis=1)
    return jnp.pad(stacked, ((0, 0), (0, SUBLANES - stacked.shape[1]), (0, 0)))


def _stacked_weights(p):
    gw = GROUP_WIDTH
    depth = p['w_in'].shape[0]
    place = lambda a, before, after: jnp.pad(a, ((0, 0), (0, 0), (before, after)))
    wl = jnp.concatenate([place(p['rwkv_w2'], 0, 2 * gw), place(p['rwkv_a2'], gw, gw),
                          place(p['rwkv_g2'], 2 * gw, 0)], axis=1)
    pc = gw // len(POOL_WINDOWS)
    wbd = jnp.concatenate([place(p['pool_w'][:, g], g * pc, gw - (g + 1) * pc)
                           for g in range(len(POOL_WINDOWS))], axis=1)
    return dict(
        norm_mix_g=p['norm_mix_g'][:, None, :],
        w_in=p['w_in'].astype(BF16),
        conv_w=jnp.pad(p['conv_w'], ((0, 0), (0, CONV_PAD - CONV_WIDTH), (0, 0))),
        conv_vec=_rows8([p['conv_b'], p['conv_ln_g'], p['conv_ln_b']]),
        rwkv_mu=p['rwkv_mu'][:, None, :],
        rwkv_wl=wl.astype(BF16),
        rwkv_vec=_rows8([p['rwkv_w0'], p['rwkv_a0'], p['rwkv_k_k'], p['rwkv_k_a'],
                         p['rwkv_r_k'].reshape(depth, gw), p['rwkv_gn_g'], p['rwkv_gn_b']]),
        pool_wbd=wbd.astype(BF16),
        pool_scale=p['pool_scale'][:, None, :],
        attn_qg=jnp.tile(p['attn_q_norm'], (1, N_Q_HEADS))[:, None, :],
        attn_kg=jnp.tile(p['attn_k_norm'], (1, N_KV_HEADS))[:, None, :],
        attn_sinks=p['attn_sinks'],
        norm_ffn_g=p['norm_ffn_g'][:, None, :],
        pending_casts={name: p[name] for name in _LATE_WEIGHTS},
    )


def _trunk(x, conv_hist, rwkv_state, shift_prev, pool_hist, k_cache, v_cache, pos0, chunk, w):
    b, l, _ = x.shape
    has_past = conv_hist is not None
    kvw = N_KV_HEADS * HEAD_DIM
    new = [[] for _ in range(4)]
    kv_caches = kv_tails = ()
    if has_past:
        kv_caches = (k_cache.reshape(DEPTH, b, WINDOW, kvw), v_cache.reshape(DEPTH, b, WINDOW, kvw))
    for li in range(DEPTH):
        local_hists = rwkv_past = ()
        if has_past:
            local_hists = (jnp.pad(conv_hist[li], ((0, 0), (CONV_PAD - CONV_HIST, 0), (0, 0))),
                           jnp.pad(pool_hist[li], ((0, 0), (POOL_PAD - POOL_HIST, 0), (0, 0))))
            rwkv_past = (shift_prev[li][:, None, :], rwkv_state[li])
        ub, ud, ya, yc, conv_tail, pool_tail = _projmix(
            x, li, w['norm_mix_g'], w['w_in'], w['conv_w'], w['conv_vec'],
            w['pool_wbd'], w['pool_scale'], pos0, local_hists)
        pending = w.pop('pending_casts', {})
        yb, s_new, cast = _rwkv(ub, li, w['rwkv_mu'], w['rwkv_wl'], w['rwkv_vec'], min(CHUNK, l),
                                rwkv_past, tuple(pending.values()))
        w.update(zip(pending.keys(), cast))
        yd, *kv_tails = _attn(ud, li, w['attn_qg'], w['attn_kg'], w['attn_sinks'], chunk, kv_caches,
                              tuple(kv_tails))
        flat = lambda y: y.reshape(b * l, GROUP_WIDTH)
        x = _outffn(x.reshape(b * l, D_MODEL), flat(ya), flat(yb), flat(yc), flat(yd), li, w['w_out'],
                    w['norm_ffn_g'], w['ffn_w_gate'], w['ffn_w_up'], w['ffn_w_down']
                    ).reshape(b, l, D_MODEL)
        new[0].append(conv_tail[:, CONV_PAD - CONV_HIST:, :])
        new[1].append(s_new)
        new[2].append(ub[:, l - 1, :])
        new[3].append(pool_tail[:, POOL_PAD - POOL_HIST:, :])
    kv_new = tuple(t.reshape(DEPTH, b, WINDOW, N_KV_HEADS, HEAD_DIM) for t in kv_tails)
    return x, tuple(jnp.stack(n) for n in new) + kv_new


def kernel(x_prompt, x_sample, cache_conv, state_rwkv, state_rwkv_shift, cache_pool, cache_k, cache_v, norm_mix_g, w_in, conv_w, conv_b, conv_ln_g, conv_ln_b, rwkv_mu, rwkv_w0, rwkv_w2, rwkv_a0, rwkv_a2, rwkv_g2, rwkv_k_k, rwkv_k_a, rwkv_r_k, rwkv_gn_g, rwkv_gn_b, pool_w, pool_scale, attn_q_norm, attn_k_norm, attn_sinks, w_out, norm_ffn_g, ffn_w_gate, ffn_w_up, ffn_w_down):
    w = _stacked_weights(dict(
        norm_mix_g=norm_mix_g, w_in=w_in, conv_w=conv_w, conv_b=conv_b, conv_ln_g=conv_ln_g,
        conv_ln_b=conv_ln_b, rwkv_mu=rwkv_mu, rwkv_w0=rwkv_w0, rwkv_w2=rwkv_w2, rwkv_a0=rwkv_a0,
        rwkv_a2=rwkv_a2, rwkv_g2=rwkv_g2, rwkv_k_k=rwkv_k_k, rwkv_k_a=rwkv_k_a, rwkv_r_k=rwkv_r_k,
        rwkv_gn_g=rwkv_gn_g, rwkv_gn_b=rwkv_gn_b, pool_w=pool_w, pool_scale=pool_scale,
        attn_q_norm=attn_q_norm, attn_k_norm=attn_k_norm, attn_sinks=attn_sinks, w_out=w_out,
        norm_ffn_g=norm_ffn_g, ffn_w_gate=ffn_w_gate, ffn_w_up=ffn_w_up, ffn_w_down=ffn_w_down))
    y_p, (conv_p, rwkv_p, shift_p, pool_p, k_p, v_p) = _trunk(
        x_prompt, None, None, None, None, None, None, 0, CHUNK, w)
    y_s, (conv_s, rwkv_s, shift_s, pool_s, k_s, v_s) = _trunk(
        x_sample, cache_conv, state_rwkv, state_rwkv_shift, cache_pool, cache_k, cache_v,
        PAST_LEN, x_sample.shape[1], w)
    return (y_p, y_s, conv_p, conv_s, rwkv_p, rwkv_s, shift_p, shift_s,
            pool_p, pool_s, k_p, k_s, v_p, v_s)
```

```python
import functools
import math

import jax
import jax.numpy as jnp
from jax import lax
from jax.experimental import pallas as pl
from jax.experimental.pallas import tpu as pltpu

F32 = jnp.float32
BF16 = jnp.bfloat16

D_MODEL = 1024
DEPTH = 2
PAST_LEN = 1024
CHUNK = 64
GROUP_WIDTH = 256
CONV_WIDTH = 31
CONV_HIST = CONV_WIDTH - 1
CONV_PAD = 32
RWKV_HEAD = 64
RWKV_HEADS = 4
POOL_WINDOWS = (2, 4, 8, 16)
POOL_HIST = 15
POOL_PAD = 16
HEAD_DIM = 64
N_Q_HEADS = 4
N_KV_HEADS = 2
WINDOW = 128
D_FF = 2816
A_COLS = 512
B_COLS = 896
C_COLS = 256
D_COLS = 512
IN_COLS = A_COLS + B_COLS + C_COLS + D_COLS
RMS_EPS = 1e-6
LN_EPS = 1e-5
GN_EPS = 64e-5
ATTN_SCALE = HEAD_DIM ** -0.5
NEG_INF = -1e30

DECAY_RANK = 32
AAA_RANK = 32
Q_COLS = N_Q_HEADS * HEAD_DIM
KV_COLS = N_KV_HEADS * HEAD_DIM

VMEM_LIMIT_BYTES = 56 * 1024 * 1024
SUBLANES = 8
MATMUL_ROWS = 512
MIXER_ROWS = 256
LOCAL_ROWS = 128


def _dot(a, b):
    return jnp.dot(a, b, preferred_element_type=F32)


def _bdot(a, b):
    return lax.dot_general(a, b, (((2,), (1,)), ((0,), (0,))), preferred_element_type=F32)


def _bdot_nt(a, b):
    return lax.dot_general(a, b, (((2,), (2,)), ((0,), (0,))), preferred_element_type=F32)


def _bdot_tn(a, b):
    return lax.dot_general(a, b, (((1,), (1,)), ((0,), (0,))), preferred_element_type=F32)


def _sigmoid(x):
    return 1.0 / (1.0 + jnp.exp(-x))


def _split_dot_right(x, m_bf16, terms):
    acc = None
    rem = x
    for i in range(terms):
        hi = rem.astype(BF16)
        d = _dot(hi, m_bf16)
        acc = d if acc is None else acc + d
        if i + 1 < terms:
            rem = rem - hi.astype(F32)
    return acc


def _split_dot_left(m_bf16, x, terms):
    acc = None
    rem = x
    for i in range(terms):
        hi = rem.astype(BF16)
        d = _dot(m_bf16, hi)
        acc = d if acc is None else acc + d
        if i + 1 < terms:
            rem = rem - hi.astype(F32)
    return acc


def _block_matrix(n, blk, value):
    sh = int(math.log2(blk))
    r = lax.shift_right_logical(lax.broadcasted_iota(jnp.int32, (n, n), 0), sh)
    c = lax.shift_right_logical(lax.broadcasted_iota(jnp.int32, (n, n), 1), sh)
    return jnp.where(r == c, value, 0.0).astype(BF16)


def _params(sem):
    return pltpu.CompilerParams(dimension_semantics=sem, vmem_limit_bytes=VMEM_LIMIT_BYTES)


def _layer_block(shape, layer, n_grid):
    zeros = (0,) * len(shape)
    if n_grid == 1:
        index_map = lambda i: (layer,) + zeros
    else:
        index_map = lambda i, j: (layer,) + zeros
    return pl.BlockSpec((None,) + tuple(shape), index_map, pipeline_mode=pl.Buffered(1))


def _phase_copies(ext_ref, sh_ref):
    n = sh_ref.shape[1]
    for s in range(1, SUBLANES):
        sh_ref[s - 1] = ext_ref[s:s + n, :]


def _shifted_rows(ext_ref, sh_ref, start, rb):
    a, s = divmod(start, SUBLANES)
    base = a * SUBLANES
    return ext_ref[base:base + rb, :] if s == 0 else sh_ref[s - 1, base:base + rb, :]


def _conv_rows(ext_ref, sh_ref, w_ref, vec_ref, r0, rb):
    off = CONV_PAD - CONV_HIST
    acc = jnp.zeros((rb, GROUP_WIDTH), F32)
    for j in range(CONV_WIDTH):
        acc = acc + w_ref[j:j + 1, :] * _shifted_rows(ext_ref, sh_ref, r0 + off + j, rb)
    acc = acc + vec_ref[0:1, :]
    mu = jnp.mean(acc, axis=-1, keepdims=True)
    d = acc - mu
    var = jnp.mean(d * d, axis=-1, keepdims=True)
    yn = d * lax.rsqrt(var + LN_EPS) * vec_ref[1:2, :] + vec_ref[2:3, :]
    return yn * _sigmoid(yn)


def _pool_rows(ext_ref, sh_ref, wbd_ref, sc_ref, r0, rb, pos_start):
    base = POOL_PAD + r0
    sums = []
    acc = None
    for i in range(max(POOL_WINDOWS)):
        sh = _shifted_rows(ext_ref, sh_ref, base - i, rb)
        acc = sh if acc is None else acc + sh
        if i + 1 in POOL_WINDOWS:
            sums.append(acc)
    pos = pos_start + r0 + lax.broadcasted_iota(jnp.int32, (rb, 1), 0)
    means = [s / jnp.minimum(w, pos + 1).astype(F32) for s, w in zip(sums, POOL_WINDOWS)]
    lane = lax.broadcasted_iota(jnp.int32, (1, GROUP_WIDTH), 1)
    pc = GROUP_WIDTH // len(POOL_WINDOWS)
    mean = means[-1]
    for g in range(len(POOL_WINDOWS) - 2, -1, -1):
        mean = jnp.where(lane < (g + 1) * pc, means[g], mean)
    d = mean - ext_ref[base:base + rb, :]
    return _dot(d.astype(BF16), wbd_ref[...]) * sc_ref[...]


def _projmix_body(*refs, nb, tl, rb, pos0, has_past):
    x_ref, g_ref, w_ref, cw_ref, cvec_ref, pwbd_ref, psc_ref = refs[:7]
    n_in = 9 if has_past else 7
    (ub_ref, ud_ref, ya_ref, yc_ref, ctail_ref, ptail_ref,
     cext_ref, csh_ref, pext_ref, psh_ref) = refs[n_in:]
    li = pl.program_id(1)

    @pl.when(li == 0)
    def _():
        if has_past:
            cext_ref[:, 0:CONV_PAD, :] = refs[7][...]
            pext_ref[:, 0:POOL_PAD, :] = refs[8][...]
        else:
            cext_ref[:, 0:CONV_PAD, :] = jnp.zeros((nb, CONV_PAD, GROUP_WIDTH), F32)
            pext_ref[:, 0:POOL_PAD, :] = jnp.zeros((nb, POOL_PAD, GROUP_WIDTH), F32)

    x = x_ref[...].reshape(nb * tl, D_MODEL)
    ms = jnp.mean(x * x, axis=-1, keepdims=True)
    xn = ((x * lax.rsqrt(ms + RMS_EPS)) * g_ref[...]).astype(BF16)
    b0 = A_COLS
    c0 = A_COLS + B_COLS
    d0 = c0 + C_COLS
    ua = _dot(xn, w_ref[:, 0:A_COLS])
    uc = _dot(xn, w_ref[:, c0:d0])
    ub_ref[...] = _dot(xn, w_ref[:, b0:c0]).reshape(nb, tl, B_COLS)
    ud_ref[...] = _dot(xn, w_ref[:, d0:IN_COLS]).reshape(nb, tl, D_COLS)
    glu = ua[:, :GROUP_WIDTH] * _sigmoid(ua[:, GROUP_WIDTH:])
    for b in range(nb):
        cext = cext_ref.at[b]
        pext = pext_ref.at[b]
        cext[CONV_PAD:CONV_PAD + tl, :] = glu[b * tl:(b + 1) * tl]
        pext[POOL_PAD:POOL_PAD + tl, :] = uc[b * tl:(b + 1) * tl]
        _phase_copies(cext, csh_ref)
        _phase_copies(pext, psh_ref)
        for r0 in range(0, tl, rb):
            ya_ref[b, r0:r0 + rb, :] = _conv_rows(cext, csh_ref, cw_ref, cvec_ref, r0, rb).astype(BF16)
            yc_ref[b, r0:r0 + rb, :] = _pool_rows(pext, psh_ref, pwbd_ref, psc_ref, r0, rb,
                                                  pos0 + li * tl).astype(BF16)
        ctail = cext[tl:tl + CONV_PAD, :]
        ptail = pext[tl:tl + POOL_PAD, :]
        ctail_ref[b] = ctail
        ptail_ref[b] = ptail
        cext[0:CONV_PAD, :] = ctail
        pext[0:POOL_PAD, :] = ptail


def _projmix(x, layer, g, w_in, conv_w, conv_vec, pool_wbd, pool_scale, pos0, hists=()):
    b, l, _ = x.shape
    nb = b if l <= 64 else 1
    tl = min(l, MATMUL_ROWS)
    rb = min(tl, LOCAL_ROWS)
    assert l % tl == 0 and tl % rb == 0 and tl >= CONV_PAD and b % nb == 0
    gw = GROUP_WIDTH
    tile = lambda w: pl.BlockSpec((nb, tl, w), lambda i, j: (i, j, 0))
    per_seq = lambda r: pl.BlockSpec((nb, r, gw), lambda i, j: (i, 0, 0))
    return pl.pallas_call(
        functools.partial(_projmix_body, nb=nb, tl=tl, rb=rb, pos0=pos0, has_past=bool(hists)),
        grid=(b // nb, l // tl),
        in_specs=[
            tile(D_MODEL),
            _layer_block((1, D_MODEL), layer, 2),
            _layer_block((D_MODEL, IN_COLS), layer, 2),
            _layer_block((CONV_PAD, gw), layer, 2), _layer_block((8, gw), layer, 2),
            _layer_block((gw, gw), layer, 2), _layer_block((1, gw), layer, 2),
        ] + ([per_seq(CONV_PAD), per_seq(POOL_PAD)] if hists else []),
        out_specs=[tile(B_COLS), tile(D_COLS), tile(gw), tile(gw), per_seq(CONV_PAD), per_seq(POOL_PAD)],
        out_shape=[
            jax.ShapeDtypeStruct((b, l, B_COLS), F32),
            jax.ShapeDtypeStruct((b, l, D_COLS), F32),
            jax.ShapeDtypeStruct((b, l, gw), BF16),
            jax.ShapeDtypeStruct((b, l, gw), BF16),
            jax.ShapeDtypeStruct((b, CONV_PAD, gw), F32),
            jax.ShapeDtypeStruct((b, POOL_PAD, gw), F32),
        ],
        scratch_shapes=[
            pltpu.VMEM((nb, CONV_PAD + tl, gw), F32),
            pltpu.VMEM((SUBLANES - 1, CONV_PAD + tl - SUBLANES, gw), F32),
            pltpu.VMEM((nb, POOL_PAD + tl, gw), F32),
            pltpu.VMEM((SUBLANES - 1, POOL_PAD + tl - SUBLANES, gw), F32),
        ],
        compiler_params=_params(("parallel", "arbitrary")),
        name="projmix",
    )(x, g, w_in, conv_w, conv_vec, pool_wbd, pool_scale, *hists)


def _attn_body(*refs, layer, nb, tq, chunk, hist_valid, n_alias):
    sink_ref, u_ref, qg_ref, kg_ref = refs[:4]
    n_in = (6 if hist_valid else 4) + n_alias
    y_ref, kt_ref, vt_ref, kh_ref, vh_ref = refs[n_in:]
    li = pl.program_id(0)
    hd = HEAD_DIM
    rows = nb * tq
    ncq = tq // chunk
    kw = WINDOW + chunk

    @pl.when(li == 0)
    def _():
        if hist_valid:
            kh_ref[...] = refs[4][...]
            vh_ref[...] = refs[5][...]
        else:
            kh_ref[...] = jnp.zeros(kh_ref.shape, F32)
            vh_ref[...] = jnp.zeros(vh_ref.shape, F32)

    u = u_ref[...].reshape(rows, D_COLS)
    q = u[:, 0:Q_COLS]
    k = u[:, Q_COLS:Q_COLS + KV_COLS]
    v = u[:, Q_COLS + KV_COLS:D_COLS]
    inv = 1.0 / hd
    qms = _split_dot_right(q * q, _block_matrix(Q_COLS, hd, inv), 1)
    kms = _split_dot_right(k * k, _block_matrix(KV_COLS, hd, inv), 2)
    qn = (q * lax.rsqrt(qms + RMS_EPS)) * (qg_ref[...] * ATTN_SCALE)
    kn = (k * lax.rsqrt(kms + RMS_EPS)) * kg_ref[...]
    kcat = jnp.concatenate([kh_ref[...], kn.reshape(nb, tq, 2 * hd)], axis=1)
    vcat = jnp.concatenate([vh_ref[...], v.reshape(nb, tq, 2 * hd)], axis=1)
    ktail = kcat[:, tq:tq + WINDOW, :]
    vtail = vcat[:, tq:tq + WINDOW, :]
    if n_alias:
        kt_ref[...] = ktail
        vt_ref[...] = vtail
    else:
        for lyr in range(DEPTH):
            kt_ref[lyr] = ktail if lyr == layer else jnp.zeros(ktail.shape, F32)
            vt_ref[lyr] = vtail if lyr == layer else jnp.zeros(vtail.shape, F32)
    kh_ref[...] = ktail
    vh_ref[...] = vtail
    kcat_b = kcat.astype(BF16)
    vcat_b = vcat.astype(BF16)

    slot_hi = lax.broadcasted_iota(jnp.int32, (1, 2 * hd), 1) >= hd
    q_tiles = []
    for h in range(N_Q_HEADS):
        col = qn[:, (h // 2) * 2 * hd:(h // 2 + 1) * 2 * hd]
        g = h // (N_Q_HEADS // N_KV_HEADS)
        if h % 2 != g:
            col = pltpu.roll(col, hd, axis=1)
        q_tiles.append(jnp.where(slot_hi if g == 1 else jnp.logical_not(slot_hi), col, 0.0).astype(BF16))
    pairs = [(b, c) for c in range(ncq) for b in range(nb)]
    qs = jnp.stack([jnp.concatenate([t[b * tq + c * chunk:b * tq + (c + 1) * chunk] for t in q_tiles], axis=0)
                    for b, c in pairs])
    ks = jnp.stack([kcat_b[b, c * chunk:c * chunk + kw] for b, c in pairs])
    vs = jnp.stack([vcat_b[b, c * chunk:c * chunk + kw] for b, c in pairs])
    st = _bdot_nt(ks, qs)
    nq = N_Q_HEADS * chunk
    if not hist_valid:
        n_edge = min(ncq, WINDOW // chunk) * nb
        cpos = [c * chunk for _, c in pairs[:n_edge]]
        kpos = lax.broadcasted_iota(jnp.int32, (1, kw, nq), 1) + (li * tq - WINDOW)
        edge = jnp.concatenate([jnp.where(kpos + cp >= 0, st[i:i + 1], NEG_INF) for i, cp in enumerate(cpos)],
                               axis=0)
        st = jnp.concatenate([edge, st[n_edge:]], axis=0) if n_edge < len(pairs) else edge
    hlane = lax.broadcasted_iota(jnp.int32, (1, 1, nq), 2)
    sk = jnp.full((1, 1, nq), sink_ref[layer, N_Q_HEADS - 1], F32)
    for h in range(N_Q_HEADS - 2, -1, -1):
        sk = jnp.where(hlane < (h + 1) * chunk, sink_ref[layer, h], sk)
    m = jnp.maximum(jnp.max(st, axis=1, keepdims=True), sk)
    p = jnp.exp(st - m)
    den = jnp.sum(p, axis=1, keepdims=True) + jnp.exp(sk - m)
    pn = (p * (1.0 / den)).astype(BF16)
    o = _bdot_tn(pn, vs)
    lo = jnp.logical_not(slot_hi)
    for i, (b, c) in enumerate(pairs):
        oc = o[i]
        col0 = jnp.where(lo, oc[0:chunk], pltpu.roll(oc[chunk:2 * chunk], hd, axis=1))
        col1 = jnp.where(lo, pltpu.roll(oc[2 * chunk:3 * chunk], hd, axis=1), oc[3 * chunk:4 * chunk])
        y_ref[b, c * chunk:(c + 1) * chunk, :] = jnp.concatenate([col0, col1], axis=1).astype(BF16)


def _attn(ud, layer, qg, kg, sinks, chunk, caches=(), tails=()):
    b, l, _ = ud.shape
    tq = min(l, MIXER_ROWS)
    assert l % tq == 0 and tq % chunk == 0
    kvw = N_KV_HEADS * HEAD_DIM
    layer_spec = pl.BlockSpec((None, b, WINDOW, kvw), lambda j: (layer, 0, 0, 0))
    n_fixed = 4 + len(caches)
    return pl.pallas_call(
        functools.partial(_attn_body, layer=layer, nb=b, tq=tq, chunk=chunk, hist_valid=bool(caches),
                          n_alias=len(tails)),
        grid=(l // tq,),
        in_specs=[
            pl.BlockSpec(memory_space=pltpu.SMEM),
            pl.BlockSpec((b, tq, D_COLS), lambda j: (0, j, 0)),
            _layer_block((1, GROUP_WIDTH), layer, 1),
            _layer_block((1, kvw), layer, 1),
        ] + [layer_spec for _ in caches] + [pl.BlockSpec(memory_space=pl.ANY) for _ in tails],
        out_specs=[pl.BlockSpec((b, tq, GROUP_WIDTH), lambda j: (0, j, 0))] + 2 * [
            layer_spec if tails else pl.BlockSpec((DEPTH, b, WINDOW, kvw), lambda j: (0, 0, 0, 0))],
        out_shape=[
            jax.ShapeDtypeStruct((b, l, GROUP_WIDTH), BF16),
            jax.ShapeDtypeStruct((DEPTH, b, WINDOW, kvw), F32),
            jax.ShapeDtypeStruct((DEPTH, b, WINDOW, kvw), F32),
        ],
        input_output_aliases={n_fixed + i: 1 + i for i in range(len(tails))},
        scratch_shapes=[pltpu.VMEM((b, WINDOW, kvw), F32), pltpu.VMEM((b, WINDOW, kvw), F32)],
        compiler_params=_params(("arbitrary",)),
        name="attn",
    )(sinks, ud, qg, kg, *caches, *tails)


def _rwkv_body(*refs, nb, tb, chunk, n_cast, has_past, layer, n_alias):
    u_ref, mu_ref, wl_ref, vec_ref = refs[:4]
    n_in = 6 if has_past else 4
    cast_in = refs[n_in:n_in + n_cast]
    n_out0 = n_in + n_cast + n_alias
    y_ref, sn_ref = refs[n_out0:n_out0 + 2]
    cast_out = refs[n_out0 + 2:n_out0 + 2 + n_cast]
    prev_ref, s_ref, yacc_ref = refs[n_out0 + 2 + n_cast:]
    li = pl.program_id(0)
    for src, dst in zip(cast_in, cast_out):
        dst[...] = src[...].astype(BF16)
    gw = GROUP_WIDTH
    hd = RWKV_HEAD
    nh = RWKV_HEADS
    rows = nb * tb
    nc = tb // chunk

    @pl.when(li == 0)
    def _():
        if not has_past:
            prev_ref[...] = jnp.zeros(prev_ref.shape, F32)
            s_ref[...] = jnp.zeros(s_ref.shape, F32)
            return
        sp_ref, s0_ref = refs[4:6]
        prev_ref[...] = sp_ref[...]
        zero = jnp.zeros((nb, hd, hd), F32)
        for h in range(nh):
            sh = s0_ref[:, h]
            s_ref[h * nb:(h + 1) * nb] = jnp.concatenate([sh, zero] if h % 2 == 0 else [zero, sh], axis=-1)

    row = lax.broadcasted_iota(jnp.int32, (tb, 1), 0)
    mu = mu_ref[...]
    xs_parts = []
    for b in range(nb):
        ub = u_ref[b]
        prev = jnp.where(row == 0, prev_ref[b], pltpu.roll(ub, 1, axis=0))
        prev_ref[b] = ub[tb - 1:tb, :]
        xs_parts.append(ub + mu * (prev - ub))
    xs = jnp.concatenate(xs_parts, axis=0)
    r = xs[:, 0:gw]
    k = xs[:, gw:2 * gw]
    v = xs[:, 2 * gw:3 * gw]
    lat = xs[:, 3 * gw:B_COLS]
    lane_lat = lax.broadcasted_iota(jnp.int32, (1, B_COLS - 3 * gw), 1)
    act = jnp.where(lane_lat < DECAY_RANK, jnp.tanh(lat),
                    jnp.where(lane_lat < DECAY_RANK + AAA_RANK, lat, _sigmoid(lat)))
    lo = _dot(act.astype(BF16), wl_ref[...])
    w0 = vec_ref[0:1, :]
    a0 = vec_ref[1:2, :]
    k_k = vec_ref[2:3, :]
    k_a = vec_ref[3:4, :]
    r_k = vec_ref[4:5, :]
    gn_g = vec_ref[5:6, :]
    gn_b = vec_ref[6:7, :]
    z = -(w0 + lo[:, 0:gw])
    softplus = jnp.maximum(z, 0.0) + jnp.log(1.0 + jnp.exp(-jnp.abs(z)))
    logw = -jnp.exp(-softplus - 0.5)
    a_rate = _sigmoid(a0 + lo[:, gw:2 * gw])
    gate = lo[:, 2 * gw:3 * gw]
    ones_blk = _block_matrix(gw, hd, 1.0)
    kk = k * k_k
    kk = kk * lax.rsqrt(jnp.maximum(_split_dot_right(kk * kk, ones_blk, 1), 1e-24))
    k_mod = k * (1.0 + (a_rate - 1.0) * k_a)
    b_v = kk * a_rate
    bonus = _split_dot_right(r * k_mod * r_k, ones_blk, 1) * v

    grp = min(rows, 256)
    gi = lax.broadcasted_iota(jnp.int32, (grp, grp), 0)
    gj = lax.broadcasted_iota(jnp.int32, (grp, grp), 1)
    csh = int(math.log2(chunk))
    tri = jnp.where(jnp.logical_and(lax.shift_right_logical(gi, csh) == lax.shift_right_logical(gj, csh),
                                    gj <= gi), 1.0, 0.0).astype(BF16)
    cum = jnp.concatenate([_split_dot_left(tri, logw[g0:g0 + grp], 2) for g0 in range(0, rows, grp)], axis=0)
    cum3 = cum.reshape(nb * nc, chunk, gw)
    cum_c = cum3[:, chunk - 1:chunk, :]
    e_end = jnp.exp(cum_c - cum3).reshape(rows, gw)
    w_c = jnp.exp(cum_c)
    e_neg = jnp.exp(-cum)
    dense = dict(
        a=-kk * jnp.exp(cum - logw),
        r=r * jnp.exp(cum),
        bt=b_v * e_neg,
        kt=k_mod * e_neg,
        bh=b_v * e_end,
        kh=k_mod * e_end,
    )
    slot_hi = lax.broadcasted_iota(jnp.int32, (1, 2 * hd), 1) >= hd

    def head_tile(x, h, own_slot):
        col = x[:, (h // 2) * 2 * hd:(h // 2 + 1) * 2 * hd]
        keep = slot_hi if (h % 2 == 1) == own_slot else jnp.logical_not(slot_hi)
        if x.dtype == BF16:
            return col * jnp.where(keep, 1.0, 0.0).astype(BF16)
        return jnp.where(keep, col, 0.0)

    v_sw = jnp.concatenate([pltpu.roll(v[:, j * 2 * hd:(j + 1) * 2 * hd], hd, axis=1) for j in range(nh // 2)],
                           axis=1)
    tiles = {"r": [head_tile(dense["r"], h, True) for h in range(nh)]}
    dense_b = {name: dense[name].astype(BF16) for name in ("a", "bt", "kt", "bh", "kh")}
    tiles_b = {name: [head_tile(dense_b[name], h, True) for h in range(nh)] for name in ("a", "bh", "kh")}
    for name in ("bt", "kt"):
        tiles_b[name] = [dense_b[name][:, (h // 2) * 2 * hd:(h // 2 + 1) * 2 * hd] for h in range(nh)]
    v_sw_b = v_sw.astype(BF16)
    v_tiles = [head_tile(v_sw_b, h, False) for h in range(nh)]

    def blocks(per_head, c):
        return jnp.stack([per_head[h][b * tb + c * chunk:b * tb + (c + 1) * chunk]
                          for h in range(nh) for b in range(nb)])

    n = nh * nb
    ri = lax.broadcasted_iota(jnp.int32, (2 * chunk, 2 * chunk), 0)
    ci = jnp.bitwise_and(lax.broadcasted_iota(jnp.int32, (2 * chunk, 2 * chunk), 1), chunk - 1)
    gmask = ci < jnp.bitwise_and(ri, chunk - 1) + lax.shift_right_logical(ri, csh)
    zeros_c = jnp.zeros((n, chunk, 2 * hd), BF16)
    n_sq = int(math.log2(chunk))
    for c in range(nc):
        a_b = blocks(tiles_b["a"], c)
        r_f = blocks(tiles["r"], c)
        v_b = blocks(v_tiles, c)
        ar = jnp.concatenate([a_b, r_f.astype(BF16)], axis=1)
        bk = jnp.concatenate([blocks(tiles_b["bt"], c), blocks(tiles_b["kt"], c)], axis=1)
        bhkh = jnp.concatenate([blocks(tiles_b["bh"], c), blocks(tiles_b["kh"], c)], axis=1)
        g = jnp.where(gmask, _bdot_nt(ar, bk), 0.0)
        g_top = g[:, :chunk, :]
        g_bot = g[:, chunk:, :].astype(BF16)
        w = a_b.astype(F32) + _bdot(g_top.astype(BF16), jnp.concatenate([zeros_c, v_b], axis=1))
        p = g_top[:, :, :chunk]
        for i in range(n_sq):
            pb = p.astype(BF16)
            if i + 1 < n_sq:
                res = _bdot(pb, jnp.concatenate([w.astype(BF16), pb], axis=2))
                w = w + res[:, :, :2 * hd]
                p = res[:, :, 2 * hd:]
            else:
                hc = chunk // 2
                low = w[:, hc:, :] + _bdot(pb[:, hc:, :hc], w[:, :hc, :].astype(BF16))
                w = jnp.concatenate([w[:, :hc, :], low], axis=1)
        xv = jnp.concatenate([w.astype(BF16), v_b], axis=1)
        ry = _bdot(g_bot, xv)
        mp = _bdot_tn(xv, bhkh)
        s_old = s_ref[...]
        s_b = s_old.astype(BF16)
        y_nt = _bdot_nt((ry + r_f).astype(BF16), s_b)
        wc = jnp.stack([w_c[b * nc + c][:, (h // 2) * 2 * hd:(h // 2 + 1) * 2 * hd]
                        for h in range(nh) for b in range(nb)])
        psi = jnp.concatenate(
            [mp[h * nb:(h + 1) * nb, (1 - h % 2) * hd:(2 - h % 2) * hd, :] for h in range(nh)], axis=0)
        s_ref[...] = s_old * wc + _bdot(s_b, mp.astype(BF16)) + psi
        for b in range(nb):
            ys = [y_nt[h * nb + b] + ry[h * nb + b][:, (1 - h % 2) * hd:(2 - h % 2) * hd] for h in range(nh)]
            yacc_ref[b * tb + c * chunk:b * tb + (c + 1) * chunk, :] = jnp.concatenate(ys, axis=1)

    y = yacc_ref[...]
    avg_blk = _block_matrix(gw, hd, 1.0 / hd)
    m = _split_dot_right(y, avg_blk, 1)
    d = y - m
    var = _split_dot_right(d * d, avg_blk, 1)
    yn = d * lax.rsqrt(var + GN_EPS) * gn_g + gn_b
    out = ((yn + bonus) * gate).astype(BF16)
    for b in range(nb):
        y_ref[b] = out[b * tb:(b + 1) * tb]

    @pl.when(li == pl.num_programs(0) - 1)
    def _():
        for h in range(nh):
            s_h = s_ref[h * nb:(h + 1) * nb, :, (h % 2) * hd:(h % 2 + 1) * hd]
            if n_alias:
                sn_ref[:, h] = s_h
            else:
                for lyr in range(DEPTH):
                    sn_ref[lyr, :, h] = s_h if lyr == layer else jnp.zeros(s_h.shape, F32)


def _rwkv(ub, layer, mu, wl, vec, chunk, past=(), to_cast=(), state_buf=()):
    b, l, _ = ub.shape
    tb = min(l, MIXER_ROWS)
    assert l % tb == 0 and tb % chunk == 0
    steps = l // tb
    hd = RWKV_HEAD
    n = RWKV_HEADS * b
    sshape = (b, RWKV_HEADS, hd, hd)
    cast_specs = []
    for wt in to_cast:
        depth, rows, cols = wt.shape
        slab = rows // steps
        assert rows % steps == 0 and slab % (2 * SUBLANES) == 0
        cast_specs.append(pl.BlockSpec((depth, slab, cols), lambda j: (0, j, 0)))
    outs = pl.pallas_call(
        functools.partial(_rwkv_body, nb=b, tb=tb, chunk=chunk, n_cast=len(to_cast), has_past=bool(past),
                          layer=layer, n_alias=len(state_buf)),
        grid=(steps,),
        in_specs=[
            pl.BlockSpec((b, tb, B_COLS), lambda j: (0, j, 0)),
            _layer_block((1, B_COLS), layer, 1),
            _layer_block((128, 3 * GROUP_WIDTH), layer, 1),
            _layer_block((8, GROUP_WIDTH), layer, 1),
        ] + ([pl.BlockSpec((b, 1, B_COLS), lambda j: (0, 0, 0)),
              pl.BlockSpec(sshape, lambda j: (0, 0, 0, 0))] if past else []) + cast_specs
        + [pl.BlockSpec(memory_space=pl.ANY) for _ in state_buf],
        out_specs=[
            pl.BlockSpec((b, tb, GROUP_WIDTH), lambda j: (0, j, 0)),
            pl.BlockSpec((None,) + sshape, lambda j: (layer, 0, 0, 0, 0)) if state_buf
            else pl.BlockSpec((DEPTH,) + sshape, lambda j: (0, 0, 0, 0, 0)),
        ] + cast_specs,
        input_output_aliases={4 + len(past) + len(to_cast) + i: 1 for i in range(len(state_buf))},
        out_shape=[
            jax.ShapeDtypeStruct((b, l, GROUP_WIDTH), BF16),
            jax.ShapeDtypeStruct((DEPTH,) + sshape, F32),
        ] + [jax.ShapeDtypeStruct(wt.shape, BF16) for wt in to_cast],
        scratch_shapes=[
            pltpu.VMEM((b, 1, B_COLS), F32),
            pltpu.VMEM((n, hd, 2 * hd), F32),
            pltpu.VMEM((b * tb, GROUP_WIDTH), F32),
        ],
        compiler_params=_params(("arbitrary",)),
        name="rwkv",
    )(ub, mu, wl, vec, *past, *to_cast, *state_buf)
    return outs[0], outs[1], tuple(outs[2:])


def _outffn_body(x_ref, ya_ref, yb_ref, yc_ref, yd_ref, g_ref, wo_in, wg_in, wu_in, wd_in, o_ref,
                 *scratch, layer, staged):
    if staged:
        wo_ref, wg_ref, wu_ref, wd_ref, sems = scratch
        pairs = ((wo_in, wo_ref), (wg_in, wg_ref), (wu_in, wu_ref), (wd_in, wd_ref))
        copies = [pltpu.make_async_copy(src.at[layer], dst, sems.at[k]) for k, (src, dst) in enumerate(pairs)]
        for copy in copies:
            copy.start()
        wait = lambda *ks: [copies[k].wait() for k in ks]
    else:
        wo_ref, wg_ref, wu_ref, wd_ref = wo_in, wg_in, wu_in, wd_in
        wait = lambda *ks: None
    ycat = jnp.concatenate([ya_ref[...], yb_ref[...], yc_ref[...], yd_ref[...]], axis=-1)
    wait(0)
    x1 = x_ref[...] + _dot(ycat, wo_ref[...])
    ms = jnp.mean(x1 * x1, axis=-1, keepdims=True)
    hn = ((x1 * lax.rsqrt(ms + RMS_EPS)) * g_ref[...]).astype(BF16)
    wait(1, 2)
    hg = _dot(hn, wg_ref[...])
    hu = _dot(hn, wu_ref[...])
    act = (hg * _sigmoid(hg) * hu).astype(BF16)
    wait(3)
    o_ref[...] = x1 + _dot(act, wd_ref[...])


def _outffn(x2d, ya, yb, yc, yd, layer, wo, g, wg, wu, wd):
    t = x2d.shape[0]
    tm = min(t, MATMUL_ROWS)
    assert t % tm == 0
    staged = t == tm
    row = lambda w: pl.BlockSpec((tm, w), lambda i: (i, 0))
    weight_shapes = ((4 * GROUP_WIDTH, D_MODEL), (D_MODEL, D_FF), (D_MODEL, D_FF), (D_FF, D_MODEL))
    if staged:
        weight_specs = [pl.BlockSpec(memory_space=pl.ANY) for _ in weight_shapes]
        scratch = [pltpu.VMEM(s, BF16) for s in weight_shapes] + [pltpu.SemaphoreType.DMA((len(weight_shapes),))]
    else:
        weight_specs = [_layer_block(s, layer, 1) for s in weight_shapes]
        scratch = []
    return pl.pallas_call(
        functools.partial(_outffn_body, layer=layer, staged=staged),
        grid=(t // tm,),
        in_specs=[
            row(D_MODEL), row(GROUP_WIDTH), row(GROUP_WIDTH), row(GROUP_WIDTH), row(GROUP_WIDTH),
            _layer_block((1, D_MODEL), layer, 1),
        ] + weight_specs,
        out_specs=row(D_MODEL),
        out_shape=jax.ShapeDtypeStruct((t, D_MODEL), F32),
        scratch_shapes=scratch,
        compiler_params=_params(("parallel",)),
        name="outffn",
    )(x2d, ya, yb, yc, yd, g, wo, wg, wu, wd)


_LATE_WEIGHTS = ('w_out', 'ffn_w_gate', 'ffn_w_up', 'ffn_w_down')


def _rows8(vectors):
    stacked = jnp.stack(vectors, axis=1)
    return jnp.pad(stacked, ((0, 0), (0, SUBLANES - stacked.shape[1]), (0, 0)))


def _stacked_weights(p):
    gw = GROUP_WIDTH
    depth = p['w_in'].shape[0]
    place = lambda a, before, after: jnp.pad(a, ((0, 0), (0, 0), (before, after)))
    wl = jnp.concatenate([place(p['rwkv_w2'], 0, 2 * gw), place(p['rwkv_a2'], gw, gw),
                          place(p['rwkv_g2'], 2 * gw, 0)], axis=1)
    pc = gw // len(POOL_WINDOWS)
    wbd = jnp.concatenate([place(p['pool_w'][:, g], g * pc, gw - (g + 1) * pc)
                           for g in range(len(POOL_WINDOWS))], axis=1)
    return dict(
        norm_mix_g=p['norm_mix_g'][:, None, :],
        w_in=p['w_in'].astype(BF16),
        conv_w=jnp.pad(p['conv_w'], ((0, 0), (0, CONV_PAD - CONV_WIDTH), (0, 0))),
        conv_vec=_rows8([p['conv_b'], p['conv_ln_g'], p['conv_ln_b']]),
        rwkv_mu=p['rwkv_mu'][:, None, :],
        rwkv_wl=wl.astype(BF16),
        rwkv_vec=_rows8([p['rwkv_w0'], p['rwkv_a0'], p['rwkv_k_k'], p['rwkv_k_a'],
                         p['rwkv_r_k'].reshape(depth, gw), p['rwkv_gn_g'], p['rwkv_gn_b']]),
        pool_wbd=wbd.astype(BF16),
        pool_scale=p['pool_scale'][:, None, :],
        attn_qg=jnp.tile(p['attn_q_norm'], (1, N_Q_HEADS))[:, None, :],
        attn_kg=jnp.tile(p['attn_k_norm'], (1, N_KV_HEADS))[:, None, :],
        attn_sinks=p['attn_sinks'],
        norm_ffn_g=p['norm_ffn_g'][:, None, :],
        pending_casts={name: p[name] for name in _LATE_WEIGHTS},
    )


def _trunk(x, conv_hist, rwkv_state, shift_prev, pool_hist, k_cache, v_cache, pos0, chunk, w):
    b, l, _ = x.shape
    has_past = conv_hist is not None
    kvw = N_KV_HEADS * HEAD_DIM
    new = [[] for _ in range(3)]
    kv_caches = kv_tails = state_buf = ()
    if has_past:
        kv_caches = (k_cache.reshape(DEPTH, b, WINDOW, kvw), v_cache.reshape(DEPTH, b, WINDOW, kvw))
    for li in range(DEPTH):
        local_hists = rwkv_past = ()
        if has_past:
            local_hists = (jnp.pad(conv_hist[li], ((0, 0), (CONV_PAD - CONV_HIST, 0), (0, 0))),
                           jnp.pad(pool_hist[li], ((0, 0), (POOL_PAD - POOL_HIST, 0), (0, 0))))
            rwkv_past = (shift_prev[li][:, None, :], rwkv_state[li])
        ub, ud, ya, yc, conv_tail, pool_tail = _projmix(
            x, li, w['norm_mix_g'], w['w_in'], w['conv_w'], w['conv_vec'],
            w['pool_wbd'], w['pool_scale'], pos0, local_hists)
        pending = w.pop('pending_casts', {})
        yb, s_new, cast = _rwkv(ub, li, w['rwkv_mu'], w['rwkv_wl'], w['rwkv_vec'], min(CHUNK, l),
                                rwkv_past, tuple(pending.values()), state_buf)
        state_buf = (s_new,)
        w.update(zip(pending.keys(), cast))
        yd, *kv_tails = _attn(ud, li, w['attn_qg'], w['attn_kg'], w['attn_sinks'], chunk, kv_caches,
                              tuple(kv_tails))
        flat = lambda y: y.reshape(b * l, GROUP_WIDTH)
        x = _outffn(x.reshape(b * l, D_MODEL), flat(ya), flat(yb), flat(yc), flat(yd), li, w['w_out'],
                    w['norm_ffn_g'], w['ffn_w_gate'], w['ffn_w_up'], w['ffn_w_down']
                    ).reshape(b, l, D_MODEL)
        new[0].append(conv_tail[:, CONV_PAD - CONV_HIST:, :])
        new[1].append(ub[:, l - 1, :])
        new[2].append(pool_tail[:, POOL_PAD - POOL_HIST:, :])
    kv_new = tuple(t.reshape(DEPTH, b, WINDOW, N_KV_HEADS, HEAD_DIM) for t in kv_tails)
    conv_new, shift_new, pool_new = (jnp.stack(n) for n in new)
    return x, (conv_new, state_buf[0], shift_new, pool_new) + kv_new


def kernel(x_prompt, x_sample, cache_conv, state_rwkv, state_rwkv_shift, cache_pool, cache_k, cache_v, norm_mix_g, w_in, conv_w, conv_b, conv_ln_g, conv_ln_b, rwkv_mu, rwkv_w0, rwkv_w2, rwkv_a0, rwkv_a2, rwkv_g2, rwkv_k_k, rwkv_k_a, rwkv_r_k, rwkv_gn_g, rwkv_gn_b, pool_w, pool_scale, attn_q_norm, attn_k_norm, attn_sinks, w_out, norm_ffn_g, ffn_w_gate, ffn_w_up, ffn_w_down):
    w = _stacked_weights(dict(
        norm_mix_g=norm_mix_g, w_in=w_in, conv_w=conv_w, conv_b=conv_b, conv_ln_g=conv_ln_g,
        conv_ln_b=conv_ln_b, rwkv_mu=rwkv_mu, rwkv_w0=rwkv_w0, rwkv_w2=rwkv_w2, rwkv_a0=rwkv_a0,
        rwkv_a2=rwkv_a2, rwkv_g2=rwkv_g2, rwkv_k_k=rwkv_k_k, rwkv_k_a=rwkv_k_a, rwkv_r_k=rwkv_r_k,
        rwkv_gn_g=rwkv_gn_g, rwkv_gn_b=rwkv_gn_b, pool_w=pool_w, pool_scale=pool_scale,
        attn_q_norm=attn_q_norm, attn_k_norm=attn_k_norm, attn_sinks=attn_sinks, w_out=w_out,
        norm_ffn_g=norm_ffn_g, ffn_w_gate=ffn_w_gate, ffn_w_up=ffn_w_up, ffn_w_down=ffn_w_down))
    y_p, (conv_p, rwkv_p, shift_p, pool_p, k_p, v_p) = _trunk(
        x_prompt, None, None, None, None, None, None, 0, CHUNK, w)
    y_s, (conv_s, rwkv_s, shift_s, pool_s, k_s, v_s) = _trunk(
        x_sample, cache_conv, state_rwkv, state_rwkv_shift, cache_pool, cache_k, cache_v,
        PAST_LEN, x_sample.shape[1], w)
    return (y_p, y_s, conv_p, conv_s, rwkv_p, rwkv_s, shift_p, shift_s,
            pool_p, pool_s, k_p, k_s, v_p, v_s)
```

```python
import functools
import math

import jax
import jax.numpy as jnp
from jax import lax
from jax.experimental import pallas as pl
from jax.experimental.pallas import tpu as pltpu

F32 = jnp.float32
BF16 = jnp.bfloat16

D_MODEL = 1024
DEPTH = 2
PAST_LEN = 1024
CHUNK = 64
GROUP_WIDTH = 256
CONV_WIDTH = 31
CONV_HIST = CONV_WIDTH - 1
CONV_PAD = 32
RWKV_HEAD = 64
RWKV_HEADS = 4
POOL_WINDOWS = (2, 4, 8, 16)
POOL_HIST = 15
POOL_PAD = 16
HEAD_DIM = 64
N_Q_HEADS = 4
N_KV_HEADS = 2
WINDOW = 128
D_FF = 2816
A_COLS = 512
B_COLS = 896
C_COLS = 256
D_COLS = 512
IN_COLS = A_COLS + B_COLS + C_COLS + D_COLS
RMS_EPS = 1e-6
LN_EPS = 1e-5
GN_EPS = 64e-5
ATTN_SCALE = HEAD_DIM ** -0.5
NEG_INF = -1e30

DECAY_RANK = 32
AAA_RANK = 32
Q_COLS = N_Q_HEADS * HEAD_DIM
KV_COLS = N_KV_HEADS * HEAD_DIM

VMEM_LIMIT_BYTES = 56 * 1024 * 1024
SUBLANES = 8
MATMUL_ROWS = 512
MIXER_ROWS = 256
LOCAL_ROWS = 128


def _dot(a, b):
    return jnp.dot(a, b, preferred_element_type=F32)


def _bdot(a, b):
    return lax.dot_general(a, b, (((2,), (1,)), ((0,), (0,))), preferred_element_type=F32)


def _bdot_nt(a, b):
    return lax.dot_general(a, b, (((2,), (2,)), ((0,), (0,))), preferred_element_type=F32)


def _bdot_tn(a, b):
    return lax.dot_general(a, b, (((1,), (1,)), ((0,), (0,))), preferred_element_type=F32)


def _sigmoid(x):
    return 1.0 / (1.0 + jnp.exp(-x))


def _split_dot_right(x, m_bf16, terms):
    acc = None
    rem = x
    for i in range(terms):
        hi = rem.astype(BF16)
        d = _dot(hi, m_bf16)
        acc = d if acc is None else acc + d
        if i + 1 < terms:
            rem = rem - hi.astype(F32)
    return acc


def _split_dot_left(m_bf16, x, terms):
    acc = None
    rem = x
    for i in range(terms):
        hi = rem.astype(BF16)
        d = _dot(m_bf16, hi)
        acc = d if acc is None else acc + d
        if i + 1 < terms:
            rem = rem - hi.astype(F32)
    return acc


def _block_matrix(n, blk, value):
    sh = int(math.log2(blk))
    r = lax.shift_right_logical(lax.broadcasted_iota(jnp.int32, (n, n), 0), sh)
    c = lax.shift_right_logical(lax.broadcasted_iota(jnp.int32, (n, n), 1), sh)
    return jnp.where(r == c, value, 0.0).astype(BF16)


def _params(sem):
    return pltpu.CompilerParams(dimension_semantics=sem, vmem_limit_bytes=VMEM_LIMIT_BYTES)


def _layer_block(shape, layer, n_grid):
    zeros = (0,) * len(shape)
    if n_grid == 1:
        index_map = lambda i: (layer,) + zeros
    else:
        index_map = lambda i, j: (layer,) + zeros
    return pl.BlockSpec((None,) + tuple(shape), index_map, pipeline_mode=pl.Buffered(1))


def _phase_copies(ext_ref, sh_ref):
    n = sh_ref.shape[1]
    for s in range(1, SUBLANES):
        sh_ref[s - 1] = ext_ref[s:s + n, :]


def _shifted_rows(ext_ref, sh_ref, start, rb):
    a, s = divmod(start, SUBLANES)
    base = a * SUBLANES
    return ext_ref[base:base + rb, :] if s == 0 else sh_ref[s - 1, base:base + rb, :]


def _conv_rows(ext_ref, sh_ref, w_ref, vec_ref, r0, rb):
    off = CONV_PAD - CONV_HIST
    acc = jnp.zeros((rb, GROUP_WIDTH), F32)
    for j in range(CONV_WIDTH):
        acc = acc + w_ref[j:j + 1, :] * _shifted_rows(ext_ref, sh_ref, r0 + off + j, rb)
    acc = acc + vec_ref[0:1, :]
    mu = jnp.mean(acc, axis=-1, keepdims=True)
    d = acc - mu
    var = jnp.mean(d * d, axis=-1, keepdims=True)
    yn = d * lax.rsqrt(var + LN_EPS) * vec_ref[1:2, :] + vec_ref[2:3, :]
    return yn * _sigmoid(yn)


def _pool_rows(ext_ref, sh_ref, wbd_ref, sc_ref, r0, rb, pos_start):
    base = POOL_PAD + r0
    sums = []
    acc = None
    for i in range(max(POOL_WINDOWS)):
        sh = _shifted_rows(ext_ref, sh_ref, base - i, rb)
        acc = sh if acc is None else acc + sh
        if i + 1 in POOL_WINDOWS:
            sums.append(acc)
    pos = pos_start + r0 + lax.broadcasted_iota(jnp.int32, (rb, 1), 0)
    means = [s / jnp.minimum(w, pos + 1).astype(F32) for s, w in zip(sums, POOL_WINDOWS)]
    lane = lax.broadcasted_iota(jnp.int32, (1, GROUP_WIDTH), 1)
    pc = GROUP_WIDTH // len(POOL_WINDOWS)
    mean = means[-1]
    for g in range(len(POOL_WINDOWS) - 2, -1, -1):
        mean = jnp.where(lane < (g + 1) * pc, means[g], mean)
    d = mean - ext_ref[base:base + rb, :]
    return _dot(d.astype(BF16), wbd_ref[...]) * sc_ref[...]


def _projmix_body(*refs, nb, tl, rb, pos0, has_past):
    x_ref, g_ref, w_ref, cw_ref, cvec_ref, pwbd_ref, psc_ref = refs[:7]
    n_in = 9 if has_past else 7
    (ub_ref, ud_ref, ya_ref, yc_ref, ctail_ref, ptail_ref,
     cext_ref, csh_ref, pext_ref, psh_ref) = refs[n_in:]
    li = pl.program_id(1)

    @pl.when(li == 0)
    def _():
        if has_past:
            cext_ref[:, 0:CONV_PAD, :] = refs[7][...]
            pext_ref[:, 0:POOL_PAD, :] = refs[8][...]
        else:
            cext_ref[:, 0:CONV_PAD, :] = jnp.zeros((nb, CONV_PAD, GROUP_WIDTH), F32)
            pext_ref[:, 0:POOL_PAD, :] = jnp.zeros((nb, POOL_PAD, GROUP_WIDTH), F32)

    x = x_ref[...].reshape(nb * tl, D_MODEL)
    ms = jnp.mean(x * x, axis=-1, keepdims=True)
    xn = ((x * lax.rsqrt(ms + RMS_EPS)) * g_ref[...]).astype(BF16)
    b0 = A_COLS
    c0 = A_COLS + B_COLS
    d0 = c0 + C_COLS
    ua = _dot(xn, w_ref[:, 0:A_COLS])
    uc = _dot(xn, w_ref[:, c0:d0])
    ub_ref[...] = _dot(xn, w_ref[:, b0:c0]).reshape(nb, tl, B_COLS)
    ud_ref[...] = _dot(xn, w_ref[:, d0:IN_COLS]).reshape(nb, tl, D_COLS)
    glu = ua[:, :GROUP_WIDTH] * _sigmoid(ua[:, GROUP_WIDTH:])
    for b in range(nb):
        cext = cext_ref.at[b]
        pext = pext_ref.at[b]
        cext[CONV_PAD:CONV_PAD + tl, :] = glu[b * tl:(b + 1) * tl]
        pext[POOL_PAD:POOL_PAD + tl, :] = uc[b * tl:(b + 1) * tl]
        _phase_copies(cext, csh_ref)
        _phase_copies(pext, psh_ref)
        for r0 in range(0, tl, rb):
            ya_ref[b, r0:r0 + rb, :] = _conv_rows(cext, csh_ref, cw_ref, cvec_ref, r0, rb).astype(BF16)
            yc_ref[b, r0:r0 + rb, :] = _pool_rows(pext, psh_ref, pwbd_ref, psc_ref, r0, rb,
                                                  pos0 + li * tl).astype(BF16)
        ctail = cext[tl:tl + CONV_PAD, :]
        ptail = pext[tl:tl + POOL_PAD, :]
        ctail_ref[b] = ctail
        ptail_ref[b] = ptail
        cext[0:CONV_PAD, :] = ctail
        pext[0:POOL_PAD, :] = ptail


def _projmix(x, layer, g, w_in, conv_w, conv_vec, pool_wbd, pool_scale, pos0, hists=()):
    b, l, _ = x.shape
    nb = b if l <= 64 else 1
    tl = min(l, MATMUL_ROWS)
    rb = min(tl, LOCAL_ROWS)
    assert l % tl == 0 and tl % rb == 0 and tl >= CONV_PAD and b % nb == 0
    gw = GROUP_WIDTH
    tile = lambda w: pl.BlockSpec((nb, tl, w), lambda i, j: (i, j, 0))
    per_seq = lambda r: pl.BlockSpec((nb, r, gw), lambda i, j: (i, 0, 0))
    return pl.pallas_call(
        functools.partial(_projmix_body, nb=nb, tl=tl, rb=rb, pos0=pos0, has_past=bool(hists)),
        grid=(b // nb, l // tl),
        in_specs=[
            tile(D_MODEL),
            _layer_block((1, D_MODEL), layer, 2),
            _layer_block((D_MODEL, IN_COLS), layer, 2),
            _layer_block((CONV_PAD, gw), layer, 2), _layer_block((8, gw), layer, 2),
            _layer_block((gw, gw), layer, 2), _layer_block((1, gw), layer, 2),
        ] + ([per_seq(CONV_PAD), per_seq(POOL_PAD)] if hists else []),
        out_specs=[tile(B_COLS), tile(D_COLS), tile(gw), tile(gw), per_seq(CONV_PAD), per_seq(POOL_PAD)],
        out_shape=[
            jax.ShapeDtypeStruct((b, l, B_COLS), F32),
            jax.ShapeDtypeStruct((b, l, D_COLS), F32),
            jax.ShapeDtypeStruct((b, l, gw), BF16),
            jax.ShapeDtypeStruct((b, l, gw), BF16),
            jax.ShapeDtypeStruct((b, CONV_PAD, gw), F32),
            jax.ShapeDtypeStruct((b, POOL_PAD, gw), F32),
        ],
        scratch_shapes=[
            pltpu.VMEM((nb, CONV_PAD + tl, gw), F32),
            pltpu.VMEM((SUBLANES - 1, CONV_PAD + tl - SUBLANES, gw), F32),
            pltpu.VMEM((nb, POOL_PAD + tl, gw), F32),
            pltpu.VMEM((SUBLANES - 1, POOL_PAD + tl - SUBLANES, gw), F32),
        ],
        compiler_params=_params(("parallel", "arbitrary")),
        name="projmix",
    )(x, g, w_in, conv_w, conv_vec, pool_wbd, pool_scale, *hists)


def _attn_body(*refs, layer, nb, tq, chunk, hist_valid, n_alias):
    sink_ref, u_ref, qg_ref, kg_ref = refs[:4]
    n_in = (6 if hist_valid else 4) + n_alias
    y_ref, kt_ref, vt_ref, kh_ref, vh_ref = refs[n_in:]
    li = pl.program_id(0)
    hd = HEAD_DIM
    rows = nb * tq
    ncq = tq // chunk
    kw = WINDOW + chunk

    @pl.when(li == 0)
    def _():
        if hist_valid:
            kh_ref[...] = refs[4][...]
            vh_ref[...] = refs[5][...]
        else:
            kh_ref[...] = jnp.zeros(kh_ref.shape, F32)
            vh_ref[...] = jnp.zeros(vh_ref.shape, F32)

    u = u_ref[...].reshape(rows, D_COLS)
    q = u[:, 0:Q_COLS]
    k = u[:, Q_COLS:Q_COLS + KV_COLS]
    v = u[:, Q_COLS + KV_COLS:D_COLS]
    inv = 1.0 / hd
    qms = _split_dot_right(q * q, _block_matrix(Q_COLS, hd, inv), 1)
    kms = _split_dot_right(k * k, _block_matrix(KV_COLS, hd, inv), 2)
    qn = (q * lax.rsqrt(qms + RMS_EPS)) * (qg_ref[...] * ATTN_SCALE)
    kn = (k * lax.rsqrt(kms + RMS_EPS)) * kg_ref[...]
    kcat = jnp.concatenate([kh_ref[...], kn.reshape(nb, tq, 2 * hd)], axis=1)
    vcat = jnp.concatenate([vh_ref[...], v.reshape(nb, tq, 2 * hd)], axis=1)
    ktail = kcat[:, tq:tq + WINDOW, :]
    vtail = vcat[:, tq:tq + WINDOW, :]
    if n_alias:
        kt_ref[...] = ktail
        vt_ref[...] = vtail
    else:
        for lyr in range(DEPTH):
            kt_ref[lyr] = ktail if lyr == layer else jnp.zeros(ktail.shape, F32)
            vt_ref[lyr] = vtail if lyr == layer else jnp.zeros(vtail.shape, F32)
    kh_ref[...] = ktail
    vh_ref[...] = vtail
    kcat_b = kcat.astype(BF16)
    vcat_b = vcat.astype(BF16)

    slot_hi = lax.broadcasted_iota(jnp.int32, (1, 2 * hd), 1) >= hd
    q_tiles = []
    for h in range(N_Q_HEADS):
        col = qn[:, (h // 2) * 2 * hd:(h // 2 + 1) * 2 * hd]
        g = h // (N_Q_HEADS // N_KV_HEADS)
        if h % 2 != g:
            col = pltpu.roll(col, hd, axis=1)
        q_tiles.append(jnp.where(slot_hi if g == 1 else jnp.logical_not(slot_hi), col, 0.0).astype(BF16))
    pairs = [(b, c) for c in range(ncq) for b in range(nb)]
    qs = jnp.stack([jnp.concatenate([t[b * tq + c * chunk:b * tq + (c + 1) * chunk] for t in q_tiles], axis=0)
                    for b, c in pairs])
    ks = jnp.stack([kcat_b[b, c * chunk:c * chunk + kw] for b, c in pairs])
    vs = jnp.stack([vcat_b[b, c * chunk:c * chunk + kw] for b, c in pairs])
    st = _bdot_nt(ks, qs)
    nq = N_Q_HEADS * chunk
    if not hist_valid:
        n_edge = min(ncq, WINDOW // chunk) * nb
        cpos = [c * chunk for _, c in pairs[:n_edge]]
        kpos = lax.broadcasted_iota(jnp.int32, (1, kw, nq), 1) + (li * tq - WINDOW)
        edge = jnp.concatenate([jnp.where(kpos + cp >= 0, st[i:i + 1], NEG_INF) for i, cp in enumerate(cpos)],
                               axis=0)
        st = jnp.concatenate([edge, st[n_edge:]], axis=0) if n_edge < len(pairs) else edge
    hlane = lax.broadcasted_iota(jnp.int32, (1, 1, nq), 2)
    sk = jnp.full((1, 1, nq), sink_ref[layer, N_Q_HEADS - 1], F32)
    for h in range(N_Q_HEADS - 2, -1, -1):
        sk = jnp.where(hlane < (h + 1) * chunk, sink_ref[layer, h], sk)
    m = jnp.maximum(jnp.max(st, axis=1, keepdims=True), sk)
    p = jnp.exp(st - m)
    den = jnp.sum(p, axis=1, keepdims=True) + jnp.exp(sk - m)
    pn = (p * (1.0 / den)).astype(BF16)
    o = _bdot_tn(pn, vs)
    lo = jnp.logical_not(slot_hi)
    for i, (b, c) in enumerate(pairs):
        oc = o[i]
        col0 = jnp.where(lo, oc[0:chunk], pltpu.roll(oc[chunk:2 * chunk], hd, axis=1))
        col1 = jnp.where(lo, pltpu.roll(oc[2 * chunk:3 * chunk], hd, axis=1), oc[3 * chunk:4 * chunk])
        y_ref[b, c * chunk:(c + 1) * chunk, :] = jnp.concatenate([col0, col1], axis=1).astype(BF16)


def _attn(ud, layer, qg, kg, sinks, chunk, caches=(), tails=()):
    b, l, _ = ud.shape
    tq = min(l, MIXER_ROWS)
    assert l % tq == 0 and tq % chunk == 0
    kvw = N_KV_HEADS * HEAD_DIM
    layer_spec = pl.BlockSpec((None, b, WINDOW, kvw), lambda j: (layer, 0, 0, 0))
    n_fixed = 4 + len(caches)
    return pl.pallas_call(
        functools.partial(_attn_body, layer=layer, nb=b, tq=tq, chunk=chunk, hist_valid=bool(caches),
                          n_alias=len(tails)),
        grid=(l // tq,),
        in_specs=[
            pl.BlockSpec(memory_space=pltpu.SMEM),
            pl.BlockSpec((b, tq, D_COLS), lambda j: (0, j, 0)),
            _layer_block((1, GROUP_WIDTH), layer, 1),
            _layer_block((1, kvw), layer, 1),
        ] + [layer_spec for _ in caches] + [pl.BlockSpec(memory_space=pl.ANY) for _ in tails],
        out_specs=[pl.BlockSpec((b, tq, GROUP_WIDTH), lambda j: (0, j, 0))] + 2 * [
            layer_spec if tails else pl.BlockSpec((DEPTH, b, WINDOW, kvw), lambda j: (0, 0, 0, 0))],
        out_shape=[
            jax.ShapeDtypeStruct((b, l, GROUP_WIDTH), BF16),
            jax.ShapeDtypeStruct((DEPTH, b, WINDOW, kvw), F32),
            jax.ShapeDtypeStruct((DEPTH, b, WINDOW, kvw), F32),
        ],
        input_output_aliases={n_fixed + i: 1 + i for i in range(len(tails))},
        scratch_shapes=[pltpu.VMEM((b, WINDOW, kvw), F32), pltpu.VMEM((b, WINDOW, kvw), F32)],
        compiler_params=_params(("arbitrary",)),
        name="attn",
    )(sinks, ud, qg, kg, *caches, *tails)


def _rwkv_body(*refs, nb, tb, chunk, n_cast, has_past, layer, n_alias):
    u_ref, mu_ref, wl_ref, vec_ref = refs[:4]
    n_in = 6 if has_past else 4
    cast_in = refs[n_in:n_in + n_cast]
    n_out0 = n_in + n_cast + n_alias
    y_ref, sn_ref = refs[n_out0:n_out0 + 2]
    cast_out = refs[n_out0 + 2:n_out0 + 2 + n_cast]
    prev_ref, s_ref, yacc_ref = refs[n_out0 + 2 + n_cast:]
    li = pl.program_id(0)
    for src, dst in zip(cast_in, cast_out):
        dst[...] = src[...].astype(BF16)
    gw = GROUP_WIDTH
    hd = RWKV_HEAD
    nh = RWKV_HEADS
    rows = nb * tb
    nc = tb // chunk

    @pl.when(li == 0)
    def _():
        if not has_past:
            prev_ref[...] = jnp.zeros(prev_ref.shape, F32)
            s_ref[...] = jnp.zeros(s_ref.shape, F32)
            return
        sp_ref, s0_ref = refs[4:6]
        prev_ref[...] = sp_ref[...]
        zero = jnp.zeros((nb, hd, hd), F32)
        for h in range(nh):
            sh = s0_ref[:, h]
            s_ref[h * nb:(h + 1) * nb] = jnp.concatenate([sh, zero] if h % 2 == 0 else [zero, sh], axis=-1)

    row = lax.broadcasted_iota(jnp.int32, (tb, 1), 0)
    mu = mu_ref[...]
    xs_parts = []
    for b in range(nb):
        ub = u_ref[b]
        prev = jnp.where(row == 0, prev_ref[b], pltpu.roll(ub, 1, axis=0))
        prev_ref[b] = ub[tb - 1:tb, :]
        xs_parts.append(ub + mu * (prev - ub))
    xs = jnp.concatenate(xs_parts, axis=0)
    r = xs[:, 0:gw]
    k = xs[:, gw:2 * gw]
    v = xs[:, 2 * gw:3 * gw]
    lat = xs[:, 3 * gw:B_COLS]
    lane_lat = lax.broadcasted_iota(jnp.int32, (1, B_COLS - 3 * gw), 1)
    act = jnp.where(lane_lat < DECAY_RANK, jnp.tanh(lat),
                    jnp.where(lane_lat < DECAY_RANK + AAA_RANK, lat, _sigmoid(lat)))
    lo = _dot(act.astype(BF16), wl_ref[...])
    w0 = vec_ref[0:1, :]
    a0 = vec_ref[1:2, :]
    k_k = vec_ref[2:3, :]
    k_a = vec_ref[3:4, :]
    r_k = vec_ref[4:5, :]
    gn_g = vec_ref[5:6, :]
    gn_b = vec_ref[6:7, :]
    z = -(w0 + lo[:, 0:gw])
    softplus = jnp.maximum(z, 0.0) + jnp.log(1.0 + jnp.exp(-jnp.abs(z)))
    logw = -jnp.exp(-softplus - 0.5)
    a_rate = _sigmoid(a0 + lo[:, gw:2 * gw])
    gate = lo[:, 2 * gw:3 * gw]
    ones_blk = _block_matrix(gw, hd, 1.0)
    kk = k * k_k
    kk = kk * lax.rsqrt(jnp.maximum(_split_dot_right(kk * kk, ones_blk, 1), 1e-24))
    k_mod = k * (1.0 + (a_rate - 1.0) * k_a)
    b_v = kk * a_rate
    bonus = _split_dot_right(r * k_mod * r_k, ones_blk, 1) * v

    grp = min(rows, 256)
    gi = lax.broadcasted_iota(jnp.int32, (grp, grp), 0)
    gj = lax.broadcasted_iota(jnp.int32, (grp, grp), 1)
    csh = int(math.log2(chunk))
    tri = jnp.where(jnp.logical_and(lax.shift_right_logical(gi, csh) == lax.shift_right_logical(gj, csh),
                                    gj <= gi), 1.0, 0.0).astype(BF16)
    cum = jnp.concatenate([_split_dot_left(tri, logw[g0:g0 + grp], 2) for g0 in range(0, rows, grp)], axis=0)
    cum3 = cum.reshape(nb * nc, chunk, gw)
    cum_c = cum3[:, chunk - 1:chunk, :]
    e_end = jnp.exp(cum_c - cum3).reshape(rows, gw)
    w_c = jnp.exp(cum_c)
    e_neg = jnp.exp(-cum)
    dense = dict(
        a=-kk * jnp.exp(cum - logw),
        r=r * jnp.exp(cum),
        bt=b_v * e_neg,
        kt=k_mod * e_neg,
        bh=b_v * e_end,
        kh=k_mod * e_end,
    )
    slot_hi = lax.broadcasted_iota(jnp.int32, (1, 2 * hd), 1) >= hd

    def head_tile(x, h, own_slot):
        col = x[:, (h // 2) * 2 * hd:(h // 2 + 1) * 2 * hd]
        keep = slot_hi if (h % 2 == 1) == own_slot else jnp.logical_not(slot_hi)
        if x.dtype == BF16:
            return col * jnp.where(keep, 1.0, 0.0).astype(BF16)
        return jnp.where(keep, col, 0.0)

    v_sw = jnp.concatenate([pltpu.roll(v[:, j * 2 * hd:(j + 1) * 2 * hd], hd, axis=1) for j in range(nh // 2)],
                           axis=1)
    tiles = {"r": [head_tile(dense["r"], h, True) for h in range(nh)]}
    dense_b = {name: dense[name].astype(BF16) for name in ("a", "bt", "kt", "bh", "kh")}
    tiles_b = {name: [head_tile(dense_b[name], h, True) for h in range(nh)] for name in ("a", "bh", "kh")}
    for name in ("bt", "kt"):
        tiles_b[name] = [dense_b[name][:, (h // 2) * 2 * hd:(h // 2 + 1) * 2 * hd] for h in range(nh)]
    v_sw_b = v_sw.astype(BF16)
    v_tiles = [head_tile(v_sw_b, h, False) for h in range(nh)]

    def blocks(per_head, c):
        return jnp.stack([per_head[h][b * tb + c * chunk:b * tb + (c + 1) * chunk]
                          for h in range(nh) for b in range(nb)])

    n = nh * nb
    ri = lax.broadcasted_iota(jnp.int32, (2 * chunk, 2 * chunk), 0)
    ci = jnp.bitwise_and(lax.broadcasted_iota(jnp.int32, (2 * chunk, 2 * chunk), 1), chunk - 1)
    gmask = ci < jnp.bitwise_and(ri, chunk - 1) + lax.shift_right_logical(ri, csh)
    zeros_c = jnp.zeros((n, chunk, 2 * hd), BF16)
    n_sq = int(math.log2(chunk))
    for c in range(nc):
        a_b = blocks(tiles_b["a"], c)
        r_f = blocks(tiles["r"], c)
        v_b = blocks(v_tiles, c)
        ar = jnp.concatenate([a_b, r_f.astype(BF16)], axis=1)
        bk = jnp.concatenate([blocks(tiles_b["bt"], c), blocks(tiles_b["kt"], c)], axis=1)
        bhkh = jnp.concatenate([blocks(tiles_b["bh"], c), blocks(tiles_b["kh"], c)], axis=1)
        g = jnp.where(gmask, _bdot_nt(ar, bk), 0.0)
        g_top = g[:, :chunk, :]
        g_bot = g[:, chunk:, :].astype(BF16)
        w = a_b.astype(F32) + _bdot(g_top.astype(BF16), jnp.concatenate([zeros_c, v_b], axis=1))
        p = g_top[:, :, :chunk]
        for i in range(n_sq):
            pb = p.astype(BF16)
            if i + 1 < n_sq:
                res = _bdot(pb, jnp.concatenate([w.astype(BF16), pb], axis=2))
                w = w + res[:, :, :2 * hd]
                p = res[:, :, 2 * hd:]
            else:
                hc = chunk // 2
                low = w[:, hc:, :] + _bdot(pb[:, hc:, :hc], w[:, :hc, :].astype(BF16))
                w = jnp.concatenate([w[:, :hc, :], low], axis=1)
        xv = jnp.concatenate([w.astype(BF16), v_b], axis=1)
        ry = _bdot(g_bot, xv)
        mp = _bdot_tn(xv, bhkh)
        s_old = s_ref[...]
        s_b = s_old.astype(BF16)
        y_nt = _bdot_nt((ry + r_f).astype(BF16), s_b)
        wc = jnp.stack([w_c[b * nc + c][:, (h // 2) * 2 * hd:(h // 2 + 1) * 2 * hd]
                        for h in range(nh) for b in range(nb)])
        psi = jnp.concatenate(
            [mp[h * nb:(h + 1) * nb, (1 - h % 2) * hd:(2 - h % 2) * hd, :] for h in range(nh)], axis=0)
        s_ref[...] = s_old * wc + _bdot(s_b, mp.astype(BF16)) + psi
        for b in range(nb):
            ys = [y_nt[h * nb + b] + ry[h * nb + b][:, (1 - h % 2) * hd:(2 - h % 2) * hd] for h in range(nh)]
            yacc_ref[b * tb + c * chunk:b * tb + (c + 1) * chunk, :] = jnp.concatenate(ys, axis=1)

    y = yacc_ref[...]
    avg_blk = _block_matrix(gw, hd, 1.0 / hd)
    m = _split_dot_right(y, avg_blk, 1)
    d = y - m
    var = _split_dot_right(d * d, avg_blk, 1)
    yn = d * lax.rsqrt(var + GN_EPS) * gn_g + gn_b
    out = ((yn + bonus) * gate).astype(BF16)
    for b in range(nb):
        y_ref[b] = out[b * tb:(b + 1) * tb]

    @pl.when(li == pl.num_programs(0) - 1)
    def _():
        for h in range(nh):
            s_h = s_ref[h * nb:(h + 1) * nb, :, (h % 2) * hd:(h % 2 + 1) * hd]
            if n_alias:
                sn_ref[:, h] = s_h
            else:
                for lyr in range(DEPTH):
                    sn_ref[lyr, :, h] = s_h if lyr == layer else jnp.zeros(s_h.shape, F32)


def _rwkv(ub, layer, mu, wl, vec, chunk, past=(), to_cast=(), state_buf=()):
    b, l, _ = ub.shape
    tb = min(l, MIXER_ROWS)
    assert l % tb == 0 and tb % chunk == 0
    steps = l // tb
    hd = RWKV_HEAD
    n = RWKV_HEADS * b
    sshape = (b, RWKV_HEADS, hd, hd)
    cast_specs = []
    for wt in to_cast:
        depth, rows, cols = wt.shape
        slab = rows // steps
        assert rows % steps == 0 and slab % (2 * SUBLANES) == 0
        cast_specs.append(pl.BlockSpec((depth, slab, cols), lambda j: (0, j, 0)))
    outs = pl.pallas_call(
        functools.partial(_rwkv_body, nb=b, tb=tb, chunk=chunk, n_cast=len(to_cast), has_past=bool(past),
                          layer=layer, n_alias=len(state_buf)),
        grid=(steps,),
        in_specs=[
            pl.BlockSpec((b, tb, B_COLS), lambda j: (0, j, 0)),
            _layer_block((1, B_COLS), layer, 1),
            _layer_block((128, 3 * GROUP_WIDTH), layer, 1),
            _layer_block((8, GROUP_WIDTH), layer, 1),
        ] + ([pl.BlockSpec((b, 1, B_COLS), lambda j: (0, 0, 0)),
              pl.BlockSpec(sshape, lambda j: (0, 0, 0, 0))] if past else []) + cast_specs
        + [pl.BlockSpec(memory_space=pl.ANY) for _ in state_buf],
        out_specs=[
            pl.BlockSpec((b, tb, GROUP_WIDTH), lambda j: (0, j, 0)),
            pl.BlockSpec((None,) + sshape, lambda j: (layer, 0, 0, 0, 0)) if state_buf
            else pl.BlockSpec((DEPTH,) + sshape, lambda j: (0, 0, 0, 0, 0)),
        ] + cast_specs,
        input_output_aliases={4 + len(past) + len(to_cast) + i: 1 for i in range(len(state_buf))},
        out_shape=[
            jax.ShapeDtypeStruct((b, l, GROUP_WIDTH), BF16),
            jax.ShapeDtypeStruct((DEPTH,) + sshape, F32),
        ] + [jax.ShapeDtypeStruct(wt.shape, BF16) for wt in to_cast],
        scratch_shapes=[
            pltpu.VMEM((b, 1, B_COLS), F32),
            pltpu.VMEM((n, hd, 2 * hd), F32),
            pltpu.VMEM((b * tb, GROUP_WIDTH), F32),
        ],
        compiler_params=_params(("arbitrary",)),
        name="rwkv",
    )(ub, mu, wl, vec, *past, *to_cast, *state_buf)
    return outs[0], outs[1], tuple(outs[2:])


def _outffn_body(x_ref, ya_ref, yb_ref, yc_ref, yd_ref, g_ref, wo_in, wg_in, wu_in, wd_in, o_ref,
                 *scratch, layer, staged):
    if staged:
        wo_ref, wg_ref, wu_ref, wd_ref, sems = scratch
        pairs = ((wo_in, wo_ref), (wg_in, wg_ref), (wu_in, wu_ref), (wd_in, wd_ref))
        copies = [pltpu.make_async_copy(src.at[layer], dst, sems.at[k]) for k, (src, dst) in enumerate(pairs)]
        for k, copy in enumerate(copies):
            copy.start(priority=k % 2)
        wait = lambda *ks: [copies[k].wait() for k in ks]
    else:
        wo_ref, wg_ref, wu_ref, wd_ref = wo_in, wg_in, wu_in, wd_in
        wait = lambda *ks: None
    ycat = jnp.concatenate([ya_ref[...], yb_ref[...], yc_ref[...], yd_ref[...]], axis=-1)
    wait(0)
    x1 = x_ref[...] + _dot(ycat, wo_ref[...])
    ms = jnp.mean(x1 * x1, axis=-1, keepdims=True)
    hn = ((x1 * lax.rsqrt(ms + RMS_EPS)) * g_ref[...]).astype(BF16)
    wait(1, 2)
    hg = _dot(hn, wg_ref[...])
    hu = _dot(hn, wu_ref[...])
    act = (hg * _sigmoid(hg) * hu).astype(BF16)
    wait(3)
    o_ref[...] = x1 + _dot(act, wd_ref[...])


def _outffn(x2d, ya, yb, yc, yd, layer, wo, g, wg, wu, wd):
    t = x2d.shape[0]
    tm = min(t, MATMUL_ROWS)
    assert t % tm == 0
    staged = t == tm
    row = lambda w: pl.BlockSpec((tm, w), lambda i: (i, 0))
    weight_shapes = ((4 * GROUP_WIDTH, D_MODEL), (D_MODEL, D_FF), (D_MODEL, D_FF), (D_FF, D_MODEL))
    if staged:
        weight_specs = [pl.BlockSpec(memory_space=pl.ANY) for _ in weight_shapes]
        scratch = [pltpu.VMEM(s, BF16) for s in weight_shapes] + [pltpu.SemaphoreType.DMA((len(weight_shapes),))]
    else:
        weight_specs = [_layer_block(s, layer, 1) for s in weight_shapes]
        scratch = []
    return pl.pallas_call(
        functools.partial(_outffn_body, layer=layer, staged=staged),
        grid=(t // tm,),
        in_specs=[
            row(D_MODEL), row(GROUP_WIDTH), row(GROUP_WIDTH), row(GROUP_WIDTH), row(GROUP_WIDTH),
            _layer_block((1, D_MODEL), layer, 1),
        ] + weight_specs,
        out_specs=row(D_MODEL),
        out_shape=jax.ShapeDtypeStruct((t, D_MODEL), F32),
        scratch_shapes=scratch,
        compiler_params=_params(("parallel",)),
        name="outffn",
    )(x2d, ya, yb, yc, yd, g, wo, wg, wu, wd)


_LATE_WEIGHTS = ('w_out', 'ffn_w_gate', 'ffn_w_up', 'ffn_w_down')


def _rows8(vectors):
    stacked = jnp.stack(vectors, axis=1)
    return jnp.pad(stacked, ((0, 0), (0, SUBLANES - stacked.shape[1]), (0, 0)))


def _stacked_weights(p):
    gw = GROUP_WIDTH
    depth = p['w_in'].shape[0]
    place = lambda a, before, after: jnp.pad(a, ((0, 0), (0, 0), (before, after)))
    wl = jnp.concatenate([place(p['rwkv_w2'], 0, 2 * gw), place(p['rwkv_a2'], gw, gw),
                          place(p['rwkv_g2'], 2 * gw, 0)], axis=1)
    pc = gw // len(POOL_WINDOWS)
    wbd = jnp.concatenate([place(p['pool_w'][:, g], g * pc, gw - (g + 1) * pc)
                           for g in range(len(POOL_WINDOWS))], axis=1)
    return dict(
        norm_mix_g=p['norm_mix_g'][:, None, :],
        w_in=p['w_in'].astype(BF16),
        conv_w=jnp.pad(p['conv_w'], ((0, 0), (0, CONV_PAD - CONV_WIDTH), (0, 0))),
        conv_vec=_rows8([p['conv_b'], p['conv_ln_g'], p['conv_ln_b']]),
        rwkv_mu=p['rwkv_mu'][:, None, :],
        rwkv_wl=wl.astype(BF16),
        rwkv_vec=_rows8([p['rwkv_w0'], p['rwkv_a0'], p['rwkv_k_k'], p['rwkv_k_a'],
                         p['rwkv_r_k'].reshape(depth, gw), p['rwkv_gn_g'], p['rwkv_gn_b']]),
        pool_wbd=wbd.astype(BF16),
        pool_scale=p['pool_scale'][:, None, :],
        attn_qg=jnp.tile(p['attn_q_norm'], (1, N_Q_HEADS))[:, None, :],
        attn_kg=jnp.tile(p['attn_k_norm'], (1, N_KV_HEADS))[:, None, :],
        attn_sinks=p['attn_sinks'],
        norm_ffn_g=p['norm_ffn_g'][:, None, :],
        pending_casts={name: p[name] for name in _LATE_WEIGHTS},
    )


def _trunk(x, conv_hist, rwkv_state, shift_prev, pool_hist, k_cache, v_cache, pos0, chunk, w):
    b, l, _ = x.shape
    has_past = conv_hist is not None
    kvw = N_KV_HEADS * HEAD_DIM
    new = [[] for _ in range(3)]
    kv_caches = kv_tails = state_buf = ()
    if has_past:
        kv_caches = (k_cache.reshape(DEPTH, b, WINDOW, kvw), v_cache.reshape(DEPTH, b, WINDOW, kvw))
    for li in range(DEPTH):
        local_hists = rwkv_past = ()
        if has_past:
            local_hists = (jnp.pad(conv_hist[li], ((0, 0), (CONV_PAD - CONV_HIST, 0), (0, 0))),
                           jnp.pad(pool_hist[li], ((0, 0), (POOL_PAD - POOL_HIST, 0), (0, 0))))
            rwkv_past = (shift_prev[li][:, None, :], rwkv_state[li])
        ub, ud, ya, yc, conv_tail, pool_tail = _projmix(
            x, li, w['norm_mix_g'], w['w_in'], w['conv_w'], w['conv_vec'],
            w['pool_wbd'], w['pool_scale'], pos0, local_hists)
        pending = w.pop('pending_casts', {})
        yb, s_new, cast = _rwkv(ub, li, w['rwkv_mu'], w['rwkv_wl'], w['rwkv_vec'], min(CHUNK, l),
                                rwkv_past, tuple(pending.values()), state_buf)
        state_buf = (s_new,)
        w.update(zip(pending.keys(), cast))
        yd, *kv_tails = _attn(ud, li, w['attn_qg'], w['attn_kg'], w['attn_sinks'], chunk, kv_caches,
                              tuple(kv_tails))
        flat = lambda y: y.reshape(b * l, GROUP_WIDTH)
        x = _outffn(x.reshape(b * l, D_MODEL), flat(ya), flat(yb), flat(yc), flat(yd), li, w['w_out'],
                    w['norm_ffn_g'], w['ffn_w_gate'], w['ffn_w_up'], w['ffn_w_down']
                    ).reshape(b, l, D_MODEL)
        new[0].append(conv_tail[:, CONV_PAD - CONV_HIST:, :])
        new[1].append(ub[:, l - 1, :])
        new[2].append(pool_tail[:, POOL_PAD - POOL_HIST:, :])
    kv_new = tuple(t.reshape(DEPTH, b, WINDOW, N_KV_HEADS, HEAD_DIM) for t in kv_tails)
    conv_new, shift_new, pool_new = (jnp.stack(n) for n in new)
    return x, (conv_new, state_buf[0], shift_new, pool_new) + kv_new


def kernel(x_prompt, x_sample, cache_conv, state_rwkv, state_rwkv_shift, cache_pool, cache_k, cache_v, norm_mix_g, w_in, conv_w, conv_b, conv_ln_g, conv_ln_b, rwkv_mu, rwkv_w0, rwkv_w2, rwkv_a0, rwkv_a2, rwkv_g2, rwkv_k_k, rwkv_k_a, rwkv_r_k, rwkv_gn_g, rwkv_gn_b, pool_w, pool_scale, attn_q_norm, attn_k_norm, attn_sinks, w_out, norm_ffn_g, ffn_w_gate, ffn_w_up, ffn_w_down):
    w = _stacked_weights(dict(
        norm_mix_g=norm_mix_g, w_in=w_in, conv_w=conv_w, conv_b=conv_b, conv_ln_g=conv_ln_g,
        conv_ln_b=conv_ln_b, rwkv_mu=rwkv_mu, rwkv_w0=rwkv_w0, rwkv_w2=rwkv_w2, rwkv_a0=rwkv_a0,
        rwkv_a2=rwkv_a2, rwkv_g2=rwkv_g2, rwkv_k_k=rwkv_k_k, rwkv_k_a=rwkv_k_a, rwkv_r_k=rwkv_r_k,
        rwkv_gn_g=rwkv_gn_g, rwkv_gn_b=rwkv_gn_b, pool_w=pool_w, pool_scale=pool_scale,
        attn_q_norm=attn_q_norm, attn_k_norm=attn_k_norm, attn_sinks=attn_sinks, w_out=w_out,
        norm_ffn_g=norm_ffn_g, ffn_w_gate=ffn_w_gate, ffn_w_up=ffn_w_up, ffn_w_down=ffn_w_down))
    y_p, (conv_p, rwkv_p, shift_p, pool_p, k_p, v_p) = _trunk(
        x_prompt, None, None, None, None, None, None, 0, CHUNK, w)
    y_s, (conv_s, rwkv_s, shift_s, pool_s, k_s, v_s) = _trunk(
        x_sample, cache_conv, state_rwkv, state_rwkv_shift, cache_pool, cache_k, cache_v,
        PAST_LEN, x_sample.shape[1], w)
    return (y_p, y_s, conv_p, conv_s, rwkv_p, rwkv_s, shift_p, shift_s,
            pool_p, pool_s, k_p, k_s, v_p, v_s)
```
